```python
import math
import jax
import jax.numpy as jnp
from jax import lax
import numpy as np

D_MODEL = 1024
BATCH = 8
SEQ = 4096
DEPTH = 2

GRID_W = 64
CTX_LEN = 256
EPS = 1e-6
N_MOD = 6
BRANCH_W = 512
N_BRANCH = 3

DA_HEADS = 4
DA_HD = 64
DA_VD = 2 * DA_HD
ATTN_BLOCK = 128
ROPE_BASE = 10000.0

ML_HEADS = 4
ML_HD = BRANCH_W // ML_HEADS
ML_CONV = 3
ML_CHUNK = 128
ML_NGATE = 2 * 2 * ML_HEADS

S5_P = 16
S5_G = BRANCH_W // S5_P
S5_N = 64

PEER_HEADS = 8
PEER_NKEYS = 128
PEER_EXPERTS = PEER_NKEYS * PEER_NKEYS
PEER_QD = 256
PEER_TOPK = 16
PEER_BLOCK = 128

IN_SPLITS = (DA_HEADS * 2 * DA_HD, DA_HEADS * 2 * DA_HD, DA_HEADS * DA_VD,
             BRANCH_W, BRANCH_W, BRANCH_W, ML_NGATE,
             BRANCH_W,
             N_BRANCH * D_MODEL)
IN_WIDTH = sum(IN_SPLITS)
IN_IDX = tuple(int(v) for v in np.cumsum(IN_SPLITS)[:-1])

kernel_name = 'hybrid_diffattn_mlstm_s5_peer_block'


def rms_norm(x, g):
    xf = x.astype(jnp.float32)
    y = xf * lax.rsqrt(jnp.mean(xf * xf, axis=-1, keepdims=True) + EPS)
    return (y * g.astype(jnp.float32)).astype(x.dtype)


def axial_rope_tables(n_tokens):
    rows = n_tokens // GRID_W
    n_freq = DA_HD // 4
    inv = ROPE_BASE ** (-jnp.arange(n_freq, dtype=jnp.float32) / n_freq)
    r = jnp.repeat(jnp.arange(rows, dtype=jnp.float32), GRID_W)
    col = jnp.tile(jnp.arange(GRID_W, dtype=jnp.float32), rows)
    ang = jnp.concatenate([r[:, None] * inv, col[:, None] * inv], axis=-1)
    return jnp.cos(ang), jnp.sin(ang)


def apply_rope(x, cos, sin):
    half = x.shape[-1] // 2
    xf = x.astype(jnp.float32)
    x1, x2 = xf[..., :half], xf[..., half:]
    c = cos[:, None, None, :]
    s = sin[:, None, None, :]
    return jnp.concatenate([x1 * c - x2 * s, x1 * s + x2 * c], axis=-1).astype(x.dtype)


def dwconv_centred(x, w, b):
    ch = x.shape[-1]
    pad = w.shape[0] // 2
    y = lax.conv_general_dilated(x, w[:, None, :].astype(x.dtype), (1,), [(pad, pad)],
                                 dimension_numbers=('NWC', 'WIO', 'NWC'),
                                 feature_group_count=ch)
    return y + b


def _diff_attend(q, k, v, lam):
    s = jnp.einsum('bqhmd,bkhmd->bhmqk', q, k, preferred_element_type=jnp.float32) * (DA_HD ** -0.5)
    p = jax.nn.softmax(s, axis=-1)
    w = p[:, :, 0] - lam * p[:, :, 1]
    return jnp.einsum('bhqk,bkhe->bqhe', w.astype(v.dtype), v)


def _diff_head_out(o, sub_g, lam_init):
    bsz, n = o.shape[:2]
    y = rms_norm(o, sub_g.reshape(DA_HEADS, DA_VD)) * (1.0 - lam_init)
    return y.reshape(bsz, n, DA_HEADS * DA_VD)


def diff_attention_branch(q, k, v, q_c, k_c, v_c, lam_q1, lam_k1, lam_q2, lam_k2, sub_g,
                          rope_cos, rope_sin, lam_init, need_ctx):
    bsz, n = q.shape[:2]
    n_c = k_c.shape[1]
    shp = (DA_HEADS, 2, DA_HD)
    q = apply_rope(q.reshape(bsz, n, *shp), rope_cos, rope_sin)
    k = apply_rope(k.reshape(bsz, n, *shp), rope_cos, rope_sin)
    v = v.reshape(bsz, n, DA_HEADS, DA_VD)
    k_c = k_c.reshape(bsz, n_c, *shp)
    v_c = v_c.reshape(bsz, n_c, DA_HEADS, DA_VD)
    f32 = jnp.float32
    lam = (jnp.exp(jnp.sum(lam_q1.astype(f32) * lam_k1.astype(f32)))
           - jnp.exp(jnp.sum(lam_q2.astype(f32) * lam_k2.astype(f32))) + lam_init)
    k_all = jnp.concatenate([k, k_c], axis=1)
    v_all = jnp.concatenate([v, v_c], axis=1)
    nb = n // ATTN_BLOCK
    q_blocks = jnp.moveaxis(q.reshape(bsz, nb, ATTN_BLOCK, *shp), 1, 0)
    o = lax.map(lambda qb: _diff_attend(qb, k_all, v_all, lam), q_blocks)
    o = jnp.moveaxis(o, 0, 1).reshape(bsz, n, DA_HEADS, DA_VD)
    y = _diff_head_out(o, sub_g, lam_init)
    y_c = None
    if need_ctx:
        q_c = q_c.reshape(bsz, n_c, *shp)
        y_c = _diff_head_out(_diff_attend(q_c, k_c, v_c, lam), sub_g, lam_init)
    return y, y_c


def _mlstm_chunks(q, k, v, log_i, log_f, state):
    bsz, nh, n, d = q.shape
    nc = n // ML_CHUNK

    def chunked(a):
        return jnp.moveaxis(a.reshape(bsz, nh, nc, ML_CHUNK, *a.shape[3:]), 2, 0)

    tri = jnp.tril(jnp.ones((ML_CHUNK, ML_CHUNK), dtype=bool))

    def step(carry, blk):
        C, nrm, m = carry
        qq, kk, vv, li, lf = blk
        b = jnp.cumsum(lf, axis=-1)
        dmat = jnp.where(tri, b[..., :, None] - b[..., None, :] + li[..., None, :], -jnp.inf)
        inter = b + m[..., None]
        m_t = jnp.maximum(inter, jnp.max(dmat, axis=-1))
        s = jnp.einsum('bhtd,bhsd->bhts', qq, kk) * jnp.exp(dmat - m_t[..., None])
        w_prev = jnp.exp(inter - m_t)
        num = (jnp.einsum('bhts,bhse->bhte', s, vv)
               + w_prev[..., None] * jnp.einsum('bhtd,bhed->bhte', qq, C))
        den = jnp.sum(s, axis=-1) + w_prev * jnp.einsum('bhtd,bhd->bht', qq, nrm)
        h = num / jnp.maximum(jnp.abs(den), jnp.exp(-m_t))[..., None]
        b_end = b[..., -1]
        g = b_end[..., None] - b + li
        m_new = jnp.maximum(b_end + m, jnp.max(g, axis=-1))
        wg = jnp.exp(g - m_new[..., None])
        decay = jnp.exp(b_end + m - m_new)
        C = decay[..., None, None] * C + jnp.einsum('bhse,bhsd->bhed', vv * wg[..., None], kk)
        nrm = decay[..., None] * nrm + jnp.einsum('bhs,bhsd->bhd', wg, kk)
        return (C, nrm, m_new), h

    xs = (chunked(q), chunked(k), chunked(v), chunked(log_i), chunked(log_f))
    state, hs = lax.scan(step, state, xs)
    return jnp.moveaxis(hs, 0, 2).reshape(bsz, nh, n, d), state


def _mlstm_out(h, o, norm_g):
    bsz, nh, n, d = h.shape
    hn = rms_norm(jnp.transpose(h, (0, 2, 1, 3)), norm_g.reshape(nh, d))
    og = jax.nn.sigmoid(o.astype(jnp.float32)).reshape(bsz, n, nh, d)
    return (hn * og).reshape(bsz, n, nh * d).astype(o.dtype)


def mlstm_branch(xm, v, o, g, xm_c, v_c, o_c, g_c, conv_w, conv_b, wq, wk, gate_b, norm_g,
                 need_ctx):
    f32 = jnp.float32

    def prepare(xm_, v_, g_):
        bsz, n = xm_.shape[:2]
        xconv = jax.nn.silu(dwconv_centred(xm_, conv_w, conv_b)).astype(f32)
        xh = xconv.reshape(bsz, n, ML_HEADS, ML_HD)
        q = jnp.einsum('blhd,hde->bhle', xh, wq.astype(f32)) * (ML_HD ** -0.5)
        k = jnp.einsum('blhd,hde->bhle', xh, wk.astype(f32))
        vh = jnp.transpose(v_.astype(f32).reshape(bsz, n, ML_HEADS, ML_HD), (0, 2, 1, 3))
        gg = (g_.astype(f32) + gate_b.astype(f32)).reshape(bsz, n, 2, 2, ML_HEADS)
        gg = jnp.transpose(gg, (2, 3, 0, 4, 1))
        return q, k, vh, gg[:, 0], jax.nn.log_sigmoid(gg[:, 1])

    q, k, vh, log_i, log_f = prepare(xm, v, g)
    q_c, k_c, vh_c, log_ic, log_fc = prepare(xm_c, v_c, g_c)
    bsz = q.shape[0]
    zero = (jnp.zeros((bsz, ML_HEADS, ML_HD, ML_HD), f32),
            jnp.zeros((bsz, ML_HEADS, ML_HD), f32),
            jnp.zeros((bsz, ML_HEADS), f32))

    def flip(a):
        return jnp.flip(a, axis=2)

    h_cf, st_f = _mlstm_chunks(q_c, k_c, vh_c, log_ic[0], log_fc[0], zero)
    h_cb, st_b = _mlstm_chunks(flip(q_c), flip(k_c), flip(vh_c), flip(log_ic[1]), flip(log_fc[1]), zero)
    h_f, _ = _mlstm_chunks(q, k, vh, log_i[0], log_f[0], st_f)
    h_b, _ = _mlstm_chunks(flip(q), flip(k), flip(vh), flip(log_i[1]), flip(log_f[1]), st_b)
    y = _mlstm_out(h_f + flip(h_b), o, norm_g)
    y_c = _mlstm_out(h_cf + flip(h_cb), o_c, norm_g) if need_ctx else None
    return y, y_c


def _s5_discretise(lam_re, lam_im, log_dt, b_re, b_im):
    f32 = jnp.float32
    lam_re, lam_im = lam_re.astype(f32), lam_im.astype(f32)
    b_re, b_im = b_re.astype(f32), b_im.astype(f32)
    dt = jnp.exp(log_dt.astype(f32))[:, None]
    mag = jnp.exp(lam_re * dt)
    ab_re = mag * jnp.cos(lam_im * dt)
    ab_im = mag * jnp.sin(lam_im * dt)
    den = lam_re * lam_re + lam_im * lam_im
    z_re = ((ab_re - 1.0) * lam_re + ab_im * lam_im) / den
    z_im = (ab_im * lam_re - (ab_re - 1.0) * lam_im) / den
    bb_re = z_re[..., None] * b_re - z_im[..., None] * b_im
    bb_im = z_re[..., None] * b_im + z_im[..., None] * b_re
    return ab_re, ab_im, bb_re, bb_im


def _complex_affine_combine(e1, e2):
    a1r, a1i, b1r, b1i = e1
    a2r, a2i, b2r, b2i = e2
    return (a1r * a2r - a1i * a2i, a1r * a2i + a1i * a2r,
            a2r * b1r - a2i * b1i + b2r, a2r * b1i + a2i * b1r + b2i)


def _s5_states(u, ab_re, ab_im, bb_re, bb_im, s0_re, s0_im, reverse):
    bu_re = jnp.einsum('blgp,gnp->blgn', u, bb_re)
    bu_im = jnp.einsum('blgp,gnp->blgn', u, bb_im)
    first = -1 if reverse else 0
    bu_re = bu_re.at[:, first].add(ab_re * s0_re - ab_im * s0_im)
    bu_im = bu_im.at[:, first].add(ab_re * s0_im + ab_im * s0_re)
    a_re = jnp.broadcast_to(ab_re, bu_re.shape)
    a_im = jnp.broadcast_to(ab_im, bu_im.shape)
    _, _, x_re, x_im = lax.associative_scan(_complex_affine_combine, (a_re, a_im, bu_re, bu_im),
                                            reverse=reverse, axis=1)
    return x_re, x_im


def _s5_readout(x_re, x_im, c_re, c_im):
    return (jnp.einsum('blgn,gpn->blgp', x_re, c_re.astype(jnp.float32))
            - jnp.einsum('blgn,gpn->blgp', x_im, c_im.astype(jnp.float32)))


def _s5_glu(y, glu_w, glu_b):
    g = jax.nn.gelu(y, approximate=False)
    return g * jax.nn.sigmoid(g @ glu_w.astype(jnp.float32) + glu_b.astype(jnp.float32))


def s5_branch(u, u_c, lam_re, lam_im, log_dt, b_re, b_im, c_re, c_im, d_skip, glu_w, glu_b,
              need_ctx):
    f32 = jnp.float32
    bsz, n = u.shape[:2]
    n_c = u_c.shape[1]
    ug = u.astype(f32).reshape(bsz, n, S5_G, S5_P)
    ucg = u_c.astype(f32).reshape(bsz, n_c, S5_G, S5_P)
    dsk = d_skip.astype(f32).reshape(S5_G, S5_P)
    y = dsk * ug
    y_c = dsk * ucg if need_ctx else None
    zeros = jnp.zeros((bsz, S5_G, S5_N), f32)
    for r in range(2):
        rev = r == 1
        ab_re, ab_im, bb_re, bb_im = _s5_discretise(lam_re[r], lam_im[r], log_dt[r], b_re[r], b_im[r])
        xc_re, xc_im = _s5_states(ucg, ab_re, ab_im, bb_re, bb_im, zeros, zeros, rev)
        end = 0 if rev else -1
        x_re, x_im = _s5_states(ug, ab_re, ab_im, bb_re, bb_im, xc_re[:, end], xc_im[:, end], rev)
        y = y + _s5_readout(x_re, x_im, c_re[r], c_im[r])
        if need_ctx:
            y_c = y_c + _s5_readout(xc_re, xc_im, c_re[r], c_im[r])
    out = _s5_glu(y.reshape(bsz, n, S5_G * S5_P), glu_w, glu_b).astype(u.dtype)
    out_c = None
    if need_ctx:
        out_c = _s5_glu(y_c.reshape(bsz, n_c, S5_G * S5_P), glu_w, glu_b).astype(u.dtype)
    return out, out_c


def merge_branches(ya, yb, ys, gate_pre, w_br, w_out):
    ys_all = jnp.stack([ya, yb, ys], axis=-2)
    proj = jnp.einsum('blrc,rcd->blrd', ys_all, w_br)
    d_model = gate_pre.shape[-1] // N_BRANCH
    gates = jax.nn.sigmoid(gate_pre.astype(jnp.float32).reshape(*gate_pre.shape[:-1], N_BRANCH, d_model))
    merged = jnp.sum(gates * proj.astype(jnp.float32), axis=-2).astype(ya.dtype)
    return merged @ w_out


def token_mixer(h, hc, p, rope_cos, rope_sin, lam_init, need_ctx):
    z = jnp.split(h @ p['w_in'], IN_IDX, axis=-1)
    zc = jnp.split(hc @ p['w_in'], IN_IDX, axis=-1)
    ya, ya_c = diff_attention_branch(z[0], z[1], z[2], zc[0], zc[1], zc[2],
                                     p['da_lam_q1'], p['da_lam_k1'], p['da_lam_q2'], p['da_lam_k2'],
                                     p['da_sub_g'], rope_cos, rope_sin, lam_init, need_ctx)
    yb, yb_c = mlstm_branch(z[3], z[4], z[5], z[6], zc[3], zc[4], zc[5], zc[6],
                            p['ml_conv_w'], p['ml_conv_b'], p['ml_wq'], p['ml_wk'],
                            p['ml_gate_b'], p['ml_norm_g'], need_ctx)
    ys, ys_c = s5_branch(z[7], zc[7], p['s5_lam_re'], p['s5_lam_im'], p['s5_log_dt'],
                         p['s5_b_re'], p['s5_b_im'], p['s5_c_re'], p['s5_c_im'], p['s5_d'],
                         p['s5_glu_w'], p['s5_glu_b'], need_ctx)
    y = merge_branches(ya, yb, ys, z[8], p['w_br'], p['w_out'])
    y_c = merge_branches(ya_c, yb_c, ys_c, zc[8], p['w_br'], p['w_out']) if need_ctx else None
    return y, y_c


def peer_ffn(h, wq, sub_k, u_tab, v_tab):
    n_tok = h.shape[0]
    half = PEER_QD // 2
    q = (h @ wq).astype(jnp.float32).reshape(n_tok, PEER_HEADS, 2, half)
    s1 = jnp.einsum('thc,kc->thk', q[:, :, 0], sub_k[0].astype(jnp.float32))
    s2 = jnp.einsum('thc,kc->thk', q[:, :, 1], sub_k[1].astype(jnp.float32))
    v1, i1 = lax.top_k(s1, PEER_TOPK)
    v2, i2 = lax.top_k(s2, PEER_TOPK)
    cand_s = (v1[..., :, None] + v2[..., None, :]).reshape(n_tok, PEER_HEADS, PEER_TOPK * PEER_TOPK)
    cand_i = (i1[..., :, None] * PEER_NKEYS + i2[..., None, :]).reshape(n_tok, PEER_HEADS, PEER_TOPK * PEER_TOPK)
    top_s, top_j = lax.top_k(cand_s, PEER_TOPK)
    experts = jnp.take_along_axis(cand_i, top_j, axis=-1)
    gates = jax.nn.softmax(top_s, axis=-1)
    nb = n_tok // PEER_BLOCK

    def block(args):
        hb, eb, gb = args
        act = jax.nn.gelu(jnp.einsum('td,thkd->thk', hb, u_tab[eb]).astype(jnp.float32), approximate=False)
        return jnp.einsum('thk,thkd->td', (gb * act).astype(hb.dtype), v_tab[eb])

    out = lax.map(block, (h.reshape(nb, PEER_BLOCK, -1),
                          experts.reshape(nb, PEER_BLOCK, PEER_HEADS, PEER_TOPK),
                          gates.reshape(nb, PEER_BLOCK, PEER_HEADS, PEER_TOPK)))
    return out.reshape(n_tok, -1)


def setup_inputs(seed: int = 0) -> dict:
    key = jax.random.key(seed)
    ks = iter(jax.random.split(key, 48))
    f32 = jnp.float32

    def nrm(shape, std):
        return std * jax.random.normal(next(ks), shape, f32)

    D = D_MODEL
    x = nrm((BATCH, SEQ, D), 1.0)
    c = nrm((BATCH, D), 1.0)
    ctx = nrm((BATCH, CTX_LEN, D), 1.0)
    c_ctx = nrm((D,), 1.0)
    w_mod = nrm((DEPTH, D, N_MOD * D), 0.5 * D ** -0.5)
    b_mod = nrm((DEPTH, N_MOD * D), 0.02)
    norm1_g = 1.0 + nrm((DEPTH, D), 0.02)
    norm2_g = 1.0 + nrm((DEPTH, D), 0.02)
    w_in = nrm((DEPTH, D, IN_WIDTH), D ** -0.5)
    da_lam_q1 = nrm((DEPTH, DA_HD), 0.1)
    da_lam_k1 = nrm((DEPTH, DA_HD), 0.1)
    da_lam_q2 = nrm((DEPTH, DA_HD), 0.1)
    da_lam_k2 = nrm((DEPTH, DA_HD), 0.1)
    da_sub_g = 1.0 + nrm((DEPTH, DA_HEADS * DA_VD), 0.02)
    ml_conv_w = nrm((DEPTH, ML_CONV, BRANCH_W), ML_CONV ** -0.5)
    ml_conv_b = nrm((DEPTH, BRANCH_W), 0.02)
    ml_wq = nrm((DEPTH, ML_HEADS, ML_HD, ML_HD), ML_HD ** -0.5)
    ml_wk = nrm((DEPTH, ML_HEADS, ML_HD, ML_HD), ML_HD ** -0.5)
    f_bias = jnp.linspace(3.0, 6.0, ML_HEADS, dtype=f32)
    gate_base = jnp.concatenate([jnp.zeros((ML_HEADS,), f32), f_bias, jnp.zeros((ML_HEADS,), f32), f_bias])
    ml_gate_b = gate_base + nrm((DEPTH, ML_NGATE), 0.1)
    ml_norm_g = 1.0 + nrm((DEPTH, BRANCH_W), 0.02)
    s5_lam_re = -0.5 + nrm((DEPTH, 2, S5_G, S5_N), 0.01)
    s5_lam_im = math.pi * jnp.arange(S5_N, dtype=f32) + nrm((DEPTH, 2, S5_G, S5_N), 0.01)
    s5_log_dt = jax.random.uniform(next(ks), (DEPTH, 2, S5_G), f32, math.log(1e-3), math.log(1e-1))
    s5_b_re = nrm((DEPTH, 2, S5_G, S5_N, S5_P), (2 * S5_P) ** -0.5)
    s5_b_im = nrm((DEPTH, 2, S5_G, S5_N, S5_P), (2 * S5_P) ** -0.5)
    s5_c_re = nrm((DEPTH, 2, S5_G, S5_P, S5_N), S5_N ** -0.5)
    s5_c_im = nrm((DEPTH, 2, S5_G, S5_P, S5_N), S5_N ** -0.5)
    s5_d = nrm((DEPTH, BRANCH_W), 0.5)
    s5_glu_w = nrm((DEPTH, BRANCH_W, BRANCH_W), BRANCH_W ** -0.5)
    s5_glu_b = nrm((DEPTH, BRANCH_W), 0.02)
    w_br = nrm((DEPTH, N_BRANCH, BRANCH_W, D), BRANCH_W ** -0.5)
    w_out = nrm((DEPTH, D, D), D ** -0.5)
    peer_wq = nrm((DEPTH, D, PEER_HEADS * PEER_QD), D ** -0.5)
    peer_sub_k = nrm((DEPTH, 2, PEER_NKEYS, PEER_QD // 2), (PEER_QD // 2) ** -0.5)
    peer_u = nrm((DEPTH, PEER_EXPERTS, D), D ** -0.5)
    peer_v = nrm((DEPTH, PEER_EXPERTS, D), 1.0)
    final_g = 1.0 + nrm((D,), 0.02)
    return {'x': x, 'c': c, 'ctx': ctx, 'c_ctx': c_ctx, 'w_mod': w_mod, 'b_mod': b_mod,
            'norm1_g': norm1_g, 'norm2_g': norm2_g, 'w_in': w_in,
            'da_lam_q1': da_lam_q1, 'da_lam_k1': da_lam_k1, 'da_lam_q2': da_lam_q2,
            'da_lam_k2': da_lam_k2, 'da_sub_g': da_sub_g,
            'ml_conv_w': ml_conv_w, 'ml_conv_b': ml_conv_b, 'ml_wq': ml_wq, 'ml_wk': ml_wk,
            'ml_gate_b': ml_gate_b, 'ml_norm_g': ml_norm_g,
            's5_lam_re': s5_lam_re, 's5_lam_im': s5_lam_im, 's5_log_dt': s5_log_dt,
            's5_b_re': s5_b_re, 's5_b_im': s5_b_im, 's5_c_re': s5_c_re, 's5_c_im': s5_c_im,
            's5_d': s5_d, 's5_glu_w': s5_glu_w, 's5_glu_b': s5_glu_b,
            'w_br': w_br, 'w_out': w_out, 'peer_wq': peer_wq, 'peer_sub_k': peer_sub_k,
            'peer_u': peer_u, 'peer_v': peer_v, 'final_g': final_g}


def reference(x, c, ctx, c_ctx, w_mod, b_mod, norm1_g, norm2_g, w_in,
              da_lam_q1, da_lam_k1, da_lam_q2, da_lam_k2, da_sub_g,
              ml_conv_w, ml_conv_b, ml_wq, ml_wk, ml_gate_b, ml_norm_g,
              s5_lam_re, s5_lam_im, s5_log_dt, s5_b_re, s5_b_im, s5_c_re, s5_c_im,
              s5_d, s5_glu_w, s5_glu_b, w_br, w_out, peer_wq, peer_sub_k, peer_u, peer_v,
              final_g):
    bsz, n, d_model = x.shape
    n_c = ctx.shape[1]
    rope_cos, rope_sin = axial_rope_tables(n)
    xc = ctx
    for l in range(DEPTH):
        need_ctx = l < DEPTH - 1
        lam_init = 0.8 - 0.6 * math.exp(-0.3 * l)
        m_lat = jnp.split(jax.nn.silu(c) @ w_mod[l] + b_mod[l], N_MOD, axis=-1)
        m_ctx = jnp.split(jax.nn.silu(c_ctx) @ w_mod[l] + b_mod[l], N_MOD, axis=-1)
        sh1, sc1, g1, sh2, sc2, g2 = [t[:, None, :] for t in m_lat]
        csh1, csc1, cg1, csh2, csc2, cg2 = m_ctx
        p = dict(w_in=w_in[l], da_lam_q1=da_lam_q1[l], da_lam_k1=da_lam_k1[l],
                 da_lam_q2=da_lam_q2[l], da_lam_k2=da_lam_k2[l], da_sub_g=da_sub_g[l],
                 ml_conv_w=ml_conv_w[l], ml_conv_b=ml_conv_b[l], ml_wq=ml_wq[l], ml_wk=ml_wk[l],
                 ml_gate_b=ml_gate_b[l], ml_norm_g=ml_norm_g[l],
                 s5_lam_re=s5_lam_re[l], s5_lam_im=s5_lam_im[l], s5_log_dt=s5_log_dt[l],
                 s5_b_re=s5_b_re[l], s5_b_im=s5_b_im[l], s5_c_re=s5_c_re[l], s5_c_im=s5_c_im[l],
                 s5_d=s5_d[l], s5_glu_w=s5_glu_w[l], s5_glu_b=s5_glu_b[l],
                 w_br=w_br[l], w_out=w_out[l])
        h = rms_norm(x, norm1_g[l]) * (1.0 + sc1) + sh1
        hc = rms_norm(xc, norm1_g[l]) * (1.0 + csc1) + csh1
        y, y_c = token_mixer(h, hc, p, rope_cos, rope_sin, lam_init, need_ctx)
        x = x + g1 * y
        h2 = rms_norm(x, norm2_g[l]) * (1.0 + sc2) + sh2
        tokens = h2.reshape(-1, d_model)
        if need_ctx:
            xc = xc + cg1 * y_c
            hc2 = rms_norm(xc, norm2_g[l]) * (1.0 + csc2) + csh2
            tokens = jnp.concatenate([tokens, hc2.reshape(-1, d_model)], axis=0)
        f = peer_ffn(tokens, peer_wq[l], peer_sub_k[l], peer_u[l], peer_v[l])
        x = x + g2 * f[:bsz * n].reshape(bsz, n, d_model)
        if need_ctx:
            xc = xc + cg2 * f[bsz * n:].reshape(bsz, n_c, d_model)
    return rms_norm(x, final_g)
```

```python
import functools
import math

import jax
import jax.numpy as jnp
from jax import lax
from jax.experimental import pallas as pl
from jax.experimental.pallas import tpu as pltpu

F32 = jnp.float32
BF16 = jnp.bfloat16

EPS = 1e-6
N_MOD = 6
BRANCH_W = 512
GRID_W = 64
ROPE_BASE = 10000.0
DA_HEADS = 4
DA_HD = 64
DA_VD = 2 * DA_HD
ML_HEADS = 4
ML_HD = BRANCH_W // ML_HEADS
ML_CHUNK = 128
S5_P = 16
S5_G = BRANCH_W // S5_P
S5_N = 64
S5_T = 16
PEER_HEADS = 8
PEER_NKEYS = 128
PEER_TOPK = 16
LANES = 128
VMEM_LIMIT = 56 * 1024 * 1024

Z_GATE = 0
Z_Q, Z_K, Z_V, Z_XM, Z_VM, Z_OM, Z_U = (3072 + i * BRANCH_W for i in range(7))


def _cparams(sem):
    return pltpu.CompilerParams(dimension_semantics=sem, vmem_limit_bytes=VMEM_LIMIT)


def _nt_dot(a, b):
    return lax.dot_general(a, b, (((1,), (1,)), ((), ())), preferred_element_type=F32)


def _gelu(x):
    return 0.5 * x * (1.0 + lax.erf(x * (2.0 ** -0.5)))


def _log_sigmoid(x):
    return jnp.minimum(x, 0.0) - jnp.log1p(jnp.exp(-jnp.abs(x)))


def _mod_kernel(c_ref, w_ref, b_ref, o_ref):
    c = c_ref[...]
    a = c * jax.nn.sigmoid(c)
    o_ref[...] = jnp.dot(a.astype(BF16), w_ref[...], preferred_element_type=F32) + b_ref[...]


def _modulation(c_all, w, b):
    rows, d = c_all.shape
    n = w.shape[1]
    tn = 1536
    return pl.pallas_call(
        _mod_kernel,
        out_shape=jax.ShapeDtypeStruct((rows, n), F32),
        grid=(n // tn,),
        in_specs=[pl.BlockSpec((rows, d), lambda j: (0, 0)),
                  pl.BlockSpec((d, tn), lambda j: (0, j)),
                  pl.BlockSpec((1, tn), lambda j: (0, j))],
        out_specs=pl.BlockSpec((rows, tn), lambda j: (0, j)),
        compiler_params=_cparams(("parallel",)),
        name="modulation",
    )(c_all, w, b)


def _modnorm_matmul_kernel(x_ref, g_ref, sc_ref, sh_ref, w_ref, o_ref, h_ref):
    @pl.when(pl.program_id(1) == 0)
    def _():
        x = x_ref[...]
        y = x * lax.rsqrt(jnp.mean(x * x, axis=-1, keepdims=True) + EPS) * g_ref[...]
        h_ref[...] = (y * (1.0 + sc_ref[...]) + sh_ref[...]).astype(h_ref.dtype)

    o_ref[...] = jnp.dot(h_ref[...], w_ref[...], preferred_element_type=F32).astype(o_ref.dtype)


def _row_batch(i, n_lat_blocks, blocks_per_batch, bsz):
    return jnp.where(i < n_lat_blocks, i // blocks_per_batch, bsz)


def _modnorm_matmul(x_all, n_rows, g, sc, sh, w, *, bsz, seq, tm, tn, out_dtype=F32):
    d = x_all.shape[1]
    n = w.shape[1]
    nlb = bsz * seq // tm
    bpb = seq // tm
    mod_spec = pl.BlockSpec((None, 1, d), lambda i, j: (_row_batch(i, nlb, bpb, bsz), 0, 0))
    return pl.pallas_call(
        _modnorm_matmul_kernel,
        out_shape=jax.ShapeDtypeStruct((n_rows, n), out_dtype),
        grid=(n_rows // tm, n // tn),
        in_specs=[pl.BlockSpec((tm, d), lambda i, j: (i, 0)),
                  pl.BlockSpec((1, d), lambda i, j: (0, 0)),
                  mod_spec, mod_spec,
                  pl.BlockSpec((d, tn), lambda i, j: (0, j))],
        out_specs=pl.BlockSpec((tm, tn), lambda i, j: (i, j)),
        scratch_shapes=[pltpu.VMEM((tm, d), BF16)],
        compiler_params=_cparams(("parallel", "arbitrary")),
        name="modnorm_matmul",
    )(x_all, g, sc, sh, w)


def _rope_kernel(q_ref, k_ref, v_ref, cos_ref, sin_ref, qo_ref, ko_ref, vo_ref):
    cos = cos_ref[...]
    sin = sin_ref[...]
    lane = lax.broadcasted_iota(jnp.int32, cos.shape, 1)
    first_half = (lane % DA_HD) < (DA_HD // 2)

    def rope(x):
        partner = jnp.where(first_half, pltpu.roll(x, LANES - DA_HD // 2, 1),
                            pltpu.roll(x, DA_HD // 2, 1))
        return x * cos + partner * sin

    for h in range(DA_HEADS):
        sl = slice(h * LANES, (h + 1) * LANES)
        qo_ref[:, sl] = (rope(q_ref[:, sl]) * (DA_HD ** -0.5)).astype(BF16)
        ko_ref[:, sl] = rope(k_ref[:, sl]).astype(BF16)
    vo_ref[...] = v_ref[...].astype(BF16)


def _rope_qkv(z, cos_tab, sin_tab, *, bsz, seq, tm):
    rows = z.shape[0]
    nlb = bsz * seq // tm
    bpb = seq // tm
    w = BRANCH_W
    tab_spec = pl.BlockSpec((tm, LANES), lambda i: (jnp.where(i < nlb, i % bpb, bpb), 0))
    out = jax.ShapeDtypeStruct((rows, w), BF16)
    return pl.pallas_call(
        _rope_kernel,
        out_shape=(out, out, out),
        grid=(rows // tm,),
        in_specs=[pl.BlockSpec((tm, w), lambda i: (i, Z_Q // w)),
                  pl.BlockSpec((tm, w), lambda i: (i, Z_K // w)),
                  pl.BlockSpec((tm, w), lambda i: (i, Z_V // w)),
                  tab_spec, tab_spec],
        out_specs=(pl.BlockSpec((tm, w), lambda i: (i, 0)),) * 3,
        compiler_params=_cparams(("parallel",)),
        name="rope_qkv",
    )(z, z, z, cos_tab, sin_tab)


def _attn_kernel(*refs, n_seg, out_scale):
    lam_ref, q_ref = refs[0], refs[1]
    k_refs = refs[2:2 + n_seg]
    v_refs = refs[2 + n_seg:2 + 2 * n_seg]
    g_ref, o_ref = refs[2 + 2 * n_seg], refs[3 + 2 * n_seg]
    tq = q_ref.shape[0]
    q = q_ref[...].astype(F32)
    lane = lax.broadcasted_iota(jnp.int32, q.shape, 1)
    qq = jnp.concatenate([jnp.where(lane < DA_HD, q, 0.0), jnp.where(lane >= DA_HD, q, 0.0)],
                         axis=0).astype(BF16)
    s = [_nt_dot(qq, k_ref[...]) for k_ref in k_refs]
    m = functools.reduce(jnp.maximum, [jnp.max(t, axis=-1, keepdims=True) for t in s])
    p = [jnp.exp(t - m) for t in s]
    denom = functools.reduce(jnp.add, [jnp.sum(t, axis=-1, keepdims=True) for t in p])
    inv = 1.0 / denom
    c0 = inv[:tq]
    c1 = inv[tq:] * lam_ref[...]
    o = None
    for t, v_ref in zip(p, v_refs):
        w = (t[:tq] * c0 - t[tq:] * c1).astype(BF16)
        part = jnp.dot(w, v_ref[...], preferred_element_type=F32)
        o = part if o is None else o + part
    y = o * lax.rsqrt(jnp.mean(o * o, axis=-1, keepdims=True) + EPS) * g_ref[...]
    o_ref[...] = y * out_scale


def _diff_attention(qr, kr, vb, lam, sub_g, *, q_row0, n_q, kv_segs, bsz, tq, out_scale):
    nqb = n_q // tq
    n_seg = len(kv_segs)
    q_blk0 = q_row0 // tq

    def kv_spec(row0, length):
        return pl.BlockSpec((length, LANES), lambda b, h, i: (row0 // length + b, h))

    in_specs = [pl.BlockSpec((1, 1), lambda b, h, i: (0, 0)),
                pl.BlockSpec((tq, LANES), lambda b, h, i: (q_blk0 + b * nqb + i, h))]
    in_specs += [kv_spec(r0, ln) for r0, ln in kv_segs] * 2
    in_specs += [pl.BlockSpec((1, LANES), lambda b, h, i: (0, h))]
    args = [lam, qr] + [kr] * n_seg + [vb] * n_seg + [sub_g]
    return pl.pallas_call(
        functools.partial(_attn_kernel, n_seg=n_seg, out_scale=out_scale),
        out_shape=jax.ShapeDtypeStruct((bsz * n_q, BRANCH_W), F32),
        grid=(bsz, DA_HEADS, nqb),
        in_specs=in_specs,
        out_specs=pl.BlockSpec((tq, LANES), lambda b, h, i: (b * nqb + i, h)),
        compiler_params=_cparams(("parallel", "parallel", "parallel")),
        name="diff_attention",
    )(*args)


def _ml_prep_kernel(x_ref, cw_ref, cb_ref, wq_ref, wk_ref, q_ref, k_ref):
    x = x_ref[...]
    n = x.shape[0]
    row = lax.broadcasted_iota(jnp.int32, x.shape, 0)
    x_prev = jnp.where(row == 0, 0.0, pltpu.roll(x, 1, 0))
    x_next = jnp.where(row == n - 1, 0.0, pltpu.roll(x, n - 1, 0))
    cw = cw_ref[...]
    y = x_prev * cw[0:1] + x * cw[1:2] + x_next * cw[2:3] + cb_ref[...]
    xc = (y * jax.nn.sigmoid(y)).astype(BF16)
    q_ref[...] = jnp.dot(xc, wq_ref[...], preferred_element_type=F32) * (ML_HD ** -0.5)
    k_ref[...] = jnp.dot(xc, wk_ref[...], preferred_element_type=F32)


def _ml_prep(z, conv_w, conv_b, wq, wk, *, row0, seq_len, bsz):
    blk0 = row0 // seq_len
    out = jax.ShapeDtypeStruct((bsz * seq_len, BRANCH_W), F32)
    head_w = pl.BlockSpec((None, ML_HD, ML_HD), lambda b, h: (h, 0, 0))
    return pl.pallas_call(
        _ml_prep_kernel,
        out_shape=(out, out),
        grid=(bsz, ML_HEADS),
        in_specs=[pl.BlockSpec((seq_len, LANES), lambda b, h: (blk0 + b, Z_XM // LANES + h)),
                  pl.BlockSpec((3, LANES), lambda b, h: (0, h)),
                  pl.BlockSpec((1, LANES), lambda b, h: (0, h)),
                  head_w, head_w],
        out_specs=(pl.BlockSpec((seq_len, LANES), lambda b, h: (b, h)),) * 2,
        compiler_params=_cparams(("parallel", "parallel")),
        name="mlstm_prep",
    )(z, conv_w, conv_b, wq, wk)


def _mlstm_kernel(q_ref, k_ref, v_ref, gc_ref, gr_ref, h_ref, c_ref, n_ref, m_ref):
    direction = pl.program_id(1)

    @pl.when(pl.program_id(2) == 0)
    def _():
        c_ref[...] = jnp.zeros_like(c_ref)
        n_ref[...] = jnp.zeros_like(n_ref)
        m_ref[...] = jnp.zeros_like(m_ref)

    t_idx = lax.broadcasted_iota(jnp.int32, (ML_CHUNK, ML_CHUNK), 0)
    s_idx = lax.broadcasted_iota(jnp.int32, (ML_CHUNK, ML_CHUNK), 1)
    mask = (s_idx - t_idx) * jnp.where(direction == 0, 1, -1) <= 0
    mask_f = mask.astype(F32)
    nh = ML_HEADS
    gc = gc_ref[...]
    gr = gr_ref[...]
    lf_c = _log_sigmoid(gc)
    lf_r = _log_sigmoid(gr)
    b_c = jnp.dot(mask_f, lf_c, preferred_element_type=F32, precision=lax.Precision.HIGHEST)
    b_r = lax.dot_general(lf_r, mask_f, (((1,), (1,)), ((), ())), preferred_element_type=F32,
                          precision=lax.Precision.HIGHEST)
    b_end_all = jnp.sum(lf_c, axis=0, keepdims=True)

    for h in range(nh):
        sl = slice(h * ML_HD, (h + 1) * ML_HD)
        q = q_ref[:, sl].astype(BF16)
        k = k_ref[:, sl].astype(BF16)
        v = v_ref[:, sl]
        li_c = gc[:, h:h + 1]
        li_r = gr[h:h + 1, :]
        bc = b_c[:, nh + h:nh + h + 1]
        br = b_r[nh + h:nh + h + 1, :]
        b_end = b_end_all[:, nh + h:nh + h + 1]
        m_prev = m_ref[h]
        c_prev = c_ref[h]
        n_prev = n_ref[h]

        dmat = jnp.where(mask, bc - br + li_r, -jnp.inf)
        inter = bc + m_prev
        m_t = jnp.maximum(inter, jnp.max(dmat, axis=-1, keepdims=True))
        s = _nt_dot(q, k) * jnp.exp(dmat - m_t)
        w_prev = jnp.exp(inter - m_t)
        qf = q.astype(F32)
        num = (jnp.dot(s.astype(BF16), v.astype(BF16), preferred_element_type=F32)
               + w_prev * _nt_dot(q, c_prev.astype(BF16)))
        den = (jnp.sum(s, axis=-1, keepdims=True)
               + w_prev * jnp.sum(qf * n_prev, axis=-1, keepdims=True))
        h_ref[:, sl] = num / jnp.maximum(jnp.abs(den), jnp.exp(-m_t))

        g_c = b_end - bc + li_c
        g_r = b_end - br + li_r
        m_new = jnp.maximum(b_end + m_prev, jnp.max(g_r, axis=-1, keepdims=True))
        wg = jnp.exp(g_c - m_new)
        decay = jnp.exp(b_end + m_prev - m_new)
        vw = (v * wg).astype(BF16)
        c_ref[h] = decay * c_prev + lax.dot_general(
            vw, k, (((0,), (0,)), ((), ())), preferred_element_type=F32)
        n_ref[h] = decay * n_prev + jnp.sum(wg * k.astype(F32), axis=0, keepdims=True)
        m_ref[h] = m_new


def _mlstm_scan(q_all, k_all, z, gates_c, gates_r, *, bsz, seq, ctx_len):
    rows = q_all.shape[0]
    ncl = seq // ML_CHUNK
    ncc = ctx_len // ML_CHUNK
    lat_blocks = bsz * ncl

    def blk(b, d, s):
        jc = jnp.where(d == 0, s, ncc - 1 - s)
        jl = jnp.where(d == 0, s - ncc, ncl - 1 - (s - ncc))
        return jnp.where(s < ncc, lat_blocks + b * ncc + jc, b * ncl + jl)

    row_spec = pl.BlockSpec((ML_CHUNK, BRANCH_W), lambda b, d, s: (blk(b, d, s), 0))
    nh = ML_HEADS
    return pl.pallas_call(
        _mlstm_kernel,
        out_shape=jax.ShapeDtypeStruct((2, rows, BRANCH_W), F32),
        grid=(bsz, 2, ncc + ncl),
        in_specs=[row_spec, row_spec,
                  pl.BlockSpec((ML_CHUNK, BRANCH_W), lambda b, d, s: (blk(b, d, s), Z_VM // BRANCH_W)),
                  pl.BlockSpec((None, ML_CHUNK, 2 * nh), lambda b, d, s: (d, blk(b, d, s), 0)),
                  pl.BlockSpec((None, 2 * nh, ML_CHUNK), lambda b, d, s: (d, 0, blk(b, d, s)))],
        out_specs=pl.BlockSpec((None, ML_CHUNK, BRANCH_W), lambda b, d, s: (d, blk(b, d, s), 0)),
        scratch_shapes=[pltpu.VMEM((nh, ML_HD, ML_HD), F32),
                        pltpu.VMEM((nh, 1, ML_HD), F32),
                        pltpu.VMEM((nh, 1, 1), F32)],
        compiler_params=_cparams(("parallel", "parallel", "arbitrary")),
        name="mlstm_scan",
    )(q_all, k_all, z, gates_c, gates_r)


def _s5_kernel(uc_ref, ul_ref, m_ref, w_ref, v_ref, a1_ref, a2_ref, yc_ref, yl_ref,
               ec_ref, el_ref, xc_ref, xl_ref, *, bsz):
    w = w_ref[...]
    ec_ref[...] = jnp.dot(uc_ref[...], w, preferred_element_type=F32)
    el_ref[...] = jnp.dot(ul_ref[...], w, preferred_element_type=F32)
    a1 = a1_ref[...]
    a2 = a2_ref[...]
    half = LANES

    def advance(x, e, a1d, a2d):
        return a1d * x + a2d * pltpu.roll(x, S5_N, 1) + e

    def scan(e_ref, x_ref, state):
        n_chunks = e_ref.shape[0] // bsz

        def body(j, st):
            xf, xb = st
            rf = pl.multiple_of(j * bsz, bsz)
            rb = pl.multiple_of((n_chunks - 1 - j) * bsz, bsz)
            x_ref[pl.ds(rf, bsz), 0:half] = xf
            x_ref[pl.ds(rb, bsz), half:2 * half] = xb
            xf = advance(xf, e_ref[pl.ds(rf, bsz), 0:half], a1[:, 0:half], a2[:, 0:half])
            xb = advance(xb, e_ref[pl.ds(rb, bsz), half:2 * half], a1[:, half:], a2[:, half:])
            return xf, xb

        return lax.fori_loop(0, n_chunks, body, state)

    zero = jnp.zeros((bsz, half), F32)
    state = scan(ec_ref, xc_ref, (zero, zero))
    scan(el_ref, xl_ref, state)
    mm = m_ref[...]
    vv = v_ref[...]
    yc_ref[...] = (jnp.dot(uc_ref[...], mm, preferred_element_type=F32)
                   + jnp.dot(xc_ref[...].astype(BF16), vv, preferred_element_type=F32))
    yl_ref[...] = (jnp.dot(ul_ref[...], mm, preferred_element_type=F32)
                   + jnp.dot(xl_ref[...].astype(BF16), vv, preferred_element_type=F32))


def _s5_tables(lam_re, lam_im, log_dt, b_re, b_im, c_re, c_im, d_skip):
    t = S5_T
    dt = jnp.exp(log_dt)[..., None]
    den = lam_re * lam_re + lam_im * lam_im
    kk = jnp.arange(t + 1, dtype=F32)[:, None, None, None]
    mag = jnp.exp(kk * lam_re * dt)
    ang = kk * lam_im * dt
    p_re, p_im = mag * jnp.cos(ang), mag * jnp.sin(ang)
    ab_re, ab_im = p_re[1], p_im[1]
    z_re = ((ab_re - 1.0) * lam_re + ab_im * lam_im) / den
    z_im = (ab_im * lam_re - (ab_re - 1.0) * lam_im) / den
    bb_re = z_re[..., None] * b_re - z_im[..., None] * b_im
    bb_im = z_re[..., None] * b_im + z_im[..., None] * b_re
    ab_k_re = p_re[..., None] * bb_re - p_im[..., None] * bb_im
    ab_k_im = p_re[..., None] * bb_im + p_im[..., None] * bb_re
    taps = (jnp.einsum('rgpn,krgnq->krgpq', c_re, ab_k_re[:t])
            - jnp.einsum('rgpn,krgnq->krgpq', c_im, ab_k_im[:t]))
    s_i = jnp.arange(t)[:, None]
    t_i = jnp.arange(t)[None, :]
    lag_f = jnp.clip(t_i - s_i, 0, t - 1)
    lag_b = jnp.clip(s_i - t_i, 0, t - 1)
    tf = jnp.where((t_i >= s_i)[..., None, None, None], taps[lag_f, 0], 0.0)
    tb = jnp.where((s_i >= t_i)[..., None, None, None], taps[lag_b, 1], 0.0)
    skip = (jnp.eye(t)[:, :, None, None, None]
            * (jnp.eye(S5_P)[None, None, None] * d_skip.reshape(S5_G, S5_P, 1)[None, None]))
    m = jnp.transpose(tf + tb + skip, (2, 0, 4, 1, 3)).reshape(S5_G, t * S5_P, t * S5_P)
    wf_re = jnp.transpose(ab_k_re[:t, 0][::-1], (1, 0, 3, 2)).reshape(S5_G, t * S5_P, S5_N)
    wf_im = jnp.transpose(ab_k_im[:t, 0][::-1], (1, 0, 3, 2)).reshape(S5_G, t * S5_P, S5_N)
    wb_re = jnp.transpose(ab_k_re[:t, 1], (1, 0, 3, 2)).reshape(S5_G, t * S5_P, S5_N)
    wb_im = jnp.transpose(ab_k_im[:t, 1], (1, 0, 3, 2)).reshape(S5_G, t * S5_P, S5_N)
    w = jnp.concatenate([wf_re, wf_im, wb_re, wb_im], axis=-1)
    ca_re = (c_re[None] * p_re[:, :, :, None, :] - c_im[None] * p_im[:, :, :, None, :])
    ca_im = (c_re[None] * p_im[:, :, :, None, :] + c_im[None] * p_re[:, :, :, None, :])

    def to_rows(a):
        return jnp.transpose(a, (1, 3, 0, 2)).reshape(S5_G, S5_N, t * S5_P)

    v = jnp.concatenate([to_rows(ca_re[1:t + 1, 0]), to_rows(-ca_im[1:t + 1, 0]),
                         to_rows(ca_re[1:t + 1, 1][::-1]), to_rows(-ca_im[1:t + 1, 1][::-1])], axis=1)
    at_re, at_im = p_re[t], p_im[t]
    a1 = jnp.concatenate([at_re[0], at_re[0], at_re[1], at_re[1]], axis=-1)[:, None, :]
    a2 = jnp.concatenate([-at_im[0], at_im[0], -at_im[1], at_im[1]], axis=-1)[:, None, :]
    return m.astype(BF16), w.astype(BF16), v.astype(BF16), a1, a2


def _s5_rows(u, bsz, length):
    nc = length // S5_T
    u = u.reshape(bsz, nc, S5_T, S5_G, S5_P)
    return jnp.transpose(u, (3, 1, 0, 2, 4)).reshape(S5_G, nc * bsz, S5_T * S5_P)


def _s5_unrows(y, bsz, length):
    nc = length // S5_T
    y = y.reshape(S5_G, nc, bsz, S5_T, S5_P)
    return jnp.transpose(y, (2, 1, 3, 0, 4)).reshape(bsz * length, BRANCH_W)


def _s5_mix(u_ctx, u_lat, tables, *, bsz):
    m, w, v, a1, a2 = tables
    rc, rl = u_ctx.shape[1], u_lat.shape[1]
    wd = S5_T * S5_P

    def rows_spec(r):
        return pl.BlockSpec((None, r, wd), lambda g: (g, 0, 0))

    sq = pl.BlockSpec((None, wd, wd), lambda g: (g, 0, 0))
    vec = pl.BlockSpec((None, 1, wd), lambda g: (g, 0, 0))
    return pl.pallas_call(
        functools.partial(_s5_kernel, bsz=bsz),
        out_shape=(jax.ShapeDtypeStruct((S5_G, rc, wd), F32),
                   jax.ShapeDtypeStruct((S5_G, rl, wd), F32)),
        grid=(S5_G,),
        in_specs=[rows_spec(rc), rows_spec(rl), sq, sq, sq, vec, vec],
        out_specs=(rows_spec(rc), rows_spec(rl)),
        scratch_shapes=[pltpu.VMEM((rc, wd), F32), pltpu.VMEM((rl, wd), F32),
                        pltpu.VMEM((rc, wd), F32), pltpu.VMEM((rl, wd), F32)],
        compiler_params=_cparams(("parallel",)),
        name="s5_mix",
    )(u_ctx, u_lat, m, w, v, a1, a2)


def _merge_kernel(x_ref, ya_ref, hf_ref, hb_ref, om_ref, ys_ref, gate_ref, g1_ref, mlg_ref,
                  gluw_ref, glub_ref, wbr_ref, wout_ref, o_ref):
    hsum = hf_ref[...] + hb_ref[...]
    og = jax.nn.sigmoid(om_ref[...])
    mlg = mlg_ref[...]
    yb_parts = []
    for h in range(ML_HEADS):
        sl = slice(h * ML_HD, (h + 1) * ML_HD)
        hh = hsum[:, sl]
        hn = hh * lax.rsqrt(jnp.mean(hh * hh, axis=-1, keepdims=True) + EPS) * mlg[:, sl]
        yb_parts.append(hn * og[:, sl])
    yb = jnp.concatenate(yb_parts, axis=-1)
    gl = _gelu(ys_ref[...])
    ys = gl * jax.nn.sigmoid(
        jnp.dot(gl.astype(BF16), gluw_ref[...], preferred_element_type=F32) + glub_ref[...])
    d = o_ref.shape[1]
    merged = None
    for r, y in enumerate((ya_ref[...], yb, ys)):
        proj = jnp.dot(y.astype(BF16), wbr_ref[r], preferred_element_type=F32)
        term = jax.nn.sigmoid(gate_ref[:, r * d:(r + 1) * d]) * proj
        merged = term if merged is None else merged + term
    y = jnp.dot(merged.astype(BF16), wout_ref[...], preferred_element_type=F32)
    o_ref[...] = x_ref[...] + g1_ref[...] * y


def _merge(x_all, ya, h_dirs, z, ys, g1, ml_norm_g, glu_w, glu_b, w_br, w_out, *, n_rows, bsz, seq, tm):
    d = x_all.shape[1]
    nlb = bsz * seq // tm
    bpb = seq // tm
    w = BRANCH_W

    def rows(width, col_blk=0):
        return pl.BlockSpec((tm, width), lambda i: (i, col_blk))

    def full(shape):
        return pl.BlockSpec(shape, lambda i: (0,) * len(shape))

    return pl.pallas_call(
        _merge_kernel,
        out_shape=jax.ShapeDtypeStruct((n_rows, d), F32),
        grid=(n_rows // tm,),
        in_specs=[rows(d), rows(w),
                  pl.BlockSpec((None, tm, w), lambda i: (0, i, 0)),
                  pl.BlockSpec((None, tm, w), lambda i: (1, i, 0)),
                  rows(w, Z_OM // w), rows(w),
                  rows(3 * d, Z_GATE // (3 * d)),
                  pl.BlockSpec((None, 1, d), lambda i: (_row_batch(i, nlb, bpb, bsz), 0, 0)),
                  full((1, w)), full((w, w)), full((1, w)), full((3, w, d)), full((d, d))],
        out_specs=rows(d),
        compiler_params=_cparams(("parallel",)),
        name="merge",
    )(x_all, ya, h_dirs, h_dirs, z, ys, z, g1, ml_norm_g, glu_w, glu_b, w_br, w_out)


def _take_top(s, n_take, vals_ref):
    rows = s.shape[0]
    ridx = lax.broadcasted_iota(jnp.int32, s.shape, 0)
    for i in range(n_take):
        mx = jnp.max(s, axis=0, keepdims=True)
        first = jnp.min(jnp.where(s == mx, ridx, rows), axis=0, keepdims=True)
        s = jnp.where(ridx == first, -jnp.inf, s)
        vals_ref[i:i + 1, :] = mx
    return s


def _peer_topk_kernel(q_ref, k1_ref, k2_ref, e1_ref, s1_ref, e2_ref, s2_ref, tau_ref,
                      v1_ref, v2_ref, top_ref):
    k = PEER_TOPK
    q = q_ref[...].astype(BF16)
    s1 = _nt_dot(k1_ref[...], q[:, :LANES])
    s2 = _nt_dot(k2_ref[...], q[:, LANES:])
    sel1 = _take_top(s1, k, v1_ref) == -jnp.inf
    sel2 = _take_top(s2, k, v2_ref) == -jnp.inf
    v2_head = v2_ref[0:8, :]
    jrow = lax.broadcasted_iota(jnp.int32, v2_head.shape, 0)
    pieces = [v1_ref[0:1, :] + v2_ref[...]]
    for i in range(1, 8):
        pieces.append(jnp.where(jrow < k // (i + 1), v1_ref[i:i + 1, :] + v2_head, -jnp.inf))
    pieces.append(v1_ref[8:16, :] + v2_ref[0:1, :])
    _take_top(jnp.concatenate(pieces, axis=0), k, top_ref)
    top = top_ref[...]
    tau = top[k - 1:k, :]
    zsum = jnp.sum(jnp.exp(top - top[0:1, :]), axis=0, keepdims=True)
    e1_ref[...] = jnp.where(sel1, jnp.exp(s1 - v1_ref[0:1, :]), 0.0) / zsum
    e2_ref[...] = jnp.where(sel2, jnp.exp(s2 - v2_ref[0:1, :]), 0.0)
    s1_ref[...] = s1
    s2_ref[...] = s2
    tau_ref[...] = tau


def _peer_topk(q, sub_k, *, n_rows, tm):
    nk = PEER_NKEYS
    tab = jax.ShapeDtypeStruct((PEER_HEADS, nk, n_rows), F32)
    tab_spec = pl.BlockSpec((None, nk, tm), lambda i, h: (h, 0, i))
    return pl.pallas_call(
        _peer_topk_kernel,
        out_shape=(tab, tab, tab, tab, jax.ShapeDtypeStruct((PEER_HEADS, 1, n_rows), F32)),
        grid=(n_rows // tm, PEER_HEADS),
        in_specs=[pl.BlockSpec((tm, 2 * LANES), lambda i, h: (i, h)),
                  pl.BlockSpec((None, nk, LANES), lambda i, h: (0, 0, 0)),
                  pl.BlockSpec((None, nk, LANES), lambda i, h: (1, 0, 0))],
        out_specs=(tab_spec, tab_spec, tab_spec, tab_spec,
                   pl.BlockSpec((None, 1, tm), lambda i, h: (h, 0, i))),
        scratch_shapes=[pltpu.VMEM((PEER_TOPK, tm), F32)] * 3,
        compiler_params=_cparams(("parallel", "parallel")),
        name="peer_topk",
    )(q, sub_k, sub_k)


def _peer_dense_kernel(h_ref, u_ref, vt_ref, e1_ref, s1_ref, e2_ref, s2_ref, tau_ref,
                       x_ref, g2_ref, o_ref, acc_ref, act_ref, p_ref, *, a_per_blk):
    j = pl.program_id(1)

    @pl.when(j == 0)
    def _():
        acc_ref[...] = jnp.zeros_like(acc_ref)

    act_ref[...] = _gelu(_nt_dot(u_ref[...], h_ref[...]))
    tm = h_ref.shape[0]
    nk = PEER_NKEYS
    a0 = pl.multiple_of(j * a_per_blk, a_per_blk)
    for lt in range(tm // LANES):
        ls = slice(lt * LANES, (lt + 1) * LANES)
        for i in range(a_per_blk):
            grp = (i // 8) * 8
            gate = None
            for h in range(PEER_HEADS):
                e1 = e1_ref[h, pl.ds(a0 + grp, 8), ls][i % 8:i % 8 + 1]
                s1 = s1_ref[h, pl.ds(a0 + grp, 8), ls][i % 8:i % 8 + 1]
                hit = (s1 + s2_ref[h, :, ls]) >= tau_ref[h, :, ls]
                term = jnp.where(hit, e1 * e2_ref[h, :, ls], 0.0)
                gate = term if gate is None else gate + term
            rs = slice(i * nk, (i + 1) * nk)
            p_ref[rs, ls] = (gate * act_ref[rs, ls]).astype(BF16)
    acc_ref[...] += jnp.dot(vt_ref[...], p_ref[...], preferred_element_type=F32)

    @pl.when(j == pl.num_programs(1) - 1)
    def _():
        o_ref[...] = x_ref[...] + g2_ref[...] * acc_ref[...].T


def _peer_dense(h2, u_tab, vt_tab, tabs, x_all, g2, *, n_rows, bsz, seq, tm, te):
    d = x_all.shape[1]
    n_exp = u_tab.shape[0]
    nlb = bsz * seq // tm
    bpb = seq // tm
    e1, s1, e2, s2, tau = tabs
    nk = PEER_NKEYS
    tab_spec = pl.BlockSpec((PEER_HEADS, nk, tm), lambda i, j: (0, 0, i))
    return pl.pallas_call(
        functools.partial(_peer_dense_kernel, a_per_blk=te // nk),
        out_shape=jax.ShapeDtypeStruct((n_rows, d), F32),
        grid=(n_rows // tm, n_exp // te),
        in_specs=[pl.BlockSpec((tm, d), lambda i, j: (i, 0)),
                  pl.BlockSpec((te, d), lambda i, j: (j, 0)),
                  pl.BlockSpec((d, te), lambda i, j: (0, j)),
                  tab_spec, tab_spec, tab_spec, tab_spec,
                  pl.BlockSpec((PEER_HEADS, 1, tm), lambda i, j: (0, 0, i)),
                  pl.BlockSpec((tm, d), lambda i, j: (i, 0)),
                  pl.BlockSpec((None, 1, d), lambda i, j: (_row_batch(i, nlb, bpb, bsz), 0, 0))],
        out_specs=pl.BlockSpec((tm, d), lambda i, j: (i, 0)),
        scratch_shapes=[pltpu.VMEM((d, tm), F32), pltpu.VMEM((te, tm), F32),
                        pltpu.VMEM((te, tm), BF16)],
        compiler_params=_cparams(("parallel", "arbitrary")),
        name="peer_dense",
    )(h2, u_tab, vt_tab, e1, s1, e2, s2, tau, x_all, g2)


def _modnorm_kernel(x_ref, g_ref, sc_ref, sh_ref, o_ref):
    x = x_ref[...]
    y = x * lax.rsqrt(jnp.mean(x * x, axis=-1, keepdims=True) + EPS) * g_ref[...]
    o_ref[...] = (y * (1.0 + sc_ref[...]) + sh_ref[...]).astype(o_ref.dtype)


def _modnorm(x_all, n_rows, g, sc, sh, *, bsz, seq, tm, out_dtype):
    d = x_all.shape[1]
    nlb = bsz * seq // tm
    bpb = seq // tm
    mod_spec = pl.BlockSpec((None, 1, d), lambda i: (_row_batch(i, nlb, bpb, bsz), 0, 0))
    return pl.pallas_call(
        _modnorm_kernel,
        out_shape=jax.ShapeDtypeStruct((n_rows, d), out_dtype),
        grid=(n_rows // tm,),
        in_specs=[pl.BlockSpec((tm, d), lambda i: (i, 0)),
                  pl.BlockSpec((1, d), lambda i: (0, 0)), mod_spec, mod_spec],
        out_specs=pl.BlockSpec((tm, d), lambda i: (i, 0)),
        compiler_params=_cparams(("parallel",)),
        name="modnorm",
    )(x_all, g, sc, sh)


def _rope_tables(seq, tm):
    rows = seq // GRID_W
    n_freq = DA_HD // 4
    inv = ROPE_BASE ** (-jnp.arange(n_freq, dtype=F32) / n_freq)
    r = jnp.repeat(jnp.arange(rows, dtype=F32), GRID_W)
    col = jnp.tile(jnp.arange(GRID_W, dtype=F32), rows)
    ang = jnp.concatenate([r[:, None] * inv, col[:, None] * inv], axis=-1)
    cos, sin = jnp.cos(ang), jnp.sin(ang)
    cos_t = jnp.tile(cos, (1, 4))
    sin_t = jnp.tile(jnp.concatenate([-sin, sin], axis=-1), (1, 2))
    ident = jnp.ones((tm, LANES), F32)
    return (jnp.concatenate([cos_t, ident], axis=0),
            jnp.concatenate([sin_t, jnp.zeros((tm, LANES), F32)], axis=0))


def kernel(x, c, ctx, c_ctx, w_mod, b_mod, norm1_g, norm2_g, w_in, da_lam_q1, da_lam_k1, da_lam_q2, da_lam_k2, da_sub_g, ml_conv_w, ml_conv_b, ml_wq, ml_wk, ml_gate_b, ml_norm_g, s5_lam_re, s5_lam_im, s5_log_dt, s5_b_re, s5_b_im, s5_c_re, s5_c_im, s5_d, s5_glu_w, s5_glu_b, w_br, w_out, peer_wq, peer_sub_k, peer_u, peer_v, final_g):
    bsz, seq, d = x.shape
    ctx_len = ctx.shape[1]
    depth = w_in.shape[0]
    t_lat = bsz * seq
    t_ctx = bsz * ctx_len
    t_all = t_lat + t_ctx
    tm = math.gcd(512, math.gcd(seq, t_ctx))
    tm_merge = min(tm, 256)
    tq = min(256, ctx_len)
    nh = ML_HEADS

    x_all = jnp.concatenate([x.reshape(t_lat, d), ctx.reshape(t_ctx, d)], axis=0)
    pad = (-(bsz + 1)) % 8
    c_all = jnp.concatenate([c, c_ctx[None], jnp.zeros((pad, d), F32)], axis=0)
    cos_tab, sin_tab = _rope_tables(seq, tm)

    for l in range(depth):
        need_ctx = l < depth - 1
        n_rows = t_all if need_ctx else t_lat
        lam_init = 0.8 - 0.6 * math.exp(-0.3 * l)

        mods = _modulation(c_all, w_mod[l].astype(BF16), b_mod[l][None])
        sh1, sc1, g1, sh2, sc2, g2 = [mods[:bsz + 1, i * d:(i + 1) * d][:, None, :]
                                      for i in range(N_MOD)]

        wi = w_in[l]
        i_q, i_k, i_v, i_xm, i_vm, i_om, i_g, i_u, i_gate = (
            0, 512, 1024, 1536, 2048, 2560, 3072, 3088, 3600)
        w_main = jnp.concatenate([wi[:, i_gate:], wi[:, :i_g], wi[:, i_u:i_gate]], axis=1).astype(BF16)
        w_gate = jnp.pad(wi[:, i_g:i_u], ((0, 0), (0, LANES - (i_u - i_g)))).astype(BF16)
        n1 = norm1_g[l][None]
        z = _modnorm_matmul(x_all, t_all, n1, sc1, sh1, w_main, bsz=bsz, seq=seq, tm=tm, tn=512)
        zg = _modnorm_matmul(x_all, t_all, n1, sc1, sh1, w_gate, bsz=bsz, seq=seq, tm=tm, tn=LANES)

        qr, kr, vb = _rope_qkv(z, cos_tab, sin_tab, bsz=bsz, seq=seq, tm=tm)
        lam = (jnp.exp(jnp.sum(da_lam_q1[l] * da_lam_k1[l]))
               - jnp.exp(jnp.sum(da_lam_q2[l] * da_lam_k2[l])) + lam_init).reshape(1, 1)
        sub_g = da_sub_g[l][None]
        ya = _diff_attention(qr, kr, vb, lam, sub_g, q_row0=0, n_q=seq,
                             kv_segs=[(0, seq), (t_lat, ctx_len)], bsz=bsz, tq=tq,
                             out_scale=1.0 - lam_init)
        if need_ctx:
            ya_c = _diff_attention(qr, kr, vb, lam, sub_g, q_row0=t_lat, n_q=ctx_len,
                                   kv_segs=[(t_lat, ctx_len)], bsz=bsz, tq=tq,
                                   out_scale=1.0 - lam_init)
            ya = jnp.concatenate([ya, ya_c], axis=0)

        wq_b, wk_b = ml_wq[l].astype(BF16), ml_wk[l].astype(BF16)
        cb = ml_conv_b[l][None]
        q_l, k_l = _ml_prep(z, ml_conv_w[l], cb, wq_b, wk_b, row0=0, seq_len=seq, bsz=bsz)
        q_c, k_c = _ml_prep(z, ml_conv_w[l], cb, wq_b, wk_b, row0=t_lat, seq_len=ctx_len, bsz=bsz)
        q_ml = jnp.concatenate([q_l, q_c], axis=0)
        k_ml = jnp.concatenate([k_l, k_c], axis=0)
        gates = (zg[:, :4 * nh] + ml_gate_b[l]).reshape(t_all, 2, 2 * nh)
        gates_c = jnp.transpose(gates, (1, 0, 2))
        gates_r = jnp.transpose(gates, (1, 2, 0))
        h_dirs = _mlstm_scan(q_ml, k_ml, z, gates_c, gates_r, bsz=bsz, seq=seq, ctx_len=ctx_len)

        tables = _s5_tables(s5_lam_re[l], s5_lam_im[l], s5_log_dt[l], s5_b_re[l], s5_b_im[l],
                            s5_c_re[l], s5_c_im[l], s5_d[l])
        u = z[:, Z_U:Z_U + BRANCH_W].astype(BF16)
        y_c, y_l = _s5_mix(_s5_rows(u[t_lat:], bsz, ctx_len), _s5_rows(u[:t_lat], bsz, seq),
                           tables, bsz=bsz)
        ys = _s5_unrows(y_l, bsz, seq)
        if need_ctx:
            ys = jnp.concatenate([ys, _s5_unrows(y_c, bsz, ctx_len)], axis=0)

        x_all = _merge(x_all, ya, h_dirs, z, ys, g1, ml_norm_g[l][None],
                       s5_glu_w[l].astype(BF16), s5_glu_b[l][None], w_br[l].astype(BF16),
                       w_out[l].astype(BF16), n_rows=n_rows, bsz=bsz, seq=seq, tm=tm_merge)

        n2 = norm2_g[l][None]
        pq = _modnorm_matmul(x_all, n_rows, n2, sc2, sh2, peer_wq[l].astype(BF16),
                             bsz=bsz, seq=seq, tm=tm, tn=512)
        tabs = _peer_topk(pq, peer_sub_k[l].astype(BF16), n_rows=n_rows, tm=tm)
        h2 = _modnorm(x_all, n_rows, n2, sc2, sh2, bsz=bsz, seq=seq, tm=tm, out_dtype=BF16)
        x_all = _peer_dense(h2, peer_u[l].astype(BF16), peer_v[l].astype(BF16).T, tabs, x_all, g2,
                            n_rows=n_rows, bsz=bsz, seq=seq, tm=tm, te=1024)

    ones = jnp.ones((bsz + 1, 1, d), F32)
    out = _modnorm(x_all, t_lat, final_g[None], ones * 0.0, ones * 0.0, bsz=bsz, seq=seq, tm=tm,
                   out_dtype=F32)
    return out.reshape(bsz, seq, d)
```

```python
import functools
import math

import jax
import jax.numpy as jnp
from jax import lax
from jax.experimental import pallas as pl
from jax.experimental.pallas import tpu as pltpu

F32 = jnp.float32
BF16 = jnp.bfloat16

EPS = 1e-6
N_MOD = 6
BRANCH_W = 512
GRID_W = 64
ROPE_BASE = 10000.0
DA_HEADS = 4
DA_HD = 64
DA_VD = 2 * DA_HD
ML_HEADS = 4
ML_HD = BRANCH_W // ML_HEADS
ML_CHUNK = 128
S5_P = 16
S5_G = BRANCH_W // S5_P
S5_N = 64
S5_T = 16
PEER_HEADS = 8
PEER_NKEYS = 128
PEER_TOPK = 16
LANES = 128
VMEM_LIMIT = 56 * 1024 * 1024

Z_GATE = 0
Z_Q, Z_K, Z_V, Z_XM, Z_VM, Z_OM, Z_U = (3072 + i * BRANCH_W for i in range(7))


def _cparams(sem):
    return pltpu.CompilerParams(dimension_semantics=sem, vmem_limit_bytes=VMEM_LIMIT)


def _nt_dot(a, b):
    return lax.dot_general(a, b, (((1,), (1,)), ((), ())), preferred_element_type=F32)


def _gelu(x):
    return 0.5 * x * (1.0 + lax.erf(x * (2.0 ** -0.5)))


def _log_sigmoid(x):
    return jnp.minimum(x, 0.0) - jnp.log1p(jnp.exp(-jnp.abs(x)))


def _mod_kernel(c_ref, w_ref, b_ref, o_ref):
    c = c_ref[...]
    a = c * jax.nn.sigmoid(c)
    o_ref[...] = jnp.dot(a.astype(BF16), w_ref[...], preferred_element_type=F32) + b_ref[...]


def _modulation(c_all, w, b):
    rows, d = c_all.shape
    n = w.shape[1]
    tn = 1536
    return pl.pallas_call(
        _mod_kernel,
        out_shape=jax.ShapeDtypeStruct((rows, n), F32),
        grid=(n // tn,),
        in_specs=[pl.BlockSpec((rows, d), lambda j: (0, 0)),
                  pl.BlockSpec((d, tn), lambda j: (0, j)),
                  pl.BlockSpec((1, tn), lambda j: (0, j))],
        out_specs=pl.BlockSpec((rows, tn), lambda j: (0, j)),
        compiler_params=_cparams(("parallel",)),
        name="modulation",
    )(c_all, w, b)


def _modnorm_matmul_kernel(x_ref, g_ref, sc_ref, sh_ref, w_ref, o_ref, *h_out):
    x = x_ref[...]
    y = x * lax.rsqrt(jnp.mean(x * x, axis=-1, keepdims=True) + EPS) * g_ref[...]
    h = (y * (1.0 + sc_ref[...]) + sh_ref[...]).astype(BF16)
    o_ref[...] = jnp.dot(h, w_ref[...], preferred_element_type=F32).astype(o_ref.dtype)
    if h_out:
        h_out[0][...] = h


def _row_batch(i, n_lat_blocks, blocks_per_batch, bsz):
    return jnp.where(i < n_lat_blocks, i // blocks_per_batch, bsz)


def _modnorm_matmul(x_all, n_rows, g, sc, sh, w, *, bsz, seq, tm, tn, out_dtype=F32, emit_h=False):
    d = x_all.shape[1]
    n = w.shape[1]
    nlb = bsz * seq // tm
    bpb = seq // tm
    assert not emit_h or tn == n
    mod_spec = pl.BlockSpec((None, 1, d), lambda j, i: (_row_batch(i, nlb, bpb, bsz), 0, 0))
    out_shape = [jax.ShapeDtypeStruct((n_rows, n), out_dtype)]
    out_specs = [pl.BlockSpec((tm, tn), lambda j, i: (i, j))]
    if emit_h:
        out_shape.append(jax.ShapeDtypeStruct((n_rows, d), BF16))
        out_specs.append(pl.BlockSpec((tm, d), lambda j, i: (i, 0)))
    out = pl.pallas_call(
        _modnorm_matmul_kernel,
        out_shape=tuple(out_shape),
        grid=(n // tn, n_rows // tm),
        in_specs=[pl.BlockSpec((tm, d), lambda j, i: (i, 0)),
                  pl.BlockSpec((1, d), lambda j, i: (0, 0)),
                  mod_spec, mod_spec,
                  pl.BlockSpec((d, tn), lambda j, i: (0, j))],
        out_specs=tuple(out_specs),
        compiler_params=_cparams(("parallel", "parallel")),
        name="modnorm_matmul",
    )(x_all, g, sc, sh, w)
    return out if emit_h else out[0]


def _rope_kernel(q_ref, k_ref, v_ref, cos_ref, sin_ref, qo_ref, ko_ref, vo_ref):
    cos = cos_ref[...]
    sin = sin_ref[...]
    lane = lax.broadcasted_iota(jnp.int32, cos.shape, 1)
    first_half = (lane % DA_HD) < (DA_HD // 2)

    def rope(x):
        partner = jnp.where(first_half, pltpu.roll(x, LANES - DA_HD // 2, 1),
                            pltpu.roll(x, DA_HD // 2, 1))
        return x * cos + partner * sin

    for h in range(DA_HEADS):
        sl = slice(h * LANES, (h + 1) * LANES)
        qo_ref[:, sl] = (rope(q_ref[:, sl].astype(F32)) * (DA_HD ** -0.5)).astype(BF16)
        ko_ref[:, sl] = rope(k_ref[:, sl].astype(F32)).astype(BF16)
    vo_ref[...] = v_ref[...].astype(BF16)


def _rope_qkv(z, cos_tab, sin_tab, *, bsz, seq, tm):
    rows = z.shape[0]
    nlb = bsz * seq // tm
    bpb = seq // tm
    w = BRANCH_W
    tab_spec = pl.BlockSpec((tm, LANES), lambda i: (jnp.where(i < nlb, i % bpb, bpb), 0))
    out = jax.ShapeDtypeStruct((rows, w), BF16)
    return pl.pallas_call(
        _rope_kernel,
        out_shape=(out, out, out),
        grid=(rows // tm,),
        in_specs=[pl.BlockSpec((tm, w), lambda i: (i, Z_Q // w)),
                  pl.BlockSpec((tm, w), lambda i: (i, Z_K // w)),
                  pl.BlockSpec((tm, w), lambda i: (i, Z_V // w)),
                  tab_spec, tab_spec],
        out_specs=(pl.BlockSpec((tm, w), lambda i: (i, 0)),) * 3,
        compiler_params=_cparams(("parallel",)),
        name="rope_qkv",
    )(z, z, z, cos_tab, sin_tab)


def _attn_kernel(*refs, n_seg, out_scale):
    lam_ref, q_ref = refs[0], refs[1]
    k_refs = refs[2:2 + n_seg]
    v_refs = refs[2 + n_seg:2 + 2 * n_seg]
    g_ref, o_ref = refs[2 + 2 * n_seg], refs[3 + 2 * n_seg]
    tq = q_ref.shape[0]
    q = q_ref[...].astype(F32)
    lane = lax.broadcasted_iota(jnp.int32, q.shape, 1)
    qq = jnp.concatenate([jnp.where(lane < DA_HD, q, 0.0), jnp.where(lane >= DA_HD, q, 0.0)],
                         axis=0).astype(BF16)
    s = [_nt_dot(qq, k_ref[...]) for k_ref in k_refs]
    m = functools.reduce(jnp.maximum, [jnp.max(t, axis=-1, keepdims=True) for t in s])
    p = [jnp.exp(t - m) for t in s]
    denom = functools.reduce(jnp.add, [jnp.sum(t, axis=-1, keepdims=True) for t in p])
    inv = 1.0 / denom
    c0 = inv[:tq]
    c1 = inv[tq:] * lam_ref[...]
    o = None
    for t, v_ref in zip(p, v_refs):
        w = (t[:tq] * c0 - t[tq:] * c1).astype(BF16)
        part = jnp.dot(w, v_ref[...], preferred_element_type=F32)
        o = part if o is None else o + part
    y = o * lax.rsqrt(jnp.mean(o * o, axis=-1, keepdims=True) + EPS) * g_ref[...]
    o_ref[...] = y * out_scale


def _diff_attention(qr, kr, vb, lam, sub_g, *, q_row0, n_q, kv_segs, bsz, tq, out_scale):
    nqb = n_q // tq
    n_seg = len(kv_segs)
    q_blk0 = q_row0 // tq

    def kv_spec(row0, length):
        return pl.BlockSpec((length, LANES), lambda b, h, i: (row0 // length + b, h))

    in_specs = [pl.BlockSpec((1, 1), lambda b, h, i: (0, 0)),
                pl.BlockSpec((tq, LANES), lambda b, h, i: (q_blk0 + b * nqb + i, h))]
    in_specs += [kv_spec(r0, ln) for r0, ln in kv_segs] * 2
    in_specs += [pl.BlockSpec((1, LANES), lambda b, h, i: (0, h))]
    args = [lam, qr] + [kr] * n_seg + [vb] * n_seg + [sub_g]
    return pl.pallas_call(
        functools.partial(_attn_kernel, n_seg=n_seg, out_scale=out_scale),
        out_shape=jax.ShapeDtypeStruct((bsz * n_q, BRANCH_W), F32),
        grid=(bsz, DA_HEADS, nqb),
        in_specs=in_specs,
        out_specs=pl.BlockSpec((tq, LANES), lambda b, h, i: (b * nqb + i, h)),
        compiler_params=_cparams(("parallel", "parallel", "parallel")),
        name="diff_attention",
    )(*args)


def _ml_prep_kernel(x_ref, cw_ref, cb_ref, wq_ref, wk_ref, q_ref, k_ref):
    x = x_ref[...].astype(F32)
    n = x.shape[0]
    row = lax.broadcasted_iota(jnp.int32, x.shape, 0)
    x_prev = jnp.where(row == 0, 0.0, pltpu.roll(x, 1, 0))
    x_next = jnp.where(row == n - 1, 0.0, pltpu.roll(x, n - 1, 0))
    cw = cw_ref[...]
    y = x_prev * cw[0:1] + x * cw[1:2] + x_next * cw[2:3] + cb_ref[...]
    xc = (y * jax.nn.sigmoid(y)).astype(BF16)
    q_ref[...] = jnp.dot(xc, wq_ref[...], preferred_element_type=F32) * (ML_HD ** -0.5)
    k_ref[...] = jnp.dot(xc, wk_ref[...], preferred_element_type=F32)


def _ml_prep(z, conv_w, conv_b, wq, wk, *, row0, seq_len, bsz):
    blk0 = row0 // seq_len
    out = jax.ShapeDtypeStruct((bsz * seq_len, BRANCH_W), F32)
    head_w = pl.BlockSpec((None, ML_HD, ML_HD), lambda b, h: (h, 0, 0))
    return pl.pallas_call(
        _ml_prep_kernel,
        out_shape=(out, out),
        grid=(bsz, ML_HEADS),
        in_specs=[pl.BlockSpec((seq_len, LANES), lambda b, h: (blk0 + b, Z_XM // LANES + h)),
                  pl.BlockSpec((3, LANES), lambda b, h: (0, h)),
                  pl.BlockSpec((1, LANES), lambda b, h: (0, h)),
                  head_w, head_w],
        out_specs=(pl.BlockSpec((seq_len, LANES), lambda b, h: (b, h)),) * 2,
        compiler_params=_cparams(("parallel", "parallel")),
        name="mlstm_prep",
    )(z, conv_w, conv_b, wq, wk)


def _mlstm_kernel(q_ref, k_ref, v_ref, gc_ref, gr_ref, h_ref, c_ref, n_ref, m_ref):
    direction = pl.program_id(1)

    @pl.when(pl.program_id(2) == 0)
    def _():
        c_ref[...] = jnp.zeros_like(c_ref)
        n_ref[...] = jnp.zeros_like(n_ref)
        m_ref[...] = jnp.zeros_like(m_ref)

    t_idx = lax.broadcasted_iota(jnp.int32, (ML_CHUNK, ML_CHUNK), 0)
    s_idx = lax.broadcasted_iota(jnp.int32, (ML_CHUNK, ML_CHUNK), 1)
    mask = (s_idx - t_idx) * jnp.where(direction == 0, 1, -1) <= 0
    mask_f = mask.astype(F32)
    nh = ML_HEADS
    gc = gc_ref[...]
    gr = gr_ref[...]
    lf_c = _log_sigmoid(gc)
    lf_r = _log_sigmoid(gr)
    b_c = jnp.dot(mask_f, lf_c, preferred_element_type=F32, precision=lax.Precision.HIGHEST)
    b_r = lax.dot_general(lf_r, mask_f, (((1,), (1,)), ((), ())), preferred_element_type=F32,
                          precision=lax.Precision.HIGHEST)
    b_end_all = jnp.sum(lf_c, axis=0, keepdims=True)

    for h in range(nh):
        sl = slice(h * ML_HD, (h + 1) * ML_HD)
        q = q_ref[:, sl].astype(BF16)
        k = k_ref[:, sl].astype(BF16)
        v = v_ref[:, sl].astype(F32)
        li_c = gc[:, h:h + 1]
        li_r = gr[h:h + 1, :]
        bc = b_c[:, nh + h:nh + h + 1]
        br = b_r[nh + h:nh + h + 1, :]
        b_end = b_end_all[:, nh + h:nh + h + 1]
        m_prev = m_ref[h]
        c_prev = c_ref[h]
        n_prev = n_ref[h]

        dmat = jnp.where(mask, bc - br + li_r, -jnp.inf)
        inter = bc + m_prev
        m_t = jnp.maximum(inter, jnp.max(dmat, axis=-1, keepdims=True))
        s = _nt_dot(q, k) * jnp.exp(dmat - m_t)
        w_prev = jnp.exp(inter - m_t)
        qf = q.astype(F32)
        num = (jnp.dot(s.astype(BF16), v.astype(BF16), preferred_element_type=F32)
               + w_prev * _nt_dot(q, c_prev.astype(BF16)))
        den = (jnp.sum(s, axis=-1, keepdims=True)
               + w_prev * jnp.sum(qf * n_prev, axis=-1, keepdims=True))
        h_ref[:, sl] = num / jnp.maximum(jnp.abs(den), jnp.exp(-m_t))

        g_c = b_end - bc + li_c
        g_r = b_end - br + li_r
        m_new = jnp.maximum(b_end + m_prev, jnp.max(g_r, axis=-1, keepdims=True))
        wg = jnp.exp(g_c - m_new)
        decay = jnp.exp(b_end + m_prev - m_new)
        vw = (v * wg).astype(BF16)
        c_ref[h] = decay * c_prev + lax.dot_general(
            vw, k, (((0,), (0,)), ((), ())), preferred_element_type=F32)
        n_ref[h] = decay * n_prev + jnp.sum(wg * k.astype(F32), axis=0, keepdims=True)
        m_ref[h] = m_new


def _mlstm_scan(q_all, k_all, z, gates_c, gates_r, *, bsz, seq, ctx_len):
    rows = q_all.shape[0]
    ncl = seq // ML_CHUNK
    ncc = ctx_len // ML_CHUNK
    lat_blocks = bsz * ncl

    def blk(b, d, s):
        jc = jnp.where(d == 0, s, ncc - 1 - s)
        jl = jnp.where(d == 0, s - ncc, ncl - 1 - (s - ncc))
        return jnp.where(s < ncc, lat_blocks + b * ncc + jc, b * ncl + jl)

    row_spec = pl.BlockSpec((ML_CHUNK, BRANCH_W), lambda b, d, s: (blk(b, d, s), 0))
    nh = ML_HEADS
    return pl.pallas_call(
        _mlstm_kernel,
        out_shape=jax.ShapeDtypeStruct((2, rows, BRANCH_W), F32),
        grid=(bsz, 2, ncc + ncl),
        in_specs=[row_spec, row_spec,
                  pl.BlockSpec((ML_CHUNK, BRANCH_W), lambda b, d, s: (blk(b, d, s), Z_VM // BRANCH_W)),
                  pl.BlockSpec((None, ML_CHUNK, 2 * nh), lambda b, d, s: (d, blk(b, d, s), 0)),
                  pl.BlockSpec((None, 2 * nh, ML_CHUNK), lambda b, d, s: (d, 0, blk(b, d, s)))],
        out_specs=pl.BlockSpec((None, ML_CHUNK, BRANCH_W), lambda b, d, s: (d, blk(b, d, s), 0)),
        scratch_shapes=[pltpu.VMEM((nh, ML_HD, ML_HD), F32),
                        pltpu.VMEM((nh, 1, ML_HD), F32),
                        pltpu.VMEM((nh, 1, 1), F32)],
        compiler_params=_cparams(("parallel", "parallel", "arbitrary")),
        name="mlstm_scan",
    )(q_all, k_all, z, gates_c, gates_r)


def _s5_kernel(uc_ref, ul_ref, m_ref, w_ref, v_ref, a1_ref, a2_ref, yc_ref, yl_ref,
               ec_ref, el_ref, xc_ref, xl_ref, *, bsz):
    w = w_ref[...]
    ec_ref[...] = jnp.dot(uc_ref[...], w, preferred_element_type=F32)
    el_ref[...] = jnp.dot(ul_ref[...], w, preferred_element_type=F32)
    a1 = a1_ref[...]
    a2 = a2_ref[...]
    half = LANES

    def advance(x, e, a1d, a2d):
        return a1d * x + a2d * pltpu.roll(x, S5_N, 1) + e

    def scan(e_ref, x_ref, state):
        n_chunks = e_ref.shape[0] // bsz

        def body(j, st):
            xf, xb = st
            rf = pl.multiple_of(j * bsz, bsz)
            rb = pl.multiple_of((n_chunks - 1 - j) * bsz, bsz)
            x_ref[pl.ds(rf, bsz), 0:half] = xf
            x_ref[pl.ds(rb, bsz), half:2 * half] = xb
            xf = advance(xf, e_ref[pl.ds(rf, bsz), 0:half], a1[:, 0:half], a2[:, 0:half])
            xb = advance(xb, e_ref[pl.ds(rb, bsz), half:2 * half], a1[:, half:], a2[:, half:])
            return xf, xb

        return lax.fori_loop(0, n_chunks, body, state)

    zero = jnp.zeros((bsz, half), F32)
    state = scan(ec_ref, xc_ref, (zero, zero))
    scan(el_ref, xl_ref, state)
    mm = m_ref[...]
    vv = v_ref[...]
    yc_ref[...] = (jnp.dot(uc_ref[...], mm, preferred_element_type=F32)
                   + jnp.dot(xc_ref[...].astype(BF16), vv, preferred_element_type=F32)
                   ).astype(yc_ref.dtype)
    yl_ref[...] = (jnp.dot(ul_ref[...], mm, preferred_element_type=F32)
                   + jnp.dot(xl_ref[...].astype(BF16), vv, preferred_element_type=F32)
                   ).astype(yl_ref.dtype)


def _s5_tables(lam_re, lam_im, log_dt, b_re, b_im, c_re, c_im, d_skip):
    t = S5_T
    dt = jnp.exp(log_dt)[..., None]
    den = lam_re * lam_re + lam_im * lam_im
    kk = jnp.arange(t + 1, dtype=F32)[:, None, None, None]
    mag = jnp.exp(kk * lam_re * dt)
    ang = kk * lam_im * dt
    p_re, p_im = mag * jnp.cos(ang), mag * jnp.sin(ang)
    ab_re, ab_im = p_re[1], p_im[1]
    z_re = ((ab_re - 1.0) * lam_re + ab_im * lam_im) / den
    z_im = (ab_im * lam_re - (ab_re - 1.0) * lam_im) / den
    bb_re = z_re[..., None] * b_re - z_im[..., None] * b_im
    bb_im = z_re[..., None] * b_im + z_im[..., None] * b_re
    ab_k_re = p_re[..., None] * bb_re - p_im[..., None] * bb_im
    ab_k_im = p_re[..., None] * bb_im + p_im[..., None] * bb_re
    taps = (jnp.einsum('rgpn,krgnq->krgpq', c_re, ab_k_re[:t])
            - jnp.einsum('rgpn,krgnq->krgpq', c_im, ab_k_im[:t]))
    s_i = jnp.arange(t)[:, None]
    t_i = jnp.arange(t)[None, :]
    lag_f = jnp.clip(t_i - s_i, 0, t - 1)
    lag_b = jnp.clip(s_i - t_i, 0, t - 1)
    tf = jnp.where((t_i >= s_i)[..., None, None, None], taps[lag_f, 0], 0.0)
    tb = jnp.where((s_i >= t_i)[..., None, None, None], taps[lag_b, 1], 0.0)
    skip = (jnp.eye(t)[:, :, None, None, None]
            * (jnp.eye(S5_P)[None, None, None] * d_skip.reshape(S5_G, S5_P, 1)[None, None]))
    m = jnp.transpose(tf + tb + skip, (2, 0, 4, 1, 3)).reshape(S5_G, t * S5_P, t * S5_P)
    wf_re = jnp.transpose(ab_k_re[:t, 0][::-1], (1, 0, 3, 2)).reshape(S5_G, t * S5_P, S5_N)
    wf_im = jnp.transpose(ab_k_im[:t, 0][::-1], (1, 0, 3, 2)).reshape(S5_G, t * S5_P, S5_N)
    wb_re = jnp.transpose(ab_k_re[:t, 1], (1, 0, 3, 2)).reshape(S5_G, t * S5_P, S5_N)
    wb_im = jnp.transpose(ab_k_im[:t, 1], (1, 0, 3, 2)).reshape(S5_G, t * S5_P, S5_N)
    w = jnp.concatenate([wf_re, wf_im, wb_re, wb_im], axis=-1)
    ca_re = (c_re[None] * p_re[:, :, :, None, :] - c_im[None] * p_im[:, :, :, None, :])
    ca_im = (c_re[None] * p_im[:, :, :, None, :] + c_im[None] * p_re[:, :, :, None, :])

    def to_rows(a):
        return jnp.transpose(a, (1, 3, 0, 2)).reshape(S5_G, S5_N, t * S5_P)

    v = jnp.concatenate([to_rows(ca_re[1:t + 1, 0]), to_rows(-ca_im[1:t + 1, 0]),
                         to_rows(ca_re[1:t + 1, 1][::-1]), to_rows(-ca_im[1:t + 1, 1][::-1])], axis=1)
    at_re, at_im = p_re[t], p_im[t]
    a1 = jnp.concatenate([at_re[0], at_re[0], at_re[1], at_re[1]], axis=-1)[:, None, :]
    a2 = jnp.concatenate([-at_im[0], at_im[0], -at_im[1], at_im[1]], axis=-1)[:, None, :]
    return m.astype(BF16), w.astype(BF16), v.astype(BF16), a1, a2


def _s5_rows(u, bsz, length):
    nc = length // S5_T
    u = u.reshape(bsz, nc, S5_T, S5_G, S5_P)
    return jnp.transpose(u, (3, 1, 0, 2, 4)).reshape(S5_G, nc * bsz, S5_T * S5_P)


def _s5_unrows(y, bsz, length):
    nc = length // S5_T
    y = y.reshape(S5_G, nc, bsz, S5_T, S5_P)
    return jnp.transpose(y, (2, 1, 3, 0, 4)).reshape(bsz * length, BRANCH_W)


def _s5_mix(u_ctx, u_lat, tables, *, bsz):
    m, w, v, a1, a2 = tables
    rc, rl = u_ctx.shape[1], u_lat.shape[1]
    wd = S5_T * S5_P

    def rows_spec(r):
        return pl.BlockSpec((None, r, wd), lambda g: (g, 0, 0))

    sq = pl.BlockSpec((None, wd, wd), lambda g: (g, 0, 0))
    vec = pl.BlockSpec((None, 1, wd), lambda g: (g, 0, 0))
    return pl.pallas_call(
        functools.partial(_s5_kernel, bsz=bsz),
        out_shape=(jax.ShapeDtypeStruct((S5_G, rc, wd), BF16),
                   jax.ShapeDtypeStruct((S5_G, rl, wd), BF16)),
        grid=(S5_G,),
        in_specs=[rows_spec(rc), rows_spec(rl), sq, sq, sq, vec, vec],
        out_specs=(rows_spec(rc), rows_spec(rl)),
        scratch_shapes=[pltpu.VMEM((rc, wd), F32), pltpu.VMEM((rl, wd), F32),
                        pltpu.VMEM((rc, wd), F32), pltpu.VMEM((rl, wd), F32)],
        compiler_params=_cparams(("parallel",)),
        name="s5_mix",
    )(u_ctx, u_lat, m, w, v, a1, a2)


def _merge_kernel(x_ref, ya_ref, hf_ref, hb_ref, om_ref, ys_ref, gate_ref, g1_ref, mlg_ref,
                  gluw_ref, glub_ref, wbr_ref, wout_ref, o_ref):
    hsum = hf_ref[...] + hb_ref[...]
    og = jax.nn.sigmoid(om_ref[...].astype(F32))
    mlg = mlg_ref[...]
    yb_parts = []
    for h in range(ML_HEADS):
        sl = slice(h * ML_HD, (h + 1) * ML_HD)
        hh = hsum[:, sl]
        hn = hh * lax.rsqrt(jnp.mean(hh * hh, axis=-1, keepdims=True) + EPS) * mlg[:, sl]
        yb_parts.append(hn * og[:, sl])
    yb = jnp.concatenate(yb_parts, axis=-1)
    gl = _gelu(ys_ref[...].astype(F32))
    ys = gl * jax.nn.sigmoid(
        jnp.dot(gl.astype(BF16), gluw_ref[...], preferred_element_type=F32) + glub_ref[...])
    d = o_ref.shape[1]
    merged = None
    for r, y in enumerate((ya_ref[...], yb, ys)):
        proj = jnp.dot(y.astype(BF16), wbr_ref[r], preferred_element_type=F32)
        term = jax.nn.sigmoid(gate_ref[:, r * d:(r + 1) * d].astype(F32)) * proj
        merged = term if merged is None else merged + term
    y = jnp.dot(merged.astype(BF16), wout_ref[...], preferred_element_type=F32)
    o_ref[...] = x_ref[...] + g1_ref[...] * y


def _merge(x_all, ya, h_dirs, z, ys, g1, ml_norm_g, glu_w, glu_b, w_br, w_out, *, n_rows, bsz, seq, tm):
    d = x_all.shape[1]
    nlb = bsz * seq // tm
    bpb = seq // tm
    w = BRANCH_W

    def rows(width, col_blk=0):
        return pl.BlockSpec((tm, width), lambda i: (i, col_blk))

    def full(shape):
        return pl.BlockSpec(shape, lambda i: (0,) * len(shape))

    return pl.pallas_call(
        _merge_kernel,
        out_shape=jax.ShapeDtypeStruct((n_rows, d), F32),
        grid=(n_rows // tm,),
        in_specs=[rows(d), rows(w),
                  pl.BlockSpec((None, tm, w), lambda i: (0, i, 0)),
                  pl.BlockSpec((None, tm, w), lambda i: (1, i, 0)),
                  rows(w, Z_OM // w), rows(w),
                  rows(3 * d, Z_GATE // (3 * d)),
                  pl.BlockSpec((None, 1, d), lambda i: (_row_batch(i, nlb, bpb, bsz), 0, 0)),
                  full((1, w)), full((w, w)), full((1, w)), full((3, w, d)), full((d, d))],
        out_specs=rows(d),
        compiler_params=_cparams(("parallel",)),
        name="merge",
    )(x_all, ya, h_dirs, h_dirs, z, ys, z, g1, ml_norm_g, glu_w, glu_b, w_br, w_out)


NOT_TOP = 127.0


def _take_top(s, n_take, vals_ref, want_rank):
    rows = s.shape[0]
    ridx = lax.broadcasted_iota(jnp.int32, s.shape, 0).astype(F32)
    rank = jnp.full(s.shape, NOT_TOP, F32) if want_rank else None
    for i in range(n_take):
        mx = jnp.max(s, axis=0, keepdims=True)
        first = jnp.min(jnp.where(s == mx, ridx, float(rows)), axis=0, keepdims=True)
        hit = ridx == first
        s = jnp.where(hit, -jnp.inf, s)
        if want_rank:
            rank = jnp.where(hit, float(i), rank)
        vals_ref[i:i + 1, :] = mx
    return s, rank


def _peer_topk_kernel(q_ref, k1_ref, k2_ref, e1_ref, cnt1_ref, e2_ref, rank2_ref,
                      v1_ref, v2_ref, top_ref, cnt_ref):
    k = PEER_TOPK
    q = q_ref[...].astype(BF16)
    s1 = _nt_dot(k1_ref[...], q[:, :LANES])
    s2 = _nt_dot(k2_ref[...], q[:, LANES:])
    _, rank1 = _take_top(s1, k, v1_ref, True)
    _, rank2 = _take_top(s2, k, v2_ref, True)
    v2_head = v2_ref[0:8, :]
    jrow = lax.broadcasted_iota(jnp.int32, v2_head.shape, 0)
    pieces = [v1_ref[0:1, :] + v2_ref[...]]
    for i in range(1, 8):
        pieces.append(jnp.where(jrow < k // (i + 1), v1_ref[i:i + 1, :] + v2_head, -jnp.inf))
    pieces.append(v1_ref[8:16, :] + v2_ref[0:1, :])
    cand = jnp.concatenate(pieces, axis=0)
    cand_left, _ = _take_top(cand, k, top_ref, False)
    picked = jnp.where(cand_left != cand, 1.0, 0.0)
    cnt_ref[0:1, :] = jnp.sum(picked[0:16], axis=0, keepdims=True)
    for i in range(1, 8):
        cnt_ref[i:i + 1, :] = jnp.sum(picked[8 + 8 * i:16 + 8 * i], axis=0, keepdims=True)
    cnt_ref[8:16, :] = picked[72:80]
    cnt1 = jnp.zeros(s1.shape, F32)
    for i in range(k):
        cnt1 = jnp.where(rank1 == float(i), cnt_ref[i:i + 1, :], cnt1)
    top = top_ref[...]
    zsum = jnp.sum(jnp.exp(top - top[0:1, :]), axis=0, keepdims=True)
    e1_ref[...] = jnp.where(rank1 < k, jnp.exp(s1 - v1_ref[0:1, :]), 0.0) / zsum
    cnt1_ref[...] = cnt1
    e2_ref[...] = jnp.where(rank2 < k, jnp.exp(s2 - v2_ref[0:1, :]), 0.0).astype(BF16)
    rank2_ref[...] = rank2.astype(BF16)


def _peer_topk(q, sub_k, *, n_rows, tm):
    nk = PEER_NKEYS
    row_tab = jax.ShapeDtypeStruct((PEER_HEADS, nk, n_rows), F32)
    tile_tab = jax.ShapeDtypeStruct((PEER_HEADS, nk, n_rows), BF16)
    tab_spec = pl.BlockSpec((None, nk, tm), lambda i, h: (h, 0, i))
    return pl.pallas_call(
        _peer_topk_kernel,
        out_shape=(row_tab, row_tab, tile_tab, tile_tab),
        grid=(n_rows // tm, PEER_HEADS),
        in_specs=[pl.BlockSpec((tm, 2 * LANES), lambda i, h: (i, h)),
                  pl.BlockSpec((None, nk, LANES), lambda i, h: (0, 0, 0)),
                  pl.BlockSpec((None, nk, LANES), lambda i, h: (1, 0, 0))],
        out_specs=(tab_spec,) * 4,
        scratch_shapes=[pltpu.VMEM((PEER_TOPK, tm), F32)] * 4,
        compiler_params=_cparams(("parallel", "parallel")),
        name="peer_topk",
    )(q, sub_k, sub_k)


GATE_ROWS = 128
GATE_KEYS = 2


def _peer_dense_kernel(h_ref, u_ref, vt_ref, e1_ref, cnt1_ref, e2_ref, rank2_ref,
                       x_ref, g2_ref, o_ref, acc_ref, act_ref, p_ref, *, a_per_blk):
    j = pl.program_id(1)
    n_blk = pl.num_programs(1) - 1
    cur = j % 2
    tm = h_ref.shape[0]
    nk = PEER_NKEYS

    @pl.when(j == 0)
    def _():
        acc_ref[...] = jnp.zeros_like(acc_ref)
        p_ref[1] = jnp.zeros(p_ref.shape[1:], p_ref.dtype)

    @pl.when(j < n_blk)
    def _():
        act_ref[...] = _gelu(_nt_dot(u_ref[...], h_ref[...])).astype(BF16)
        acc_ref[...] += jnp.dot(vt_ref[...], p_ref[1 - cur], preferred_element_type=F32)
        a0 = pl.multiple_of(j * a_per_blk, a_per_blk)

        def row_bf16(ref, h, i, ls):
            grp = (i // 8) * 8
            w = ref[h, pl.ds(a0 + grp, 8), ls][i % 8:i % 8 + 1]
            return jnp.broadcast_to(w, (GATE_ROWS, LANES)).astype(BF16)

        for lt in range(tm // LANES):
            ls = slice(lt * LANES, (lt + 1) * LANES)
            for c in range(nk // GATE_ROWS):
                rows_b = slice(c * GATE_ROWS, (c + 1) * GATE_ROWS)
                for i0 in range(0, a_per_blk, GATE_KEYS):
                    gates = [None] * GATE_KEYS
                    for h in range(PEER_HEADS):
                        e2 = e2_ref[h, rows_b, ls]
                        rank2 = rank2_ref[h, rows_b, ls]
                        for ii in range(GATE_KEYS):
                            lead = row_bf16(cnt1_ref, h, i0 + ii, ls) - rank2
                            term = jnp.minimum(jnp.maximum(lead, 0.0),
                                               row_bf16(e1_ref, h, i0 + ii, ls)) * e2
                            gates[ii] = term if gates[ii] is None else gates[ii] + term
                    for ii in range(GATE_KEYS):
                        r0 = (i0 + ii) * nk + c * GATE_ROWS
                        p_ref[cur, r0:r0 + GATE_ROWS, ls] = gates[ii] * act_ref[r0:r0 + GATE_ROWS, ls]

    @pl.when(j == n_blk)
    def _():
        acc = acc_ref[...] + jnp.dot(vt_ref[...], p_ref[1 - cur], preferred_element_type=F32)
        o_ref[...] = x_ref[...] + g2_ref[...] * acc.T


def _peer_dense(h2, u_tab, vt_tab, tabs, x_all, g2, *, n_rows, bsz, seq, tm, te):
    d = x_all.shape[1]
    n_exp = u_tab.shape[0]
    nlb = bsz * seq // tm
    bpb = seq // tm
    nk = PEER_NKEYS
    n_blk = n_exp // te
    tab_spec = pl.BlockSpec((PEER_HEADS, nk, tm), lambda i, j: (0, 0, i))
    return pl.pallas_call(
        functools.partial(_peer_dense_kernel, a_per_blk=te // nk),
        out_shape=jax.ShapeDtypeStruct((n_rows, d), F32),
        grid=(n_rows // tm, n_blk + 1),
        in_specs=[pl.BlockSpec((tm, d), lambda i, j: (i, 0)),
                  pl.BlockSpec((te, d), lambda i, j: (jnp.minimum(j, n_blk - 1), 0)),
                  pl.BlockSpec((d, te), lambda i, j: (0, jnp.maximum(j - 1, 0))),
                  tab_spec, tab_spec, tab_spec, tab_spec,
                  pl.BlockSpec((tm, d), lambda i, j: (i, 0)),
                  pl.BlockSpec((None, 1, d), lambda i, j: (_row_batch(i, nlb, bpb, bsz), 0, 0))],
        out_specs=pl.BlockSpec((tm, d), lambda i, j: (i, 0)),
        scratch_shapes=[pltpu.VMEM((d, tm), F32), pltpu.VMEM((te, tm), BF16),
                        pltpu.VMEM((2, te, tm), BF16)],
        compiler_params=_cparams(("parallel", "arbitrary")),
        name="peer_dense",
    )(h2, u_tab, vt_tab, *tabs, x_all, g2)


def _transpose_cast_kernel(x_ref, o_ref):
    o_ref[...] = x_ref[...].T.astype(o_ref.dtype)


def _transpose_cast(w, *, tr):
    rows, cols = w.shape
    return pl.pallas_call(
        _transpose_cast_kernel,
        out_shape=jax.ShapeDtypeStruct((cols, rows), BF16),
        grid=(rows // tr,),
        in_specs=[pl.BlockSpec((tr, cols), lambda i: (i, 0))],
        out_specs=pl.BlockSpec((cols, tr), lambda i: (0, i)),
        compiler_params=_cparams(("parallel",)),
        name="transpose_cast",
    )(w)


def _modnorm_kernel(x_ref, g_ref, sc_ref, sh_ref, o_ref):
    x = x_ref[...]
    y = x * lax.rsqrt(jnp.mean(x * x, axis=-1, keepdims=True) + EPS) * g_ref[...]
    o_ref[...] = (y * (1.0 + sc_ref[...]) + sh_ref[...]).astype(o_ref.dtype)


def _modnorm(x_all, n_rows, g, sc, sh, *, bsz, seq, tm, out_dtype):
    d = x_all.shape[1]
    nlb = bsz * seq // tm
    bpb = seq // tm
    mod_spec = pl.BlockSpec((None, 1, d), lambda i: (_row_batch(i, nlb, bpb, bsz), 0, 0))
    return pl.pallas_call(
        _modnorm_kernel,
        out_shape=jax.ShapeDtypeStruct((n_rows, d), out_dtype),
        grid=(n_rows // tm,),
        in_specs=[pl.BlockSpec((tm, d), lambda i: (i, 0)),
                  pl.BlockSpec((1, d), lambda i: (0, 0)), mod_spec, mod_spec],
        out_specs=pl.BlockSpec((tm, d), lambda i: (i, 0)),
        compiler_params=_cparams(("parallel",)),
        name="modnorm",
    )(x_all, g, sc, sh)


def _rope_tables(seq, tm):
    rows = seq // GRID_W
    n_freq = DA_HD // 4
    inv = ROPE_BASE ** (-jnp.arange(n_freq, dtype=F32) / n_freq)
    r = jnp.repeat(jnp.arange(rows, dtype=F32), GRID_W)
    col = jnp.tile(jnp.arange(GRID_W, dtype=F32), rows)
    ang = jnp.concatenate([r[:, None] * inv, col[:, None] * inv], axis=-1)
    cos, sin = jnp.cos(ang), jnp.sin(ang)
    cos_t = jnp.tile(cos, (1, 4))
    sin_t = jnp.tile(jnp.concatenate([-sin, sin], axis=-1), (1, 2))
    ident = jnp.ones((tm, LANES), F32)
    return (jnp.concatenate([cos_t, ident], axis=0),
            jnp.concatenate([sin_t, jnp.zeros((tm, LANES), F32)], axis=0))


def kernel(x, c, ctx, c_ctx, w_mod, b_mod, norm1_g, norm2_g, w_in, da_lam_q1, da_lam_k1, da_lam_q2, da_lam_k2, da_sub_g, ml_conv_w, ml_conv_b, ml_wq, ml_wk, ml_gate_b, ml_norm_g, s5_lam_re, s5_lam_im, s5_log_dt, s5_b_re, s5_b_im, s5_c_re, s5_c_im, s5_d, s5_glu_w, s5_glu_b, w_br, w_out, peer_wq, peer_sub_k, peer_u, peer_v, final_g):
    bsz, seq, d = x.shape
    ctx_len = ctx.shape[1]
    depth = w_in.shape[0]
    t_lat = bsz * seq
    t_ctx = bsz * ctx_len
    t_all = t_lat + t_ctx
    tm = math.gcd(512, math.gcd(seq, t_ctx))
    tm_merge = min(tm, 256)
    tq = min(256, ctx_len)
    nh = ML_HEADS

    x_all = jnp.concatenate([x.reshape(t_lat, d), ctx.reshape(t_ctx, d)], axis=0)
    pad = (-(bsz + 1)) % 8
    c_all = jnp.concatenate([c, c_ctx[None], jnp.zeros((pad, d), F32)], axis=0)
    cos_tab, sin_tab = _rope_tables(seq, tm)

    for l in range(depth):
        need_ctx = l < depth - 1
        n_rows = t_all if need_ctx else t_lat
        lam_init = 0.8 - 0.6 * math.exp(-0.3 * l)

        mods = _modulation(c_all, w_mod[l].astype(BF16), b_mod[l][None])
        sh1, sc1, g1, sh2, sc2, g2 = [mods[:bsz + 1, i * d:(i + 1) * d][:, None, :]
                                      for i in range(N_MOD)]

        wi = w_in[l]
        i_q, i_k, i_v, i_xm, i_vm, i_om, i_g, i_u, i_gate = (
            0, 512, 1024, 1536, 2048, 2560, 3072, 3088, 3600)
        w_main = jnp.concatenate([wi[:, i_gate:], wi[:, :i_g], wi[:, i_u:i_gate]], axis=1).astype(BF16)
        w_gate = jnp.pad(wi[:, i_g:i_u], ((0, 0), (0, LANES - (i_u - i_g)))).astype(BF16)
        n1 = norm1_g[l][None]
        z = _modnorm_matmul(x_all, t_all, n1, sc1, sh1, w_main, bsz=bsz, seq=seq, tm=tm,
                            tn=w_main.shape[1] // 2, out_dtype=BF16)
        zg = _modnorm_matmul(x_all, t_all, n1, sc1, sh1, w_gate, bsz=bsz, seq=seq, tm=tm, tn=LANES)

        qr, kr, vb = _rope_qkv(z, cos_tab, sin_tab, bsz=bsz, seq=seq, tm=tm)
        lam = (jnp.exp(jnp.sum(da_lam_q1[l] * da_lam_k1[l]))
               - jnp.exp(jnp.sum(da_lam_q2[l] * da_lam_k2[l])) + lam_init).reshape(1, 1)
        sub_g = da_sub_g[l][None]
        ya = _diff_attention(qr, kr, vb, lam, sub_g, q_row0=0, n_q=seq,
                             kv_segs=[(0, seq), (t_lat, ctx_len)], bsz=bsz, tq=tq,
                             out_scale=1.0 - lam_init)
        if need_ctx:
            ya_c = _diff_attention(qr, kr, vb, lam, sub_g, q_row0=t_lat, n_q=ctx_len,
                                   kv_segs=[(t_lat, ctx_len)], bsz=bsz, tq=tq,
                                   out_scale=1.0 - lam_init)
            ya = jnp.concatenate([ya, ya_c], axis=0)

        wq_b, wk_b = ml_wq[l].astype(BF16), ml_wk[l].astype(BF16)
        cb = ml_conv_b[l][None]
        q_l, k_l = _ml_prep(z, ml_conv_w[l], cb, wq_b, wk_b, row0=0, seq_len=seq, bsz=bsz)
        q_c, k_c = _ml_prep(z, ml_conv_w[l], cb, wq_b, wk_b, row0=t_lat, seq_len=ctx_len, bsz=bsz)
        q_ml = jnp.concatenate([q_l, q_c], axis=0)
        k_ml = jnp.concatenate([k_l, k_c], axis=0)
        gates = (zg[:, :4 * nh] + ml_gate_b[l]).reshape(t_all, 2, 2 * nh)
        gates_c = jnp.transpose(gates, (1, 0, 2))
        gates_r = jnp.transpose(gates, (1, 2, 0))
        h_dirs = _mlstm_scan(q_ml, k_ml, z, gates_c, gates_r, bsz=bsz, seq=seq, ctx_len=ctx_len)

        tables = _s5_tables(s5_lam_re[l], s5_lam_im[l], s5_log_dt[l], s5_b_re[l], s5_b_im[l],
                            s5_c_re[l], s5_c_im[l], s5_d[l])
        u = z[:, Z_U:Z_U + BRANCH_W]
        y_c, y_l = _s5_mix(_s5_rows(u[t_lat:], bsz, ctx_len), _s5_rows(u[:t_lat], bsz, seq),
                           tables, bsz=bsz)
        ys = _s5_unrows(y_l, bsz, seq)
        if need_ctx:
            ys = jnp.concatenate([ys, _s5_unrows(y_c, bsz, ctx_len)], axis=0)

        x_all = _merge(x_all, ya, h_dirs, z, ys, g1, ml_norm_g[l][None],
                       s5_glu_w[l].astype(BF16), s5_glu_b[l][None], w_br[l].astype(BF16),
                       w_out[l].astype(BF16), n_rows=n_rows, bsz=bsz, seq=seq, tm=tm_merge)

        n2 = norm2_g[l][None]
        wq_p = peer_wq[l].astype(BF16)
        pq, h2 = _modnorm_matmul(x_all, n_rows, n2, sc2, sh2, wq_p, bsz=bsz, seq=seq, tm=tm,
                                 tn=wq_p.shape[1], emit_h=True)
        tabs = _peer_topk(pq, peer_sub_k[l].astype(BF16), n_rows=n_rows, tm=tm)
        x_all = _peer_dense(h2, peer_u[l].astype(BF16), _transpose_cast(peer_v[l], tr=1024), tabs,
                            x_all, g2, n_rows=n_rows, bsz=bsz, seq=seq, tm=tm, te=1024)

    ones = jnp.ones((bsz + 1, 1, d), F32)
    out = _modnorm(x_all, t_lat, final_g[None], ones * 0.0, ones * 0.0, bsz=bsz, seq=seq, tm=tm,
                   out_dtype=F32)
    return out.reshape(bsz, seq, d)
```

```python
import functools
import math

import jax
import jax.numpy as jnp
from jax import lax
from jax.experimental import pallas as pl
from jax.experimental.pallas import tpu as pltpu

F32 = jnp.float32
BF16 = jnp.bfloat16

EPS = 1e-6
LOG2E = 1.4426950408889634
N_MOD = 6
BRANCH_W = 512
GRID_W = 64
ROPE_BASE = 10000.0
DA_HEADS = 4
DA_HD = 64
DA_VD = 2 * DA_HD
ML_HEADS = 4
ML_HD = BRANCH_W // ML_HEADS
ML_CHUNK = 128
S5_P = 16
S5_G = BRANCH_W // S5_P
S5_N = 64
S5_T = 16
PEER_HEADS = 8
PEER_NKEYS = 128
PEER_TOPK = 16
LANES = 128
VMEM_LIMIT = 56 * 1024 * 1024

Z_GATE = 0
Z_Q, Z_K, Z_V, Z_XM, Z_VM, Z_OM, Z_U = (3072 + i * BRANCH_W for i in range(7))


def _cparams(sem):
    return pltpu.CompilerParams(dimension_semantics=sem, vmem_limit_bytes=VMEM_LIMIT)


def _nt_dot(a, b):
    return lax.dot_general(a, b, (((1,), (1,)), ((), ())), preferred_element_type=F32)


def _gelu(x):
    return 0.5 * x * (1.0 + lax.erf(x * (2.0 ** -0.5)))


def _log_sigmoid(x):
    return jnp.minimum(x, 0.0) - jnp.log1p(jnp.exp(-jnp.abs(x)))


def _mod_kernel(c_ref, w_ref, b_ref, o_ref):
    c = c_ref[...]
    a = c * jax.nn.sigmoid(c)
    o_ref[...] = jnp.dot(a.astype(BF16), w_ref[...], preferred_element_type=F32) + b_ref[...]


def _modulation(c_all, w, b):
    rows, d = c_all.shape
    n = w.shape[1]
    tn = 1536
    return pl.pallas_call(
        _mod_kernel,
        out_shape=jax.ShapeDtypeStruct((rows, n), F32),
        grid=(n // tn,),
        in_specs=[pl.BlockSpec((rows, d), lambda j: (0, 0)),
                  pl.BlockSpec((d, tn), lambda j: (0, j)),
                  pl.BlockSpec((1, tn), lambda j: (0, j))],
        out_specs=pl.BlockSpec((rows, tn), lambda j: (0, j)),
        compiler_params=_cparams(("parallel",)),
        name="modulation",
    )(c_all, w, b)


def _modnorm_matmul_kernel(x_ref, g_ref, sc_ref, sh_ref, w_ref, o_ref, *h_out):
    x = x_ref[...]
    y = x * lax.rsqrt(jnp.mean(x * x, axis=-1, keepdims=True) + EPS) * g_ref[...]
    h = (y * (1.0 + sc_ref[...]) + sh_ref[...]).astype(BF16)
    o_ref[...] = jnp.dot(h, w_ref[...], preferred_element_type=F32).astype(o_ref.dtype)
    if h_out:
        h_out[0][...] = h


def _row_batch(i, n_lat_blocks, blocks_per_batch, bsz):
    return jnp.where(i < n_lat_blocks, i // blocks_per_batch, bsz)


def _modnorm_matmul(x_all, n_rows, g, sc, sh, w, *, bsz, seq, tm, tn, out_dtype=F32, emit_h=False):
    d = x_all.shape[1]
    n = w.shape[1]
    nlb = bsz * seq // tm
    bpb = seq // tm
    assert not emit_h or tn == n
    mod_spec = pl.BlockSpec((None, 1, d), lambda j, i: (_row_batch(i, nlb, bpb, bsz), 0, 0))
    out_shape = [jax.ShapeDtypeStruct((n_rows, n), out_dtype)]
    out_specs = [pl.BlockSpec((tm, tn), lambda j, i: (i, j))]
    if emit_h:
        out_shape.append(jax.ShapeDtypeStruct((n_rows, d), BF16))
        out_specs.append(pl.BlockSpec((tm, d), lambda j, i: (i, 0)))
    out = pl.pallas_call(
        _modnorm_matmul_kernel,
        out_shape=tuple(out_shape),
        grid=(n // tn, n_rows // tm),
        in_specs=[pl.BlockSpec((tm, d), lambda j, i: (i, 0)),
                  pl.BlockSpec((1, d), lambda j, i: (0, 0)),
                  mod_spec, mod_spec,
                  pl.BlockSpec((d, tn), lambda j, i: (0, j))],
        out_specs=tuple(out_specs),
        compiler_params=_cparams(("parallel", "parallel")),
        name="modnorm_matmul",
    )(x_all, g, sc, sh, w)
    return out if emit_h else out[0]


def _rope_kernel(q_ref, k_ref, v_ref, cos_ref, sin_ref, qo_ref, ko_ref, vo_ref):
    cos = cos_ref[...]
    sin = sin_ref[...]
    lane = lax.broadcasted_iota(jnp.int32, cos.shape, 1)
    first_half = (lane % DA_HD) < (DA_HD // 2)

    def rope(x):
        partner = jnp.where(first_half, pltpu.roll(x, LANES - DA_HD // 2, 1),
                            pltpu.roll(x, DA_HD // 2, 1))
        return x * cos + partner * sin

    for h in range(DA_HEADS):
        sl = slice(h * LANES, (h + 1) * LANES)
        qo_ref[:, sl] = (rope(q_ref[:, sl].astype(F32)) * (DA_HD ** -0.5 * LOG2E)).astype(BF16)
        ko_ref[:, sl] = rope(k_ref[:, sl].astype(F32)).astype(BF16)
    vo_ref[...] = v_ref[...].astype(BF16)


def _rope_qkv(z, cos_tab, sin_tab, *, bsz, seq, tm):
    rows = z.shape[0]
    nlb = bsz * seq // tm
    bpb = seq // tm
    w = BRANCH_W
    tab_spec = pl.BlockSpec((tm, LANES), lambda i: (jnp.where(i < nlb, i % bpb, bpb), 0))
    out = jax.ShapeDtypeStruct((rows, w), BF16)
    return pl.pallas_call(
        _rope_kernel,
        out_shape=(out, out, out),
        grid=(rows // tm,),
        in_specs=[pl.BlockSpec((tm, w), lambda i: (i, Z_Q // w)),
                  pl.BlockSpec((tm, w), lambda i: (i, Z_K // w)),
                  pl.BlockSpec((tm, w), lambda i: (i, Z_V // w)),
                  tab_spec, tab_spec],
        out_specs=(pl.BlockSpec((tm, w), lambda i: (i, 0)),) * 3,
        compiler_params=_cparams(("parallel",)),
        name="rope_qkv",
    )(z, z, z, cos_tab, sin_tab)


def _attn_kernel(*refs, n_seg, n_sub, out_scale):
    lam_ref, q_ref = refs[0], refs[1]
    k_refs = refs[2:2 + n_seg]
    v_refs = refs[2 + n_seg:2 + 2 * n_seg]
    g_ref, o_ref = refs[2 + 2 * n_seg], refs[3 + 2 * n_seg]
    tq = q_ref.shape[0] // n_sub
    lane = lax.broadcasted_iota(jnp.int32, (tq, LANES), 1)
    scores = []
    for sub in range(n_sub):
        q = q_ref[sub * tq:(sub + 1) * tq, :].astype(F32)
        qq = jnp.concatenate([jnp.where(lane < DA_HD, q, 0.0), jnp.where(lane >= DA_HD, q, 0.0)],
                             axis=0).astype(BF16)
        scores.append([_nt_dot(qq, k_ref[...]) for k_ref in k_refs])
    for sub in range(n_sub):
        rows = slice(sub * tq, (sub + 1) * tq)
        s = scores[sub]
        m = functools.reduce(jnp.maximum, [jnp.max(t, axis=-1, keepdims=True) for t in s])
        p = [jnp.exp2(t - m) for t in s]
        denom = functools.reduce(jnp.add, [jnp.sum(t, axis=-1, keepdims=True) for t in p])
        inv = 1.0 / denom
        c0 = inv[:tq]
        c1 = inv[tq:] * lam_ref[...]
        o = None
        for t, v_ref in zip(p, v_refs):
            w = (t[:tq] * c0 - t[tq:] * c1).astype(BF16)
            part = jnp.dot(w, v_ref[...], preferred_element_type=F32)
            o = part if o is None else o + part
        y = o * lax.rsqrt(jnp.mean(o * o, axis=-1, keepdims=True) + EPS) * g_ref[...]
        o_ref[rows, :] = y * out_scale


def _diff_attention(qr, kr, vb, lam, sub_g, *, q_row0, n_q, kv_segs, bsz, tq, out_scale):
    n_sub = 2 if n_q % (2 * tq) == 0 else 1
    tq = tq * n_sub
    nqb = n_q // tq
    n_seg = len(kv_segs)
    q_blk0 = q_row0 // tq

    def kv_spec(row0, length):
        return pl.BlockSpec((length, LANES), lambda b, h, i: (row0 // length + b, h))

    in_specs = [pl.BlockSpec((1, 1), lambda b, h, i: (0, 0)),
                pl.BlockSpec((tq, LANES), lambda b, h, i: (q_blk0 + b * nqb + i, h))]
    in_specs += [kv_spec(r0, ln) for r0, ln in kv_segs] * 2
    in_specs += [pl.BlockSpec((1, LANES), lambda b, h, i: (0, h))]
    args = [lam, qr] + [kr] * n_seg + [vb] * n_seg + [sub_g]
    return pl.pallas_call(
        functools.partial(_attn_kernel, n_seg=n_seg, n_sub=n_sub, out_scale=out_scale),
        out_shape=jax.ShapeDtypeStruct((bsz * n_q, BRANCH_W), F32),
        grid=(bsz, DA_HEADS, nqb),
        in_specs=in_specs,
        out_specs=pl.BlockSpec((tq, LANES), lambda b, h, i: (b * nqb + i, h)),
        compiler_params=_cparams(("parallel", "parallel", "parallel")),
        name="diff_attention",
    )(*args)


def _ml_prep_kernel(x_ref, cw_ref, cb_ref, wq_ref, wk_ref, q_ref, k_ref):
    x = x_ref[...].astype(F32)
    n = x.shape[0]
    row = lax.broadcasted_iota(jnp.int32, x.shape, 0)
    x_prev = jnp.where(row == 0, 0.0, pltpu.roll(x, 1, 0))
    x_next = jnp.where(row == n - 1, 0.0, pltpu.roll(x, n - 1, 0))
    cw = cw_ref[...]
    y = x_prev * cw[0:1] + x * cw[1:2] + x_next * cw[2:3] + cb_ref[...]
    xc = (y * jax.nn.sigmoid(y)).astype(BF16)
    q_ref[...] = jnp.dot(xc, wq_ref[...], preferred_element_type=F32) * (ML_HD ** -0.5)
    k_ref[...] = jnp.dot(xc, wk_ref[...], preferred_element_type=F32)


def _ml_prep(z, conv_w, conv_b, wq, wk, *, row0, seq_len, bsz):
    blk0 = row0 // seq_len
    out = jax.ShapeDtypeStruct((bsz * seq_len, BRANCH_W), F32)
    head_w = pl.BlockSpec((None, ML_HD, ML_HD), lambda b, h: (h, 0, 0))
    return pl.pallas_call(
        _ml_prep_kernel,
        out_shape=(out, out),
        grid=(bsz, ML_HEADS),
        in_specs=[pl.BlockSpec((seq_len, LANES), lambda b, h: (blk0 + b, Z_XM // LANES + h)),
                  pl.BlockSpec((3, LANES), lambda b, h: (0, h)),
                  pl.BlockSpec((1, LANES), lambda b, h: (0, h)),
                  head_w, head_w],
        out_specs=(pl.BlockSpec((seq_len, LANES), lambda b, h: (b, h)),) * 2,
        compiler_params=_cparams(("parallel", "parallel")),
        name="mlstm_prep",
    )(z, conv_w, conv_b, wq, wk)


def _mlstm_kernel(qf_ref, kf_ref, vf_ref, gcf_ref, grf_ref, qb_ref, kb_ref, vb_ref, gcb_ref, grb_ref,
                  hf_ref, hb_ref, c_ref, n_ref, m_ref):
    @pl.when(pl.program_id(1) == 0)
    def _():
        c_ref[...] = jnp.zeros_like(c_ref)
        n_ref[...] = jnp.zeros_like(n_ref)
        m_ref[...] = jnp.zeros_like(m_ref)

    t_idx = lax.broadcasted_iota(jnp.int32, (ML_CHUNK, ML_CHUNK), 0)
    s_idx = lax.broadcasted_iota(jnp.int32, (ML_CHUNK, ML_CHUNK), 1)
    nh = ML_HEADS
    dirs = ((qf_ref, kf_ref, vf_ref, gcf_ref, grf_ref, hf_ref),
            (qb_ref, kb_ref, vb_ref, gcb_ref, grb_ref, hb_ref))
    units = []
    for d, (q_ref, k_ref, v_ref, gc_ref, gr_ref, h_ref) in enumerate(dirs):
        mask = s_idx <= t_idx if d == 0 else s_idx >= t_idx
        mask_f = mask.astype(F32)
        gc = gc_ref[...]
        gr = gr_ref[...]
        lf_c = _log_sigmoid(gc)
        lf_r = _log_sigmoid(gr)
        b_c = jnp.dot(mask_f, lf_c, preferred_element_type=F32, precision=lax.Precision.HIGHEST)
        b_r = lax.dot_general(lf_r, mask_f, (((1,), (1,)), ((), ())), preferred_element_type=F32,
                              precision=lax.Precision.HIGHEST)
        b_end_all = jnp.sum(lf_c, axis=0, keepdims=True)

        for h in range(nh):
            sl = slice(h * ML_HD, (h + 1) * ML_HD)
            q = q_ref[:, sl].astype(BF16)
            k = k_ref[:, sl].astype(BF16)
            v = v_ref[:, sl].astype(F32)
            li_c = gc[:, h:h + 1]
            li_r = gr[h:h + 1, :]
            bc = b_c[:, nh + h:nh + h + 1]
            br = b_r[nh + h:nh + h + 1, :]
            b_end = b_end_all[:, nh + h:nh + h + 1]
            m_prev = m_ref[d, h]
            c_prev = c_ref[d, h]
            n_prev = n_ref[d, h]
            g_c = b_end - bc + li_c
            g_r = b_end - br + li_r
            m_new = jnp.maximum(b_end + m_prev, jnp.max(g_r, axis=-1, keepdims=True))
            wg = jnp.exp(g_c - m_new)
            decay = jnp.exp(b_end + m_prev - m_new)
            vw = (v * wg).astype(BF16)
            units.append(dict(
                d=d, h=h, sl=sl, h_ref=h_ref, q=q, v=v, mask=mask, m_prev=m_prev, n_prev=n_prev,
                bc=bc, br=br, li_r=li_r,
                qk=_nt_dot(q, k), qc=_nt_dot(q, c_prev.astype(BF16)),
                c_new=decay * c_prev + lax.dot_general(
                    vw, k, (((0,), (0,)), ((), ())), preferred_element_type=F32),
                n_new=decay * n_prev + jnp.sum(wg * k.astype(F32), axis=0, keepdims=True),
                m_new=m_new))

    for u in units:
        dmat = jnp.where(u["mask"], u["bc"] - u["br"] + u["li_r"], -jnp.inf)
        inter = u["bc"] + u["m_prev"]
        m_t = jnp.maximum(inter, jnp.max(dmat, axis=-1, keepdims=True))
        s = u["qk"] * jnp.exp(dmat - m_t)
        w_prev = jnp.exp(inter - m_t)
        num = (jnp.dot(s.astype(BF16), u["v"].astype(BF16), preferred_element_type=F32)
               + w_prev * u["qc"])
        den = (jnp.sum(s, axis=-1, keepdims=True)
               + w_prev * jnp.sum(u["q"].astype(F32) * u["n_prev"], axis=-1, keepdims=True))
        u["h_ref"][:, u["sl"]] = num / jnp.maximum(jnp.abs(den), jnp.exp(-m_t))
        c_ref[u["d"], u["h"]] = u["c_new"]
        n_ref[u["d"], u["h"]] = u["n_new"]
        m_ref[u["d"], u["h"]] = u["m_new"]


def _mlstm_scan(q_all, k_all, z, gates_c, gates_r, *, bsz, seq, ctx_len):
    rows = q_all.shape[0]
    ncl = seq // ML_CHUNK
    ncc = ctx_len // ML_CHUNK
    lat_blocks = bsz * ncl
    nh = ML_HEADS

    def blk(d, b, s):
        jc = s if d == 0 else ncc - 1 - s
        jl = s - ncc if d == 0 else ncl - 1 - (s - ncc)
        return jnp.where(s < ncc, lat_blocks + b * ncc + jc, b * ncl + jl)

    def dir_specs(d):
        return [pl.BlockSpec((ML_CHUNK, BRANCH_W), lambda b, s: (blk(d, b, s), 0)),
                pl.BlockSpec((ML_CHUNK, BRANCH_W), lambda b, s: (blk(d, b, s), 0)),
                pl.BlockSpec((ML_CHUNK, BRANCH_W), lambda b, s: (blk(d, b, s), Z_VM // BRANCH_W)),
                pl.BlockSpec((None, ML_CHUNK, 2 * nh), lambda b, s: (d, blk(d, b, s), 0)),
                pl.BlockSpec((None, 2 * nh, ML_CHUNK), lambda b, s: (d, 0, blk(d, b, s)))]

    out = jax.ShapeDtypeStruct((rows, BRANCH_W), F32)
    args = (q_all, k_all, z, gates_c, gates_r)
    return pl.pallas_call(
        _mlstm_kernel,
        out_shape=(out, out),
        grid=(bsz, ncc + ncl),
        in_specs=dir_specs(0) + dir_specs(1),
        out_specs=(pl.BlockSpec((ML_CHUNK, BRANCH_W), lambda b, s: (blk(0, b, s), 0)),
                   pl.BlockSpec((ML_CHUNK, BRANCH_W), lambda b, s: (blk(1, b, s), 0))),
        scratch_shapes=[pltpu.VMEM((2, nh, ML_HD, ML_HD), F32),
                        pltpu.VMEM((2, nh, 1, ML_HD), F32),
                        pltpu.VMEM((2, nh, 1, 1), F32)],
        compiler_params=_cparams(("parallel", "arbitrary")),
        name="mlstm_scan",
    )(*args, *args)


def _s5_kernel(uc_ref, ul_ref, m_ref, w_ref, v_ref, a1_ref, a2_ref, yc_ref, yl_ref,
               ec_ref, el_ref, xc_ref, xl_ref, *, bsz):
    w = w_ref[...]
    ec_ref[...] = jnp.dot(uc_ref[...], w, preferred_element_type=F32)
    el_ref[...] = jnp.dot(ul_ref[...], w, preferred_element_type=F32)
    a1 = a1_ref[...]
    a2 = a2_ref[...]
    half = LANES

    def advance(x, e, a1d, a2d):
        return a1d * x + a2d * pltpu.roll(x, S5_N, 1) + e

    def scan(e_ref, x_ref, state):
        n_chunks = e_ref.shape[0] // bsz

        def body(j, st):
            xf, xb = st
            rf = pl.multiple_of(j * bsz, bsz)
            rb = pl.multiple_of((n_chunks - 1 - j) * bsz, bsz)
            x_ref[pl.ds(rf, bsz), 0:half] = xf
            x_ref[pl.ds(rb, bsz), half:2 * half] = xb
            xf = advance(xf, e_ref[pl.ds(rf, bsz), 0:half], a1[:, 0:half], a2[:, 0:half])
            xb = advance(xb, e_ref[pl.ds(rb, bsz), half:2 * half], a1[:, half:], a2[:, half:])
            return xf, xb

        return lax.fori_loop(0, n_chunks, body, state)

    zero = jnp.zeros((bsz, half), F32)
    state = scan(ec_ref, xc_ref, (zero, zero))
    scan(el_ref, xl_ref, state)
    mm = m_ref[...]
    vv = v_ref[...]
    yc_ref[...] = (jnp.dot(uc_ref[...], mm, preferred_element_type=F32)
                   + jnp.dot(xc_ref[...].astype(BF16), vv, preferred_element_type=F32)
                   ).astype(yc_ref.dtype)
    yl_ref[...] = (jnp.dot(ul_ref[...], mm, preferred_element_type=F32)
                   + jnp.dot(xl_ref[...].astype(BF16), vv, preferred_element_type=F32)
                   ).astype(yl_ref.dtype)


def _s5_tables(lam_re, lam_im, log_dt, b_re, b_im, c_re, c_im, d_skip):
    t = S5_T
    dt = jnp.exp(log_dt)[..., None]
    den = lam_re * lam_re + lam_im * lam_im
    kk = jnp.arange(t + 1, dtype=F32)[:, None, None, None]
    mag = jnp.exp(kk * lam_re * dt)
    ang = kk * lam_im * dt
    p_re, p_im = mag * jnp.cos(ang), mag * jnp.sin(ang)
    ab_re, ab_im = p_re[1], p_im[1]
    z_re = ((ab_re - 1.0) * lam_re + ab_im * lam_im) / den
    z_im = (ab_im * lam_re - (ab_re - 1.0) * lam_im) / den
    bb_re = z_re[..., None] * b_re - z_im[..., None] * b_im
    bb_im = z_re[..., None] * b_im + z_im[..., None] * b_re
    ab_k_re = p_re[..., None] * bb_re - p_im[..., None] * bb_im
    ab_k_im = p_re[..., None] * bb_im + p_im[..., None] * bb_re
    taps = (jnp.einsum('rgpn,krgnq->krgpq', c_re, ab_k_re[:t])
            - jnp.einsum('rgpn,krgnq->krgpq', c_im, ab_k_im[:t]))
    s_i = jnp.arange(t)[:, None]
    t_i = jnp.arange(t)[None, :]
    lag_f = jnp.clip(t_i - s_i, 0, t - 1)
    lag_b = jnp.clip(s_i - t_i, 0, t - 1)
    tf = jnp.where((t_i >= s_i)[..., None, None, None], taps[lag_f, 0], 0.0)
    tb = jnp.where((s_i >= t_i)[..., None, None, None], taps[lag_b, 1], 0.0)
    skip = (jnp.eye(t)[:, :, None, None, None]
            * (jnp.eye(S5_P)[None, None, None] * d_skip.reshape(S5_G, S5_P, 1)[None, None]))
    m = jnp.transpose(tf + tb + skip, (2, 0, 4, 1, 3)).reshape(S5_G, t * S5_P, t * S5_P)
    wf_re = jnp.transpose(ab_k_re[:t, 0][::-1], (1, 0, 3, 2)).reshape(S5_G, t * S5_P, S5_N)
    wf_im = jnp.transpose(ab_k_im[:t, 0][::-1], (1, 0, 3, 2)).reshape(S5_G, t * S5_P, S5_N)
    wb_re = jnp.transpose(ab_k_re[:t, 1], (1, 0, 3, 2)).reshape(S5_G, t * S5_P, S5_N)
    wb_im = jnp.transpose(ab_k_im[:t, 1], (1, 0, 3, 2)).reshape(S5_G, t * S5_P, S5_N)
    w = jnp.concatenate([wf_re, wf_im, wb_re, wb_im], axis=-1)
    ca_re = (c_re[None] * p_re[:, :, :, None, :] - c_im[None] * p_im[:, :, :, None, :])
    ca_im = (c_re[None] * p_im[:, :, :, None, :] + c_im[None] * p_re[:, :, :, None, :])

    def to_rows(a):
        return jnp.transpose(a, (1, 3, 0, 2)).reshape(S5_G, S5_N, t * S5_P)

    v = jnp.concatenate([to_rows(ca_re[1:t + 1, 0]), to_rows(-ca_im[1:t + 1, 0]),
                         to_rows(ca_re[1:t + 1, 1][::-1]), to_rows(-ca_im[1:t + 1, 1][::-1])], axis=1)
    at_re, at_im = p_re[t], p_im[t]
    a1 = jnp.concatenate([at_re[0], at_re[0], at_re[1], at_re[1]], axis=-1)[:, None, :]
    a2 = jnp.concatenate([-at_im[0], at_im[0], -at_im[1], at_im[1]], axis=-1)[:, None, :]
    return m.astype(BF16), w.astype(BF16), v.astype(BF16), a1, a2


def _s5_rows(u, bsz, length):
    nc = length // S5_T
    u = u.reshape(bsz, nc, S5_T, S5_G, S5_P)
    return jnp.transpose(u, (3, 1, 0, 2, 4)).reshape(S5_G, nc * bsz, S5_T * S5_P)


def _s5_unrows(y, bsz, length):
    nc = length // S5_T
    y = y.reshape(S5_G, nc, bsz, S5_T, S5_P)
    return jnp.transpose(y, (2, 1, 3, 0, 4)).reshape(bsz * length, BRANCH_W)


def _s5_mix(u_ctx, u_lat, tables, *, bsz):
    m, w, v, a1, a2 = tables
    rc, rl = u_ctx.shape[1], u_lat.shape[1]
    wd = S5_T * S5_P

    def rows_spec(r):
        return pl.BlockSpec((None, r, wd), lambda g: (g, 0, 0))

    sq = pl.BlockSpec((None, wd, wd), lambda g: (g, 0, 0))
    vec = pl.BlockSpec((None, 1, wd), lambda g: (g, 0, 0))
    return pl.pallas_call(
        functools.partial(_s5_kernel, bsz=bsz),
        out_shape=(jax.ShapeDtypeStruct((S5_G, rc, wd), BF16),
                   jax.ShapeDtypeStruct((S5_G, rl, wd), BF16)),
        grid=(S5_G,),
        in_specs=[rows_spec(rc), rows_spec(rl), sq, sq, sq, vec, vec],
        out_specs=(rows_spec(rc), rows_spec(rl)),
        scratch_shapes=[pltpu.VMEM((rc, wd), F32), pltpu.VMEM((rl, wd), F32),
                        pltpu.VMEM((rc, wd), F32), pltpu.VMEM((rl, wd), F32)],
        compiler_params=_cparams(("parallel",)),
        name="s5_mix",
    )(u_ctx, u_lat, m, w, v, a1, a2)


def _merge_kernel(x_ref, ya_ref, hf_ref, hb_ref, om_ref, ys_ref, gate_ref, g1_ref, mlg_ref,
                  gluw_ref, glub_ref, wbr_ref, wout_ref, o_ref):
    hsum = hf_ref[...] + hb_ref[...]
    og = jax.nn.sigmoid(om_ref[...].astype(F32))
    mlg = mlg_ref[...]
    yb_parts = []
    for h in range(ML_HEADS):
        sl = slice(h * ML_HD, (h + 1) * ML_HD)
        hh = hsum[:, sl]
        hn = hh * lax.rsqrt(jnp.mean(hh * hh, axis=-1, keepdims=True) + EPS) * mlg[:, sl]
        yb_parts.append(hn * og[:, sl])
    yb = jnp.concatenate(yb_parts, axis=-1)
    gl = _gelu(ys_ref[...].astype(F32))
    ys = gl * jax.nn.sigmoid(
        jnp.dot(gl.astype(BF16), gluw_ref[...], preferred_element_type=F32) + glub_ref[...])
    d = o_ref.shape[1]
    merged = None
    for r, y in enumerate((ya_ref[...], yb, ys)):
        proj = jnp.dot(y.astype(BF16), wbr_ref[r], preferred_element_type=F32)
        term = jax.nn.sigmoid(gate_ref[:, r * d:(r + 1) * d].astype(F32)) * proj
        merged = term if merged is None else merged + term
    y = jnp.dot(merged.astype(BF16), wout_ref[...], preferred_element_type=F32)
    o_ref[...] = x_ref[...] + g1_ref[...] * y


def _merge(x_all, ya, h_dirs, z, ys, g1, ml_norm_g, glu_w, glu_b, w_br, w_out, *, n_rows, bsz, seq, tm):
    d = x_all.shape[1]
    nlb = bsz * seq // tm
    bpb = seq // tm
    w = BRANCH_W

    def rows(width, col_blk=0):
        return pl.BlockSpec((tm, width), lambda i: (i, col_blk))

    def full(shape):
        return pl.BlockSpec(shape, lambda i: (0,) * len(shape))

    return pl.pallas_call(
        _merge_kernel,
        out_shape=jax.ShapeDtypeStruct((n_rows, d), F32),
        grid=(n_rows // tm,),
        in_specs=[rows(d), rows(w), rows(w), rows(w),
                  rows(w, Z_OM // w), rows(w),
                  rows(3 * d, Z_GATE // (3 * d)),
                  pl.BlockSpec((None, 1, d), lambda i: (_row_batch(i, nlb, bpb, bsz), 0, 0)),
                  full((1, w)), full((w, w)), full((1, w)), full((3, w, d)), full((d, d))],
        out_specs=rows(d),
        compiler_params=_cparams(("parallel",)),
        name="merge",
    )(x_all, ya, h_dirs[0], h_dirs[1], z, ys, z, g1, ml_norm_g, glu_w, glu_b, w_br, w_out)


NOT_TOP = 127.0


def _take_top(s, n_take, vals_ref, want_rank):
    rows = s.shape[0]
    ridx = lax.broadcasted_iota(jnp.int32, s.shape, 0).astype(F32)
    rank = jnp.full(s.shape, NOT_TOP, F32) if want_rank else None
    for i in range(n_take):
        mx = jnp.max(s, axis=0, keepdims=True)
        first = jnp.min(jnp.where(s == mx, ridx, float(rows)), axis=0, keepdims=True)
        hit = ridx == first
        s = jnp.where(hit, -jnp.inf, s)
        if want_rank:
            rank = jnp.where(hit, float(i), rank)
        vals_ref[i:i + 1, :] = mx
    return s, rank


def _peer_topk_kernel(q_ref, k1_ref, k2_ref, e1_ref, cnt1_ref, e2_ref, rank2_ref,
                      v1_ref, v2_ref, top_ref, cnt_ref):
    k = PEER_TOPK
    q = q_ref[...].astype(BF16)
    s1 = _nt_dot(k1_ref[...], q[:, :LANES])
    s2 = _nt_dot(k2_ref[...], q[:, LANES:])
    _, rank1 = _take_top(s1, k, v1_ref, True)
    _, rank2 = _take_top(s2, k, v2_ref, True)
    v2_head = v2_ref[0:8, :]
    jrow = lax.broadcasted_iota(jnp.int32, v2_head.shape, 0)
    pieces = [v1_ref[0:1, :] + v2_ref[...]]
    for i in range(1, 8):
        pieces.append(jnp.where(jrow < k // (i + 1), v1_ref[i:i + 1, :] + v2_head, -jnp.inf))
    pieces.append(v1_ref[8:16, :] + v2_ref[0:1, :])
    cand = jnp.concatenate(pieces, axis=0)
    cand_left, _ = _take_top(cand, k, top_ref, False)
    picked = jnp.where(cand_left != cand, 1.0, 0.0)
    cnt_ref[0:1, :] = jnp.sum(picked[0:16], axis=0, keepdims=True)
    for i in range(1, 8):
        cnt_ref[i:i + 1, :] = jnp.sum(picked[8 + 8 * i:16 + 8 * i], axis=0, keepdims=True)
    cnt_ref[8:16, :] = picked[72:80]
    cnt1 = jnp.zeros(s1.shape, F32)
    for i in range(k):
        cnt1 = jnp.where(rank1 == float(i), cnt_ref[i:i + 1, :], cnt1)
    top = top_ref[...]
    zsum = jnp.sum(jnp.exp(top - top[0:1, :]), axis=0, keepdims=True)
    e1_ref[...] = jnp.where(rank1 < k, jnp.exp(s1 - v1_ref[0:1, :]), 0.0) / zsum
    cnt1_ref[...] = cnt1
    e2_ref[...] = jnp.where(rank2 < k, jnp.exp(s2 - v2_ref[0:1, :]), 0.0).astype(BF16)
    rank2_ref[...] = rank2.astype(BF16)


def _peer_topk(q, sub_k, *, n_rows, tm):
    nk = PEER_NKEYS
    row_tab = jax.ShapeDtypeStruct((n_rows // tm, PEER_HEADS, nk, tm), F32)
    tile_tab = jax.ShapeDtypeStruct((n_rows // tm, PEER_HEADS, nk, tm), BF16)
    tab_spec = pl.BlockSpec((None, None, nk, tm), lambda i, h: (i, h, 0, 0))
    return pl.pallas_call(
        _peer_topk_kernel,
        out_shape=(row_tab, row_tab, tile_tab, tile_tab),
        grid=(n_rows // tm, PEER_HEADS),
        in_specs=[pl.BlockSpec((tm, 2 * LANES), lambda i, h: (i, h)),
                  pl.BlockSpec((None, nk, LANES), lambda i, h: (0, 0, 0)),
                  pl.BlockSpec((None, nk, LANES), lambda i, h: (1, 0, 0))],
        out_specs=(tab_spec,) * 4,
        scratch_shapes=[pltpu.VMEM((PEER_TOPK, tm), F32)] * 4,
        compiler_params=_cparams(("parallel", "parallel")),
        name="peer_topk",
    )(q, sub_k, sub_k)


GATE_ROWS = 128
GATE_KEYS = 2


def _peer_dense_kernel(h_ref, u_ref, vt_ref, e1_ref, cnt1_ref, e2_ref, rank2_ref,
                       x_ref, g2_ref, o_ref, acc_ref, act_ref, p_ref, *, a_per_blk):
    j = pl.program_id(1)
    n_blk = pl.num_programs(1) - 1
    cur = j % 2
    tm = h_ref.shape[0]
    nk = PEER_NKEYS

    @pl.when(j == 0)
    def _():
        acc_ref[...] = jnp.zeros_like(acc_ref)
        p_ref[1] = jnp.zeros(p_ref.shape[1:], p_ref.dtype)

    @pl.when(j < n_blk)
    def _():
        act_ref[...] = _gelu(_nt_dot(u_ref[...], h_ref[...])).astype(BF16)
        acc_ref[...] += jnp.dot(vt_ref[...], p_ref[1 - cur], preferred_element_type=F32)
        a0 = pl.multiple_of(j * a_per_blk, a_per_blk)

        def row_bf16(ref, h, i, ls):
            grp = (i // 8) * 8
            w = ref[h, pl.ds(a0 + grp, 8), ls][i % 8:i % 8 + 1]
            return jnp.broadcast_to(w, (GATE_ROWS, LANES)).astype(BF16)

        for lt in range(tm // LANES):
            ls = slice(lt * LANES, (lt + 1) * LANES)
            for c in range(nk // GATE_ROWS):
                rows_b = slice(c * GATE_ROWS, (c + 1) * GATE_ROWS)
                for i0 in range(0, a_per_blk, GATE_KEYS):
                    gates = [None] * GATE_KEYS
                    for h in range(PEER_HEADS):
                        e2 = e2_ref[h, rows_b, ls]
                        rank2 = rank2_ref[h, rows_b, ls]
                        for ii in range(GATE_KEYS):
                            lead = row_bf16(cnt1_ref, h, i0 + ii, ls) - rank2
                            term = jnp.minimum(jnp.maximum(lead, 0.0),
                                               row_bf16(e1_ref, h, i0 + ii, ls)) * e2
                            gates[ii] = term if gates[ii] is None else gates[ii] + term
                    for ii in range(GATE_KEYS):
                        r0 = (i0 + ii) * nk + c * GATE_ROWS
                        p_ref[cur, r0:r0 + GATE_ROWS, ls] = gates[ii] * act_ref[r0:r0 + GATE_ROWS, ls]

    @pl.when(j == n_blk)
    def _():
        acc = acc_ref[...] + jnp.dot(vt_ref[...], p_ref[1 - cur], preferred_element_type=F32)
        o_ref[...] = x_ref[...] + g2_ref[...] * acc.T


def _peer_dense(h2, u_tab, vt_tab, tabs, x_all, g2, *, n_rows, bsz, seq, tm, te):
    d = x_all.shape[1]
    n_exp = u_tab.shape[0]
    nlb = bsz * seq // tm
    bpb = seq // tm
    nk = PEER_NKEYS
    n_blk = n_exp // te
    tab_spec = pl.BlockSpec((None, PEER_HEADS, nk, tm), lambda i, j: (i, 0, 0, 0))
    return pl.pallas_call(
        functools.partial(_peer_dense_kernel, a_per_blk=te // nk),
        out_shape=jax.ShapeDtypeStruct((n_rows, d), F32),
        grid=(n_rows // tm, n_blk + 1),
        in_specs=[pl.BlockSpec((tm, d), lambda i, j: (i, 0)),
                  pl.BlockSpec((te, d), lambda i, j: (jnp.minimum(j, n_blk - 1), 0)),
                  pl.BlockSpec((None, d, te), lambda i, j: (jnp.maximum(j - 1, 0), 0, 0)),
                  tab_spec, tab_spec, tab_spec, tab_spec,
                  pl.BlockSpec((tm, d), lambda i, j: (i, 0)),
                  pl.BlockSpec((None, 1, d), lambda i, j: (_row_batch(i, nlb, bpb, bsz), 0, 0))],
        out_specs=pl.BlockSpec((tm, d), lambda i, j: (i, 0)),
        scratch_shapes=[pltpu.VMEM((d, tm), F32), pltpu.VMEM((te, tm), BF16),
                        pltpu.VMEM((2, te, tm), BF16)],
        compiler_params=_cparams(("parallel", "arbitrary")),
        name="peer_dense",
    )(h2, u_tab, vt_tab, *tabs, x_all, g2)


def _transpose_cast_kernel(x_ref, o_ref):
    o_ref[...] = x_ref[...].T.astype(o_ref.dtype)


def _transpose_cast(w, *, tr):
    rows, cols = w.shape
    return pl.pallas_call(
        _transpose_cast_kernel,
        out_shape=jax.ShapeDtypeStruct((rows // tr, cols, tr), BF16),
        grid=(rows // tr,),
        in_specs=[pl.BlockSpec((tr, cols), lambda i: (i, 0))],
        out_specs=pl.BlockSpec((None, cols, tr), lambda i: (i, 0, 0)),
        compiler_params=_cparams(("parallel",)),
        name="transpose_cast",
    )(w)


def _modnorm_kernel(x_ref, g_ref, sc_ref, sh_ref, o_ref):
    x = x_ref[...]
    y = x * lax.rsqrt(jnp.mean(x * x, axis=-1, keepdims=True) + EPS) * g_ref[...]
    o_ref[...] = (y * (1.0 + sc_ref[...]) + sh_ref[...]).astype(o_ref.dtype)


def _modnorm(x_all, n_rows, g, sc, sh, *, bsz, seq, tm, out_dtype):
    d = x_all.shape[1]
    nlb = bsz * seq // tm
    bpb = seq // tm
    mod_spec = pl.BlockSpec((None, 1, d), lambda i: (_row_batch(i, nlb, bpb, bsz), 0, 0))
    return pl.pallas_call(
        _modnorm_kernel,
        out_shape=jax.ShapeDtypeStruct((n_rows, d), out_dtype),
        grid=(n_rows // tm,),
        in_specs=[pl.BlockSpec((tm, d), lambda i: (i, 0)),
                  pl.BlockSpec((1, d), lambda i: (0, 0)), mod_spec, mod_spec],
        out_specs=pl.BlockSpec((tm, d), lambda i: (i, 0)),
        compiler_params=_cparams(("parallel",)),
        name="modnorm",
    )(x_all, g, sc, sh)


def _rope_tables(seq, tm):
    rows = seq // GRID_W
    n_freq = DA_HD // 4
    inv = ROPE_BASE ** (-jnp.arange(n_freq, dtype=F32) / n_freq)
    r = jnp.repeat(jnp.arange(rows, dtype=F32), GRID_W)
    col = jnp.tile(jnp.arange(GRID_W, dtype=F32), rows)
    ang = jnp.concatenate([r[:, None] * inv, col[:, None] * inv], axis=-1)
    cos, sin = jnp.cos(ang), jnp.sin(ang)
    cos_t = jnp.tile(cos, (1, 4))
    sin_t = jnp.tile(jnp.concatenate([-sin, sin], axis=-1), (1, 2))
    ident = jnp.ones((tm, LANES), F32)
    return (jnp.concatenate([cos_t, ident], axis=0),
            jnp.concatenate([sin_t, jnp.zeros((tm, LANES), F32)], axis=0))


def kernel(x, c, ctx, c_ctx, w_mod, b_mod, norm1_g, norm2_g, w_in, da_lam_q1, da_lam_k1, da_lam_q2, da_lam_k2, da_sub_g, ml_conv_w, ml_conv_b, ml_wq, ml_wk, ml_gate_b, ml_norm_g, s5_lam_re, s5_lam_im, s5_log_dt, s5_b_re, s5_b_im, s5_c_re, s5_c_im, s5_d, s5_glu_w, s5_glu_b, w_br, w_out, peer_wq, peer_sub_k, peer_u, peer_v, final_g):
    bsz, seq, d = x.shape
    ctx_len = ctx.shape[1]
    depth = w_in.shape[0]
    t_lat = bsz * seq
    t_ctx = bsz * ctx_len
    t_all = t_lat + t_ctx
    tm = math.gcd(512, math.gcd(seq, t_ctx))
    tm_merge = min(tm, 256)
    tq = min(256, ctx_len)
    nh = ML_HEADS

    x_all = jnp.concatenate([x.reshape(t_lat, d), ctx.reshape(t_ctx, d)], axis=0)
    pad = (-(bsz + 1)) % 8
    c_all = jnp.concatenate([c, c_ctx[None], jnp.zeros((pad, d), F32)], axis=0)
    cos_tab, sin_tab = _rope_tables(seq, tm)

    for l in range(depth):
        need_ctx = l < depth - 1
        n_rows = t_all if need_ctx else t_lat
        lam_init = 0.8 - 0.6 * math.exp(-0.3 * l)

        mods = _modulation(c_all, w_mod[l].astype(BF16), b_mod[l][None])
        sh1, sc1, g1, sh2, sc2, g2 = [mods[:bsz + 1, i * d:(i + 1) * d][:, None, :]
                                      for i in range(N_MOD)]

        wi = w_in[l]
        i_q, i_k, i_v, i_xm, i_vm, i_om, i_g, i_u, i_gate = (
            0, 512, 1024, 1536, 2048, 2560, 3072, 3088, 3600)
        w_main = jnp.concatenate([wi[:, i_gate:], wi[:, :i_g], wi[:, i_u:i_gate]], axis=1).astype(BF16)
        w_gate = jnp.pad(wi[:, i_g:i_u], ((0, 0), (0, LANES - (i_u - i_g)))).astype(BF16)
        n1 = norm1_g[l][None]
        z = _modnorm_matmul(x_all, t_all, n1, sc1, sh1, w_main, bsz=bsz, seq=seq, tm=tm,
                            tn=w_main.shape[1] // 2, out_dtype=BF16)
        zg = _modnorm_matmul(x_all, t_all, n1, sc1, sh1, w_gate, bsz=bsz, seq=seq, tm=tm, tn=LANES)

        qr, kr, vb = _rope_qkv(z, cos_tab, sin_tab, bsz=bsz, seq=seq, tm=tm)
        lam = (jnp.exp(jnp.sum(da_lam_q1[l] * da_lam_k1[l]))
               - jnp.exp(jnp.sum(da_lam_q2[l] * da_lam_k2[l])) + lam_init).reshape(1, 1)
        sub_g = da_sub_g[l][None]
        ya = _diff_attention(qr, kr, vb, lam, sub_g, q_row0=0, n_q=seq,
                             kv_segs=[(0, seq), (t_lat, ctx_len)], bsz=bsz, tq=tq,
                             out_scale=1.0 - lam_init)
        if need_ctx:
            ya_c = _diff_attention(qr, kr, vb, lam, sub_g, q_row0=t_lat, n_q=ctx_len,
                                   kv_segs=[(t_lat, ctx_len)], bsz=bsz, tq=tq,
                                   out_scale=1.0 - lam_init)
            ya = jnp.concatenate([ya, ya_c], axis=0)

        wq_b, wk_b = ml_wq[l].astype(BF16), ml_wk[l].astype(BF16)
        cb = ml_conv_b[l][None]
        q_l, k_l = _ml_prep(z, ml_conv_w[l], cb, wq_b, wk_b, row0=0, seq_len=seq, bsz=bsz)
        q_c, k_c = _ml_prep(z, ml_conv_w[l], cb, wq_b, wk_b, row0=t_lat, seq_len=ctx_len, bsz=bsz)
        q_ml = jnp.concatenate([q_l, q_c], axis=0)
        k_ml = jnp.concatenate([k_l, k_c], axis=0)
        gates = (zg[:, :4 * nh] + ml_gate_b[l]).reshape(t_all, 2, 2 * nh)
        gates_c = jnp.transpose(gates, (1, 0, 2))
        gates_r = jnp.transpose(gates, (1, 2, 0))
        h_dirs = _mlstm_scan(q_ml, k_ml, z, gates_c, gates_r, bsz=bsz, seq=seq, ctx_len=ctx_len)

        tables = _s5_tables(s5_lam_re[l], s5_lam_im[l], s5_log_dt[l], s5_b_re[l], s5_b_im[l],
                            s5_c_re[l], s5_c_im[l], s5_d[l])
        u = z[:, Z_U:Z_U + BRANCH_W]
        y_c, y_l = _s5_mix(_s5_rows(u[t_lat:], bsz, ctx_len), _s5_rows(u[:t_lat], bsz, seq),
                           tables, bsz=bsz)
        ys = _s5_unrows(y_l, bsz, seq)
        if need_ctx:
            ys = jnp.concatenate([ys, _s5_unrows(y_c, bsz, ctx_len)], axis=0)

        x_all = _merge(x_all, ya, h_dirs, z, ys, g1, ml_norm_g[l][None],
                       s5_glu_w[l].astype(BF16), s5_glu_b[l][None], w_br[l].astype(BF16),
                       w_out[l].astype(BF16), n_rows=n_rows, bsz=bsz, seq=seq, tm=tm_merge)

        n2 = norm2_g[l][None]
        wq_p = peer_wq[l].astype(BF16)
        pq, h2 = _modnorm_matmul(x_all, n_rows, n2, sc2, sh2, wq_p, bsz=bsz, seq=seq, tm=tm,
                                 tn=wq_p.shape[1], emit_h=True)
        tabs = _peer_topk(pq, peer_sub_k[l].astype(BF16), n_rows=n_rows, tm=tm)
        te = 1024
        x_all = _peer_dense(h2, peer_u[l].astype(BF16), _transpose_cast(peer_v[l], tr=te), tabs,
                            x_all, g2, n_rows=n_rows, bsz=bsz, seq=seq, tm=tm, te=te)

    ones = jnp.ones((bsz + 1, 1, d), F32)
    out = _modnorm(x_all, t_lat, final_g[None], ones * 0.0, ones * 0.0, bsz=bsz, seq=seq, tm=tm,
                   out_dtype=F32)
    return out.reshape(bsz, seq, d)
```

```python
import functools
import math

import jax
import jax.numpy as jnp
from jax import lax
from jax.experimental import pallas as pl
from jax.experimental.pallas import tpu as pltpu

F32 = jnp.float32
BF16 = jnp.bfloat16

EPS = 1e-6
LOG2E = 1.4426950408889634
N_MOD = 6
BRANCH_W = 512
GRID_W = 64
ROPE_BASE = 10000.0
DA_HEADS = 4
DA_HD = 64
DA_VD = 2 * DA_HD
ML_HEADS = 4
ML_HD = BRANCH_W // ML_HEADS
ML_CHUNK = 128
S5_P = 16
S5_G = BRANCH_W // S5_P
S5_N = 64
S5_T = 16
PEER_HEADS = 8
PEER_NKEYS = 128
PEER_TOPK = 16
LANES = 128
VMEM_LIMIT = 56 * 1024 * 1024

Z_GATE = 0
Z_Q, Z_K, Z_V, Z_XM, Z_VM, Z_OM, Z_U = (3072 + i * BRANCH_W for i in range(7))


def _cparams(sem):
    return pltpu.CompilerParams(dimension_semantics=sem, vmem_limit_bytes=VMEM_LIMIT)


def _nt_dot(a, b):
    return lax.dot_general(a, b, (((1,), (1,)), ((), ())), preferred_element_type=F32)


def _gelu(x):
    return 0.5 * x * (1.0 + lax.erf(x * (2.0 ** -0.5)))


def _log_sigmoid(x):
    return jnp.minimum(x, 0.0) - jnp.log1p(jnp.exp(-jnp.abs(x)))


def _mod_kernel(c_ref, w_ref, b_ref, o_ref):
    c = c_ref[...]
    a = c * jax.nn.sigmoid(c)
    o_ref[...] = jnp.dot(a.astype(BF16), w_ref[...], preferred_element_type=F32) + b_ref[...]


def _modulation(c_all, w, b):
    rows, d = c_all.shape
    n = w.shape[1]
    tn = 1536
    return pl.pallas_call(
        _mod_kernel,
        out_shape=jax.ShapeDtypeStruct((rows, n), F32),
        grid=(n // tn,),
        in_specs=[pl.BlockSpec((rows, d), lambda j: (0, 0)),
                  pl.BlockSpec((d, tn), lambda j: (0, j)),
                  pl.BlockSpec((1, tn), lambda j: (0, j))],
        out_specs=pl.BlockSpec((rows, tn), lambda j: (0, j)),
        compiler_params=_cparams(("parallel",)),
        name="modulation",
    )(c_all, w, b)


def _modnorm_matmul_kernel(x_ref, g_ref, sc_ref, sh_ref, w_ref, o_ref, *h_out):
    x = x_ref[...]
    y = x * lax.rsqrt(jnp.mean(x * x, axis=-1, keepdims=True) + EPS) * g_ref[...]
    hf = y * (1.0 + sc_ref[...]) + sh_ref[...]
    h = hf.astype(BF16)
    o_ref[...] = jnp.dot(h, w_ref[...], preferred_element_type=F32).astype(o_ref.dtype)
    if h_out:
        h_out[0][...] = hf.T.astype(BF16)


def _row_batch(i, n_lat_blocks, blocks_per_batch, bsz):
    return jnp.where(i < n_lat_blocks, i // blocks_per_batch, bsz)


def _modnorm_matmul(x_all, n_rows, g, sc, sh, w, *, bsz, seq, tm, tn, out_dtype=F32, emit_h=False):
    d = x_all.shape[1]
    n = w.shape[1]
    nlb = bsz * seq // tm
    bpb = seq // tm
    assert not emit_h or tn == n
    mod_spec = pl.BlockSpec((None, 1, d), lambda j, i: (_row_batch(i, nlb, bpb, bsz), 0, 0))
    out_shape = [jax.ShapeDtypeStruct((n_rows, n), out_dtype)]
    out_specs = [pl.BlockSpec((tm, tn), lambda j, i: (i, j))]
    if emit_h:
        out_shape.append(jax.ShapeDtypeStruct((d, n_rows), BF16))
        out_specs.append(pl.BlockSpec((d, tm), lambda j, i: (0, i)))
    out = pl.pallas_call(
        _modnorm_matmul_kernel,
        out_shape=tuple(out_shape),
        grid=(n // tn, n_rows // tm),
        in_specs=[pl.BlockSpec((tm, d), lambda j, i: (i, 0)),
                  pl.BlockSpec((1, d), lambda j, i: (0, 0)),
                  mod_spec, mod_spec,
                  pl.BlockSpec((d, tn), lambda j, i: (0, j))],
        out_specs=tuple(out_specs),
        compiler_params=_cparams(("parallel", "parallel")),
        name="modnorm_matmul",
    )(x_all, g, sc, sh, w)
    return out if emit_h else out[0]


def _rope_kernel(q_ref, k_ref, v_ref, cos_ref, sin_ref, qo_ref, ko_ref, vo_ref):
    cos = cos_ref[...]
    sin = sin_ref[...]
    lane = lax.broadcasted_iota(jnp.int32, cos.shape, 1)
    first_half = (lane % DA_HD) < (DA_HD // 2)

    def rope(x):
        partner = jnp.where(first_half, pltpu.roll(x, LANES - DA_HD // 2, 1),
                            pltpu.roll(x, DA_HD // 2, 1))
        return x * cos + partner * sin

    for h in range(DA_HEADS):
        sl = slice(h * LANES, (h + 1) * LANES)
        qo_ref[:, sl] = (rope(q_ref[:, sl].astype(F32)) * (DA_HD ** -0.5 * LOG2E)).astype(BF16)
        ko_ref[:, sl] = rope(k_ref[:, sl].astype(F32)).astype(BF16)
    vo_ref[...] = v_ref[...].astype(BF16)


def _rope_qkv(z, cos_tab, sin_tab, *, bsz, seq, tm):
    rows = z.shape[0]
    nlb = bsz * seq // tm
    bpb = seq // tm
    w = BRANCH_W
    tab_spec = pl.BlockSpec((tm, LANES), lambda i: (jnp.where(i < nlb, i % bpb, bpb), 0))
    out = jax.ShapeDtypeStruct((rows, w), BF16)
    return pl.pallas_call(
        _rope_kernel,
        out_shape=(out, out, out),
        grid=(rows // tm,),
        in_specs=[pl.BlockSpec((tm, w), lambda i: (i, Z_Q // w)),
                  pl.BlockSpec((tm, w), lambda i: (i, Z_K // w)),
                  pl.BlockSpec((tm, w), lambda i: (i, Z_V // w)),
                  tab_spec, tab_spec],
        out_specs=(pl.BlockSpec((tm, w), lambda i: (i, 0)),) * 3,
        compiler_params=_cparams(("parallel",)),
        name="rope_qkv",
    )(z, z, z, cos_tab, sin_tab)


def _attn_kernel(*refs, n_seg, n_sub, out_scale):
    lam_ref, q_ref = refs[0], refs[1]
    k_refs = refs[2:2 + n_seg]
    v_refs = refs[2 + n_seg:2 + 2 * n_seg]
    g_ref, o_ref = refs[2 + 2 * n_seg], refs[3 + 2 * n_seg]
    tq = q_ref.shape[0] // n_sub
    lane = lax.broadcasted_iota(jnp.int32, (tq, LANES), 1)
    scores = []
    for sub in range(n_sub):
        q = q_ref[sub * tq:(sub + 1) * tq, :].astype(F32)
        qq = jnp.concatenate([jnp.where(lane < DA_HD, q, 0.0), jnp.where(lane >= DA_HD, q, 0.0)],
                             axis=0).astype(BF16)
        scores.append([_nt_dot(qq, k_ref[...]) for k_ref in k_refs])
    for sub in range(n_sub):
        rows = slice(sub * tq, (sub + 1) * tq)
        s = scores[sub]
        m = functools.reduce(jnp.maximum, [jnp.max(t, axis=-1, keepdims=True) for t in s])
        p = [jnp.exp2(t - m) for t in s]
        denom = functools.reduce(jnp.add, [jnp.sum(t, axis=-1, keepdims=True) for t in p])
        inv = 1.0 / denom
        c0 = inv[:tq]
        c1 = inv[tq:] * lam_ref[...]
        o = None
        for t, v_ref in zip(p, v_refs):
            w = (t[:tq] * c0 - t[tq:] * c1).astype(BF16)
            part = jnp.dot(w, v_ref[...], preferred_element_type=F32)
            o = part if o is None else o + part
        y = o * lax.rsqrt(jnp.mean(o * o, axis=-1, keepdims=True) + EPS) * g_ref[...]
        o_ref[rows, :] = y * out_scale


def _diff_attention(qr, kr, vb, lam, sub_g, *, q_row0, n_q, kv_segs, bsz, tq, out_scale):
    n_sub = 2 if n_q % (2 * tq) == 0 else 1
    tq = tq * n_sub
    nqb = n_q // tq
    n_seg = len(kv_segs)
    q_blk0 = q_row0 // tq

    def kv_spec(row0, length):
        return pl.BlockSpec((length, LANES), lambda b, h, i: (row0 // length + b, h))

    in_specs = [pl.BlockSpec((1, 1), lambda b, h, i: (0, 0)),
                pl.BlockSpec((tq, LANES), lambda b, h, i: (q_blk0 + b * nqb + i, h))]
    in_specs += [kv_spec(r0, ln) for r0, ln in kv_segs] * 2
    in_specs += [pl.BlockSpec((1, LANES), lambda b, h, i: (0, h))]
    args = [lam, qr] + [kr] * n_seg + [vb] * n_seg + [sub_g]
    return pl.pallas_call(
        functools.partial(_attn_kernel, n_seg=n_seg, n_sub=n_sub, out_scale=out_scale),
        out_shape=jax.ShapeDtypeStruct((bsz * n_q, BRANCH_W), F32),
        grid=(bsz, DA_HEADS, nqb),
        in_specs=in_specs,
        out_specs=pl.BlockSpec((tq, LANES), lambda b, h, i: (b * nqb + i, h)),
        compiler_params=_cparams(("parallel", "parallel", "parallel")),
        name="diff_attention",
    )(*args)


def _ml_prep_kernel(x_ref, cw_ref, cb_ref, wq_ref, wk_ref, q_ref, k_ref):
    x = x_ref[...].astype(F32)
    n = x.shape[0]
    row = lax.broadcasted_iota(jnp.int32, x.shape, 0)
    x_prev = jnp.where(row == 0, 0.0, pltpu.roll(x, 1, 0))
    x_next = jnp.where(row == n - 1, 0.0, pltpu.roll(x, n - 1, 0))
    cw = cw_ref[...]
    y = x_prev * cw[0:1] + x * cw[1:2] + x_next * cw[2:3] + cb_ref[...]
    xc = (y * jax.nn.sigmoid(y)).astype(BF16)
    q_ref[...] = jnp.dot(xc, wq_ref[...], preferred_element_type=F32) * (ML_HD ** -0.5)
    k_ref[...] = jnp.dot(xc, wk_ref[...], preferred_element_type=F32)


def _ml_prep(z, conv_w, conv_b, wq, wk, *, row0, seq_len, bsz):
    blk0 = row0 // seq_len
    out = jax.ShapeDtypeStruct((bsz * seq_len, BRANCH_W), F32)
    head_w = pl.BlockSpec((None, ML_HD, ML_HD), lambda b, h: (h, 0, 0))
    return pl.pallas_call(
        _ml_prep_kernel,
        out_shape=(out, out),
        grid=(bsz, ML_HEADS),
        in_specs=[pl.BlockSpec((seq_len, LANES), lambda b, h: (blk0 + b, Z_XM // LANES + h)),
                  pl.BlockSpec((3, LANES), lambda b, h: (0, h)),
                  pl.BlockSpec((1, LANES), lambda b, h: (0, h)),
                  head_w, head_w],
        out_specs=(pl.BlockSpec((seq_len, LANES), lambda b, h: (b, h)),) * 2,
        compiler_params=_cparams(("parallel", "parallel")),
        name="mlstm_prep",
    )(z, conv_w, conv_b, wq, wk)


def _mlstm_kernel(qf_ref, kf_ref, vf_ref, gcf_ref, grf_ref, qb_ref, kb_ref, vb_ref, gcb_ref, grb_ref,
                  hf_ref, hb_ref, c_ref, n_ref, m_ref):
    @pl.when(pl.program_id(1) == 0)
    def _():
        c_ref[...] = jnp.zeros_like(c_ref)
        n_ref[...] = jnp.zeros_like(n_ref)
        m_ref[...] = jnp.zeros_like(m_ref)

    t_idx = lax.broadcasted_iota(jnp.int32, (ML_CHUNK, ML_CHUNK), 0)
    s_idx = lax.broadcasted_iota(jnp.int32, (ML_CHUNK, ML_CHUNK), 1)
    nh = ML_HEADS
    dirs = ((qf_ref, kf_ref, vf_ref, gcf_ref, grf_ref, hf_ref),
            (qb_ref, kb_ref, vb_ref, gcb_ref, grb_ref, hb_ref))
    units = []
    for d, (q_ref, k_ref, v_ref, gc_ref, gr_ref, h_ref) in enumerate(dirs):
        mask = s_idx <= t_idx if d == 0 else s_idx >= t_idx
        mask_f = mask.astype(F32)
        gc = gc_ref[...]
        gr = gr_ref[...]
        lf_c = _log_sigmoid(gc)
        lf_r = _log_sigmoid(gr)
        b_c = jnp.dot(mask_f, lf_c, preferred_element_type=F32, precision=lax.Precision.HIGHEST)
        b_r = lax.dot_general(lf_r, mask_f, (((1,), (1,)), ((), ())), preferred_element_type=F32,
                              precision=lax.Precision.HIGHEST)
        b_end_all = jnp.sum(lf_c, axis=0, keepdims=True)

        for h in range(nh):
            sl = slice(h * ML_HD, (h + 1) * ML_HD)
            q = q_ref[:, sl].astype(BF16)
            k = k_ref[:, sl].astype(BF16)
            v = v_ref[:, sl].astype(F32)
            li_c = gc[:, h:h + 1]
            li_r = gr[h:h + 1, :]
            bc = b_c[:, nh + h:nh + h + 1]
            br = b_r[nh + h:nh + h + 1, :]
            b_end = b_end_all[:, nh + h:nh + h + 1]
            m_prev = m_ref[d, h]
            c_prev = c_ref[d, h]
            n_prev = n_ref[d, h]
            g_c = b_end - bc + li_c
            g_r = b_end - br + li_r
            m_new = jnp.maximum(b_end + m_prev, jnp.max(g_r, axis=-1, keepdims=True))
            wg = jnp.exp(g_c - m_new)
            decay = jnp.exp(b_end + m_prev - m_new)
            vw = (v * wg).astype(BF16)
            units.append(dict(
                d=d, h=h, sl=sl, h_ref=h_ref, q=q, v=v, mask=mask, m_prev=m_prev, n_prev=n_prev,
                bc=bc, br=br, li_r=li_r,
                qk=_nt_dot(q, k), qc=_nt_dot(q, c_prev.astype(BF16)),
                c_new=decay * c_prev + lax.dot_general(
                    vw, k, (((0,), (0,)), ((), ())), preferred_element_type=F32),
                n_new=decay * n_prev + jnp.sum(wg * k.astype(F32), axis=0, keepdims=True),
                m_new=m_new))

    for u in units:
        dmat = jnp.where(u["mask"], u["bc"] - u["br"] + u["li_r"], -jnp.inf)
        inter = u["bc"] + u["m_prev"]
        m_t = jnp.maximum(inter, jnp.max(dmat, axis=-1, keepdims=True))
        s = u["qk"] * jnp.exp(dmat - m_t)
        w_prev = jnp.exp(inter - m_t)
        num = (jnp.dot(s.astype(BF16), u["v"].astype(BF16), preferred_element_type=F32)
               + w_prev * u["qc"])
        den = (jnp.sum(s, axis=-1, keepdims=True)
               + w_prev * jnp.sum(u["q"].astype(F32) * u["n_prev"], axis=-1, keepdims=True))
        u["h_ref"][:, u["sl"]] = num / jnp.maximum(jnp.abs(den), jnp.exp(-m_t))
        c_ref[u["d"], u["h"]] = u["c_new"]
        n_ref[u["d"], u["h"]] = u["n_new"]
        m_ref[u["d"], u["h"]] = u["m_new"]


def _mlstm_scan(q_all, k_all, z, gates_c, gates_r, *, bsz, seq, ctx_len):
    rows = q_all.shape[0]
    ncl = seq // ML_CHUNK
    ncc = ctx_len // ML_CHUNK
    lat_blocks = bsz * ncl
    nh = ML_HEADS

    def blk(d, b, s):
        jc = s if d == 0 else ncc - 1 - s
        jl = s - ncc if d == 0 else ncl - 1 - (s - ncc)
        return jnp.where(s < ncc, lat_blocks + b * ncc + jc, b * ncl + jl)

    def dir_specs(d):
        return [pl.BlockSpec((ML_CHUNK, BRANCH_W), lambda b, s: (blk(d, b, s), 0)),
                pl.BlockSpec((ML_CHUNK, BRANCH_W), lambda b, s: (blk(d, b, s), 0)),
                pl.BlockSpec((ML_CHUNK, BRANCH_W), lambda b, s: (blk(d, b, s), Z_VM // BRANCH_W)),
                pl.BlockSpec((None, ML_CHUNK, 2 * nh), lambda b, s: (d, blk(d, b, s), 0)),
                pl.BlockSpec((None, 2 * nh, ML_CHUNK), lambda b, s: (d, 0, blk(d, b, s)))]

    out = jax.ShapeDtypeStruct((rows, BRANCH_W), F32)
    args = (q_all, k_all, z, gates_c, gates_r)
    return pl.pallas_call(
        _mlstm_kernel,
        out_shape=(out, out),
        grid=(bsz, ncc + ncl),
        in_specs=dir_specs(0) + dir_specs(1),
        out_specs=(pl.BlockSpec((ML_CHUNK, BRANCH_W), lambda b, s: (blk(0, b, s), 0)),
                   pl.BlockSpec((ML_CHUNK, BRANCH_W), lambda b, s: (blk(1, b, s), 0))),
        scratch_shapes=[pltpu.VMEM((2, nh, ML_HD, ML_HD), F32),
                        pltpu.VMEM((2, nh, 1, ML_HD), F32),
                        pltpu.VMEM((2, nh, 1, 1), F32)],
        compiler_params=_cparams(("parallel", "arbitrary")),
        name="mlstm_scan",
    )(*args, *args)


def _s5_kernel(uc_ref, ul_ref, m_ref, w_ref, v_ref, a1_ref, a2_ref, yc_ref, yl_ref,
               ec_ref, el_ref, xc_ref, xl_ref, *, bsz):
    w = w_ref[...]
    ec_ref[...] = jnp.dot(uc_ref[...], w, preferred_element_type=F32)
    el_ref[...] = jnp.dot(ul_ref[...], w, preferred_element_type=F32)
    a1 = a1_ref[...]
    a2 = a2_ref[...]
    half = LANES

    def advance(x, e, a1d, a2d):
        return a1d * x + a2d * pltpu.roll(x, S5_N, 1) + e

    def scan(e_ref, x_ref, state):
        n_chunks = e_ref.shape[0] // bsz

        def body(j, st):
            xf, xb = st
            rf = pl.multiple_of(j * bsz, bsz)
            rb = pl.multiple_of((n_chunks - 1 - j) * bsz, bsz)
            x_ref[pl.ds(rf, bsz), 0:half] = xf
            x_ref[pl.ds(rb, bsz), half:2 * half] = xb
            xf = advance(xf, e_ref[pl.ds(rf, bsz), 0:half], a1[:, 0:half], a2[:, 0:half])
            xb = advance(xb, e_ref[pl.ds(rb, bsz), half:2 * half], a1[:, half:], a2[:, half:])
            return xf, xb

        return lax.fori_loop(0, n_chunks, body, state)

    zero = jnp.zeros((bsz, half), F32)
    state = scan(ec_ref, xc_ref, (zero, zero))
    scan(el_ref, xl_ref, state)
    mm = m_ref[...]
    vv = v_ref[...]
    yc_ref[...] = (jnp.dot(uc_ref[...], mm, preferred_element_type=F32)
                   + jnp.dot(xc_ref[...].astype(BF16), vv, preferred_element_type=F32)
                   ).astype(yc_ref.dtype)
    yl_ref[...] = (jnp.dot(ul_ref[...], mm, preferred_element_type=F32)
                   + jnp.dot(xl_ref[...].astype(BF16), vv, preferred_element_type=F32)
                   ).astype(yl_ref.dtype)


def _s5_tables(lam_re, lam_im, log_dt, b_re, b_im, c_re, c_im, d_skip):
    t = S5_T
    dt = jnp.exp(log_dt)[..., None]
    den = lam_re * lam_re + lam_im * lam_im
    kk = jnp.arange(t + 1, dtype=F32)[:, None, None, None]
    mag = jnp.exp(kk * lam_re * dt)
    ang = kk * lam_im * dt
    p_re, p_im = mag * jnp.cos(ang), mag * jnp.sin(ang)
    ab_re, ab_im = p_re[1], p_im[1]
    z_re = ((ab_re - 1.0) * lam_re + ab_im * lam_im) / den
    z_im = (ab_im * lam_re - (ab_re - 1.0) * lam_im) / den
    bb_re = z_re[..., None] * b_re - z_im[..., None] * b_im
    bb_im = z_re[..., None] * b_im + z_im[..., None] * b_re
    ab_k_re = p_re[..., None] * bb_re - p_im[..., None] * bb_im
    ab_k_im = p_re[..., None] * bb_im + p_im[..., None] * bb_re
    taps = (jnp.einsum('rgpn,krgnq->krgpq', c_re, ab_k_re[:t])
            - jnp.einsum('rgpn,krgnq->krgpq', c_im, ab_k_im[:t]))
    s_i = jnp.arange(t)[:, None]
    t_i = jnp.arange(t)[None, :]
    lag_f = jnp.clip(t_i - s_i, 0, t - 1)
    lag_b = jnp.clip(s_i - t_i, 0, t - 1)
    tf = jnp.where((t_i >= s_i)[..., None, None, None], taps[lag_f, 0], 0.0)
    tb = jnp.where((s_i >= t_i)[..., None, None, None], taps[lag_b, 1], 0.0)
    skip = (jnp.eye(t)[:, :, None, None, None]
            * (jnp.eye(S5_P)[None, None, None] * d_skip.reshape(S5_G, S5_P, 1)[None, None]))
    m = jnp.transpose(tf + tb + skip, (2, 0, 4, 1, 3)).reshape(S5_G, t * S5_P, t * S5_P)
    wf_re = jnp.transpose(ab_k_re[:t, 0][::-1], (1, 0, 3, 2)).reshape(S5_G, t * S5_P, S5_N)
    wf_im = jnp.transpose(ab_k_im[:t, 0][::-1], (1, 0, 3, 2)).reshape(S5_G, t * S5_P, S5_N)
    wb_re = jnp.transpose(ab_k_re[:t, 1], (1, 0, 3, 2)).reshape(S5_G, t * S5_P, S5_N)
    wb_im = jnp.transpose(ab_k_im[:t, 1], (1, 0, 3, 2)).reshape(S5_G, t * S5_P, S5_N)
    w = jnp.concatenate([wf_re, wf_im, wb_re, wb_im], axis=-1)
    ca_re = (c_re[None] * p_re[:, :, :, None, :] - c_im[None] * p_im[:, :, :, None, :])
    ca_im = (c_re[None] * p_im[:, :, :, None, :] + c_im[None] * p_re[:, :, :, None, :])

    def to_rows(a):
        return jnp.transpose(a, (1, 3, 0, 2)).reshape(S5_G, S5_N, t * S5_P)

    v = jnp.concatenate([to_rows(ca_re[1:t + 1, 0]), to_rows(-ca_im[1:t + 1, 0]),
                         to_rows(ca_re[1:t + 1, 1][::-1]), to_rows(-ca_im[1:t + 1, 1][::-1])], axis=1)
    at_re, at_im = p_re[t], p_im[t]
    a1 = jnp.concatenate([at_re[0], at_re[0], at_re[1], at_re[1]], axis=-1)[:, None, :]
    a2 = jnp.concatenate([-at_im[0], at_im[0], -at_im[1], at_im[1]], axis=-1)[:, None, :]
    return m.astype(BF16), w.astype(BF16), v.astype(BF16), a1, a2


def _s5_rows(u, bsz, length):
    nc = length // S5_T
    u = u.reshape(bsz, nc, S5_T, S5_G, S5_P)
    return jnp.transpose(u, (3, 1, 0, 2, 4)).reshape(S5_G, nc * bsz, S5_T * S5_P)


def _s5_unrows(y, bsz, length):
    nc = length // S5_T
    y = y.reshape(S5_G, nc, bsz, S5_T, S5_P)
    return jnp.transpose(y, (2, 1, 3, 0, 4)).reshape(bsz * length, BRANCH_W)


def _s5_mix(u_ctx, u_lat, tables, *, bsz):
    m, w, v, a1, a2 = tables
    rc, rl = u_ctx.shape[1], u_lat.shape[1]
    wd = S5_T * S5_P

    def rows_spec(r):
        return pl.BlockSpec((None, r, wd), lambda g: (g, 0, 0))

    sq = pl.BlockSpec((None, wd, wd), lambda g: (g, 0, 0))
    vec = pl.BlockSpec((None, 1, wd), lambda g: (g, 0, 0))
    return pl.pallas_call(
        functools.partial(_s5_kernel, bsz=bsz),
        out_shape=(jax.ShapeDtypeStruct((S5_G, rc, wd), BF16),
                   jax.ShapeDtypeStruct((S5_G, rl, wd), BF16)),
        grid=(S5_G,),
        in_specs=[rows_spec(rc), rows_spec(rl), sq, sq, sq, vec, vec],
        out_specs=(rows_spec(rc), rows_spec(rl)),
        scratch_shapes=[pltpu.VMEM((rc, wd), F32), pltpu.VMEM((rl, wd), F32),
                        pltpu.VMEM((rc, wd), F32), pltpu.VMEM((rl, wd), F32)],
        compiler_params=_cparams(("parallel",)),
        name="s5_mix",
    )(u_ctx, u_lat, m, w, v, a1, a2)


def _merge_kernel(x_ref, ya_ref, hf_ref, hb_ref, om_ref, ys_ref, gate_ref, g1_ref, mlg_ref,
                  gluw_ref, glub_ref, wbr_ref, wout_ref, o_ref):
    hsum = hf_ref[...] + hb_ref[...]
    og = jax.nn.sigmoid(om_ref[...].astype(F32))
    mlg = mlg_ref[...]
    yb_parts = []
    for h in range(ML_HEADS):
        sl = slice(h * ML_HD, (h + 1) * ML_HD)
        hh = hsum[:, sl]
        hn = hh * lax.rsqrt(jnp.mean(hh * hh, axis=-1, keepdims=True) + EPS) * mlg[:, sl]
        yb_parts.append(hn * og[:, sl])
    yb = jnp.concatenate(yb_parts, axis=-1)
    gl = _gelu(ys_ref[...].astype(F32))
    ys = gl * jax.nn.sigmoid(
        jnp.dot(gl.astype(BF16), gluw_ref[...], preferred_element_type=F32) + glub_ref[...])
    d = o_ref.shape[1]
    merged = None
    for r, y in enumerate((ya_ref[...], yb, ys)):
        proj = jnp.dot(y.astype(BF16), wbr_ref[r], preferred_element_type=F32)
        term = jax.nn.sigmoid(gate_ref[:, r * d:(r + 1) * d].astype(F32)) * proj
        merged = term if merged is None else merged + term
    y = jnp.dot(merged.astype(BF16), wout_ref[...], preferred_element_type=F32)
    o_ref[...] = x_ref[...] + g1_ref[...] * y


def _merge(x_all, ya, h_dirs, z, ys, g1, ml_norm_g, glu_w, glu_b, w_br, w_out, *, n_rows, bsz, seq, tm):
    d = x_all.shape[1]
    nlb = bsz * seq // tm
    bpb = seq // tm
    w = BRANCH_W

    def rows(width, col_blk=0):
        return pl.BlockSpec((tm, width), lambda i: (i, col_blk))

    def full(shape):
        return pl.BlockSpec(shape, lambda i: (0,) * len(shape))

    return pl.pallas_call(
        _merge_kernel,
        out_shape=jax.ShapeDtypeStruct((n_rows, d), F32),
        grid=(n_rows // tm,),
        in_specs=[rows(d), rows(w), rows(w), rows(w),
                  rows(w, Z_OM // w), rows(w),
                  rows(3 * d, Z_GATE // (3 * d)),
                  pl.BlockSpec((None, 1, d), lambda i: (_row_batch(i, nlb, bpb, bsz), 0, 0)),
                  full((1, w)), full((w, w)), full((1, w)), full((3, w, d)), full((d, d))],
        out_specs=rows(d),
        compiler_params=_cparams(("parallel",)),
        name="merge",
    )(x_all, ya, h_dirs[0], h_dirs[1], z, ys, z, g1, ml_norm_g, glu_w, glu_b, w_br, w_out)


NOT_TOP = 127.0


def _take_top(s, n_take, vals_ref, want_rank):
    rows = s.shape[0]
    ridx = lax.broadcasted_iota(jnp.int32, s.shape, 0).astype(F32)
    rank = jnp.full(s.shape, NOT_TOP, F32) if want_rank else None
    for i in range(n_take):
        mx = jnp.max(s, axis=0, keepdims=True)
        first = jnp.min(jnp.where(s == mx, ridx, float(rows)), axis=0, keepdims=True)
        hit = ridx == first
        s = jnp.where(hit, -jnp.inf, s)
        if want_rank:
            rank = jnp.where(hit, float(i), rank)
        vals_ref[i:i + 1, :] = mx
    return s, rank


def _peer_topk_kernel(q_ref, k1_ref, k2_ref, e1_ref, cnt1_ref, e2_ref, rank2_ref,
                      v1_ref, v2_ref, top_ref, cnt_ref):
    k = PEER_TOPK
    q = q_ref[...].astype(BF16)
    s1 = _nt_dot(k1_ref[...], q[:, :LANES])
    s2 = _nt_dot(k2_ref[...], q[:, LANES:])
    _, rank1 = _take_top(s1, k, v1_ref, True)
    _, rank2 = _take_top(s2, k, v2_ref, True)
    v2_head = v2_ref[0:8, :]
    jrow = lax.broadcasted_iota(jnp.int32, v2_head.shape, 0)
    pieces = [v1_ref[0:1, :] + v2_ref[...]]
    for i in range(1, 8):
        pieces.append(jnp.where(jrow < k // (i + 1), v1_ref[i:i + 1, :] + v2_head, -jnp.inf))
    pieces.append(v1_ref[8:16, :] + v2_ref[0:1, :])
    cand = jnp.concatenate(pieces, axis=0)
    cand_left, _ = _take_top(cand, k, top_ref, False)
    picked = jnp.where(cand_left != cand, 1.0, 0.0)
    cnt_ref[0:1, :] = jnp.sum(picked[0:16], axis=0, keepdims=True)
    for i in range(1, 8):
        cnt_ref[i:i + 1, :] = jnp.sum(picked[8 + 8 * i:16 + 8 * i], axis=0, keepdims=True)
    cnt_ref[8:16, :] = picked[72:80]
    cnt1 = jnp.zeros(s1.shape, F32)
    for i in range(k):
        cnt1 = jnp.where(rank1 == float(i), cnt_ref[i:i + 1, :], cnt1)
    top = top_ref[...]
    zsum = jnp.sum(jnp.exp(top - top[0:1, :]), axis=0, keepdims=True)
    e1_ref[...] = jnp.where(rank1 < k, jnp.exp(s1 - v1_ref[0:1, :]), 0.0) / zsum
    cnt1_ref[...] = cnt1
    e2_ref[...] = jnp.where(rank2 < k, jnp.exp(s2 - v2_ref[0:1, :]), 0.0).astype(BF16)
    rank2_ref[...] = rank2.astype(BF16)


def _peer_topk(q, sub_k, *, n_rows, tm):
    nk = PEER_NKEYS
    row_tab = jax.ShapeDtypeStruct((n_rows // tm, PEER_HEADS, nk, tm), F32)
    tile_tab = jax.ShapeDtypeStruct((n_rows // tm, PEER_HEADS, nk, tm), BF16)
    tab_spec = pl.BlockSpec((None, None, nk, tm), lambda i, h: (i, h, 0, 0))
    return pl.pallas_call(
        _peer_topk_kernel,
        out_shape=(row_tab, row_tab, tile_tab, tile_tab),
        grid=(n_rows // tm, PEER_HEADS),
        in_specs=[pl.BlockSpec((tm, 2 * LANES), lambda i, h: (i, h)),
                  pl.BlockSpec((None, nk, LANES), lambda i, h: (0, 0, 0)),
                  pl.BlockSpec((None, nk, LANES), lambda i, h: (1, 0, 0))],
        out_specs=(tab_spec,) * 4,
        scratch_shapes=[pltpu.VMEM((PEER_TOPK, tm), F32)] * 4,
        compiler_params=_cparams(("parallel", "parallel")),
        name="peer_topk",
    )(q, sub_k, sub_k)


GATE_KEYS = 2
KEY_GROUP = 8


def _peer_dense_kernel(ht_ref, u_ref, vt_ref, e1_ref, cnt1_ref, e2_ref, rank2_ref,
                       x_ref, g2_ref, o_ref, acc_ref, act_ref, p_ref, *, a_per_blk):
    j = pl.program_id(1)
    n_blk = pl.num_programs(1) - 1
    cur = j % 2
    tm = ht_ref.shape[1]
    nk = PEER_NKEYS

    @pl.when(j == 0)
    def _():
        acc_ref[...] = jnp.zeros_like(acc_ref)
        p_ref[1] = jnp.zeros(p_ref.shape[1:], p_ref.dtype)

    @pl.when(j < n_blk)
    def _():
        act_ref[...] = _gelu(jnp.dot(u_ref[...], ht_ref[...], preferred_element_type=F32)
                             ).astype(BF16)
        acc_ref[...] += jnp.dot(vt_ref[...], p_ref[1 - cur], preferred_element_type=F32)
        a0 = pl.multiple_of(j * a_per_blk, KEY_GROUP)

        def row_bf16(ref, h, i, ls):
            grp = (i // KEY_GROUP) * KEY_GROUP
            w = ref[h, pl.ds(a0 + grp, KEY_GROUP), ls][i % KEY_GROUP:i % KEY_GROUP + 1]
            return jnp.broadcast_to(w, (nk, LANES)).astype(BF16)

        for lt in range(tm // LANES):
            ls = slice(lt * LANES, (lt + 1) * LANES)
            for i0 in range(0, a_per_blk, GATE_KEYS):
                gates = [None] * GATE_KEYS
                for h in range(PEER_HEADS):
                    e2 = e2_ref[h, :, ls]
                    rank2 = rank2_ref[h, :, ls]
                    for ii in range(GATE_KEYS):
                        lead = row_bf16(cnt1_ref, h, i0 + ii, ls) - rank2
                        term = jnp.minimum(jnp.maximum(lead, 0.0),
                                           row_bf16(e1_ref, h, i0 + ii, ls)) * e2
                        gates[ii] = term if gates[ii] is None else gates[ii] + term
                for ii in range(GATE_KEYS):
                    r0 = (i0 + ii) * nk
                    p_ref[cur, r0:r0 + nk, ls] = gates[ii] * act_ref[r0:r0 + nk, ls]

    @pl.when(j == n_blk)
    def _():
        acc = acc_ref[...] + jnp.dot(vt_ref[...], p_ref[1 - cur], preferred_element_type=F32)
        o_ref[...] = x_ref[...] + g2_ref[...] * acc.T


def _peer_dense(h2, u_tab, vt_tab, tabs, x_all, g2, *, n_rows, bsz, seq, tm, te):
    d = x_all.shape[1]
    n_exp = u_tab.shape[0]
    nlb = bsz * seq // tm
    bpb = seq // tm
    nk = PEER_NKEYS
    n_blk = n_exp // te
    assert (te // nk) % KEY_GROUP == 0
    tab_spec = pl.BlockSpec((None, PEER_HEADS, nk, tm), lambda i, j: (i, 0, 0, 0))
    return pl.pallas_call(
        functools.partial(_peer_dense_kernel, a_per_blk=te // nk),
        out_shape=jax.ShapeDtypeStruct((n_rows, d), F32),
        grid=(n_rows // tm, n_blk + 1),
        in_specs=[pl.BlockSpec((d, tm), lambda i, j: (0, i)),
                  pl.BlockSpec((te, d), lambda i, j: (jnp.minimum(j, n_blk - 1), 0)),
                  pl.BlockSpec((None, d, te), lambda i, j: (jnp.maximum(j - 1, 0), 0, 0)),
                  tab_spec, tab_spec, tab_spec, tab_spec,
                  pl.BlockSpec((tm, d), lambda i, j: (i, 0)),
                  pl.BlockSpec((None, 1, d), lambda i, j: (_row_batch(i, nlb, bpb, bsz), 0, 0))],
        out_specs=pl.BlockSpec((tm, d), lambda i, j: (i, 0)),
        scratch_shapes=[pltpu.VMEM((d, tm), F32), pltpu.VMEM((te, tm), BF16),
                        pltpu.VMEM((2, te, tm), BF16)],
        compiler_params=_cparams(("parallel", "arbitrary")),
        name="peer_dense",
    )(h2, u_tab, vt_tab, *tabs, x_all, g2)


def _transpose_cast_kernel(x_ref, o_ref):
    o_ref[...] = x_ref[...].T.astype(o_ref.dtype)


def _transpose_cast(w, *, tr):
    rows, cols = w.shape
    return pl.pallas_call(
        _transpose_cast_kernel,
        out_shape=jax.ShapeDtypeStruct((rows // tr, cols, tr), BF16),
        grid=(rows // tr,),
        in_specs=[pl.BlockSpec((tr, cols), lambda i: (i, 0))],
        out_specs=pl.BlockSpec((None, cols, tr), lambda i: (i, 0, 0)),
        compiler_params=_cparams(("parallel",)),
        name="transpose_cast",
    )(w)


def _modnorm_kernel(x_ref, g_ref, sc_ref, sh_ref, o_ref):
    x = x_ref[...]
    y = x * lax.rsqrt(jnp.mean(x * x, axis=-1, keepdims=True) + EPS) * g_ref[...]
    o_ref[...] = (y * (1.0 + sc_ref[...]) + sh_ref[...]).astype(o_ref.dtype)


def _modnorm(x_all, n_rows, g, sc, sh, *, bsz, seq, tm, out_dtype):
    d = x_all.shape[1]
    nlb = bsz * seq // tm
    bpb = seq // tm
    mod_spec = pl.BlockSpec((None, 1, d), lambda i: (_row_batch(i, nlb, bpb, bsz), 0, 0))
    return pl.pallas_call(
        _modnorm_kernel,
        out_shape=jax.ShapeDtypeStruct((n_rows, d), out_dtype),
        grid=(n_rows // tm,),
        in_specs=[pl.BlockSpec((tm, d), lambda i: (i, 0)),
                  pl.BlockSpec((1, d), lambda i: (0, 0)), mod_spec, mod_spec],
        out_specs=pl.BlockSpec((tm, d), lambda i: (i, 0)),
        compiler_params=_cparams(("parallel",)),
        name="modnorm",
    )(x_all, g, sc, sh)


def _rope_tables(seq, tm):
    rows = seq // GRID_W
    n_freq = DA_HD // 4
    inv = ROPE_BASE ** (-jnp.arange(n_freq, dtype=F32) / n_freq)
    r = jnp.repeat(jnp.arange(rows, dtype=F32), GRID_W)
    col = jnp.tile(jnp.arange(GRID_W, dtype=F32), rows)
    ang = jnp.concatenate([r[:, None] * inv, col[:, None] * inv], axis=-1)
    cos, sin = jnp.cos(ang), jnp.sin(ang)
    cos_t = jnp.tile(cos, (1, 4))
    sin_t = jnp.tile(jnp.concatenate([-sin, sin], axis=-1), (1, 2))
    ident = jnp.ones((tm, LANES), F32)
    return (jnp.concatenate([cos_t, ident], axis=0),
            jnp.concatenate([sin_t, jnp.zeros((tm, LANES), F32)], axis=0))


def kernel(x, c, ctx, c_ctx, w_mod, b_mod, norm1_g, norm2_g, w_in, da_lam_q1, da_lam_k1, da_lam_q2, da_lam_k2, da_sub_g, ml_conv_w, ml_conv_b, ml_wq, ml_wk, ml_gate_b, ml_norm_g, s5_lam_re, s5_lam_im, s5_log_dt, s5_b_re, s5_b_im, s5_c_re, s5_c_im, s5_d, s5_glu_w, s5_glu_b, w_br, w_out, peer_wq, peer_sub_k, peer_u, peer_v, final_g):
    bsz, seq, d = x.shape
    ctx_len = ctx.shape[1]
    depth = w_in.shape[0]
    t_lat = bsz * seq
    t_ctx = bsz * ctx_len
    t_all = t_lat + t_ctx
    tm = math.gcd(512, math.gcd(seq, t_ctx))
    tm_merge = min(tm, 256)
    tq = min(256, ctx_len)
    nh = ML_HEADS

    x_all = jnp.concatenate([x.reshape(t_lat, d), ctx.reshape(t_ctx, d)], axis=0)
    pad = (-(bsz + 1)) % 8
    c_all = jnp.concatenate([c, c_ctx[None], jnp.zeros((pad, d), F32)], axis=0)
    cos_tab, sin_tab = _rope_tables(seq, tm)

    for l in range(depth):
        need_ctx = l < depth - 1
        n_rows = t_all if need_ctx else t_lat
        lam_init = 0.8 - 0.6 * math.exp(-0.3 * l)

        mods = _modulation(c_all, w_mod[l].astype(BF16), b_mod[l][None])
        sh1, sc1, g1, sh2, sc2, g2 = [mods[:bsz + 1, i * d:(i + 1) * d][:, None, :]
                                      for i in range(N_MOD)]

        wi = w_in[l]
        i_q, i_k, i_v, i_xm, i_vm, i_om, i_g, i_u, i_gate = (
            0, 512, 1024, 1536, 2048, 2560, 3072, 3088, 3600)
        w_main = jnp.concatenate([wi[:, i_gate:], wi[:, :i_g], wi[:, i_u:i_gate]], axis=1).astype(BF16)
        w_gate = jnp.pad(wi[:, i_g:i_u], ((0, 0), (0, LANES - (i_u - i_g)))).astype(BF16)
        n1 = norm1_g[l][None]
        z = _modnorm_matmul(x_all, t_all, n1, sc1, sh1, w_main, bsz=bsz, seq=seq, tm=tm,
                            tn=w_main.shape[1] // 2, out_dtype=BF16)
        zg = _modnorm_matmul(x_all, t_all, n1, sc1, sh1, w_gate, bsz=bsz, seq=seq, tm=tm, tn=LANES)

        qr, kr, vb = _rope_qkv(z, cos_tab, sin_tab, bsz=bsz, seq=seq, tm=tm)
        lam = (jnp.exp(jnp.sum(da_lam_q1[l] * da_lam_k1[l]))
               - jnp.exp(jnp.sum(da_lam_q2[l] * da_lam_k2[l])) + lam_init).reshape(1, 1)
        sub_g = da_sub_g[l][None]
        ya = _diff_attention(qr, kr, vb, lam, sub_g, q_row0=0, n_q=seq,
                             kv_segs=[(0, seq), (t_lat, ctx_len)], bsz=bsz, tq=tq,
                             out_scale=1.0 - lam_init)
        if need_ctx:
            ya_c = _diff_attention(qr, kr, vb, lam, sub_g, q_row0=t_lat, n_q=ctx_len,
                                   kv_segs=[(t_lat, ctx_len)], bsz=bsz, tq=tq,
                                   out_scale=1.0 - lam_init)
            ya = jnp.concatenate([ya, ya_c], axis=0)

        wq_b, wk_b = ml_wq[l].astype(BF16), ml_wk[l].astype(BF16)
        cb = ml_conv_b[l][None]
        q_l, k_l = _ml_prep(z, ml_conv_w[l], cb, wq_b, wk_b, row0=0, seq_len=seq, bsz=bsz)
        q_c, k_c = _ml_prep(z, ml_conv_w[l], cb, wq_b, wk_b, row0=t_lat, seq_len=ctx_len, bsz=bsz)
        q_ml = jnp.concatenate([q_l, q_c], axis=0)
        k_ml = jnp.concatenate([k_l, k_c], axis=0)
        gates = (zg[:, :4 * nh] + ml_gate_b[l]).reshape(t_all, 2, 2 * nh)
        gates_c = jnp.transpose(gates, (1, 0, 2))
        gates_r = jnp.transpose(gates, (1, 2, 0))
        h_dirs = _mlstm_scan(q_ml, k_ml, z, gates_c, gates_r, bsz=bsz, seq=seq, ctx_len=ctx_len)

        tables = _s5_tables(s5_lam_re[l], s5_lam_im[l], s5_log_dt[l], s5_b_re[l], s5_b_im[l],
                            s5_c_re[l], s5_c_im[l], s5_d[l])
        u = z[:, Z_U:Z_U + BRANCH_W]
        y_c, y_l = _s5_mix(_s5_rows(u[t_lat:], bsz, ctx_len), _s5_rows(u[:t_lat], bsz, seq),
                           tables, bsz=bsz)
        ys = _s5_unrows(y_l, bsz, seq)
        if need_ctx:
            ys = jnp.concatenate([ys, _s5_unrows(y_c, bsz, ctx_len)], axis=0)

        x_all = _merge(x_all, ya, h_dirs, z, ys, g1, ml_norm_g[l][None],
                       s5_glu_w[l].astype(BF16), s5_glu_b[l][None], w_br[l].astype(BF16),
                       w_out[l].astype(BF16), n_rows=n_rows, bsz=bsz, seq=seq, tm=tm_merge)

        n2 = norm2_g[l][None]
        wq_p = peer_wq[l].astype(BF16)
        pq, h2t = _modnorm_matmul(x_all, n_rows, n2, sc2, sh2, wq_p, bsz=bsz, seq=seq, tm=tm,
                                  tn=wq_p.shape[1], emit_h=True)
        tabs = _peer_topk(pq, peer_sub_k[l].astype(BF16), n_rows=n_rows, tm=tm)
        te = 2048
        x_all = _peer_dense(h2t, peer_u[l].astype(BF16), _transpose_cast(peer_v[l], tr=te),
                            tabs, x_all, g2, n_rows=n_rows, bsz=bsz, seq=seq, tm=tm, te=te)

    ones = jnp.ones((bsz + 1, 1, d), F32)
    out = _modnorm(x_all, t_lat, final_g[None], ones * 0.0, ones * 0.0, bsz=bsz, seq=seq, tm=tm,
                   out_dtype=F32)
    return out.reshape(bsz, seq, d)
```

```python
import functools
import math

import jax
import jax.numpy as jnp
from jax import lax
from jax.experimental import pallas as pl
from jax.experimental.pallas import tpu as pltpu

F32 = jnp.float32
BF16 = jnp.bfloat16

EPS = 1e-6
LOG2E = 1.4426950408889634
N_MOD = 6
BRANCH_W = 512
GRID_W = 64
ROPE_BASE = 10000.0
DA_HEADS = 4
DA_HD = 64
DA_VD = 2 * DA_HD
ML_HEADS = 4
ML_HD = BRANCH_W // ML_HEADS
ML_CHUNK = 128
S5_P = 16
S5_G = BRANCH_W // S5_P
S5_N = 64
S5_T = 16
PEER_HEADS = 8
PEER_NKEYS = 128
PEER_TOPK = 16
LANES = 128
VMEM_LIMIT = 56 * 1024 * 1024

Z_GATE = 0
Z_Q, Z_K, Z_V, Z_XM, Z_VM, Z_OM, Z_U = (3072 + i * BRANCH_W for i in range(7))


def _cparams(sem):
    return pltpu.CompilerParams(dimension_semantics=sem, vmem_limit_bytes=VMEM_LIMIT)


def _nt_dot(a, b):
    return lax.dot_general(a, b, (((1,), (1,)), ((), ())), preferred_element_type=F32)


def _gelu(x):
    return 0.5 * x * (1.0 + lax.erf(x * (2.0 ** -0.5)))


def _log_sigmoid(x):
    return jnp.minimum(x, 0.0) - jnp.log1p(jnp.exp(-jnp.abs(x)))


def _mod_kernel(c_ref, w_ref, b_ref, o_ref):
    c = c_ref[...]
    a = c * jax.nn.sigmoid(c)
    o_ref[...] = jnp.dot(a.astype(BF16), w_ref[...], preferred_element_type=F32) + b_ref[...]


def _modulation(c_all, w, b):
    rows, d = c_all.shape
    n = w.shape[1]
    tn = 1536
    return pl.pallas_call(
        _mod_kernel,
        out_shape=jax.ShapeDtypeStruct((rows, n), F32),
        grid=(n // tn,),
        in_specs=[pl.BlockSpec((rows, d), lambda j: (0, 0)),
                  pl.BlockSpec((d, tn), lambda j: (0, j)),
                  pl.BlockSpec((1, tn), lambda j: (0, j))],
        out_specs=pl.BlockSpec((rows, tn), lambda j: (0, j)),
        compiler_params=_cparams(("parallel",)),
        name="modulation",
    )(c_all, w, b)


def _modnorm_matmul_kernel(x_ref, g_ref, sc_ref, sh_ref, w_ref, o_ref, *h_out):
    x = x_ref[...]
    y = x * lax.rsqrt(jnp.mean(x * x, axis=-1, keepdims=True) + EPS) * g_ref[...]
    hf = y * (1.0 + sc_ref[...]) + sh_ref[...]
    h = hf.astype(BF16)
    o_ref[...] = jnp.dot(h, w_ref[...], preferred_element_type=F32).astype(o_ref.dtype)
    if h_out:
        h_out[0][...] = hf.T.astype(BF16)


def _row_batch(i, n_lat_blocks, blocks_per_batch, bsz):
    return jnp.where(i < n_lat_blocks, i // blocks_per_batch, bsz)


def _modnorm_matmul(x_all, n_rows, g, sc, sh, w, *, bsz, seq, tm, tn, out_dtype=F32, emit_h=False):
    d = x_all.shape[1]
    n = w.shape[1]
    nlb = bsz * seq // tm
    bpb = seq // tm
    assert not emit_h or tn == n
    mod_spec = pl.BlockSpec((None, 1, d), lambda j, i: (_row_batch(i, nlb, bpb, bsz), 0, 0))
    out_shape = [jax.ShapeDtypeStruct((n_rows, n), out_dtype)]
    out_specs = [pl.BlockSpec((tm, tn), lambda j, i: (i, j))]
    if emit_h:
        out_shape.append(jax.ShapeDtypeStruct((d, n_rows), BF16))
        out_specs.append(pl.BlockSpec((d, tm), lambda j, i: (0, i)))
    out = pl.pallas_call(
        _modnorm_matmul_kernel,
        out_shape=tuple(out_shape),
        grid=(n // tn, n_rows // tm),
        in_specs=[pl.BlockSpec((tm, d), lambda j, i: (i, 0)),
                  pl.BlockSpec((1, d), lambda j, i: (0, 0)),
                  mod_spec, mod_spec,
                  pl.BlockSpec((d, tn), lambda j, i: (0, j))],
        out_specs=tuple(out_specs),
        compiler_params=_cparams(("parallel", "parallel")),
        name="modnorm_matmul",
    )(x_all, g, sc, sh, w)
    return out if emit_h else out[0]


def _rope_kernel(q_ref, k_ref, v_ref, cos_ref, sin_ref, qo_ref, ko_ref, vo_ref):
    cos = cos_ref[...]
    sin = sin_ref[...]
    lane = lax.broadcasted_iota(jnp.int32, cos.shape, 1)
    first_half = (lane % DA_HD) < (DA_HD // 2)

    def rope(x):
        partner = jnp.where(first_half, pltpu.roll(x, LANES - DA_HD // 2, 1),
                            pltpu.roll(x, DA_HD // 2, 1))
        return x * cos + partner * sin

    for h in range(DA_HEADS):
        sl = slice(h * LANES, (h + 1) * LANES)
        qo_ref[:, sl] = (rope(q_ref[:, sl].astype(F32)) * (DA_HD ** -0.5 * LOG2E)).astype(BF16)
        ko_ref[:, sl] = rope(k_ref[:, sl].astype(F32)).astype(BF16)
    vo_ref[...] = v_ref[...].astype(BF16)


def _rope_qkv(z, cos_tab, sin_tab, *, bsz, seq, tm):
    rows = z.shape[0]
    nlb = bsz * seq // tm
    bpb = seq // tm
    w = BRANCH_W
    tab_spec = pl.BlockSpec((tm, LANES), lambda i: (jnp.where(i < nlb, i % bpb, bpb), 0))
    out = jax.ShapeDtypeStruct((rows, w), BF16)
    return pl.pallas_call(
        _rope_kernel,
        out_shape=(out, out, out),
        grid=(rows // tm,),
        in_specs=[pl.BlockSpec((tm, w), lambda i: (i, Z_Q // w)),
                  pl.BlockSpec((tm, w), lambda i: (i, Z_K // w)),
                  pl.BlockSpec((tm, w), lambda i: (i, Z_V // w)),
                  tab_spec, tab_spec],
        out_specs=(pl.BlockSpec((tm, w), lambda i: (i, 0)),) * 3,
        compiler_params=_cparams(("parallel",)),
        name="rope_qkv",
    )(z, z, z, cos_tab, sin_tab)


def _attn_kernel(*refs, n_seg, n_sub, out_scale):
    lam_ref, q_ref = refs[0], refs[1]
    k_refs = refs[2:2 + n_seg]
    v_refs = refs[2 + n_seg:2 + 2 * n_seg]
    g_ref, o_ref = refs[2 + 2 * n_seg], refs[3 + 2 * n_seg]
    tq = q_ref.shape[0] // n_sub
    lane = lax.broadcasted_iota(jnp.int32, (tq, LANES), 1)
    scores = []
    for sub in range(n_sub):
        q = q_ref[sub * tq:(sub + 1) * tq, :].astype(F32)
        qq = jnp.concatenate([jnp.where(lane < DA_HD, q, 0.0), jnp.where(lane >= DA_HD, q, 0.0)],
                             axis=0).astype(BF16)
        scores.append([_nt_dot(qq, k_ref[...]) for k_ref in k_refs])
    for sub in range(n_sub):
        rows = slice(sub * tq, (sub + 1) * tq)
        s = scores[sub]
        m = functools.reduce(jnp.maximum, [jnp.max(t, axis=-1, keepdims=True) for t in s])
        p = [jnp.exp2(t - m) for t in s]
        denom = functools.reduce(jnp.add, [jnp.sum(t, axis=-1, keepdims=True) for t in p])
        inv = 1.0 / denom
        c0 = inv[:tq]
        c1 = inv[tq:] * lam_ref[...]
        o = None
        for t, v_ref in zip(p, v_refs):
            w = (t[:tq] * c0 - t[tq:] * c1).astype(BF16)
            part = jnp.dot(w, v_ref[...], preferred_element_type=F32)
            o = part if o is None else o + part
        y = o * lax.rsqrt(jnp.mean(o * o, axis=-1, keepdims=True) + EPS) * g_ref[...]
        o_ref[rows, :] = y * out_scale


def _diff_attention(qr, kr, vb, lam, sub_g, *, q_row0, n_q, kv_segs, bsz, tq, out_scale):
    n_sub = 2 if n_q % (2 * tq) == 0 else 1
    tq = tq * n_sub
    nqb = n_q // tq
    n_seg = len(kv_segs)
    q_blk0 = q_row0 // tq

    def kv_spec(row0, length):
        return pl.BlockSpec((length, LANES), lambda b, h, i: (row0 // length + b, h))

    in_specs = [pl.BlockSpec((1, 1), lambda b, h, i: (0, 0)),
                pl.BlockSpec((tq, LANES), lambda b, h, i: (q_blk0 + b * nqb + i, h))]
    in_specs += [kv_spec(r0, ln) for r0, ln in kv_segs] * 2
    in_specs += [pl.BlockSpec((1, LANES), lambda b, h, i: (0, h))]
    args = [lam, qr] + [kr] * n_seg + [vb] * n_seg + [sub_g]
    return pl.pallas_call(
        functools.partial(_attn_kernel, n_seg=n_seg, n_sub=n_sub, out_scale=out_scale),
        out_shape=jax.ShapeDtypeStruct((bsz * n_q, BRANCH_W), F32),
        grid=(bsz, DA_HEADS, nqb),
        in_specs=in_specs,
        out_specs=pl.BlockSpec((tq, LANES), lambda b, h, i: (b * nqb + i, h)),
        compiler_params=_cparams(("parallel", "parallel", "parallel")),
        name="diff_attention",
    )(*args)


def _ml_prep_kernel(x_ref, cw_ref, cb_ref, wq_ref, wk_ref, q_ref, k_ref):
    x = x_ref[...].astype(F32)
    n = x.shape[0]
    row = lax.broadcasted_iota(jnp.int32, x.shape, 0)
    x_prev = jnp.where(row == 0, 0.0, pltpu.roll(x, 1, 0))
    x_next = jnp.where(row == n - 1, 0.0, pltpu.roll(x, n - 1, 0))
    cw = cw_ref[...]
    y = x_prev * cw[0:1] + x * cw[1:2] + x_next * cw[2:3] + cb_ref[...]
    xc = (y * jax.nn.sigmoid(y)).astype(BF16)
    q_ref[...] = jnp.dot(xc, wq_ref[...], preferred_element_type=F32) * (ML_HD ** -0.5)
    k_ref[...] = jnp.dot(xc, wk_ref[...], preferred_element_type=F32)


def _ml_prep(z, conv_w, conv_b, wq, wk, *, row0, seq_len, bsz):
    blk0 = row0 // seq_len
    out = jax.ShapeDtypeStruct((bsz * seq_len, BRANCH_W), F32)
    head_w = pl.BlockSpec((None, ML_HD, ML_HD), lambda b, h: (h, 0, 0))
    return pl.pallas_call(
        _ml_prep_kernel,
        out_shape=(out, out),
        grid=(bsz, ML_HEADS),
        in_specs=[pl.BlockSpec((seq_len, LANES), lambda b, h: (blk0 + b, Z_XM // LANES + h)),
                  pl.BlockSpec((3, LANES), lambda b, h: (0, h)),
                  pl.BlockSpec((1, LANES), lambda b, h: (0, h)),
                  head_w, head_w],
        out_specs=(pl.BlockSpec((seq_len, LANES), lambda b, h: (b, h)),) * 2,
        compiler_params=_cparams(("parallel", "parallel")),
        name="mlstm_prep",
    )(z, conv_w, conv_b, wq, wk)


def _mlstm_kernel(qf_ref, kf_ref, vf_ref, gcf_ref, grf_ref, qb_ref, kb_ref, vb_ref, gcb_ref, grb_ref,
                  hf_ref, hb_ref, c_ref, n_ref, m_ref):
    @pl.when(pl.program_id(1) == 0)
    def _():
        c_ref[...] = jnp.zeros_like(c_ref)
        n_ref[...] = jnp.zeros_like(n_ref)
        m_ref[...] = jnp.zeros_like(m_ref)

    t_idx = lax.broadcasted_iota(jnp.int32, (ML_CHUNK, ML_CHUNK), 0)
    s_idx = lax.broadcasted_iota(jnp.int32, (ML_CHUNK, ML_CHUNK), 1)
    nh = ML_HEADS
    dirs = ((qf_ref, kf_ref, vf_ref, gcf_ref, grf_ref, hf_ref),
            (qb_ref, kb_ref, vb_ref, gcb_ref, grb_ref, hb_ref))
    units = []
    for d, (q_ref, k_ref, v_ref, gc_ref, gr_ref, h_ref) in enumerate(dirs):
        mask = s_idx <= t_idx if d == 0 else s_idx >= t_idx
        mask_f = mask.astype(F32)
        gc = gc_ref[...]
        gr = gr_ref[...]
        lf_c = _log_sigmoid(gc)
        lf_r = _log_sigmoid(gr)
        b_c = jnp.dot(mask_f, lf_c, preferred_element_type=F32, precision=lax.Precision.HIGHEST)
        b_r = lax.dot_general(lf_r, mask_f, (((1,), (1,)), ((), ())), preferred_element_type=F32,
                              precision=lax.Precision.HIGHEST)
        b_end_all = jnp.sum(lf_c, axis=0, keepdims=True)

        for h in range(nh):
            sl = slice(h * ML_HD, (h + 1) * ML_HD)
            q = q_ref[:, sl].astype(BF16)
            k = k_ref[:, sl].astype(BF16)
            v = v_ref[:, sl].astype(F32)
            li_c = gc[:, h:h + 1]
            li_r = gr[h:h + 1, :]
            bc = b_c[:, nh + h:nh + h + 1]
            br = b_r[nh + h:nh + h + 1, :]
            b_end = b_end_all[:, nh + h:nh + h + 1]
            m_prev = m_ref[d, h]
            c_prev = c_ref[d, h]
            n_prev = n_ref[d, h]
            g_c = b_end - bc + li_c
            g_r = b_end - br + li_r
            m_new = jnp.maximum(b_end + m_prev, jnp.max(g_r, axis=-1, keepdims=True))
            wg = jnp.exp(g_c - m_new)
            decay = jnp.exp(b_end + m_prev - m_new)
            vw = (v * wg).astype(BF16)
            units.append(dict(
                d=d, h=h, sl=sl, h_ref=h_ref, q=q, v=v, mask=mask, m_prev=m_prev, n_prev=n_prev,
                bc=bc, br=br, li_r=li_r,
                qk=_nt_dot(q, k), qc=_nt_dot(q, c_prev.astype(BF16)),
                c_new=decay * c_prev + lax.dot_general(
                    vw, k, (((0,), (0,)), ((), ())), preferred_element_type=F32),
                n_new=decay * n_prev + jnp.sum(wg * k.astype(F32), axis=0, keepdims=True),
                m_new=m_new))

    for u in units:
        dmat = jnp.where(u["mask"], u["bc"] - u["br"] + u["li_r"], -jnp.inf)
        inter = u["bc"] + u["m_prev"]
        m_t = jnp.maximum(inter, jnp.max(dmat, axis=-1, keepdims=True))
        s = u["qk"] * jnp.exp(dmat - m_t)
        w_prev = jnp.exp(inter - m_t)
        num = (jnp.dot(s.astype(BF16), u["v"].astype(BF16), preferred_element_type=F32)
               + w_prev * u["qc"])
        den = (jnp.sum(s, axis=-1, keepdims=True)
               + w_prev * jnp.sum(u["q"].astype(F32) * u["n_prev"], axis=-1, keepdims=True))
        u["h_ref"][:, u["sl"]] = num / jnp.maximum(jnp.abs(den), jnp.exp(-m_t))
        c_ref[u["d"], u["h"]] = u["c_new"]
        n_ref[u["d"], u["h"]] = u["n_new"]
        m_ref[u["d"], u["h"]] = u["m_new"]


def _mlstm_scan(q_all, k_all, z, gates_c, gates_r, *, bsz, seq, ctx_len):
    rows = q_all.shape[0]
    ncl = seq // ML_CHUNK
    ncc = ctx_len // ML_CHUNK
    lat_blocks = bsz * ncl
    nh = ML_HEADS

    def blk(d, b, s):
        jc = s if d == 0 else ncc - 1 - s
        jl = s - ncc if d == 0 else ncl - 1 - (s - ncc)
        return jnp.where(s < ncc, lat_blocks + b * ncc + jc, b * ncl + jl)

    def dir_specs(d):
        return [pl.BlockSpec((ML_CHUNK, BRANCH_W), lambda b, s: (blk(d, b, s), 0)),
                pl.BlockSpec((ML_CHUNK, BRANCH_W), lambda b, s: (blk(d, b, s), 0)),
                pl.BlockSpec((ML_CHUNK, BRANCH_W), lambda b, s: (blk(d, b, s), Z_VM // BRANCH_W)),
                pl.BlockSpec((None, ML_CHUNK, 2 * nh), lambda b, s: (d, blk(d, b, s), 0)),
                pl.BlockSpec((None, 2 * nh, ML_CHUNK), lambda b, s: (d, 0, blk(d, b, s)))]

    out = jax.ShapeDtypeStruct((rows, BRANCH_W), F32)
    args = (q_all, k_all, z, gates_c, gates_r)
    return pl.pallas_call(
        _mlstm_kernel,
        out_shape=(out, out),
        grid=(bsz, ncc + ncl),
        in_specs=dir_specs(0) + dir_specs(1),
        out_specs=(pl.BlockSpec((ML_CHUNK, BRANCH_W), lambda b, s: (blk(0, b, s), 0)),
                   pl.BlockSpec((ML_CHUNK, BRANCH_W), lambda b, s: (blk(1, b, s), 0))),
        scratch_shapes=[pltpu.VMEM((2, nh, ML_HD, ML_HD), F32),
                        pltpu.VMEM((2, nh, 1, ML_HD), F32),
                        pltpu.VMEM((2, nh, 1, 1), F32)],
        compiler_params=_cparams(("parallel", "arbitrary")),
        name="mlstm_scan",
    )(*args, *args)


def _s5_kernel(uc_ref, ul_ref, m_ref, w_ref, v_ref, a1_ref, a2_ref, yc_ref, yl_ref,
               ec_ref, el_ref, xc_ref, xl_ref, *, bsz):
    w = w_ref[...]
    ec_ref[...] = jnp.dot(uc_ref[...], w, preferred_element_type=F32)
    el_ref[...] = jnp.dot(ul_ref[...], w, preferred_element_type=F32)
    a1 = a1_ref[...]
    a2 = a2_ref[...]
    half = LANES
    a1f, a2f, a1b, a2b = a1[:, 0:half], a2[:, 0:half], a1[:, half:], a2[:, half:]

    def scan(e_ref, x_ref, state):
        n_chunks = e_ref.shape[0] // bsz

        def body(j, st):
            xf, xfs, xb, xbs = st
            rf = pl.multiple_of(j * bsz, bsz)
            rb = pl.multiple_of((n_chunks - 1 - j) * bsz, bsz)
            x_ref[pl.ds(rf, bsz), 0:half] = xf
            x_ref[pl.ds(rb, bsz), half:2 * half] = xb
            ef, eb = e_ref[pl.ds(rf, bsz), 0:half], e_ref[pl.ds(rb, bsz), half:2 * half]
            efs = e_ref[pl.ds(rf, bsz), 2 * half:3 * half]
            ebs = e_ref[pl.ds(rb, bsz), 3 * half:4 * half]
            return (a1f * xf + a2f * xfs + ef, a1f * xfs - a2f * xf + efs,
                    a1b * xb + a2b * xbs + eb, a1b * xbs - a2b * xb + ebs)

        return lax.fori_loop(0, n_chunks, body, state)

    zero = jnp.zeros((bsz, half), F32)
    state = scan(ec_ref, xc_ref, (zero, zero, zero, zero))
    scan(el_ref, xl_ref, state)
    mm = m_ref[...]
    vv = v_ref[...]
    yc_ref[...] = (jnp.dot(uc_ref[...], mm, preferred_element_type=F32)
                   + jnp.dot(xc_ref[...].astype(BF16), vv, preferred_element_type=F32)
                   ).astype(yc_ref.dtype)
    yl_ref[...] = (jnp.dot(ul_ref[...], mm, preferred_element_type=F32)
                   + jnp.dot(xl_ref[...].astype(BF16), vv, preferred_element_type=F32)
                   ).astype(yl_ref.dtype)


def _s5_tables(lam_re, lam_im, log_dt, b_re, b_im, c_re, c_im, d_skip):
    t = S5_T
    dt = jnp.exp(log_dt)[..., None]
    den = lam_re * lam_re + lam_im * lam_im
    kk = jnp.arange(t + 1, dtype=F32)[:, None, None, None]
    mag = jnp.exp(kk * lam_re * dt)
    ang = kk * lam_im * dt
    p_re, p_im = mag * jnp.cos(ang), mag * jnp.sin(ang)
    ab_re, ab_im = p_re[1], p_im[1]
    z_re = ((ab_re - 1.0) * lam_re + ab_im * lam_im) / den
    z_im = (ab_im * lam_re - (ab_re - 1.0) * lam_im) / den
    bb_re = z_re[..., None] * b_re - z_im[..., None] * b_im
    bb_im = z_re[..., None] * b_im + z_im[..., None] * b_re
    ab_k_re = p_re[..., None] * bb_re - p_im[..., None] * bb_im
    ab_k_im = p_re[..., None] * bb_im + p_im[..., None] * bb_re
    taps = (jnp.einsum('rgpn,krgnq->krgpq', c_re, ab_k_re[:t])
            - jnp.einsum('rgpn,krgnq->krgpq', c_im, ab_k_im[:t]))
    s_i = jnp.arange(t)[:, None]
    t_i = jnp.arange(t)[None, :]
    lag_f = jnp.clip(t_i - s_i, 0, t - 1)
    lag_b = jnp.clip(s_i - t_i, 0, t - 1)
    tf = jnp.where((t_i >= s_i)[..., None, None, None], taps[lag_f, 0], 0.0)
    tb = jnp.where((s_i >= t_i)[..., None, None, None], taps[lag_b, 1], 0.0)
    skip = (jnp.eye(t)[:, :, None, None, None]
            * (jnp.eye(S5_P)[None, None, None] * d_skip.reshape(S5_G, S5_P, 1)[None, None]))
    m = jnp.transpose(tf + tb + skip, (2, 0, 4, 1, 3)).reshape(S5_G, t * S5_P, t * S5_P)
    wf_re = jnp.transpose(ab_k_re[:t, 0][::-1], (1, 0, 3, 2)).reshape(S5_G, t * S5_P, S5_N)
    wf_im = jnp.transpose(ab_k_im[:t, 0][::-1], (1, 0, 3, 2)).reshape(S5_G, t * S5_P, S5_N)
    wb_re = jnp.transpose(ab_k_re[:t, 1], (1, 0, 3, 2)).reshape(S5_G, t * S5_P, S5_N)
    wb_im = jnp.transpose(ab_k_im[:t, 1], (1, 0, 3, 2)).reshape(S5_G, t * S5_P, S5_N)
    w = jnp.concatenate([wf_re, wf_im, wb_re, wb_im, wf_im, wf_re, wb_im, wb_re], axis=-1)
    ca_re = (c_re[None] * p_re[:, :, :, None, :] - c_im[None] * p_im[:, :, :, None, :])
    ca_im = (c_re[None] * p_im[:, :, :, None, :] + c_im[None] * p_re[:, :, :, None, :])

    def to_rows(a):
        return jnp.transpose(a, (1, 3, 0, 2)).reshape(S5_G, S5_N, t * S5_P)

    v = jnp.concatenate([to_rows(ca_re[1:t + 1, 0]), to_rows(-ca_im[1:t + 1, 0]),
                         to_rows(ca_re[1:t + 1, 1][::-1]), to_rows(-ca_im[1:t + 1, 1][::-1])], axis=1)
    at_re, at_im = p_re[t], p_im[t]
    a1 = jnp.concatenate([at_re[0], at_re[0], at_re[1], at_re[1]], axis=-1)[:, None, :]
    a2 = jnp.concatenate([-at_im[0], at_im[0], -at_im[1], at_im[1]], axis=-1)[:, None, :]
    return m.astype(BF16), w.astype(BF16), v.astype(BF16), a1, a2


def _s5_rows(u, bsz, length):
    nc = length // S5_T
    u = u.reshape(bsz, nc, S5_T, S5_G, S5_P)
    return jnp.transpose(u, (3, 1, 0, 2, 4)).reshape(S5_G, nc * bsz, S5_T * S5_P)


def _s5_unrows(y, bsz, length):
    nc = length // S5_T
    y = y.reshape(S5_G, nc, bsz, S5_T, S5_P)
    return jnp.transpose(y, (2, 1, 3, 0, 4)).reshape(bsz * length, BRANCH_W)


def _s5_mix(u_ctx, u_lat, tables, *, bsz):
    m, w, v, a1, a2 = tables
    rc, rl = u_ctx.shape[1], u_lat.shape[1]
    wd = S5_T * S5_P

    def rows_spec(r):
        return pl.BlockSpec((None, r, wd), lambda g: (g, 0, 0))

    sq = pl.BlockSpec((None, wd, wd), lambda g: (g, 0, 0))
    vec = pl.BlockSpec((None, 1, wd), lambda g: (g, 0, 0))
    return pl.pallas_call(
        functools.partial(_s5_kernel, bsz=bsz),
        out_shape=(jax.ShapeDtypeStruct((S5_G, rc, wd), BF16),
                   jax.ShapeDtypeStruct((S5_G, rl, wd), BF16)),
        grid=(S5_G,),
        in_specs=[rows_spec(rc), rows_spec(rl), sq,
                  pl.BlockSpec((None, wd, 2 * wd), lambda g: (g, 0, 0)), sq, vec, vec],
        out_specs=(rows_spec(rc), rows_spec(rl)),
        scratch_shapes=[pltpu.VMEM((rc, 2 * wd), F32), pltpu.VMEM((rl, 2 * wd), F32),
                        pltpu.VMEM((rc, wd), F32), pltpu.VMEM((rl, wd), F32)],
        compiler_params=_cparams(("parallel",)),
        name="s5_mix",
    )(u_ctx, u_lat, m, w, v, a1, a2)


def _merge_kernel(x_ref, ya_ref, hf_ref, hb_ref, om_ref, ys_ref, gate_ref, g1_ref, mlg_ref,
                  gluw_ref, glub_ref, wbr_ref, wout_ref, o_ref):
    hsum = hf_ref[...] + hb_ref[...]
    og = jax.nn.sigmoid(om_ref[...].astype(F32))
    mlg = mlg_ref[...]
    yb_parts = []
    for h in range(ML_HEADS):
        sl = slice(h * ML_HD, (h + 1) * ML_HD)
        hh = hsum[:, sl]
        hn = hh * lax.rsqrt(jnp.mean(hh * hh, axis=-1, keepdims=True) + EPS) * mlg[:, sl]
        yb_parts.append(hn * og[:, sl])
    yb = jnp.concatenate(yb_parts, axis=-1)
    gl = _gelu(ys_ref[...].astype(F32))
    ys = gl * jax.nn.sigmoid(
        jnp.dot(gl.astype(BF16), gluw_ref[...], preferred_element_type=F32) + glub_ref[...])
    d = o_ref.shape[1]
    merged = None
    for r, y in enumerate((ya_ref[...], yb, ys)):
        proj = jnp.dot(y.astype(BF16), wbr_ref[r], preferred_element_type=F32)
        term = jax.nn.sigmoid(gate_ref[:, r * d:(r + 1) * d].astype(F32)) * proj
        merged = term if merged is None else merged + term
    y = jnp.dot(merged.astype(BF16), wout_ref[...], preferred_element_type=F32)
    o_ref[...] = x_ref[...] + g1_ref[...] * y


def _merge(x_all, ya, h_dirs, z, ys, g1, ml_norm_g, glu_w, glu_b, w_br, w_out, *, n_rows, bsz, seq, tm):
    d = x_all.shape[1]
    nlb = bsz * seq // tm
    bpb = seq // tm
    w = BRANCH_W

    def rows(width, col_blk=0):
        return pl.BlockSpec((tm, width), lambda i: (i, col_blk))

    def full(shape):
        return pl.BlockSpec(shape, lambda i: (0,) * len(shape))

    return pl.pallas_call(
        _merge_kernel,
        out_shape=jax.ShapeDtypeStruct((n_rows, d), F32),
        grid=(n_rows // tm,),
        in_specs=[rows(d), rows(w), rows(w), rows(w),
                  rows(w, Z_OM // w), rows(w),
                  rows(3 * d, Z_GATE // (3 * d)),
                  pl.BlockSpec((None, 1, d), lambda i: (_row_batch(i, nlb, bpb, bsz), 0, 0)),
                  full((1, w)), full((w, w)), full((1, w)), full((3, w, d)), full((d, d))],
        out_specs=rows(d),
        compiler_params=_cparams(("parallel",)),
        name="merge",
    )(x_all, ya, h_dirs[0], h_dirs[1], z, ys, z, g1, ml_norm_g, glu_w, glu_b, w_br, w_out)


NOT_TOP = 127.0


def _take_top(s, n_take, vals_ref, want_rank):
    rows = s.shape[0]
    ridx = lax.broadcasted_iota(jnp.int32, s.shape, 0).astype(F32)
    rank = jnp.full(s.shape, NOT_TOP, F32) if want_rank else None
    for i in range(n_take):
        mx = jnp.max(s, axis=0, keepdims=True)
        first = jnp.min(jnp.where(s == mx, ridx, float(rows)), axis=0, keepdims=True)
        hit = ridx == first
        s = jnp.where(hit, -jnp.inf, s)
        if want_rank:
            rank = jnp.where(hit, float(i), rank)
        vals_ref[i:i + 1, :] = mx
    return s, rank


SUBLANES = 8


def _sort_network_pairs(n):
    pairs = []
    p = 1
    while p < n:
        k = p
        while k >= 1:
            for j in range(k % p, n - k, 2 * k):
                for i in range(min(k, n - j - k)):
                    if (i + j) // (2 * p) == (i + j + k) // (2 * p):
                        pairs.append((i + j, i + j + k))
            k //= 2
        p *= 2
    return pairs


def _top_sorted(s):
    n = s.shape[0] // SUBLANES
    assert n == PEER_TOPK
    x = [s[SUBLANES * i:SUBLANES * (i + 1), :] for i in range(n)]

    def exchange(i, j):
        x[i], x[j] = jnp.maximum(x[i], x[j]), jnp.minimum(x[i], x[j])

    for i, j in _sort_network_pairs(n):
        exchange(i, j)
    shift = SUBLANES // 2
    while shift >= 1:
        y = [pltpu.roll(v, shift, 0) for v in x]
        x = [jnp.maximum(x[i], y[n - 1 - i]) for i in range(n)]
        d = n // 2
        while d >= 1:
            for i in range(n):
                if not i & d:
                    exchange(i, i + d)
            d //= 2
        shift //= 2
    return x


def _slabs(s):
    return [s[SUBLANES * i:SUBLANES * (i + 1), :] for i in range(s.shape[0] // SUBLANES)]


def _tie_count(s, top):
    flag = jnp.zeros(top[0].shape, F32)
    for i in range(len(top) - 1):
        flag = flag + jnp.where(top[i] == top[i + 1], 1.0, 0.0)
    reach = functools.reduce(jnp.add, [jnp.where(v >= top[-1], 1.0, 0.0) for v in _slabs(s)])
    reach = jnp.sum(reach, axis=0, keepdims=True)
    return flag + jnp.where(reach != float(PEER_TOPK), 1.0, 0.0)


def _candidate_counts(v1_ref, v2_ref, top_ref, cnt_ref):
    k = PEER_TOPK
    v2_head = v2_ref[0:8, :]
    jrow = lax.broadcasted_iota(jnp.int32, v2_head.shape, 0)
    pieces = [v1_ref[0:1, :] + v2_ref[...]]
    for i in range(1, 8):
        pieces.append(jnp.where(jrow < k // (i + 1), v1_ref[i:i + 1, :] + v2_head, -jnp.inf))
    pieces.append(v1_ref[8:16, :] + v2_ref[0:1, :])
    cand = jnp.concatenate(pieces, axis=0)
    cand_left, _ = _take_top(cand, k, top_ref, False)
    picked = jnp.where(cand_left != cand, 1.0, 0.0)
    cnt_ref[0:1, :] = jnp.sum(picked[0:16], axis=0, keepdims=True)
    for i in range(1, 8):
        cnt_ref[i:i + 1, :] = jnp.sum(picked[8 + 8 * i:16 + 8 * i], axis=0, keepdims=True)
    cnt_ref[8:16, :] = picked[72:80]
    top = top_ref[...]
    return jnp.sum(jnp.exp(top - top[0:1, :]), axis=0, keepdims=True)


def _peer_topk_kernel(q_ref, k1_ref, k2_ref, e1_ref, cnt1_ref, e2_ref, rank2_ref,
                      v1_ref, v2_ref, top_ref, cnt_ref):
    k = PEER_TOPK
    q = q_ref[...].astype(BF16)
    s1 = _nt_dot(k1_ref[...], q[:, :LANES])
    s2 = _nt_dot(k2_ref[...], q[:, LANES:])
    top1 = _top_sorted(s1)
    top2 = _top_sorted(s2)
    any_tie = jnp.max(_tie_count(s1, top1) + _tie_count(s2, top2)) > 0.0

    @pl.when(any_tie)
    def _():
        _, rank1 = _take_top(s1, k, v1_ref, True)
        _, rank2 = _take_top(s2, k, v2_ref, True)
        zsum = _candidate_counts(v1_ref, v2_ref, top_ref, cnt_ref)
        cnt1 = jnp.zeros(s1.shape, F32)
        for i in range(k):
            cnt1 = jnp.where(rank1 == float(i), cnt_ref[i:i + 1, :], cnt1)
        e1_ref[...] = jnp.where(rank1 < k, jnp.exp(s1 - v1_ref[0:1, :]), 0.0) / zsum
        cnt1_ref[...] = cnt1
        e2_ref[...] = jnp.where(rank2 < k, jnp.exp(s2 - v2_ref[0:1, :]), 0.0).astype(BF16)
        rank2_ref[...] = rank2.astype(BF16)

    @pl.when(jnp.logical_not(any_tie))
    def _():
        for i in range(k):
            v1_ref[i:i + 1, :] = top1[i][0:1, :]
            v2_ref[i:i + 1, :] = top2[i][0:1, :]
        zsum = _candidate_counts(v1_ref, v2_ref, top_ref, cnt_ref)
        cnt_rows = [jnp.broadcast_to(cnt_ref[i:i + 1, :], top1[0].shape) for i in range(k)]
        cnt1, rank2 = [], []
        for v in _slabs(s1):
            c = jnp.zeros(v.shape, F32)
            for i in range(k):
                c = jnp.where(v == top1[i], cnt_rows[i], c)
            cnt1.append(c)
        for v in _slabs(s2):
            r = jnp.full(v.shape, NOT_TOP, F32)
            for i in range(k):
                r = jnp.where(v == top2[i], float(i), r)
            rank2.append(r)
        e1_ref[...] = jnp.where(s1 >= v1_ref[k - 1:k, :], jnp.exp(s1 - v1_ref[0:1, :]), 0.0) / zsum
        cnt1_ref[...] = jnp.concatenate(cnt1, axis=0)
        e2_ref[...] = jnp.where(s2 >= v2_ref[k - 1:k, :], jnp.exp(s2 - v2_ref[0:1, :]), 0.0
                                ).astype(BF16)
        rank2_ref[...] = jnp.concatenate(rank2, axis=0).astype(BF16)


def _peer_topk(q, sub_k, *, n_rows, tm):
    nk = PEER_NKEYS
    row_tab = jax.ShapeDtypeStruct((n_rows // tm, PEER_HEADS, nk, tm), F32)
    tile_tab = jax.ShapeDtypeStruct((n_rows // tm, PEER_HEADS, nk, tm), BF16)
    tab_spec = pl.BlockSpec((None, None, nk, tm), lambda i, h: (i, h, 0, 0))
    return pl.pallas_call(
        _peer_topk_kernel,
        out_shape=(row_tab, row_tab, tile_tab, tile_tab),
        grid=(n_rows // tm, PEER_HEADS),
        in_specs=[pl.BlockSpec((tm, 2 * LANES), lambda i, h: (i, h)),
                  pl.BlockSpec((None, nk, LANES), lambda i, h: (0, 0, 0)),
                  pl.BlockSpec((None, nk, LANES), lambda i, h: (1, 0, 0))],
        out_specs=(tab_spec,) * 4,
        scratch_shapes=[pltpu.VMEM((PEER_TOPK, tm), F32)] * 4,
        compiler_params=_cparams(("parallel", "parallel")),
        name="peer_topk",
    )(q, sub_k, sub_k)


GATE_KEYS = 2
KEY_GROUP = 8


def _peer_dense_kernel(ht_ref, u_ref, vt_ref, e1_ref, cnt1_ref, e2_ref, rank2_ref,
                       x_ref, g2_ref, o_ref, acc_ref, act_ref, p_ref, *, a_per_blk):
    j = pl.program_id(1)
    n_blk = pl.num_programs(1) - 1
    cur = j % 2
    tm = ht_ref.shape[1]
    nk = PEER_NKEYS

    @pl.when(j == 0)
    def _():
        acc_ref[...] = jnp.zeros_like(acc_ref)
        p_ref[1] = jnp.zeros(p_ref.shape[1:], p_ref.dtype)

    @pl.when(j < n_blk)
    def _():
        act_ref[...] = _gelu(jnp.dot(u_ref[...], ht_ref[...], preferred_element_type=F32)
                             ).astype(BF16)
        acc_ref[...] += jnp.dot(vt_ref[...], p_ref[1 - cur], preferred_element_type=F32)
        a0 = pl.multiple_of(j * a_per_blk, KEY_GROUP)

        def row_bf16(ref, h, i, ls):
            grp = (i // KEY_GROUP) * KEY_GROUP
            w = ref[h, pl.ds(a0 + grp, KEY_GROUP), ls][i % KEY_GROUP:i % KEY_GROUP + 1]
            return jnp.broadcast_to(w, (nk, LANES)).astype(BF16)

        for lt in range(tm // LANES):
            ls = slice(lt * LANES, (lt + 1) * LANES)
            for i0 in range(0, a_per_blk, GATE_KEYS):
                gates = [None] * GATE_KEYS
                for h in range(PEER_HEADS):
                    e2 = e2_ref[h, :, ls]
                    rank2 = rank2_ref[h, :, ls]
                    for ii in range(GATE_KEYS):
                        lead = row_bf16(cnt1_ref, h, i0 + ii, ls) - rank2
                        term = jnp.minimum(jnp.maximum(lead, 0.0),
                                           row_bf16(e1_ref, h, i0 + ii, ls)) * e2
                        gates[ii] = term if gates[ii] is None else gates[ii] + term
                for ii in range(GATE_KEYS):
                    r0 = (i0 + ii) * nk
                    p_ref[cur, r0:r0 + nk, ls] = gates[ii] * act_ref[r0:r0 + nk, ls]

    @pl.when(j == n_blk)
    def _():
        acc = acc_ref[...] + jnp.dot(vt_ref[...], p_ref[1 - cur], preferred_element_type=F32)
        o_ref[...] = x_ref[...] + g2_ref[...] * acc.T


def _peer_dense(h2, u_tab, vt_tab, tabs, x_all, g2, *, n_rows, bsz, seq, tm, te):
    d = x_all.shape[1]
    n_exp = u_tab.shape[0]
    nlb = bsz * seq // tm
    bpb = seq // tm
    nk = PEER_NKEYS
    n_blk = n_exp // te
    assert (te // nk) % KEY_GROUP == 0
    tab_spec = pl.BlockSpec((None, PEER_HEADS, nk, tm), lambda i, j: (i, 0, 0, 0))
    return pl.pallas_call(
        functools.partial(_peer_dense_kernel, a_per_blk=te // nk),
        out_shape=jax.ShapeDtypeStruct((n_rows, d), F32),
        grid=(n_rows // tm, n_blk + 1),
        in_specs=[pl.BlockSpec((d, tm), lambda i, j: (0, i)),
                  pl.BlockSpec((te, d), lambda i, j: (jnp.minimum(j, n_blk - 1), 0)),
                  pl.BlockSpec((None, d, te), lambda i, j: (jnp.maximum(j - 1, 0), 0, 0)),
                  tab_spec, tab_spec, tab_spec, tab_spec,
                  pl.BlockSpec((tm, d), lambda i, j: (i, 0)),
                  pl.BlockSpec((None, 1, d), lambda i, j: (_row_batch(i, nlb, bpb, bsz), 0, 0))],
        out_specs=pl.BlockSpec((tm, d), lambda i, j: (i, 0)),
        scratch_shapes=[pltpu.VMEM((d, tm), F32), pltpu.VMEM((te, tm), BF16),
                        pltpu.VMEM((2, te, tm), BF16)],
        compiler_params=_cparams(("parallel", "arbitrary")),
        name="peer_dense",
    )(h2, u_tab, vt_tab, *tabs, x_all, g2)


def _transpose_cast_kernel(x_ref, o_ref):
    o_ref[...] = x_ref[...].T.astype(o_ref.dtype)


def _transpose_cast(w, *, tr):
    rows, cols = w.shape
    return pl.pallas_call(
        _transpose_cast_kernel,
        out_shape=jax.ShapeDtypeStruct((rows // tr, cols, tr), BF16),
        grid=(rows // tr,),
        in_specs=[pl.BlockSpec((tr, cols), lambda i: (i, 0))],
        out_specs=pl.BlockSpec((None, cols, tr), lambda i: (i, 0, 0)),
        compiler_params=_cparams(("parallel",)),
        name="transpose_cast",
    )(w)


def _modnorm_kernel(x_ref, g_ref, sc_ref, sh_ref, o_ref):
    x = x_ref[...]
    y = x * lax.rsqrt(jnp.mean(x * x, axis=-1, keepdims=True) + EPS) * g_ref[...]
    o_ref[...] = (y * (1.0 + sc_ref[...]) + sh_ref[...]).astype(o_ref.dtype)


def _modnorm(x_all, n_rows, g, sc, sh, *, bsz, seq, tm, out_dtype):
    d = x_all.shape[1]
    nlb = bsz * seq // tm
    bpb = seq // tm
    mod_spec = pl.BlockSpec((None, 1, d), lambda i: (_row_batch(i, nlb, bpb, bsz), 0, 0))
    return pl.pallas_call(
        _modnorm_kernel,
        out_shape=jax.ShapeDtypeStruct((n_rows, d), out_dtype),
        grid=(n_rows // tm,),
        in_specs=[pl.BlockSpec((tm, d), lambda i: (i, 0)),
                  pl.BlockSpec((1, d), lambda i: (0, 0)), mod_spec, mod_spec],
        out_specs=pl.BlockSpec((tm, d), lambda i: (i, 0)),
        compiler_params=_cparams(("parallel",)),
        name="modnorm",
    )(x_all, g, sc, sh)


def _rope_tables(seq, tm):
    rows = seq // GRID_W
    n_freq = DA_HD // 4
    inv = ROPE_BASE ** (-jnp.arange(n_freq, dtype=F32) / n_freq)
    r = jnp.repeat(jnp.arange(rows, dtype=F32), GRID_W)
    col = jnp.tile(jnp.arange(GRID_W, dtype=F32), rows)
    ang = jnp.concatenate([r[:, None] * inv, col[:, None] * inv], axis=-1)
    cos, sin = jnp.cos(ang), jnp.sin(ang)
    cos_t = jnp.tile(cos, (1, 4))
    sin_t = jnp.tile(jnp.concatenate([-sin, sin], axis=-1), (1, 2))
    ident = jnp.ones((tm, LANES), F32)
    return (jnp.concatenate([cos_t, ident], axis=0),
            jnp.concatenate([sin_t, jnp.zeros((tm, LANES), F32)], axis=0))


def kernel(x, c, ctx, c_ctx, w_mod, b_mod, norm1_g, norm2_g, w_in, da_lam_q1, da_lam_k1, da_lam_q2, da_lam_k2, da_sub_g, ml_conv_w, ml_conv_b, ml_wq, ml_wk, ml_gate_b, ml_norm_g, s5_lam_re, s5_lam_im, s5_log_dt, s5_b_re, s5_b_im, s5_c_re, s5_c_im, s5_d, s5_glu_w, s5_glu_b, w_br, w_out, peer_wq, peer_sub_k, peer_u, peer_v, final_g):
    bsz, seq, d = x.shape
    ctx_len = ctx.shape[1]
    depth = w_in.shape[0]
    t_lat = bsz * seq
    t_ctx = bsz * ctx_len
    t_all = t_lat + t_ctx
    tm = math.gcd(512, math.gcd(seq, t_ctx))
    tm_merge = min(tm, 256)
    tq = min(256, ctx_len)
    nh = ML_HEADS

    x_all = jnp.concatenate([x.reshape(t_lat, d), ctx.reshape(t_ctx, d)], axis=0)
    pad = (-(bsz + 1)) % 8
    c_all = jnp.concatenate([c, c_ctx[None], jnp.zeros((pad, d), F32)], axis=0)
    cos_tab, sin_tab = _rope_tables(seq, tm)

    for l in range(depth):
        need_ctx = l < depth - 1
        n_rows = t_all if need_ctx else t_lat
        lam_init = 0.8 - 0.6 * math.exp(-0.3 * l)

        mods = _modulation(c_all, w_mod[l].astype(BF16), b_mod[l][None])
        sh1, sc1, g1, sh2, sc2, g2 = [mods[:bsz + 1, i * d:(i + 1) * d][:, None, :]
                                      for i in range(N_MOD)]

        wi = w_in[l]
        i_q, i_k, i_v, i_xm, i_vm, i_om, i_g, i_u, i_gate = (
            0, 512, 1024, 1536, 2048, 2560, 3072, 3088, 3600)
        w_main = jnp.concatenate([wi[:, i_gate:], wi[:, :i_g], wi[:, i_u:i_gate]], axis=1).astype(BF16)
        w_gate = jnp.pad(wi[:, i_g:i_u], ((0, 0), (0, LANES - (i_u - i_g)))).astype(BF16)
        n1 = norm1_g[l][None]
        z = _modnorm_matmul(x_all, t_all, n1, sc1, sh1, w_main, bsz=bsz, seq=seq, tm=tm,
                            tn=w_main.shape[1] // 2, out_dtype=BF16)
        zg = _modnorm_matmul(x_all, t_all, n1, sc1, sh1, w_gate, bsz=bsz, seq=seq, tm=tm, tn=LANES)

        qr, kr, vb = _rope_qkv(z, cos_tab, sin_tab, bsz=bsz, seq=seq, tm=tm)
        lam = (jnp.exp(jnp.sum(da_lam_q1[l] * da_lam_k1[l]))
               - jnp.exp(jnp.sum(da_lam_q2[l] * da_lam_k2[l])) + lam_init).reshape(1, 1)
        sub_g = da_sub_g[l][None]
        ya = _diff_attention(qr, kr, vb, lam, sub_g, q_row0=0, n_q=seq,
                             kv_segs=[(0, seq), (t_lat, ctx_len)], bsz=bsz, tq=tq,
                             out_scale=1.0 - lam_init)
        if need_ctx:
            ya_c = _diff_attention(qr, kr, vb, lam, sub_g, q_row0=t_lat, n_q=ctx_len,
                                   kv_segs=[(t_lat, ctx_len)], bsz=bsz, tq=tq,
                                   out_scale=1.0 - lam_init)
            ya = jnp.concatenate([ya, ya_c], axis=0)

        wq_b, wk_b = ml_wq[l].astype(BF16), ml_wk[l].astype(BF16)
        cb = ml_conv_b[l][None]
        q_l, k_l = _ml_prep(z, ml_conv_w[l], cb, wq_b, wk_b, row0=0, seq_len=seq, bsz=bsz)
        q_c, k_c = _ml_prep(z, ml_conv_w[l], cb, wq_b, wk_b, row0=t_lat, seq_len=ctx_len, bsz=bsz)
        q_ml = jnp.concatenate([q_l, q_c], axis=0)
        k_ml = jnp.concatenate([k_l, k_c], axis=0)
        gates = (zg[:, :4 * nh] + ml_gate_b[l]).reshape(t_all, 2, 2 * nh)
        gates_c = jnp.transpose(gates, (1, 0, 2))
        gates_r = jnp.transpose(gates, (1, 2, 0))
        h_dirs = _mlstm_scan(q_ml, k_ml, z, gates_c, gates_r, bsz=bsz, seq=seq, ctx_len=ctx_len)

        tables = _s5_tables(s5_lam_re[l], s5_lam_im[l], s5_log_dt[l], s5_b_re[l], s5_b_im[l],
                            s5_c_re[l], s5_c_im[l], s5_d[l])
        u = z[:, Z_U:Z_U + BRANCH_W]
        y_c, y_l = _s5_mix(_s5_rows(u[t_lat:], bsz, ctx_len), _s5_rows(u[:t_lat], bsz, seq),
                           tables, bsz=bsz)
        ys = _s5_unrows(y_l, bsz, seq)
        if need_ctx:
            ys = jnp.concatenate([ys, _s5_unrows(y_c, bsz, ctx_len)], axis=0)

        x_all = _merge(x_all, ya, h_dirs, z, ys, g1, ml_norm_g[l][None],
                       s5_glu_w[l].astype(BF16), s5_glu_b[l][None], w_br[l].astype(BF16),
                       w_out[l].astype(BF16), n_rows=n_rows, bsz=bsz, seq=seq, tm=tm_merge)

        n2 = norm2_g[l][None]
        wq_p = peer_wq[l].astype(BF16)
        pq, h2t = _modnorm_matmul(x_all, n_rows, n2, sc2, sh2, wq_p, bsz=bsz, seq=seq, tm=tm,
                                  tn=wq_p.shape[1], emit_h=True)
        tabs = _peer_topk(pq, peer_sub_k[l].astype(BF16), n_rows=n_rows, tm=tm)
        te = 2048
        x_all = _peer_dense(h2t, peer_u[l].astype(BF16), _transpose_cast(peer_v[l], tr=te),
                            tabs, x_all, g2, n_rows=n_rows, bsz=bsz, seq=seq, tm=tm, te=te)

    ones = jnp.ones((bsz + 1, 1, d), F32)
    out = _modnorm(x_all, t_lat, final_g[None], ones * 0.0, ones * 0.0, bsz=bsz, seq=seq, tm=tm,
                   out_dtype=F32)
    return out.reshape(bsz, seq, d)
```

```python
import functools
import math

import jax
import jax.numpy as jnp
from jax import lax
from jax.experimental import pallas as pl
from jax.experimental.pallas import tpu as pltpu

F32 = jnp.float32
BF16 = jnp.bfloat16

EPS = 1e-6
LOG2E = 1.4426950408889634
N_MOD = 6
BRANCH_W = 512
GRID_W = 64
ROPE_BASE = 10000.0
DA_HEADS = 4
DA_HD = 64
DA_VD = 2 * DA_HD
ML_HEADS = 4
ML_HD = BRANCH_W // ML_HEADS
ML_CHUNK = 128
S5_P = 16
S5_G = BRANCH_W // S5_P
S5_N = 64
S5_T = 16
PEER_HEADS = 8
PEER_NKEYS = 128
PEER_TOPK = 16
LANES = 128
VMEM_LIMIT = 56 * 1024 * 1024

Z_GATE = 0
Z_Q, Z_K, Z_V, Z_XM, Z_VM, Z_OM, Z_U = (3072 + i * BRANCH_W for i in range(7))


def _cparams(sem):
    return pltpu.CompilerParams(dimension_semantics=sem, vmem_limit_bytes=VMEM_LIMIT)


def _nt_dot(a, b):
    return lax.dot_general(a, b, (((1,), (1,)), ((), ())), preferred_element_type=F32)


def _skip_refs(kernel_fn, start, count, *refs):
    return kernel_fn(*refs[:start], *refs[start + count:])


def _gelu(x):
    return 0.5 * x * (1.0 + lax.erf(x * (2.0 ** -0.5)))


def _log_sigmoid(x):
    return jnp.minimum(x, 0.0) - jnp.log1p(jnp.exp(-jnp.abs(x)))


def _mod_kernel(c_ref, w_ref, b_ref, o_ref):
    c = c_ref[...]
    a = c * jax.nn.sigmoid(c)
    o_ref[...] = jnp.dot(a.astype(BF16), w_ref[...], preferred_element_type=F32) + b_ref[...]


def _modulation(c_all, w, b):
    rows, d = c_all.shape
    n = w.shape[1]
    tn = 1536
    return pl.pallas_call(
        _mod_kernel,
        out_shape=jax.ShapeDtypeStruct((rows, n), F32),
        grid=(n // tn,),
        in_specs=[pl.BlockSpec((rows, d), lambda j: (0, 0)),
                  pl.BlockSpec((d, tn), lambda j: (0, j)),
                  pl.BlockSpec((1, tn), lambda j: (0, j))],
        out_specs=pl.BlockSpec((rows, tn), lambda j: (0, j)),
        compiler_params=_cparams(("parallel",)),
        name="modulation",
    )(c_all, w, b)


def _modnorm_matmul_kernel(x_ref, g_ref, sc_ref, sh_ref, w_ref, o_ref, *h_out):
    x = x_ref[...]
    y = x * lax.rsqrt(jnp.mean(x * x, axis=-1, keepdims=True) + EPS) * g_ref[...]
    hf = y * (1.0 + sc_ref[...]) + sh_ref[...]
    h = hf.astype(BF16)
    o_ref[...] = jnp.dot(h, w_ref[...], preferred_element_type=F32).astype(o_ref.dtype)
    if h_out:
        h_out[0][...] = hf.T.astype(BF16)


def _row_batch(i, n_lat_blocks, blocks_per_batch, bsz):
    return jnp.where(i < n_lat_blocks, i // blocks_per_batch, bsz)


def _modnorm_matmul(x_all, n_rows, g, sc, sh, w, *, bsz, seq, tm, tn, out_dtype=F32, emit_h=False):
    d = x_all.shape[1]
    n = w.shape[1]
    nlb = bsz * seq // tm
    bpb = seq // tm
    assert not emit_h or tn == n
    mod_spec = pl.BlockSpec((None, 1, d), lambda j, i: (_row_batch(i, nlb, bpb, bsz), 0, 0))
    out_shape = [jax.ShapeDtypeStruct((n_rows, n), out_dtype)]
    out_specs = [pl.BlockSpec((tm, tn), lambda j, i: (i, j))]
    if emit_h:
        out_shape.append(jax.ShapeDtypeStruct((d, n_rows), BF16))
        out_specs.append(pl.BlockSpec((d, tm), lambda j, i: (0, i)))
    out = pl.pallas_call(
        _modnorm_matmul_kernel,
        out_shape=tuple(out_shape),
        grid=(n // tn, n_rows // tm),
        in_specs=[pl.BlockSpec((tm, d), lambda j, i: (i, 0)),
                  pl.BlockSpec((1, d), lambda j, i: (0, 0)),
                  mod_spec, mod_spec,
                  pl.BlockSpec((d, tn), lambda j, i: (0, j))],
        out_specs=tuple(out_specs),
        compiler_params=_cparams(("parallel", "parallel")),
        name="modnorm_matmul",
    )(x_all, g, sc, sh, w)
    return out if emit_h else out[0]


def _rope_kernel(q_ref, k_ref, v_ref, cos_ref, sin_ref, qo_ref, ko_ref, vo_ref):
    cos = cos_ref[...]
    sin = sin_ref[...]
    lane = lax.broadcasted_iota(jnp.int32, cos.shape, 1)
    first_half = (lane % DA_HD) < (DA_HD // 2)

    def rope(x):
        partner = jnp.where(first_half, pltpu.roll(x, LANES - DA_HD // 2, 1),
                            pltpu.roll(x, DA_HD // 2, 1))
        return x * cos + partner * sin

    for h in range(DA_HEADS):
        sl = slice(h * LANES, (h + 1) * LANES)
        qo_ref[:, sl] = (rope(q_ref[:, sl].astype(F32)) * (DA_HD ** -0.5 * LOG2E)).astype(BF16)
        ko_ref[:, sl] = rope(k_ref[:, sl].astype(F32)).astype(BF16)
    vo_ref[...] = v_ref[...].astype(BF16)


def _rope_qkv(z, cos_tab, sin_tab, *, bsz, seq, tm):
    rows = z.shape[0]
    nlb = bsz * seq // tm
    bpb = seq // tm
    w = BRANCH_W
    tab_spec = pl.BlockSpec((tm, LANES), lambda i: (jnp.where(i < nlb, i % bpb, bpb), 0))
    out = jax.ShapeDtypeStruct((rows, w), BF16)
    return pl.pallas_call(
        _rope_kernel,
        out_shape=(out, out, out),
        grid=(rows // tm,),
        in_specs=[pl.BlockSpec((tm, w), lambda i: (i, Z_Q // w)),
                  pl.BlockSpec((tm, w), lambda i: (i, Z_K // w)),
                  pl.BlockSpec((tm, w), lambda i: (i, Z_V // w)),
                  tab_spec, tab_spec],
        out_specs=(pl.BlockSpec((tm, w), lambda i: (i, 0)),) * 3,
        compiler_params=_cparams(("parallel",)),
        name="rope_qkv",
    )(z, z, z, cos_tab, sin_tab)


def _attn_kernel(*refs, n_seg, n_sub, out_scale):
    lam_ref, q_ref = refs[0], refs[1]
    k_refs = refs[2:2 + n_seg]
    v_refs = refs[2 + n_seg:2 + 2 * n_seg]
    g_ref, o_ref = refs[2 + 2 * n_seg], refs[3 + 2 * n_seg]
    tq = q_ref.shape[0] // n_sub
    lane = lax.broadcasted_iota(jnp.int32, (tq, LANES), 1)
    scores = []
    for sub in range(n_sub):
        q = q_ref[sub * tq:(sub + 1) * tq, :].astype(F32)
        qq = jnp.concatenate([jnp.where(lane < DA_HD, q, 0.0), jnp.where(lane >= DA_HD, q, 0.0)],
                             axis=0).astype(BF16)
        scores.append([_nt_dot(qq, k_ref[...]) for k_ref in k_refs])
    for sub in range(n_sub):
        rows = slice(sub * tq, (sub + 1) * tq)
        s = scores[sub]
        m = functools.reduce(jnp.maximum, [jnp.max(t, axis=-1, keepdims=True) for t in s])
        p = [jnp.exp2(t - m) for t in s]
        denom = functools.reduce(jnp.add, [jnp.sum(t, axis=-1, keepdims=True) for t in p])
        inv = 1.0 / denom
        c0 = inv[:tq]
        c1 = inv[tq:] * lam_ref[...]
        o = None
        for t, v_ref in zip(p, v_refs):
            w = (t[:tq] * c0 - t[tq:] * c1).astype(BF16)
            part = jnp.dot(w, v_ref[...], preferred_element_type=F32)
            o = part if o is None else o + part
        y = o * lax.rsqrt(jnp.mean(o * o, axis=-1, keepdims=True) + EPS) * g_ref[...]
        o_ref[rows, :] = (y * out_scale).astype(o_ref.dtype)


def _diff_attention(qr, kr, vb, lam, sub_g, *, q_row0, n_q, kv_segs, bsz, tq, out_scale, out_rows,
                    into=None):
    n_sub = 2 if n_q % (2 * tq) == 0 else 1
    tq = tq * n_sub
    nqb = n_q // tq
    n_seg = len(kv_segs)
    q_blk0 = q_row0 // tq

    def kv_spec(row0, length):
        return pl.BlockSpec((length, LANES), lambda b, h, i: (row0 // length + b, h))

    in_specs = [pl.BlockSpec((1, 1), lambda b, h, i: (0, 0)),
                pl.BlockSpec((tq, LANES), lambda b, h, i: (q_blk0 + b * nqb + i, h))]
    in_specs += [kv_spec(r0, ln) for r0, ln in kv_segs] * 2
    in_specs += [pl.BlockSpec((1, LANES), lambda b, h, i: (0, h))]
    args = [lam, qr] + [kr] * n_seg + [vb] * n_seg + [sub_g]
    kern = functools.partial(_attn_kernel, n_seg=n_seg, n_sub=n_sub, out_scale=out_scale)
    aliases = {}
    if into is not None:
        aliases = {len(args): 0}
        kern = functools.partial(_skip_refs, kern, len(args), 1)
        in_specs += [pl.BlockSpec(memory_space=pl.ANY)]
        args += [into]
    return pl.pallas_call(
        kern,
        out_shape=jax.ShapeDtypeStruct((out_rows, BRANCH_W), BF16),
        grid=(bsz, DA_HEADS, nqb),
        in_specs=in_specs,
        out_specs=pl.BlockSpec((tq, LANES), lambda b, h, i: (q_blk0 + b * nqb + i, h)),
        input_output_aliases=aliases,
        compiler_params=_cparams(("parallel", "parallel", "parallel")),
        name="diff_attention",
    )(*args)


def _ml_prep_kernel(x_ref, cw_ref, cb_ref, wq_ref, wk_ref, q_ref, k_ref, kt_ref):
    x = x_ref[...].astype(F32)
    n = x.shape[0]
    row = lax.broadcasted_iota(jnp.int32, x.shape, 0)
    x_prev = jnp.where(row == 0, 0.0, pltpu.roll(x, 1, 0))
    x_next = jnp.where(row == n - 1, 0.0, pltpu.roll(x, n - 1, 0))
    cw = cw_ref[...]
    y = x_prev * cw[0:1] + x * cw[1:2] + x_next * cw[2:3] + cb_ref[...]
    xc = (y * jax.nn.sigmoid(y)).astype(BF16)
    q_ref[...] = (jnp.dot(xc, wq_ref[...], preferred_element_type=F32) * (ML_HD ** -0.5)
                  ).astype(q_ref.dtype)
    k = jnp.dot(xc, wk_ref[...], preferred_element_type=F32)
    k_ref[...] = k.astype(k_ref.dtype)
    kt_ref[...] = k.T.astype(kt_ref.dtype)


def _ml_prep(z, conv_w, conv_b, wq, wk, *, row0, seq_len, bsz, into=None):
    rows = z.shape[0]
    blk0 = row0 // seq_len
    out = jax.ShapeDtypeStruct((rows, BRANCH_W), BF16)
    out_t = jax.ShapeDtypeStruct((BRANCH_W, rows), BF16)
    head_w = pl.BlockSpec((None, ML_HD, ML_HD), lambda b, h: (h, 0, 0))
    row_out = pl.BlockSpec((seq_len, LANES), lambda b, h: (blk0 + b, h))
    in_specs = [pl.BlockSpec((seq_len, LANES), lambda b, h: (blk0 + b, Z_XM // LANES + h)),
                pl.BlockSpec((3, LANES), lambda b, h: (0, h)),
                pl.BlockSpec((1, LANES), lambda b, h: (0, h)),
                head_w, head_w]
    args = [z, conv_w, conv_b, wq, wk]
    aliases = {}
    kern = _ml_prep_kernel
    if into is not None:
        n_in = len(args)
        in_specs += [pl.BlockSpec(memory_space=pl.ANY)] * len(into)
        args += list(into)
        aliases = {n_in + i: i for i in range(len(into))}
        kern = functools.partial(_skip_refs, _ml_prep_kernel, n_in, len(into))
    return pl.pallas_call(
        kern,
        out_shape=(out, out, out_t),
        grid=(bsz, ML_HEADS),
        in_specs=in_specs,
        out_specs=(row_out, row_out, pl.BlockSpec((LANES, seq_len), lambda b, h: (h, blk0 + b))),
        input_output_aliases=aliases,
        compiler_params=_cparams(("parallel", "parallel")),
        name="mlstm_prep",
    )(*args)


def _mlstm_kernel(qf_ref, kf_ref, ktf_ref, vf_ref, gcf_ref, grf_ref,
                  qb_ref, kb_ref, ktb_ref, vb_ref, gcb_ref, grb_ref,
                  hf_ref, hb_ref, c_ref, n_ref, m_ref):
    @pl.when(pl.program_id(1) == 0)
    def _():
        c_ref[...] = jnp.zeros_like(c_ref)
        n_ref[...] = jnp.zeros_like(n_ref)
        m_ref[...] = jnp.zeros_like(m_ref)

    t_idx = lax.broadcasted_iota(jnp.int32, (ML_CHUNK, ML_CHUNK), 0)
    s_idx = lax.broadcasted_iota(jnp.int32, (ML_CHUNK, ML_CHUNK), 1)
    nh = ML_HEADS
    dirs = ((qf_ref, kf_ref, ktf_ref, vf_ref, gcf_ref, grf_ref, hf_ref),
            (qb_ref, kb_ref, ktb_ref, vb_ref, gcb_ref, grb_ref, hb_ref))
    ones = jnp.ones((ML_CHUNK, LANES), BF16)
    units = []
    for d, (q_ref, k_ref, kt_ref, v_ref, gc_ref, gr_ref, h_ref) in enumerate(dirs):
        mask = s_idx <= t_idx if d == 0 else s_idx >= t_idx
        mask_f = mask.astype(F32)
        gc = gc_ref[...]
        gr = gr_ref[...]
        lf_c = _log_sigmoid(gc)
        lf_r = _log_sigmoid(gr)
        b_c = jnp.dot(mask_f, lf_c, preferred_element_type=F32, precision=lax.Precision.HIGHEST)
        b_r = lax.dot_general(lf_r, mask_f, (((1,), (1,)), ((), ())), preferred_element_type=F32,
                              precision=lax.Precision.HIGHEST)
        b_end_all = jnp.sum(lf_c, axis=0, keepdims=True)

        for h in range(nh):
            sl = slice(h * ML_HD, (h + 1) * ML_HD)
            q = q_ref[:, sl]
            k = k_ref[:, sl]
            kt = kt_ref[sl, :]
            v = v_ref[:, sl].astype(F32)
            li_c = gc[:, h:h + 1]
            li_r = gr[h:h + 1, :]
            bc = b_c[:, nh + h:nh + h + 1]
            br = b_r[nh + h:nh + h + 1, :]
            b_end = b_end_all[:, nh + h:nh + h + 1]
            m_prev = m_ref[d, h]
            c_prev = c_ref[d, h]
            n_prev = n_ref[d, h]
            g_c = b_end - bc + li_c
            g_r = b_end - br + li_r
            m_new = jnp.maximum(b_end + m_prev, jnp.max(g_r, axis=-1, keepdims=True))
            wg = jnp.broadcast_to(jnp.exp(g_c - m_new), (ML_CHUNK, LANES))
            decay = jnp.exp(b_end + m_prev - m_new)
            vw = (v * wg).astype(BF16)
            units.append(dict(
                d=d, h=h, sl=sl, h_ref=h_ref, v=v, mask=mask, m_prev=m_prev,
                bc=bc, br=br, li_r=li_r,
                qk=_nt_dot(q, k),
                qc=jnp.dot(q, c_prev.astype(BF16), preferred_element_type=F32),
                qn=jnp.dot(q, n_prev.astype(BF16), preferred_element_type=F32),
                c_new=decay * c_prev + jnp.dot(kt, vw, preferred_element_type=F32),
                n_new=decay * n_prev + jnp.dot(kt, wg.astype(BF16), preferred_element_type=F32),
                m_new=m_new))

    for u in units:
        dmat = jnp.where(u["mask"], u["bc"] - u["br"] + u["li_r"], -jnp.inf)
        inter = u["bc"] + u["m_prev"]
        m_t = jnp.maximum(inter, jnp.max(dmat, axis=-1, keepdims=True))
        s = (u["qk"] * jnp.exp(dmat - m_t)).astype(BF16)
        w_prev = jnp.exp(inter - m_t)
        num = jnp.dot(s, u["v"].astype(BF16), preferred_element_type=F32) + w_prev * u["qc"]
        den = jnp.dot(s, ones, preferred_element_type=F32) + w_prev * u["qn"]
        u["h_ref"][:, u["sl"]] = num / jnp.maximum(jnp.abs(den), jnp.exp(-m_t))
        c_ref[u["d"], u["h"]] = u["c_new"]
        n_ref[u["d"], u["h"]] = u["n_new"]
        m_ref[u["d"], u["h"]] = u["m_new"]


def _mlstm_scan(q_all, k_all, kt_all, z, gates_c, gates_r, *, bsz, seq, ctx_len):
    rows = q_all.shape[0]
    ncl = seq // ML_CHUNK
    ncc = ctx_len // ML_CHUNK
    lat_blocks = bsz * ncl
    nh = ML_HEADS

    def blk(d, b, s):
        jc = s if d == 0 else ncc - 1 - s
        jl = s - ncc if d == 0 else ncl - 1 - (s - ncc)
        return jnp.where(s < ncc, lat_blocks + b * ncc + jc, b * ncl + jl)

    def dir_specs(d):
        return [pl.BlockSpec((ML_CHUNK, BRANCH_W), lambda b, s: (blk(d, b, s), 0)),
                pl.BlockSpec((ML_CHUNK, BRANCH_W), lambda b, s: (blk(d, b, s), 0)),
                pl.BlockSpec((BRANCH_W, ML_CHUNK), lambda b, s: (0, blk(d, b, s))),
                pl.BlockSpec((ML_CHUNK, BRANCH_W), lambda b, s: (blk(d, b, s), Z_VM // BRANCH_W)),
                pl.BlockSpec((None, ML_CHUNK, 2 * nh), lambda b, s: (d, blk(d, b, s), 0)),
                pl.BlockSpec((None, 2 * nh, ML_CHUNK), lambda b, s: (d, 0, blk(d, b, s)))]

    out = jax.ShapeDtypeStruct((rows, BRANCH_W), F32)
    args = (q_all, k_all, kt_all, z, gates_c, gates_r)
    return pl.pallas_call(
        _mlstm_kernel,
        out_shape=(out, out),
        grid=(bsz, ncc + ncl),
        in_specs=dir_specs(0) + dir_specs(1),
        out_specs=(pl.BlockSpec((ML_CHUNK, BRANCH_W), lambda b, s: (blk(0, b, s), 0)),
                   pl.BlockSpec((ML_CHUNK, BRANCH_W), lambda b, s: (blk(1, b, s), 0))),
        scratch_shapes=[pltpu.VMEM((2, nh, ML_HD, ML_HD), F32),
                        pltpu.VMEM((2, nh, ML_HD, LANES), F32),
                        pltpu.VMEM((2, nh, 1, 1), F32)],
        compiler_params=_cparams(("parallel", "arbitrary")),
        name="mlstm_scan",
    )(*args, *args)


def _s5_kernel(uc_ref, ul_ref, m_ref, w_ref, v_ref, a1_ref, a2_ref, yc_ref, yl_ref,
               ec_ref, el_ref, xc_ref, xl_ref, *, bsz):
    w = w_ref[...]
    ec_ref[...] = jnp.dot(uc_ref[...], w, preferred_element_type=F32)
    el_ref[...] = jnp.dot(ul_ref[...], w, preferred_element_type=F32)
    a1 = a1_ref[...]
    a2 = a2_ref[...]
    half = LANES
    a1f, a2f, a1b, a2b = a1[:, 0:half], a2[:, 0:half], a1[:, half:], a2[:, half:]

    def scan(e_ref, x_ref, state):
        n_chunks = e_ref.shape[0] // bsz

        def body(j, st):
            xf, xfs, xb, xbs = st
            rf = pl.multiple_of(j * bsz, bsz)
            rb = pl.multiple_of((n_chunks - 1 - j) * bsz, bsz)
            x_ref[pl.ds(rf, bsz), 0:half] = xf
            x_ref[pl.ds(rb, bsz), half:2 * half] = xb
            ef, eb = e_ref[pl.ds(rf, bsz), 0:half], e_ref[pl.ds(rb, bsz), half:2 * half]
            efs = e_ref[pl.ds(rf, bsz), 2 * half:3 * half]
            ebs = e_ref[pl.ds(rb, bsz), 3 * half:4 * half]
            return (a1f * xf + a2f * xfs + ef, a1f * xfs - a2f * xf + efs,
                    a1b * xb + a2b * xbs + eb, a1b * xbs - a2b * xb + ebs)

        return lax.fori_loop(0, n_chunks, body, state)

    zero = jnp.zeros((bsz, half), F32)
    state = scan(ec_ref, xc_ref, (zero, zero, zero, zero))
    scan(el_ref, xl_ref, state)
    mm = m_ref[...]
    vv = v_ref[...]
    yc_ref[...] = (jnp.dot(uc_ref[...], mm, preferred_element_type=F32)
                   + jnp.dot(xc_ref[...].astype(BF16), vv, preferred_element_type=F32)
                   ).astype(yc_ref.dtype)
    yl_ref[...] = (jnp.dot(ul_ref[...], mm, preferred_element_type=F32)
                   + jnp.dot(xl_ref[...].astype(BF16), vv, preferred_element_type=F32)
                   ).astype(yl_ref.dtype)


def _s5_tables(lam_re, lam_im, log_dt, b_re, b_im, c_re, c_im, d_skip):
    t, g, n, p = S5_T, S5_G, S5_N, S5_P
    dt = jnp.exp(log_dt)[..., None]
    den = lam_re * lam_re + lam_im * lam_im
    kk = jnp.arange(t + 1, dtype=F32)[None, None, :, None]
    mag = jnp.exp(kk * (lam_re * dt)[:, :, None, :])
    ang = kk * (lam_im * dt)[:, :, None, :]
    p_re, p_im = mag * jnp.cos(ang), mag * jnp.sin(ang)
    ab_re, ab_im = p_re[:, :, 1], p_im[:, :, 1]
    z_re = ((ab_re - 1.0) * lam_re + ab_im * lam_im) / den
    z_im = (ab_im * lam_re - (ab_re - 1.0) * lam_im) / den
    bt_re, bt_im = jnp.swapaxes(b_re, 2, 3), jnp.swapaxes(b_im, 2, 3)
    bb_re = z_re[:, :, None, :] * bt_re - z_im[:, :, None, :] * bt_im
    bb_im = z_re[:, :, None, :] * bt_im + z_im[:, :, None, :] * bt_re
    pk_re, pk_im = p_re[:, :, :t, None, :], p_im[:, :, :t, None, :]
    abk_re = pk_re * bb_re[:, :, None] - pk_im * bb_im[:, :, None]
    abk_im = pk_re * bb_im[:, :, None] + pk_im * bb_re[:, :, None]
    taps = (jnp.einsum('rgpn,rgkqn->rgqkp', c_re, abk_re)
            - jnp.einsum('rgpn,rgkqn->rgqkp', c_im, abk_im))
    base_f = taps[0].reshape(g, p, t * p)
    base_b = taps[1][:, :, ::-1].reshape(g, p, t * p)
    width = t * p
    m_f = jnp.stack([jnp.pad(base_f, ((0, 0), (0, 0), (p * s, 0)))[..., :width]
                     for s in range(t)], axis=1)
    m_b = jnp.stack([jnp.pad(base_b, ((0, 0), (0, 0), (0, p * (t - 1 - s))))[..., p * (t - 1 - s):]
                     for s in range(t)], axis=1)
    skip = jnp.eye(width, dtype=F32)[None] * jnp.tile(d_skip.reshape(g, p), (1, t))[:, None, :]
    m = (m_f + m_b).reshape(g, width, width) + skip
    wf_re, wf_im = (a[0][:, ::-1].reshape(g, width, n) for a in (abk_re, abk_im))
    wb_re, wb_im = (a[1].reshape(g, width, n) for a in (abk_re, abk_im))
    w = jnp.concatenate([wf_re, wf_im, wb_re, wb_im, wf_im, wf_re, wb_im, wb_re], axis=-1)
    ct_re, ct_im = jnp.swapaxes(c_re, 2, 3), jnp.swapaxes(c_im, 2, 3)
    pn_re, pn_im = jnp.swapaxes(p_re, 2, 3)[..., None], jnp.swapaxes(p_im, 2, 3)[..., None]
    ca_re = ct_re[:, :, :, None, :] * pn_re - ct_im[:, :, :, None, :] * pn_im
    ca_im = ct_re[:, :, :, None, :] * pn_im + ct_im[:, :, :, None, :] * pn_re
    v = jnp.concatenate([ca_re[0][:, :, 1:].reshape(g, n, width),
                         -ca_im[0][:, :, 1:].reshape(g, n, width),
                         ca_re[1][:, :, :0:-1].reshape(g, n, width),
                         -ca_im[1][:, :, :0:-1].reshape(g, n, width)], axis=1)
    at_re, at_im = p_re[:, :, t], p_im[:, :, t]
    a1 = jnp.concatenate([at_re[0], at_re[0], at_re[1], at_re[1]], axis=-1)[:, None, :]
    a2 = jnp.concatenate([-at_im[0], at_im[0], -at_im[1], at_im[1]], axis=-1)[:, None, :]
    return m.astype(BF16), w.astype(BF16), v.astype(BF16), a1, a2


def _s5_rows(u, bsz, length):
    nc = length // S5_T
    u = u.reshape(bsz, nc, S5_T, S5_G, S5_P)
    return jnp.transpose(u, (3, 1, 0, 2, 4)).reshape(S5_G, nc * bsz, S5_T * S5_P)


def _s5_unrows(y, bsz, length):
    nc = length // S5_T
    y = y.reshape(S5_G, nc, bsz, S5_T, S5_P)
    return jnp.transpose(y, (2, 1, 3, 0, 4)).reshape(bsz * length, BRANCH_W)


def _s5_mix(u_ctx, u_lat, tables, *, bsz):
    m, w, v, a1, a2 = tables
    rc, rl = u_ctx.shape[1], u_lat.shape[1]
    wd = S5_T * S5_P

    def rows_spec(r):
        return pl.BlockSpec((None, r, wd), lambda g: (g, 0, 0))

    sq = pl.BlockSpec((None, wd, wd), lambda g: (g, 0, 0))
    vec = pl.BlockSpec((None, 1, wd), lambda g: (g, 0, 0))
    return pl.pallas_call(
        functools.partial(_s5_kernel, bsz=bsz),
        out_shape=(jax.ShapeDtypeStruct((S5_G, rc, wd), BF16),
                   jax.ShapeDtypeStruct((S5_G, rl, wd), BF16)),
        grid=(S5_G,),
        in_specs=[rows_spec(rc), rows_spec(rl), sq,
                  pl.BlockSpec((None, wd, 2 * wd), lambda g: (g, 0, 0)), sq, vec, vec],
        out_specs=(rows_spec(rc), rows_spec(rl)),
        scratch_shapes=[pltpu.VMEM((rc, 2 * wd), F32), pltpu.VMEM((rl, 2 * wd), F32),
                        pltpu.VMEM((rc, wd), F32), pltpu.VMEM((rl, wd), F32)],
        compiler_params=_cparams(("parallel",)),
        name="s5_mix",
    )(u_ctx, u_lat, m, w, v, a1, a2)


def _merge_kernel(x_ref, ya_ref, hf_ref, hb_ref, om_ref, ys_ref, gate_ref, g1_ref, mlg_ref,
                  gluw_ref, glub_ref, wbr_ref, wout_ref, o_ref):
    hsum = hf_ref[...] + hb_ref[...]
    og = jax.nn.sigmoid(om_ref[...].astype(F32))
    mlg = mlg_ref[...]
    yb_parts = []
    for h in range(ML_HEADS):
        sl = slice(h * ML_HD, (h + 1) * ML_HD)
        hh = hsum[:, sl]
        hn = hh * lax.rsqrt(jnp.mean(hh * hh, axis=-1, keepdims=True) + EPS) * mlg[:, sl]
        yb_parts.append(hn * og[:, sl])
    yb = jnp.concatenate(yb_parts, axis=-1)
    gl = _gelu(ys_ref[...].astype(F32))
    ys = gl * jax.nn.sigmoid(
        jnp.dot(gl.astype(BF16), gluw_ref[...], preferred_element_type=F32) + glub_ref[...])
    d = o_ref.shape[1]
    merged = None
    for r, y in enumerate((ya_ref[...], yb, ys)):
        proj = jnp.dot(y.astype(BF16), wbr_ref[r], preferred_element_type=F32)
        term = jax.nn.sigmoid(gate_ref[:, r * d:(r + 1) * d].astype(F32)) * proj
        merged = term if merged is None else merged + term
    y = jnp.dot(merged.astype(BF16), wout_ref[...], preferred_element_type=F32)
    o_ref[...] = x_ref[...] + g1_ref[...] * y


def _merge(x_all, ya, h_dirs, z, ys, g1, ml_norm_g, glu_w, glu_b, w_br, w_out, *, n_rows, bsz, seq, tm):
    d = x_all.shape[1]
    nlb = bsz * seq // tm
    bpb = seq // tm
    w = BRANCH_W

    def rows(width, col_blk=0):
        return pl.BlockSpec((tm, width), lambda i: (i, col_blk))

    def full(shape):
        return pl.BlockSpec(shape, lambda i: (0,) * len(shape))

    return pl.pallas_call(
        _merge_kernel,
        out_shape=jax.ShapeDtypeStruct((n_rows, d), F32),
        grid=(n_rows // tm,),
        in_specs=[rows(d), rows(w), rows(w), rows(w),
                  rows(w, Z_OM // w), rows(w),
                  rows(3 * d, Z_GATE // (3 * d)),
                  pl.BlockSpec((None, 1, d), lambda i: (_row_batch(i, nlb, bpb, bsz), 0, 0)),
                  full((1, w)), full((w, w)), full((1, w)), full((3, w, d)), full((d, d))],
        out_specs=rows(d),
        compiler_params=_cparams(("parallel",)),
        name="merge",
    )(x_all, ya, h_dirs[0], h_dirs[1], z, ys, z, g1, ml_norm_g, glu_w, glu_b, w_br, w_out)


NOT_TOP = 127.0


def _take_top(s, n_take, vals_ref, want_rank):
    rows = s.shape[0]
    ridx = lax.broadcasted_iota(jnp.int32, s.shape, 0).astype(F32)
    rank = jnp.full(s.shape, NOT_TOP, F32) if want_rank else None
    for i in range(n_take):
        mx = jnp.max(s, axis=0, keepdims=True)
        first = jnp.min(jnp.where(s == mx, ridx, float(rows)), axis=0, keepdims=True)
        hit = ridx == first
        s = jnp.where(hit, -jnp.inf, s)
        if want_rank:
            rank = jnp.where(hit, float(i), rank)
        vals_ref[i:i + 1, :] = mx
    return s, rank


SUBLANES = 8


def _sort_network_pairs(n):
    pairs = []
    p = 1
    while p < n:
        k = p
        while k >= 1:
            for j in range(k % p, n - k, 2 * k):
                for i in range(min(k, n - j - k)):
                    if (i + j) // (2 * p) == (i + j + k) // (2 * p):
                        pairs.append((i + j, i + j + k))
            k //= 2
        p *= 2
    return pairs


def _top_sorted(s):
    n = s.shape[0] // SUBLANES
    assert n == PEER_TOPK
    x = [s[SUBLANES * i:SUBLANES * (i + 1), :] for i in range(n)]

    def exchange(i, j):
        x[i], x[j] = jnp.maximum(x[i], x[j]), jnp.minimum(x[i], x[j])

    for i, j in _sort_network_pairs(n):
        exchange(i, j)
    shift = SUBLANES // 2
    while shift >= 1:
        y = [pltpu.roll(v, shift, 0) for v in x]
        x = [jnp.maximum(x[i], y[n - 1 - i]) for i in range(n)]
        d = n // 2
        while d >= 1:
            for i in range(n):
                if not i & d:
                    exchange(i, i + d)
            d //= 2
        shift //= 2
    return x


def _slabs(s):
    return [s[SUBLANES * i:SUBLANES * (i + 1), :] for i in range(s.shape[0] // SUBLANES)]


def _tie_count(s, top):
    flag = jnp.zeros(top[0].shape, F32)
    for i in range(len(top) - 1):
        flag = flag + jnp.where(top[i] == top[i + 1], 1.0, 0.0)
    reach = functools.reduce(jnp.add, [jnp.where(v >= top[-1], 1.0, 0.0) for v in _slabs(s)])
    reach = jnp.sum(reach, axis=0, keepdims=True)
    return flag + jnp.where(reach != float(PEER_TOPK), 1.0, 0.0)


def _candidate_counts(v1_ref, v2_ref, top_ref, cnt_ref):
    k = PEER_TOPK
    v2_head = v2_ref[0:8, :]
    jrow = lax.broadcasted_iota(jnp.int32, v2_head.shape, 0)
    pieces = [v1_ref[0:1, :] + v2_ref[...]]
    for i in range(1, 8):
        pieces.append(jnp.where(jrow < k // (i + 1), v1_ref[i:i + 1, :] + v2_head, -jnp.inf))
    pieces.append(v1_ref[8:16, :] + v2_ref[0:1, :])
    cand = jnp.concatenate(pieces, axis=0)
    cand_left, _ = _take_top(cand, k, top_ref, False)
    picked = jnp.where(cand_left != cand, 1.0, 0.0)
    cnt_ref[0:1, :] = jnp.sum(picked[0:16], axis=0, keepdims=True)
    for i in range(1, 8):
        cnt_ref[i:i + 1, :] = jnp.sum(picked[8 + 8 * i:16 + 8 * i], axis=0, keepdims=True)
    cnt_ref[8:16, :] = picked[72:80]
    top = top_ref[...]
    return jnp.sum(jnp.exp(top - top[0:1, :]), axis=0, keepdims=True)


def _peer_topk_kernel(q_ref, k1_ref, k2_ref, e1_ref, cnt1_ref, e2_ref, rank2_ref,
                      v1_ref, v2_ref, top_ref, cnt_ref):
    k = PEER_TOPK
    q = q_ref[...].astype(BF16)
    s1 = _nt_dot(k1_ref[...], q[:, :LANES])
    s2 = _nt_dot(k2_ref[...], q[:, LANES:])
    top1 = _top_sorted(s1)
    top2 = _top_sorted(s2)
    any_tie = jnp.max(_tie_count(s1, top1) + _tie_count(s2, top2)) > 0.0

    @pl.when(any_tie)
    def _():
        _, rank1 = _take_top(s1, k, v1_ref, True)
        _, rank2 = _take_top(s2, k, v2_ref, True)
        zsum = _candidate_counts(v1_ref, v2_ref, top_ref, cnt_ref)
        cnt1 = jnp.zeros(s1.shape, F32)
        for i in range(k):
            cnt1 = jnp.where(rank1 == float(i), cnt_ref[i:i + 1, :], cnt1)
        e1_ref[...] = jnp.where(rank1 < k, jnp.exp(s1 - v1_ref[0:1, :]), 0.0) / zsum
        cnt1_ref[...] = cnt1
        e2_ref[...] = jnp.where(rank2 < k, jnp.exp(s2 - v2_ref[0:1, :]), 0.0).astype(BF16)
        rank2_ref[...] = rank2.astype(BF16)

    @pl.when(jnp.logical_not(any_tie))
    def _():
        for i in range(k):
            v1_ref[i:i + 1, :] = top1[i][0:1, :]
            v2_ref[i:i + 1, :] = top2[i][0:1, :]
        zsum = _candidate_counts(v1_ref, v2_ref, top_ref, cnt_ref)
        cnt_rows = [jnp.broadcast_to(cnt_ref[i:i + 1, :], top1[0].shape) for i in range(k)]
        cnt1, rank2 = [], []
        for v in _slabs(s1):
            c = jnp.zeros(v.shape, F32)
            for i in range(k):
                c = jnp.where(v == top1[i], cnt_rows[i], c)
            cnt1.append(c)
        for v in _slabs(s2):
            r = jnp.full(v.shape, NOT_TOP, F32)
            for i in range(k):
                r = jnp.where(v == top2[i], float(i), r)
            rank2.append(r)
        e1_ref[...] = jnp.where(s1 >= v1_ref[k - 1:k, :], jnp.exp(s1 - v1_ref[0:1, :]), 0.0) / zsum
        cnt1_ref[...] = jnp.concatenate(cnt1, axis=0)
        e2_ref[...] = jnp.where(s2 >= v2_ref[k - 1:k, :], jnp.exp(s2 - v2_ref[0:1, :]), 0.0
                                ).astype(BF16)
        rank2_ref[...] = jnp.concatenate(rank2, axis=0).astype(BF16)


def _peer_topk(q, sub_k, *, n_rows, tm):
    nk = PEER_NKEYS
    row_tab = jax.ShapeDtypeStruct((n_rows // tm, PEER_HEADS, nk, tm), F32)
    tile_tab = jax.ShapeDtypeStruct((n_rows // tm, PEER_HEADS, nk, tm), BF16)
    tab_spec = pl.BlockSpec((None, None, nk, tm), lambda i, h: (i, h, 0, 0))
    return pl.pallas_call(
        _peer_topk_kernel,
        out_shape=(row_tab, row_tab, tile_tab, tile_tab),
        grid=(n_rows // tm, PEER_HEADS),
        in_specs=[pl.BlockSpec((tm, 2 * LANES), lambda i, h: (i, h)),
                  pl.BlockSpec((None, nk, LANES), lambda i, h: (0, 0, 0)),
                  pl.BlockSpec((None, nk, LANES), lambda i, h: (1, 0, 0))],
        out_specs=(tab_spec,) * 4,
        scratch_shapes=[pltpu.VMEM((PEER_TOPK, tm), F32)] * 4,
        compiler_params=_cparams(("parallel", "parallel")),
        name="peer_topk",
    )(q, sub_k, sub_k)


GATE_KEYS = 2
KEY_GROUP = 8


def _peer_dense_kernel(ht_ref, u_ref, vt_ref, e1_ref, cnt1_ref, e2_ref, rank2_ref,
                       x_ref, g2_ref, fg_ref, o_ref, acc_ref, act_ref, p_ref, e2s_ref, rank2s_ref, *,
                       a_per_blk, final_norm):
    j = pl.program_id(1)
    n_blk = pl.num_programs(1) - 1
    cur = j % 2
    tm = ht_ref.shape[1]
    nk = PEER_NKEYS

    @pl.when(j == 0)
    def _():
        acc_ref[...] = jnp.zeros_like(acc_ref)
        p_ref[1] = jnp.zeros(p_ref.shape[1:], p_ref.dtype)
        e2s_ref[...] = e2_ref[...]
        rank2s_ref[...] = rank2_ref[...]

    @pl.when(j < n_blk)
    def _():
        act_ref[...] = _gelu(jnp.dot(u_ref[...], ht_ref[...], preferred_element_type=F32)
                             ).astype(BF16)
        acc_ref[...] += jnp.dot(vt_ref[...], p_ref[1 - cur], preferred_element_type=F32)
        a0 = pl.multiple_of(j * a_per_blk, KEY_GROUP)

        def row_bf16(ref, h, i, ls):
            grp = (i // KEY_GROUP) * KEY_GROUP
            w = ref[h, pl.ds(a0 + grp, KEY_GROUP), ls][i % KEY_GROUP:i % KEY_GROUP + 1]
            return jnp.broadcast_to(w, (nk, LANES)).astype(BF16)

        for lt in range(tm // LANES):
            ls = slice(lt * LANES, (lt + 1) * LANES)
            for i0 in range(0, a_per_blk, GATE_KEYS):
                gates = [None] * GATE_KEYS
                for h in range(PEER_HEADS):
                    for ii in range(GATE_KEYS):
                        lead = row_bf16(cnt1_ref, h, i0 + ii, ls) - rank2s_ref[h, :, ls]
                        term = jnp.minimum(jnp.maximum(lead, 0.0),
                                           row_bf16(e1_ref, h, i0 + ii, ls)) * e2s_ref[h, :, ls]
                        gates[ii] = term if gates[ii] is None else gates[ii] + term
                for ii in range(GATE_KEYS):
                    r0 = (i0 + ii) * nk
                    p_ref[cur, r0:r0 + nk, ls] = gates[ii] * act_ref[r0:r0 + nk, ls]

    @pl.when(j == n_blk)
    def _():
        acc = acc_ref[...] + jnp.dot(vt_ref[...], p_ref[1 - cur], preferred_element_type=F32)
        y = x_ref[...] + g2_ref[...] * acc.T
        if final_norm:
            y = y * lax.rsqrt(jnp.mean(y * y, axis=-1, keepdims=True) + EPS) * fg_ref[...]
        o_ref[...] = y


def _peer_dense(h2, u_tab, vt_tab, tabs, x_all, g2, final_g, *, n_rows, bsz, seq, tm, te,
                final_norm):
    d = x_all.shape[1]
    n_exp = u_tab.shape[0]
    nlb = bsz * seq // tm
    bpb = seq // tm
    nk = PEER_NKEYS
    n_blk = n_exp // te
    assert (te // nk) % KEY_GROUP == 0
    tab_spec = pl.BlockSpec((None, PEER_HEADS, nk, tm), lambda i, j: (i, 0, 0, 0))
    return pl.pallas_call(
        functools.partial(_peer_dense_kernel, a_per_blk=te // nk, final_norm=final_norm),
        out_shape=jax.ShapeDtypeStruct((n_rows, d), F32),
        grid=(n_rows // tm, n_blk + 1),
        in_specs=[pl.BlockSpec((d, tm), lambda i, j: (0, i)),
                  pl.BlockSpec((te, d), lambda i, j: (jnp.minimum(j, n_blk - 1), 0)),
                  pl.BlockSpec((None, d, te), lambda i, j: (jnp.maximum(j - 1, 0), 0, 0)),
                  tab_spec, tab_spec, tab_spec, tab_spec,
                  pl.BlockSpec((tm, d), lambda i, j: (i, 0)),
                  pl.BlockSpec((None, 1, d), lambda i, j: (_row_batch(i, nlb, bpb, bsz), 0, 0)),
                  pl.BlockSpec((1, d), lambda i, j: (0, 0))],
        out_specs=pl.BlockSpec((tm, d), lambda i, j: (i, 0)),
        scratch_shapes=[pltpu.VMEM((d, tm), F32), pltpu.VMEM((te, tm), BF16),
                        pltpu.VMEM((2, te, tm), BF16),
                        pltpu.VMEM((PEER_HEADS, nk, tm), BF16), pltpu.VMEM((PEER_HEADS, nk, tm), BF16)],
        compiler_params=_cparams(("parallel", "arbitrary")),
        name="peer_dense",
    )(h2, u_tab, vt_tab, *tabs, x_all, g2, final_g)


def _transpose_cast_kernel(x_ref, o_ref):
    o_ref[...] = x_ref[...].T.astype(o_ref.dtype)


def _transpose_cast(w, *, tr):
    rows, cols = w.shape
    return pl.pallas_call(
        _transpose_cast_kernel,
        out_shape=jax.ShapeDtypeStruct((rows // tr, cols, tr), BF16),
        grid=(rows // tr,),
        in_specs=[pl.BlockSpec((tr, cols), lambda i: (i, 0))],
        out_specs=pl.BlockSpec((None, cols, tr), lambda i: (i, 0, 0)),
        compiler_params=_cparams(("parallel",)),
        name="transpose_cast",
    )(w)


def _rope_tables(seq, tm):
    rows = seq // GRID_W
    n_freq = DA_HD // 4
    inv = ROPE_BASE ** (-jnp.arange(n_freq, dtype=F32) / n_freq)
    r = jnp.repeat(jnp.arange(rows, dtype=F32), GRID_W)
    col = jnp.tile(jnp.arange(GRID_W, dtype=F32), rows)
    ang = jnp.concatenate([r[:, None] * inv, col[:, None] * inv], axis=-1)
    cos, sin = jnp.cos(ang), jnp.sin(ang)
    cos_t = jnp.tile(cos, (1, 4))
    sin_t = jnp.tile(jnp.concatenate([-sin, sin], axis=-1), (1, 2))
    ident = jnp.ones((tm, LANES), F32)
    return (jnp.concatenate([cos_t, ident], axis=0),
            jnp.concatenate([sin_t, jnp.zeros((tm, LANES), F32)], axis=0))


def kernel(x, c, ctx, c_ctx, w_mod, b_mod, norm1_g, norm2_g, w_in, da_lam_q1, da_lam_k1, da_lam_q2, da_lam_k2, da_sub_g, ml_conv_w, ml_conv_b, ml_wq, ml_wk, ml_gate_b, ml_norm_g, s5_lam_re, s5_lam_im, s5_log_dt, s5_b_re, s5_b_im, s5_c_re, s5_c_im, s5_d, s5_glu_w, s5_glu_b, w_br, w_out, peer_wq, peer_sub_k, peer_u, peer_v, final_g):
    bsz, seq, d = x.shape
    ctx_len = ctx.shape[1]
    depth = w_in.shape[0]
    t_lat = bsz * seq
    t_ctx = bsz * ctx_len
    t_all = t_lat + t_ctx
    tm = math.gcd(512, math.gcd(seq, t_ctx))
    tm_merge = min(tm, 256)
    tq = min(256, ctx_len)
    nh = ML_HEADS

    x_all = jnp.concatenate([x.reshape(t_lat, d), ctx.reshape(t_ctx, d)], axis=0)
    pad = (-(bsz + 1)) % 8
    c_all = jnp.concatenate([c, c_ctx[None], jnp.zeros((pad, d), F32)], axis=0)
    cos_tab, sin_tab = _rope_tables(seq, tm)

    for l in range(depth):
        need_ctx = l < depth - 1
        n_rows = t_all if need_ctx else t_lat
        lam_init = 0.8 - 0.6 * math.exp(-0.3 * l)

        mods = _modulation(c_all, w_mod[l].astype(BF16), b_mod[l][None])
        sh1, sc1, g1, sh2, sc2, g2 = [mods[:bsz + 1, i * d:(i + 1) * d][:, None, :]
                                      for i in range(N_MOD)]

        wi = w_in[l]
        i_q, i_k, i_v, i_xm, i_vm, i_om, i_g, i_u, i_gate = (
            0, 512, 1024, 1536, 2048, 2560, 3072, 3088, 3600)
        w_main = jnp.concatenate([wi[:, i_gate:], wi[:, :i_g], wi[:, i_u:i_gate]], axis=1).astype(BF16)
        w_gate = jnp.pad(wi[:, i_g:i_u], ((0, 0), (0, LANES - (i_u - i_g)))).astype(BF16)
        n1 = norm1_g[l][None]
        z = _modnorm_matmul(x_all, t_all, n1, sc1, sh1, w_main, bsz=bsz, seq=seq, tm=tm,
                            tn=w_main.shape[1] // 2, out_dtype=BF16)
        zg = _modnorm_matmul(x_all, t_all, n1, sc1, sh1, w_gate, bsz=bsz, seq=seq, tm=tm, tn=LANES)

        qr, kr, vb = _rope_qkv(z, cos_tab, sin_tab, bsz=bsz, seq=seq, tm=tm)
        lam = (jnp.exp(jnp.sum(da_lam_q1[l] * da_lam_k1[l]))
               - jnp.exp(jnp.sum(da_lam_q2[l] * da_lam_k2[l])) + lam_init).reshape(1, 1)
        sub_g = da_sub_g[l][None]
        ya = _diff_attention(qr, kr, vb, lam, sub_g, q_row0=0, n_q=seq,
                             kv_segs=[(0, seq), (t_lat, ctx_len)], bsz=bsz, tq=tq,
                             out_scale=1.0 - lam_init, out_rows=n_rows)
        if need_ctx:
            ya = _diff_attention(qr, kr, vb, lam, sub_g, q_row0=t_lat, n_q=ctx_len,
                                 kv_segs=[(t_lat, ctx_len)], bsz=bsz, tq=tq,
                                 out_scale=1.0 - lam_init, out_rows=n_rows, into=ya)

        wq_b, wk_b = ml_wq[l].astype(BF16), ml_wk[l].astype(BF16)
        cb = ml_conv_b[l][None]
        qkt = _ml_prep(z, ml_conv_w[l], cb, wq_b, wk_b, row0=0, seq_len=seq, bsz=bsz)
        q_ml, k_ml, kt_ml = _ml_prep(z, ml_conv_w[l], cb, wq_b, wk_b, row0=t_lat, seq_len=ctx_len,
                                     bsz=bsz, into=qkt)
        gates = (zg[:, :4 * nh] + ml_gate_b[l]).reshape(t_all, 2, 2 * nh)
        gates_c = jnp.transpose(gates, (1, 0, 2))
        gates_r = jnp.transpose(gates, (1, 2, 0))
        h_dirs = _mlstm_scan(q_ml, k_ml, kt_ml, z, gates_c, gates_r, bsz=bsz, seq=seq,
                             ctx_len=ctx_len)

        tables = _s5_tables(s5_lam_re[l], s5_lam_im[l], s5_log_dt[l], s5_b_re[l], s5_b_im[l],
                            s5_c_re[l], s5_c_im[l], s5_d[l])
        u = z[:, Z_U:Z_U + BRANCH_W]
        y_c, y_l = _s5_mix(_s5_rows(u[t_lat:], bsz, ctx_len), _s5_rows(u[:t_lat], bsz, seq),
                           tables, bsz=bsz)
        ys = _s5_unrows(y_l, bsz, seq)
        if need_ctx:
            ys = jnp.concatenate([ys, _s5_unrows(y_c, bsz, ctx_len)], axis=0)

        x_all = _merge(x_all, ya, h_dirs, z, ys, g1, ml_norm_g[l][None],
                       s5_glu_w[l].astype(BF16), s5_glu_b[l][None], w_br[l].astype(BF16),
                       w_out[l].astype(BF16), n_rows=n_rows, bsz=bsz, seq=seq, tm=tm_merge)

        n2 = norm2_g[l][None]
        wq_p = peer_wq[l].astype(BF16)
        pq, h2t = _modnorm_matmul(x_all, n_rows, n2, sc2, sh2, wq_p, bsz=bsz, seq=seq, tm=tm,
                                  tn=wq_p.shape[1], emit_h=True)
        tabs = _peer_topk(pq, peer_sub_k[l].astype(BF16), n_rows=n_rows, tm=tm)
        te = 2048
        x_all = _peer_dense(h2t, peer_u[l].astype(BF16), _transpose_cast(peer_v[l], tr=te),
                            tabs, x_all, g2, final_g[None], n_rows=n_rows, bsz=bsz, seq=seq, tm=tm,
                            te=te, final_norm=not need_ctx)

    return x_all.reshape(bsz, seq, d)
```

```python
import functools
import math

import jax
import jax.numpy as jnp
from jax import lax
from jax.experimental import pallas as pl
from jax.experimental.pallas import tpu as pltpu

F32 = jnp.float32
BF16 = jnp.bfloat16

EPS = 1e-6
LOG2E = 1.4426950408889634
N_MOD = 6
BRANCH_W = 512
GRID_W = 64
ROPE_BASE = 10000.0
DA_HEADS = 4
DA_HD = 64
DA_VD = 2 * DA_HD
ML_HEADS = 4
ML_HD = BRANCH_W // ML_HEADS
ML_CHUNK = 128
S5_P = 16
S5_G = BRANCH_W // S5_P
S5_N = 64
S5_T = 16
PEER_HEADS = 8
PEER_NKEYS = 128
PEER_TOPK = 16
LANES = 128
VMEM_LIMIT = 56 * 1024 * 1024

Z_GATE = 0
Z_Q, Z_K, Z_V, Z_XM, Z_VM, Z_OM, Z_U = (3072 + i * BRANCH_W for i in range(7))


def _cparams(sem):
    return pltpu.CompilerParams(dimension_semantics=sem, vmem_limit_bytes=VMEM_LIMIT)


def _nt_dot(a, b):
    return lax.dot_general(a, b, (((1,), (1,)), ((), ())), preferred_element_type=F32)


def _skip_refs(kernel_fn, start, count, *refs):
    return kernel_fn(*refs[:start], *refs[start + count:])


def _gelu(x):
    return 0.5 * x * (1.0 + lax.erf(x * (2.0 ** -0.5)))


def _log_sigmoid(x):
    return jnp.minimum(x, 0.0) - jnp.log1p(jnp.exp(-jnp.abs(x)))


def _mod_kernel(c_ref, w_ref, b_ref, o_ref):
    c = c_ref[...]
    a = c * jax.nn.sigmoid(c)
    o_ref[...] = jnp.dot(a.astype(BF16), w_ref[...], preferred_element_type=F32) + b_ref[...]


def _modulation(c_all, w, b):
    rows, d = c_all.shape
    n = w.shape[1]
    tn = 1536
    return pl.pallas_call(
        _mod_kernel,
        out_shape=jax.ShapeDtypeStruct((rows, n), F32),
        grid=(n // tn,),
        in_specs=[pl.BlockSpec((rows, d), lambda j: (0, 0)),
                  pl.BlockSpec((d, tn), lambda j: (0, j)),
                  pl.BlockSpec((1, tn), lambda j: (0, j))],
        out_specs=pl.BlockSpec((rows, tn), lambda j: (0, j)),
        compiler_params=_cparams(("parallel",)),
        name="modulation",
    )(c_all, w, b)


def _modnorm_matmul_kernel(x_ref, g_ref, sc_ref, sh_ref, w_ref, o_ref, *h_out):
    x = x_ref[...]
    y = x * lax.rsqrt(jnp.mean(x * x, axis=-1, keepdims=True) + EPS) * g_ref[...]
    hf = y * (1.0 + sc_ref[...]) + sh_ref[...]
    h = hf.astype(BF16)
    o_ref[...] = jnp.dot(h, w_ref[...], preferred_element_type=F32).astype(o_ref.dtype)
    if h_out:
        h_out[0][...] = hf.T.astype(BF16)


def _row_batch(i, n_lat_blocks, blocks_per_batch, bsz):
    return jnp.where(i < n_lat_blocks, i // blocks_per_batch, bsz)


def _modnorm_matmul(x_all, n_rows, g, sc, sh, w, *, bsz, seq, tm, tn, out_dtype=F32, emit_h=False):
    d = x_all.shape[1]
    n = w.shape[1]
    nlb = bsz * seq // tm
    bpb = seq // tm
    assert not emit_h or tn == n
    mod_spec = pl.BlockSpec((None, 1, d), lambda j, i: (_row_batch(i, nlb, bpb, bsz), 0, 0))
    out_shape = [jax.ShapeDtypeStruct((n_rows, n), out_dtype)]
    out_specs = [pl.BlockSpec((tm, tn), lambda j, i: (i, j))]
    if emit_h:
        out_shape.append(jax.ShapeDtypeStruct((d, n_rows), BF16))
        out_specs.append(pl.BlockSpec((d, tm), lambda j, i: (0, i)))
    out = pl.pallas_call(
        _modnorm_matmul_kernel,
        out_shape=tuple(out_shape),
        grid=(n // tn, n_rows // tm),
        in_specs=[pl.BlockSpec((tm, d), lambda j, i: (i, 0)),
                  pl.BlockSpec((1, d), lambda j, i: (0, 0)),
                  mod_spec, mod_spec,
                  pl.BlockSpec((d, tn), lambda j, i: (0, j))],
        out_specs=tuple(out_specs),
        compiler_params=_cparams(("parallel", "parallel")),
        name="modnorm_matmul",
    )(x_all, g, sc, sh, w)
    return out if emit_h else out[0]


def _rope_kernel(q_ref, k_ref, v_ref, cos_ref, sin_ref, qo_ref, ko_ref, vo_ref):
    cos = cos_ref[...]
    sin = sin_ref[...]
    lane = lax.broadcasted_iota(jnp.int32, cos.shape, 1)
    first_half = (lane % DA_HD) < (DA_HD // 2)

    def rope(x):
        partner = jnp.where(first_half, pltpu.roll(x, LANES - DA_HD // 2, 1),
                            pltpu.roll(x, DA_HD // 2, 1))
        return x * cos + partner * sin

    for h in range(DA_HEADS):
        sl = slice(h * LANES, (h + 1) * LANES)
        qo_ref[:, sl] = (rope(q_ref[:, sl].astype(F32)) * (DA_HD ** -0.5 * LOG2E)).astype(BF16)
        ko_ref[:, sl] = rope(k_ref[:, sl].astype(F32)).astype(BF16)
    vo_ref[...] = v_ref[...].astype(BF16)


def _rope_qkv(z, cos_tab, sin_tab, *, bsz, seq, tm):
    rows = z.shape[0]
    nlb = bsz * seq // tm
    bpb = seq // tm
    w = BRANCH_W
    tab_spec = pl.BlockSpec((tm, LANES), lambda i: (jnp.where(i < nlb, i % bpb, bpb), 0))
    out = jax.ShapeDtypeStruct((rows, w), BF16)
    return pl.pallas_call(
        _rope_kernel,
        out_shape=(out, out, out),
        grid=(rows // tm,),
        in_specs=[pl.BlockSpec((tm, w), lambda i: (i, Z_Q // w)),
                  pl.BlockSpec((tm, w), lambda i: (i, Z_K // w)),
                  pl.BlockSpec((tm, w), lambda i: (i, Z_V // w)),
                  tab_spec, tab_spec],
        out_specs=(pl.BlockSpec((tm, w), lambda i: (i, 0)),) * 3,
        compiler_params=_cparams(("parallel",)),
        name="rope_qkv",
    )(z, z, z, cos_tab, sin_tab)


def _attn_kernel(*refs, n_seg, n_sub, out_scale):
    lam_ref, q_ref = refs[0], refs[1]
    k_refs = refs[2:2 + n_seg]
    v_refs = refs[2 + n_seg:2 + 2 * n_seg]
    g_ref, o_ref = refs[2 + 2 * n_seg], refs[3 + 2 * n_seg]
    tq = q_ref.shape[0] // n_sub
    lane = lax.broadcasted_iota(jnp.int32, (tq, LANES), 1)
    scores = []
    for sub in range(n_sub):
        q = q_ref[sub * tq:(sub + 1) * tq, :].astype(F32)
        qq = jnp.concatenate([jnp.where(lane < DA_HD, q, 0.0), jnp.where(lane >= DA_HD, q, 0.0)],
                             axis=0).astype(BF16)
        scores.append([_nt_dot(qq, k_ref[...]) for k_ref in k_refs])
    for sub in range(n_sub):
        rows = slice(sub * tq, (sub + 1) * tq)
        s = scores[sub]
        m = functools.reduce(jnp.maximum, [jnp.max(t, axis=-1, keepdims=True) for t in s])
        p = [jnp.exp2(t - m) for t in s]
        denom = functools.reduce(jnp.add, [jnp.sum(t, axis=-1, keepdims=True) for t in p])
        inv = 1.0 / denom
        c0 = inv[:tq]
        c1 = inv[tq:] * lam_ref[...]
        o = None
        for t, v_ref in zip(p, v_refs):
            w = (t[:tq] * c0 - t[tq:] * c1).astype(BF16)
            part = jnp.dot(w, v_ref[...], preferred_element_type=F32)
            o = part if o is None else o + part
        y = o * lax.rsqrt(jnp.mean(o * o, axis=-1, keepdims=True) + EPS) * g_ref[...]
        o_ref[rows, :] = (y * out_scale).astype(o_ref.dtype)


def _diff_attention(qr, kr, vb, lam, sub_g, *, q_row0, n_q, kv_segs, bsz, tq, out_scale, out_rows,
                    into=None):
    n_sub = 2 if n_q % (2 * tq) == 0 else 1
    tq = tq * n_sub
    nqb = n_q // tq
    n_seg = len(kv_segs)
    q_blk0 = q_row0 // tq

    def kv_spec(row0, length):
        return pl.BlockSpec((length, LANES), lambda b, h, i: (row0 // length + b, h))

    in_specs = [pl.BlockSpec((1, 1), lambda b, h, i: (0, 0)),
                pl.BlockSpec((tq, LANES), lambda b, h, i: (q_blk0 + b * nqb + i, h))]
    in_specs += [kv_spec(r0, ln) for r0, ln in kv_segs] * 2
    in_specs += [pl.BlockSpec((1, LANES), lambda b, h, i: (0, h))]
    args = [lam, qr] + [kr] * n_seg + [vb] * n_seg + [sub_g]
    kern = functools.partial(_attn_kernel, n_seg=n_seg, n_sub=n_sub, out_scale=out_scale)
    aliases = {}
    if into is not None:
        aliases = {len(args): 0}
        kern = functools.partial(_skip_refs, kern, len(args), 1)
        in_specs += [pl.BlockSpec(memory_space=pl.ANY)]
        args += [into]
    return pl.pallas_call(
        kern,
        out_shape=jax.ShapeDtypeStruct((out_rows, BRANCH_W), BF16),
        grid=(bsz, DA_HEADS, nqb),
        in_specs=in_specs,
        out_specs=pl.BlockSpec((tq, LANES), lambda b, h, i: (q_blk0 + b * nqb + i, h)),
        input_output_aliases=aliases,
        compiler_params=_cparams(("parallel", "parallel", "parallel")),
        name="diff_attention",
    )(*args)


def _ml_prep_kernel(x_ref, cw_ref, cb_ref, wq_ref, wk_ref, q_ref, k_ref, kt_ref):
    x = x_ref[...].astype(F32)
    n = x.shape[0]
    row = lax.broadcasted_iota(jnp.int32, x.shape, 0)
    x_prev = jnp.where(row == 0, 0.0, pltpu.roll(x, 1, 0))
    x_next = jnp.where(row == n - 1, 0.0, pltpu.roll(x, n - 1, 0))
    cw = cw_ref[...]
    y = x_prev * cw[0:1] + x * cw[1:2] + x_next * cw[2:3] + cb_ref[...]
    xc = (y * jax.nn.sigmoid(y)).astype(BF16)
    q_ref[...] = (jnp.dot(xc, wq_ref[...], preferred_element_type=F32) * (ML_HD ** -0.5)
                  ).astype(q_ref.dtype)
    k = jnp.dot(xc, wk_ref[...], preferred_element_type=F32)
    k_ref[...] = k.astype(k_ref.dtype)
    kt_ref[...] = k.T.astype(kt_ref.dtype)


def _ml_prep(z, conv_w, conv_b, wq, wk, *, row0, seq_len, bsz, into=None):
    rows = z.shape[0]
    blk0 = row0 // seq_len
    out = jax.ShapeDtypeStruct((rows, BRANCH_W), BF16)
    out_t = jax.ShapeDtypeStruct((BRANCH_W, rows), BF16)
    head_w = pl.BlockSpec((None, ML_HD, ML_HD), lambda b, h: (h, 0, 0))
    row_out = pl.BlockSpec((seq_len, LANES), lambda b, h: (blk0 + b, h))
    in_specs = [pl.BlockSpec((seq_len, LANES), lambda b, h: (blk0 + b, Z_XM // LANES + h)),
                pl.BlockSpec((3, LANES), lambda b, h: (0, h)),
                pl.BlockSpec((1, LANES), lambda b, h: (0, h)),
                head_w, head_w]
    args = [z, conv_w, conv_b, wq, wk]
    aliases = {}
    kern = _ml_prep_kernel
    if into is not None:
        n_in = len(args)
        in_specs += [pl.BlockSpec(memory_space=pl.ANY)] * len(into)
        args += list(into)
        aliases = {n_in + i: i for i in range(len(into))}
        kern = functools.partial(_skip_refs, _ml_prep_kernel, n_in, len(into))
    return pl.pallas_call(
        kern,
        out_shape=(out, out, out_t),
        grid=(bsz, ML_HEADS),
        in_specs=in_specs,
        out_specs=(row_out, row_out, pl.BlockSpec((LANES, seq_len), lambda b, h: (h, blk0 + b))),
        input_output_aliases=aliases,
        compiler_params=_cparams(("parallel", "parallel")),
        name="mlstm_prep",
    )(*args)


def _mlstm_kernel(qf_ref, kf_ref, ktf_ref, vf_ref, gcf_ref, grf_ref,
                  qb_ref, kb_ref, ktb_ref, vb_ref, gcb_ref, grb_ref,
                  hf_ref, hb_ref, c_ref, n_ref, m_ref):
    @pl.when(pl.program_id(1) == 0)
    def _():
        c_ref[...] = jnp.zeros_like(c_ref)
        n_ref[...] = jnp.zeros_like(n_ref)
        m_ref[...] = jnp.zeros_like(m_ref)

    t_idx = lax.broadcasted_iota(jnp.int32, (ML_CHUNK, ML_CHUNK), 0)
    s_idx = lax.broadcasted_iota(jnp.int32, (ML_CHUNK, ML_CHUNK), 1)
    nh = ML_HEADS
    dirs = ((qf_ref, kf_ref, ktf_ref, vf_ref, gcf_ref, grf_ref, hf_ref),
            (qb_ref, kb_ref, ktb_ref, vb_ref, gcb_ref, grb_ref, hb_ref))
    ones = jnp.ones((ML_CHUNK, LANES), BF16)
    units = []
    for d, (q_ref, k_ref, kt_ref, v_ref, gc_ref, gr_ref, h_ref) in enumerate(dirs):
        mask = s_idx <= t_idx if d == 0 else s_idx >= t_idx
        mask_f = mask.astype(F32)
        gc = gc_ref[...]
        gr = gr_ref[...]
        lf_c = _log_sigmoid(gc)
        lf_r = _log_sigmoid(gr)
        b_c = jnp.dot(mask_f, lf_c, preferred_element_type=F32, precision=lax.Precision.HIGHEST)
        b_r = lax.dot_general(lf_r, mask_f, (((1,), (1,)), ((), ())), preferred_element_type=F32,
                              precision=lax.Precision.HIGHEST)
        b_end_all = jnp.sum(lf_c, axis=0, keepdims=True)

        for h in range(nh):
            sl = slice(h * ML_HD, (h + 1) * ML_HD)
            q = q_ref[:, sl]
            k = k_ref[:, sl]
            kt = kt_ref[sl, :]
            v = v_ref[:, sl].astype(F32)
            li_c = gc[:, h:h + 1]
            li_r = gr[h:h + 1, :]
            bc = b_c[:, nh + h:nh + h + 1]
            br = b_r[nh + h:nh + h + 1, :]
            b_end = b_end_all[:, nh + h:nh + h + 1]
            m_prev = m_ref[d, h]
            c_prev = c_ref[d, h]
            n_prev = n_ref[d, h]
            g_c = b_end - bc + li_c
            g_r = b_end - br + li_r
            m_new = jnp.maximum(b_end + m_prev, jnp.max(g_r, axis=-1, keepdims=True))
            wg = jnp.broadcast_to(jnp.exp(g_c - m_new), (ML_CHUNK, LANES))
            decay = jnp.exp(b_end + m_prev - m_new)
            vw = (v * wg).astype(BF16)
            units.append(dict(
                d=d, h=h, sl=sl, h_ref=h_ref, v=v, mask=mask, m_prev=m_prev,
                bc=bc, br=br, li_r=li_r,
                qk=_nt_dot(q, k),
                qc=jnp.dot(q, c_prev.astype(BF16), preferred_element_type=F32),
                qn=jnp.dot(q, n_prev.astype(BF16), preferred_element_type=F32),
                c_new=decay * c_prev + jnp.dot(kt, vw, preferred_element_type=F32),
                n_new=decay * n_prev + jnp.dot(kt, wg.astype(BF16), preferred_element_type=F32),
                m_new=m_new))

    for u in units:
        dmat = jnp.where(u["mask"], u["bc"] - u["br"] + u["li_r"], -jnp.inf)
        inter = u["bc"] + u["m_prev"]
        m_t = jnp.maximum(inter, jnp.max(dmat, axis=-1, keepdims=True))
        s = (u["qk"] * jnp.exp(dmat - m_t)).astype(BF16)
        w_prev = jnp.exp(inter - m_t)
        num = jnp.dot(s, u["v"].astype(BF16), preferred_element_type=F32) + w_prev * u["qc"]
        den = jnp.dot(s, ones, preferred_element_type=F32) + w_prev * u["qn"]
        u["h_ref"][:, u["sl"]] = num / jnp.maximum(jnp.abs(den), jnp.exp(-m_t))
        c_ref[u["d"], u["h"]] = u["c_new"]
        n_ref[u["d"], u["h"]] = u["n_new"]
        m_ref[u["d"], u["h"]] = u["m_new"]


def _mlstm_scan(q_all, k_all, kt_all, z, gates_c, gates_r, *, bsz, seq, ctx_len):
    rows = q_all.shape[0]
    ncl = seq // ML_CHUNK
    ncc = ctx_len // ML_CHUNK
    lat_blocks = bsz * ncl
    nh = ML_HEADS

    def blk(d, b, s):
        jc = s if d == 0 else ncc - 1 - s
        jl = s - ncc if d == 0 else ncl - 1 - (s - ncc)
        return jnp.where(s < ncc, lat_blocks + b * ncc + jc, b * ncl + jl)

    def dir_specs(d):
        return [pl.BlockSpec((ML_CHUNK, BRANCH_W), lambda b, s: (blk(d, b, s), 0)),
                pl.BlockSpec((ML_CHUNK, BRANCH_W), lambda b, s: (blk(d, b, s), 0)),
                pl.BlockSpec((BRANCH_W, ML_CHUNK), lambda b, s: (0, blk(d, b, s))),
                pl.BlockSpec((ML_CHUNK, BRANCH_W), lambda b, s: (blk(d, b, s), Z_VM // BRANCH_W)),
                pl.BlockSpec((None, ML_CHUNK, 2 * nh), lambda b, s: (d, blk(d, b, s), 0)),
                pl.BlockSpec((None, 2 * nh, ML_CHUNK), lambda b, s: (d, 0, blk(d, b, s)))]

    out = jax.ShapeDtypeStruct((rows, BRANCH_W), F32)
    args = (q_all, k_all, kt_all, z, gates_c, gates_r)
    return pl.pallas_call(
        _mlstm_kernel,
        out_shape=(out, out),
        grid=(bsz, ncc + ncl),
        in_specs=dir_specs(0) + dir_specs(1),
        out_specs=(pl.BlockSpec((ML_CHUNK, BRANCH_W), lambda b, s: (blk(0, b, s), 0)),
                   pl.BlockSpec((ML_CHUNK, BRANCH_W), lambda b, s: (blk(1, b, s), 0))),
        scratch_shapes=[pltpu.VMEM((2, nh, ML_HD, ML_HD), F32),
                        pltpu.VMEM((2, nh, ML_HD, LANES), F32),
                        pltpu.VMEM((2, nh, 1, 1), F32)],
        compiler_params=_cparams(("parallel", "arbitrary")),
        name="mlstm_scan",
    )(*args, *args)


def _s5_kernel(uc_ref, ul_ref, m_ref, w_ref, v_ref, a1_ref, a2_ref, yc_ref, yl_ref,
               ec_ref, el_ref, xc_ref, xl_ref, *, bsz):
    w = w_ref[...]
    ec_ref[...] = jnp.dot(uc_ref[...], w, preferred_element_type=F32)
    el_ref[...] = jnp.dot(ul_ref[...], w, preferred_element_type=F32)
    a1 = a1_ref[...]
    a2 = a2_ref[...]
    half = LANES
    a1f, a2f, a1b, a2b = a1[:, 0:half], a2[:, 0:half], a1[:, half:], a2[:, half:]

    def scan(e_ref, x_ref, state):
        n_chunks = e_ref.shape[0] // bsz

        def body(j, st):
            xf, xfs, xb, xbs = st
            rf = pl.multiple_of(j * bsz, bsz)
            rb = pl.multiple_of((n_chunks - 1 - j) * bsz, bsz)
            x_ref[pl.ds(rf, bsz), 0:half] = xf
            x_ref[pl.ds(rb, bsz), half:2 * half] = xb
            ef, eb = e_ref[pl.ds(rf, bsz), 0:half], e_ref[pl.ds(rb, bsz), half:2 * half]
            efs = e_ref[pl.ds(rf, bsz), 2 * half:3 * half]
            ebs = e_ref[pl.ds(rb, bsz), 3 * half:4 * half]
            return (a1f * xf + a2f * xfs + ef, a1f * xfs - a2f * xf + efs,
                    a1b * xb + a2b * xbs + eb, a1b * xbs - a2b * xb + ebs)

        return lax.fori_loop(0, n_chunks, body, state)

    zero = jnp.zeros((bsz, half), F32)
    state = scan(ec_ref, xc_ref, (zero, zero, zero, zero))
    scan(el_ref, xl_ref, state)
    mm = m_ref[...]
    vv = v_ref[...]
    yc_ref[...] = (jnp.dot(uc_ref[...], mm, preferred_element_type=F32)
                   + jnp.dot(xc_ref[...].astype(BF16), vv, preferred_element_type=F32)
                   ).astype(yc_ref.dtype)
    yl_ref[...] = (jnp.dot(ul_ref[...], mm, preferred_element_type=F32)
                   + jnp.dot(xl_ref[...].astype(BF16), vv, preferred_element_type=F32)
                   ).astype(yl_ref.dtype)


def _s5_tables(lam_re, lam_im, log_dt, b_re, b_im, c_re, c_im, d_skip):
    t, g, n, p = S5_T, S5_G, S5_N, S5_P
    dt = jnp.exp(log_dt)[..., None]
    den = lam_re * lam_re + lam_im * lam_im
    kk = jnp.arange(t + 1, dtype=F32)[None, None, :, None]
    mag = jnp.exp(kk * (lam_re * dt)[:, :, None, :])
    ang = kk * (lam_im * dt)[:, :, None, :]
    p_re, p_im = mag * jnp.cos(ang), mag * jnp.sin(ang)
    ab_re, ab_im = p_re[:, :, 1], p_im[:, :, 1]
    z_re = ((ab_re - 1.0) * lam_re + ab_im * lam_im) / den
    z_im = (ab_im * lam_re - (ab_re - 1.0) * lam_im) / den
    bt_re, bt_im = jnp.swapaxes(b_re, 2, 3), jnp.swapaxes(b_im, 2, 3)
    bb_re = z_re[:, :, None, :] * bt_re - z_im[:, :, None, :] * bt_im
    bb_im = z_re[:, :, None, :] * bt_im + z_im[:, :, None, :] * bt_re
    pk_re, pk_im = p_re[:, :, :t, None, :], p_im[:, :, :t, None, :]
    abk_re = pk_re * bb_re[:, :, None] - pk_im * bb_im[:, :, None]
    abk_im = pk_re * bb_im[:, :, None] + pk_im * bb_re[:, :, None]
    taps = (jnp.einsum('rgpn,rgkqn->rgqkp', c_re, abk_re)
            - jnp.einsum('rgpn,rgkqn->rgqkp', c_im, abk_im))
    base_f = taps[0].reshape(g, p, t * p)
    base_b = taps[1][:, :, ::-1].reshape(g, p, t * p)
    width = t * p
    m_f = jnp.stack([jnp.pad(base_f, ((0, 0), (0, 0), (p * s, 0)))[..., :width]
                     for s in range(t)], axis=1)
    m_b = jnp.stack([jnp.pad(base_b, ((0, 0), (0, 0), (0, p * (t - 1 - s))))[..., p * (t - 1 - s):]
                     for s in range(t)], axis=1)
    skip = jnp.eye(width, dtype=F32)[None] * jnp.tile(d_skip.reshape(g, p), (1, t))[:, None, :]
    m = (m_f + m_b).reshape(g, width, width) + skip
    wf_re, wf_im = (a[0][:, ::-1].reshape(g, width, n) for a in (abk_re, abk_im))
    wb_re, wb_im = (a[1].reshape(g, width, n) for a in (abk_re, abk_im))
    w = jnp.concatenate([wf_re, wf_im, wb_re, wb_im, wf_im, wf_re, wb_im, wb_re], axis=-1)
    ct_re, ct_im = jnp.swapaxes(c_re, 2, 3), jnp.swapaxes(c_im, 2, 3)
    pn_re, pn_im = jnp.swapaxes(p_re, 2, 3)[..., None], jnp.swapaxes(p_im, 2, 3)[..., None]
    ca_re = ct_re[:, :, :, None, :] * pn_re - ct_im[:, :, :, None, :] * pn_im
    ca_im = ct_re[:, :, :, None, :] * pn_im + ct_im[:, :, :, None, :] * pn_re
    v = jnp.concatenate([ca_re[0][:, :, 1:].reshape(g, n, width),
                         -ca_im[0][:, :, 1:].reshape(g, n, width),
                         ca_re[1][:, :, :0:-1].reshape(g, n, width),
                         -ca_im[1][:, :, :0:-1].reshape(g, n, width)], axis=1)
    at_re, at_im = p_re[:, :, t], p_im[:, :, t]
    a1 = jnp.concatenate([at_re[0], at_re[0], at_re[1], at_re[1]], axis=-1)[:, None, :]
    a2 = jnp.concatenate([-at_im[0], at_im[0], -at_im[1], at_im[1]], axis=-1)[:, None, :]
    return m.astype(BF16), w.astype(BF16), v.astype(BF16), a1, a2


def _s5_rows(u, bsz, length):
    nc = length // S5_T
    u = u.reshape(bsz, nc, S5_T, S5_G, S5_P)
    return jnp.transpose(u, (3, 1, 0, 2, 4)).reshape(S5_G, nc * bsz, S5_T * S5_P)


def _s5_unrows(y, bsz, length):
    nc = length // S5_T
    y = y.reshape(S5_G, nc, bsz, S5_T, S5_P)
    return jnp.transpose(y, (2, 1, 3, 0, 4)).reshape(bsz * length, BRANCH_W)


def _s5_mix(u_ctx, u_lat, tables, *, bsz):
    m, w, v, a1, a2 = tables
    rc, rl = u_ctx.shape[1], u_lat.shape[1]
    wd = S5_T * S5_P

    def rows_spec(r):
        return pl.BlockSpec((None, r, wd), lambda g: (g, 0, 0))

    sq = pl.BlockSpec((None, wd, wd), lambda g: (g, 0, 0))
    vec = pl.BlockSpec((None, 1, wd), lambda g: (g, 0, 0))
    return pl.pallas_call(
        functools.partial(_s5_kernel, bsz=bsz),
        out_shape=(jax.ShapeDtypeStruct((S5_G, rc, wd), BF16),
                   jax.ShapeDtypeStruct((S5_G, rl, wd), BF16)),
        grid=(S5_G,),
        in_specs=[rows_spec(rc), rows_spec(rl), sq,
                  pl.BlockSpec((None, wd, 2 * wd), lambda g: (g, 0, 0)), sq, vec, vec],
        out_specs=(rows_spec(rc), rows_spec(rl)),
        scratch_shapes=[pltpu.VMEM((rc, 2 * wd), F32), pltpu.VMEM((rl, 2 * wd), F32),
                        pltpu.VMEM((rc, wd), F32), pltpu.VMEM((rl, wd), F32)],
        compiler_params=_cparams(("parallel",)),
        name="s5_mix",
    )(u_ctx, u_lat, m, w, v, a1, a2)


def _merge_kernel(x_ref, ya_ref, hf_ref, hb_ref, om_ref, ys_ref, gate_ref, g1_ref, mlg_ref,
                  gluw_ref, glub_ref, wbr_ref, wout_ref, o_ref):
    hsum = hf_ref[...] + hb_ref[...]
    og = jax.nn.sigmoid(om_ref[...].astype(F32))
    mlg = mlg_ref[...]
    yb_parts = []
    for h in range(ML_HEADS):
        sl = slice(h * ML_HD, (h + 1) * ML_HD)
        hh = hsum[:, sl]
        hn = hh * lax.rsqrt(jnp.mean(hh * hh, axis=-1, keepdims=True) + EPS) * mlg[:, sl]
        yb_parts.append(hn * og[:, sl])
    yb = jnp.concatenate(yb_parts, axis=-1)
    gl = _gelu(ys_ref[...].astype(F32))
    ys = gl * jax.nn.sigmoid(
        jnp.dot(gl.astype(BF16), gluw_ref[...], preferred_element_type=F32) + glub_ref[...])
    d = o_ref.shape[1]
    merged = None
    for r, y in enumerate((ya_ref[...], yb, ys)):
        proj = jnp.dot(y.astype(BF16), wbr_ref[r], preferred_element_type=F32)
        term = jax.nn.sigmoid(gate_ref[:, r * d:(r + 1) * d].astype(F32)) * proj
        merged = term if merged is None else merged + term
    y = jnp.dot(merged.astype(BF16), wout_ref[...], preferred_element_type=F32)
    o_ref[...] = x_ref[...] + g1_ref[...] * y


def _merge(x_all, ya, h_dirs, z, ys, g1, ml_norm_g, glu_w, glu_b, w_br, w_out, *, n_rows, bsz, seq, tm):
    d = x_all.shape[1]
    nlb = bsz * seq // tm
    bpb = seq // tm
    w = BRANCH_W

    def rows(width, col_blk=0):
        return pl.BlockSpec((tm, width), lambda i: (i, col_blk))

    def full(shape):
        return pl.BlockSpec(shape, lambda i: (0,) * len(shape))

    return pl.pallas_call(
        _merge_kernel,
        out_shape=jax.ShapeDtypeStruct((n_rows, d), F32),
        grid=(n_rows // tm,),
        in_specs=[rows(d), rows(w), rows(w), rows(w),
                  rows(w, Z_OM // w), rows(w),
                  rows(3 * d, Z_GATE // (3 * d)),
                  pl.BlockSpec((None, 1, d), lambda i: (_row_batch(i, nlb, bpb, bsz), 0, 0)),
                  full((1, w)), full((w, w)), full((1, w)), full((3, w, d)), full((d, d))],
        out_specs=rows(d),
        compiler_params=_cparams(("parallel",)),
        name="merge",
    )(x_all, ya, h_dirs[0], h_dirs[1], z, ys, z, g1, ml_norm_g, glu_w, glu_b, w_br, w_out)


NOT_TOP = 127.0


def _take_top(s, n_take, vals_ref, want_rank):
    rows = s.shape[0]
    ridx = lax.broadcasted_iota(jnp.int32, s.shape, 0).astype(F32)
    rank = jnp.full(s.shape, NOT_TOP, F32) if want_rank else None
    for i in range(n_take):
        mx = jnp.max(s, axis=0, keepdims=True)
        first = jnp.min(jnp.where(s == mx, ridx, float(rows)), axis=0, keepdims=True)
        hit = ridx == first
        s = jnp.where(hit, -jnp.inf, s)
        if want_rank:
            rank = jnp.where(hit, float(i), rank)
        vals_ref[i:i + 1, :] = mx
    return s, rank


SUBLANES = 8


def _sort_network_pairs(n):
    pairs = []
    p = 1
    while p < n:
        k = p
        while k >= 1:
            for j in range(k % p, n - k, 2 * k):
                for i in range(min(k, n - j - k)):
                    if (i + j) // (2 * p) == (i + j + k) // (2 * p):
                        pairs.append((i + j, i + j + k))
            k //= 2
        p *= 2
    return pairs


def _top_sorted(s):
    n = PEER_TOPK
    x = _slabs(s)
    assert len(x) <= n
    x = x + [jnp.full(x[0].shape, -jnp.inf, F32)] * (n - len(x))

    def exchange(i, j):
        x[i], x[j] = jnp.maximum(x[i], x[j]), jnp.minimum(x[i], x[j])

    for i, j in _sort_network_pairs(n):
        exchange(i, j)
    shift = SUBLANES // 2
    while shift >= 1:
        y = [pltpu.roll(v, shift, 0) for v in x]
        x = [jnp.maximum(x[i], y[n - 1 - i]) for i in range(n)]
        d = n // 2
        while d >= 1:
            for i in range(n):
                if not i & d:
                    exchange(i, i + d)
            d //= 2
        shift //= 2
    return x


def _slabs(s):
    return [s[SUBLANES * i:SUBLANES * (i + 1), :] for i in range(s.shape[0] // SUBLANES)]


def _tie_count(s, top):
    flag = jnp.zeros(top[0].shape, F32)
    for i in range(len(top) - 1):
        flag = flag + jnp.where(top[i] == top[i + 1], 1.0, 0.0)
    reach = functools.reduce(jnp.add, [jnp.where(v >= top[-1], 1.0, 0.0) for v in _slabs(s)])
    reach = jnp.sum(reach, axis=0, keepdims=True)
    return flag + jnp.where(reach != float(PEER_TOPK), 1.0, 0.0)


def _candidates(v1_ref, v2_ref):
    k = PEER_TOPK
    v2_head = v2_ref[0:8, :]
    jrow = lax.broadcasted_iota(jnp.int32, v2_head.shape, 0)
    pieces = [v1_ref[0:1, :] + v2_ref[...]]
    for i in range(1, 8):
        pieces.append(jnp.where(jrow < k // (i + 1), v1_ref[i:i + 1, :] + v2_head, -jnp.inf))
    pieces.append(v1_ref[8:16, :] + v2_ref[0:1, :])
    return jnp.concatenate(pieces, axis=0)


def _write_counts(picked, cnt_ref):
    cnt_ref[0:1, :] = jnp.sum(picked[0:16], axis=0, keepdims=True)
    for i in range(1, 8):
        cnt_ref[i:i + 1, :] = jnp.sum(picked[8 + 8 * i:16 + 8 * i], axis=0, keepdims=True)
    cnt_ref[8:16, :] = picked[72:80]


def _candidate_counts(v1_ref, v2_ref, top_ref, cnt_ref):
    cand = _candidates(v1_ref, v2_ref)
    cand_left, _ = _take_top(cand, PEER_TOPK, top_ref, False)
    _write_counts(jnp.where(cand_left != cand, 1.0, 0.0), cnt_ref)
    top = top_ref[...]
    return jnp.sum(jnp.exp(top - top[0:1, :]), axis=0, keepdims=True)


def _peer_topk_kernel(q_ref, k1_ref, k2_ref, e1_ref, cnt1_ref, e2_ref, rank2_ref,
                      v1_ref, v2_ref, top_ref, cnt_ref):
    k = PEER_TOPK
    q = q_ref[...].astype(BF16)
    s1 = _nt_dot(k1_ref[...], q[:, :LANES])
    s2 = _nt_dot(k2_ref[...], q[:, LANES:])
    top1 = _top_sorted(s1)
    top2 = _top_sorted(s2)
    for i in range(k):
        v1_ref[i:i + 1, :] = top1[i][0:1, :]
        v2_ref[i:i + 1, :] = top2[i][0:1, :]
    cand = _candidates(v1_ref, v2_ref)
    topc = _top_sorted(cand)
    any_tie = jnp.max(_tie_count(s1, top1) + _tie_count(s2, top2) + _tie_count(cand, topc)) > 0.0

    @pl.when(any_tie)
    def _():
        _, rank1 = _take_top(s1, k, v1_ref, True)
        _, rank2 = _take_top(s2, k, v2_ref, True)
        zsum = _candidate_counts(v1_ref, v2_ref, top_ref, cnt_ref)
        cnt1 = jnp.zeros(s1.shape, F32)
        for i in range(k):
            cnt1 = jnp.where(rank1 == float(i), cnt_ref[i:i + 1, :], cnt1)
        e1_ref[...] = jnp.where(rank1 < k, jnp.exp(s1 - v1_ref[0:1, :]), 0.0) / zsum
        cnt1_ref[...] = cnt1
        e2_ref[...] = jnp.where(rank2 < k, jnp.exp(s2 - v2_ref[0:1, :]), 0.0).astype(BF16)
        rank2_ref[...] = rank2.astype(BF16)

    @pl.when(jnp.logical_not(any_tie))
    def _():
        _write_counts(jnp.where(cand >= topc[k - 1][0:1, :], 1.0, 0.0), cnt_ref)
        zsum = functools.reduce(jnp.add, [jnp.exp(t[0:1, :] - topc[0][0:1, :]) for t in topc])
        cnt_rows = [jnp.broadcast_to(cnt_ref[i:i + 1, :], top1[0].shape) for i in range(k)]
        cnt1, rank2 = [], []
        for v in _slabs(s1):
            c = jnp.zeros(v.shape, F32)
            for i in range(k):
                c = jnp.where(v == top1[i], cnt_rows[i], c)
            cnt1.append(c)
        for v in _slabs(s2):
            r = jnp.full(v.shape, NOT_TOP, F32)
            for i in range(k):
                r = jnp.where(v == top2[i], float(i), r)
            rank2.append(r)
        e1_ref[...] = jnp.where(s1 >= v1_ref[k - 1:k, :], jnp.exp(s1 - v1_ref[0:1, :]), 0.0) / zsum
        cnt1_ref[...] = jnp.concatenate(cnt1, axis=0)
        e2_ref[...] = jnp.where(s2 >= v2_ref[k - 1:k, :], jnp.exp(s2 - v2_ref[0:1, :]), 0.0
                                ).astype(BF16)
        rank2_ref[...] = jnp.concatenate(rank2, axis=0).astype(BF16)


def _peer_topk(q, sub_k, *, n_rows, tm):
    nk = PEER_NKEYS
    row_tab = jax.ShapeDtypeStruct((n_rows // tm, PEER_HEADS, nk, tm), F32)
    tile_tab = jax.ShapeDtypeStruct((n_rows // tm, PEER_HEADS, nk, tm), BF16)
    tab_spec = pl.BlockSpec((None, None, nk, tm), lambda i, h: (i, h, 0, 0))
    return pl.pallas_call(
        _peer_topk_kernel,
        out_shape=(row_tab, row_tab, tile_tab, tile_tab),
        grid=(n_rows // tm, PEER_HEADS),
        in_specs=[pl.BlockSpec((tm, 2 * LANES), lambda i, h: (i, h)),
                  pl.BlockSpec((None, nk, LANES), lambda i, h: (0, 0, 0)),
                  pl.BlockSpec((None, nk, LANES), lambda i, h: (1, 0, 0))],
        out_specs=(tab_spec,) * 4,
        scratch_shapes=[pltpu.VMEM((PEER_TOPK, tm), F32)] * 4,
        compiler_params=_cparams(("parallel", "parallel")),
        name="peer_topk",
    )(q, sub_k, sub_k)


GATE_KEYS = 2
KEY_GROUP = 8


def _peer_dense_kernel(ht_ref, u_ref, vt_ref, e1_ref, cnt1_ref, e2_ref, rank2_ref,
                       x_ref, g2_ref, fg_ref, o_ref, acc_ref, act_ref, p_ref, e2s_ref, rank2s_ref, *,
                       a_per_blk, final_norm):
    j = pl.program_id(1)
    n_blk = pl.num_programs(1) - 1
    cur = j % 2
    tm = ht_ref.shape[1]
    nk = PEER_NKEYS

    @pl.when(j == 0)
    def _():
        acc_ref[...] = jnp.zeros_like(acc_ref)
        p_ref[1] = jnp.zeros(p_ref.shape[1:], p_ref.dtype)
        e2s_ref[...] = e2_ref[...]
        rank2s_ref[...] = rank2_ref[...]

    @pl.when(j < n_blk)
    def _():
        act_ref[...] = _gelu(jnp.dot(u_ref[...], ht_ref[...], preferred_element_type=F32)
                             ).astype(BF16)
        acc_ref[...] += jnp.dot(vt_ref[...], p_ref[1 - cur], preferred_element_type=F32)
        a0 = pl.multiple_of(j * a_per_blk, KEY_GROUP)

        def row_bf16(ref, h, i, ls):
            grp = (i // KEY_GROUP) * KEY_GROUP
            w = ref[h, pl.ds(a0 + grp, KEY_GROUP), ls][i % KEY_GROUP:i % KEY_GROUP + 1]
            return jnp.broadcast_to(w, (nk, LANES)).astype(BF16)

        for lt in range(tm // LANES):
            ls = slice(lt * LANES, (lt + 1) * LANES)
            for i0 in range(0, a_per_blk, GATE_KEYS):
                gates = [None] * GATE_KEYS
                for h in range(PEER_HEADS):
                    for ii in range(GATE_KEYS):
                        lead = row_bf16(cnt1_ref, h, i0 + ii, ls) - rank2s_ref[h, :, ls]
                        term = jnp.minimum(jnp.maximum(lead, 0.0),
                                           row_bf16(e1_ref, h, i0 + ii, ls)) * e2s_ref[h, :, ls]
                        gates[ii] = term if gates[ii] is None else gates[ii] + term
                for ii in range(GATE_KEYS):
                    r0 = (i0 + ii) * nk
                    p_ref[cur, r0:r0 + nk, ls] = gates[ii] * act_ref[r0:r0 + nk, ls]

    @pl.when(j == n_blk)
    def _():
        acc = acc_ref[...] + jnp.dot(vt_ref[...], p_ref[1 - cur], preferred_element_type=F32)
        y = x_ref[...] + g2_ref[...] * acc.T
        if final_norm:
            y = y * lax.rsqrt(jnp.mean(y * y, axis=-1, keepdims=True) + EPS) * fg_ref[...]
        o_ref[...] = y


def _peer_dense(h2, u_tab, vt_tab, tabs, x_all, g2, final_g, *, n_rows, bsz, seq, tm, te,
                final_norm):
    d = x_all.shape[1]
    n_exp = u_tab.shape[0]
    nlb = bsz * seq // tm
    bpb = seq // tm
    nk = PEER_NKEYS
    n_blk = n_exp // te
    assert (te // nk) % KEY_GROUP == 0
    tab_spec = pl.BlockSpec((None, PEER_HEADS, nk, tm), lambda i, j: (i, 0, 0, 0))
    return pl.pallas_call(
        functools.partial(_peer_dense_kernel, a_per_blk=te // nk, final_norm=final_norm),
        out_shape=jax.ShapeDtypeStruct((n_rows, d), F32),
        grid=(n_rows // tm, n_blk + 1),
        in_specs=[pl.BlockSpec((d, tm), lambda i, j: (0, i)),
                  pl.BlockSpec((te, d), lambda i, j: (jnp.minimum(j, n_blk - 1), 0)),
                  pl.BlockSpec((None, d, te), lambda i, j: (jnp.maximum(j - 1, 0), 0, 0)),
                  tab_spec, tab_spec, tab_spec, tab_spec,
                  pl.BlockSpec((tm, d), lambda i, j: (i, 0)),
                  pl.BlockSpec((None, 1, d), lambda i, j: (_row_batch(i, nlb, bpb, bsz), 0, 0)),
                  pl.BlockSpec((1, d), lambda i, j: (0, 0))],
        out_specs=pl.BlockSpec((tm, d), lambda i, j: (i, 0)),
        scratch_shapes=[pltpu.VMEM((d, tm), F32), pltpu.VMEM((te, tm), BF16),
                        pltpu.VMEM((2, te, tm), BF16),
                        pltpu.VMEM((PEER_HEADS, nk, tm), BF16), pltpu.VMEM((PEER_HEADS, nk, tm), BF16)],
        compiler_params=_cparams(("parallel", "arbitrary")),
        name="peer_dense",
    )(h2, u_tab, vt_tab, *tabs, x_all, g2, final_g)


def _transpose_cast_kernel(x_ref, o_ref):
    o_ref[...] = x_ref[...].T.astype(o_ref.dtype)


def _transpose_cast(w, *, tr):
    rows, cols = w.shape
    return pl.pallas_call(
        _transpose_cast_kernel,
        out_shape=jax.ShapeDtypeStruct((rows // tr, cols, tr), BF16),
        grid=(rows // tr,),
        in_specs=[pl.BlockSpec((tr, cols), lambda i: (i, 0))],
        out_specs=pl.BlockSpec((None, cols, tr), lambda i: (i, 0, 0)),
        compiler_params=_cparams(("parallel",)),
        name="transpose_cast",
    )(w)


def _rope_tables(seq, tm):
    rows = seq // GRID_W
    n_freq = DA_HD // 4
    inv = ROPE_BASE ** (-jnp.arange(n_freq, dtype=F32) / n_freq)
    r = jnp.repeat(jnp.arange(rows, dtype=F32), GRID_W)
    col = jnp.tile(jnp.arange(GRID_W, dtype=F32), rows)
    ang = jnp.concatenate([r[:, None] * inv, col[:, None] * inv], axis=-1)
    cos, sin = jnp.cos(ang), jnp.sin(ang)
    cos_t = jnp.tile(cos, (1, 4))
    sin_t = jnp.tile(jnp.concatenate([-sin, sin], axis=-1), (1, 2))
    ident = jnp.ones((tm, LANES), F32)
    return (jnp.concatenate([cos_t, ident], axis=0),
            jnp.concatenate([sin_t, jnp.zeros((tm, LANES), F32)], axis=0))


def kernel(x, c, ctx, c_ctx, w_mod, b_mod, norm1_g, norm2_g, w_in, da_lam_q1, da_lam_k1, da_lam_q2, da_lam_k2, da_sub_g, ml_conv_w, ml_conv_b, ml_wq, ml_wk, ml_gate_b, ml_norm_g, s5_lam_re, s5_lam_im, s5_log_dt, s5_b_re, s5_b_im, s5_c_re, s5_c_im, s5_d, s5_glu_w, s5_glu_b, w_br, w_out, peer_wq, peer_sub_k, peer_u, peer_v, final_g):
    bsz, seq, d = x.shape
    ctx_len = ctx.shape[1]
    depth = w_in.shape[0]
    t_lat = bsz * seq
    t_ctx = bsz * ctx_len
    t_all = t_lat + t_ctx
    tm = math.gcd(512, math.gcd(seq, t_ctx))
    tm_merge = min(tm, 256)
    tq = min(256, ctx_len)
    nh = ML_HEADS

    x_all = jnp.concatenate([x.reshape(t_lat, d), ctx.reshape(t_ctx, d)], axis=0)
    pad = (-(bsz + 1)) % 8
    c_all = jnp.concatenate([c, c_ctx[None], jnp.zeros((pad, d), F32)], axis=0)
    cos_tab, sin_tab = _rope_tables(seq, tm)

    for l in range(depth):
        need_ctx = l < depth - 1
        n_rows = t_all if need_ctx else t_lat
        lam_init = 0.8 - 0.6 * math.exp(-0.3 * l)

        mods = _modulation(c_all, w_mod[l].astype(BF16), b_mod[l][None])
        sh1, sc1, g1, sh2, sc2, g2 = [mods[:bsz + 1, i * d:(i + 1) * d][:, None, :]
                                      for i in range(N_MOD)]

        wi = w_in[l]
        i_q, i_k, i_v, i_xm, i_vm, i_om, i_g, i_u, i_gate = (
            0, 512, 1024, 1536, 2048, 2560, 3072, 3088, 3600)
        w_main = jnp.concatenate([wi[:, i_gate:], wi[:, :i_g], wi[:, i_u:i_gate]], axis=1).astype(BF16)
        w_gate = jnp.pad(wi[:, i_g:i_u], ((0, 0), (0, LANES - (i_u - i_g)))).astype(BF16)
        n1 = norm1_g[l][None]
        z = _modnorm_matmul(x_all, t_all, n1, sc1, sh1, w_main, bsz=bsz, seq=seq, tm=tm,
                            tn=w_main.shape[1] // 2, out_dtype=BF16)
        zg = _modnorm_matmul(x_all, t_all, n1, sc1, sh1, w_gate, bsz=bsz, seq=seq, tm=tm, tn=LANES)

        qr, kr, vb = _rope_qkv(z, cos_tab, sin_tab, bsz=bsz, seq=seq, tm=tm)
        lam = (jnp.exp(jnp.sum(da_lam_q1[l] * da_lam_k1[l]))
               - jnp.exp(jnp.sum(da_lam_q2[l] * da_lam_k2[l])) + lam_init).reshape(1, 1)
        sub_g = da_sub_g[l][None]
        ya = _diff_attention(qr, kr, vb, lam, sub_g, q_row0=0, n_q=seq,
                             kv_segs=[(0, seq), (t_lat, ctx_len)], bsz=bsz, tq=tq,
                             out_scale=1.0 - lam_init, out_rows=n_rows)
        if need_ctx:
            ya = _diff_attention(qr, kr, vb, lam, sub_g, q_row0=t_lat, n_q=ctx_len,
                                 kv_segs=[(t_lat, ctx_len)], bsz=bsz, tq=tq,
                                 out_scale=1.0 - lam_init, out_rows=n_rows, into=ya)

        wq_b, wk_b = ml_wq[l].astype(BF16), ml_wk[l].astype(BF16)
        cb = ml_conv_b[l][None]
        qkt = _ml_prep(z, ml_conv_w[l], cb, wq_b, wk_b, row0=0, seq_len=seq, bsz=bsz)
        q_ml, k_ml, kt_ml = _ml_prep(z, ml_conv_w[l], cb, wq_b, wk_b, row0=t_lat, seq_len=ctx_len,
                                     bsz=bsz, into=qkt)
        gates = (zg[:, :4 * nh] + ml_gate_b[l]).reshape(t_all, 2, 2 * nh)
        gates_c = jnp.transpose(gates, (1, 0, 2))
        gates_r = jnp.transpose(gates, (1, 2, 0))
        h_dirs = _mlstm_scan(q_ml, k_ml, kt_ml, z, gates_c, gates_r, bsz=bsz, seq=seq,
                             ctx_len=ctx_len)

        tables = _s5_tables(s5_lam_re[l], s5_lam_im[l], s5_log_dt[l], s5_b_re[l], s5_b_im[l],
                            s5_c_re[l], s5_c_im[l], s5_d[l])
        u = z[:, Z_U:Z_U + BRANCH_W]
        y_c, y_l = _s5_mix(_s5_rows(u[t_lat:], bsz, ctx_len), _s5_rows(u[:t_lat], bsz, seq),
                           tables, bsz=bsz)
        ys = _s5_unrows(y_l, bsz, seq)
        if need_ctx:
            ys = jnp.concatenate([ys, _s5_unrows(y_c, bsz, ctx_len)], axis=0)

        x_all = _merge(x_all, ya, h_dirs, z, ys, g1, ml_norm_g[l][None],
                       s5_glu_w[l].astype(BF16), s5_glu_b[l][None], w_br[l].astype(BF16),
                       w_out[l].astype(BF16), n_rows=n_rows, bsz=bsz, seq=seq, tm=tm_merge)

        n2 = norm2_g[l][None]
        wq_p = peer_wq[l].astype(BF16)
        pq, h2t = _modnorm_matmul(x_all, n_rows, n2, sc2, sh2, wq_p, bsz=bsz, seq=seq, tm=tm,
                                  tn=wq_p.shape[1], emit_h=True)
        tabs = _peer_topk(pq, peer_sub_k[l].astype(BF16), n_rows=n_rows, tm=tm)
        te = 2048
        x_all = _peer_dense(h2t, peer_u[l].astype(BF16), _transpose_cast(peer_v[l], tr=te),
                            tabs, x_all, g2, final_g[None], n_rows=n_rows, bsz=bsz, seq=seq, tm=tm,
                            te=te, final_norm=not need_ctx)

    return x_all.reshape(bsz, seq, d)
```

```python
import functools
import math

import jax
import jax.numpy as jnp
from jax import lax
from jax.experimental import pallas as pl
from jax.experimental.pallas import tpu as pltpu

F32 = jnp.float32
BF16 = jnp.bfloat16

EPS = 1e-6
LOG2E = 1.4426950408889634
N_MOD = 6
BRANCH_W = 512
GRID_W = 64
ROPE_BASE = 10000.0
DA_HEADS = 4
DA_HD = 64
DA_VD = 2 * DA_HD
ML_HEADS = 4
ML_HD = BRANCH_W // ML_HEADS
ML_CHUNK = 128
S5_P = 16
S5_G = BRANCH_W // S5_P
S5_N = 64
S5_T = 16
PEER_HEADS = 8
PEER_NKEYS = 128
PEER_TOPK = 16
LANES = 128
VMEM_LIMIT = 56 * 1024 * 1024

Z_GATE = 0
Z_Q, Z_K, Z_V, Z_XM, Z_VM, Z_OM, Z_U = (3072 + i * BRANCH_W for i in range(7))


def _cparams(sem):
    return pltpu.CompilerParams(dimension_semantics=sem, vmem_limit_bytes=VMEM_LIMIT)


def _nt_dot(a, b):
    return lax.dot_general(a, b, (((1,), (1,)), ((), ())), preferred_element_type=F32)


def _skip_refs(kernel_fn, start, count, *refs):
    return kernel_fn(*refs[:start], *refs[start + count:])


def _gelu(x):
    return 0.5 * x * (1.0 + lax.erf(x * (2.0 ** -0.5)))


def _log_sigmoid(x):
    return jnp.minimum(x, 0.0) - jnp.log1p(jnp.exp(-jnp.abs(x)))


def _mod_kernel(c_ref, w_ref, b_ref, o_ref):
    c = c_ref[...]
    a = c * jax.nn.sigmoid(c)
    o_ref[...] = jnp.dot(a.astype(BF16), w_ref[...], preferred_element_type=F32) + b_ref[...]


def _modulation(c_all, w, b):
    rows, d = c_all.shape
    n = w.shape[1]
    tn = 1536
    return pl.pallas_call(
        _mod_kernel,
        out_shape=jax.ShapeDtypeStruct((rows, n), F32),
        grid=(n // tn,),
        in_specs=[pl.BlockSpec((rows, d), lambda j: (0, 0)),
                  pl.BlockSpec((d, tn), lambda j: (0, j)),
                  pl.BlockSpec((1, tn), lambda j: (0, j))],
        out_specs=pl.BlockSpec((rows, tn), lambda j: (0, j)),
        compiler_params=_cparams(("parallel",)),
        name="modulation",
    )(c_all, w, b)


def _modnorm_matmul_kernel(x_ref, g_ref, sc_ref, sh_ref, w_ref, o_ref, *h_out):
    x = x_ref[...]
    y = x * lax.rsqrt(jnp.mean(x * x, axis=-1, keepdims=True) + EPS) * g_ref[...]
    hf = y * (1.0 + sc_ref[...]) + sh_ref[...]
    h = hf.astype(BF16)
    o_ref[...] = jnp.dot(h, w_ref[...], preferred_element_type=F32).astype(o_ref.dtype)
    if h_out:
        h_out[0][...] = hf.T.astype(BF16)


def _row_batch(i, n_lat_blocks, blocks_per_batch, bsz):
    return jnp.where(i < n_lat_blocks, i // blocks_per_batch, bsz)


def _modnorm_matmul(x_all, n_rows, g, sc, sh, w, *, bsz, seq, tm, tn, out_dtype=F32, emit_h=False):
    d = x_all.shape[1]
    n = w.shape[1]
    nlb = bsz * seq // tm
    bpb = seq // tm
    assert not emit_h or tn == n
    mod_spec = pl.BlockSpec((None, 1, d), lambda j, i: (_row_batch(i, nlb, bpb, bsz), 0, 0))
    out_shape = [jax.ShapeDtypeStruct((n_rows, n), out_dtype)]
    out_specs = [pl.BlockSpec((tm, tn), lambda j, i: (i, j))]
    if emit_h:
        out_shape.append(jax.ShapeDtypeStruct((d, n_rows), BF16))
        out_specs.append(pl.BlockSpec((d, tm), lambda j, i: (0, i)))
    out = pl.pallas_call(
        _modnorm_matmul_kernel,
        out_shape=tuple(out_shape),
        grid=(n // tn, n_rows // tm),
        in_specs=[pl.BlockSpec((tm, d), lambda j, i: (i, 0)),
                  pl.BlockSpec((1, d), lambda j, i: (0, 0)),
                  mod_spec, mod_spec,
                  pl.BlockSpec((d, tn), lambda j, i: (0, j))],
        out_specs=tuple(out_specs),
        compiler_params=_cparams(("parallel", "parallel")),
        name="modnorm_matmul",
    )(x_all, g, sc, sh, w)
    return out if emit_h else out[0]


def _rope_kernel(q_ref, k_ref, v_ref, cos_ref, sin_ref, qo_ref, ko_ref, vo_ref):
    cos = cos_ref[...]
    sin = sin_ref[...]
    lane = lax.broadcasted_iota(jnp.int32, cos.shape, 1)
    first_half = (lane % DA_HD) < (DA_HD // 2)

    def rope(x):
        partner = jnp.where(first_half, pltpu.roll(x, LANES - DA_HD // 2, 1),
                            pltpu.roll(x, DA_HD // 2, 1))
        return x * cos + partner * sin

    for h in range(DA_HEADS):
        sl = slice(h * LANES, (h + 1) * LANES)
        qo_ref[:, sl] = (rope(q_ref[:, sl].astype(F32)) * (DA_HD ** -0.5 * LOG2E)).astype(BF16)
        ko_ref[:, sl] = rope(k_ref[:, sl].astype(F32)).astype(BF16)
    vo_ref[...] = v_ref[...].astype(BF16)


def _rope_qkv(z, cos_tab, sin_tab, *, bsz, seq, tm):
    rows = z.shape[0]
    nlb = bsz * seq // tm
    bpb = seq // tm
    w = BRANCH_W
    tab_spec = pl.BlockSpec((tm, LANES), lambda i: (jnp.where(i < nlb, i % bpb, bpb), 0))
    out = jax.ShapeDtypeStruct((rows, w), BF16)
    return pl.pallas_call(
        _rope_kernel,
        out_shape=(out, out, out),
        grid=(rows // tm,),
        in_specs=[pl.BlockSpec((tm, w), lambda i: (i, Z_Q // w)),
                  pl.BlockSpec((tm, w), lambda i: (i, Z_K // w)),
                  pl.BlockSpec((tm, w), lambda i: (i, Z_V // w)),
                  tab_spec, tab_spec],
        out_specs=(pl.BlockSpec((tm, w), lambda i: (i, 0)),) * 3,
        compiler_params=_cparams(("parallel",)),
        name="rope_qkv",
    )(z, z, z, cos_tab, sin_tab)


def _attn_kernel(*refs, n_seg, n_sub, out_scale):
    lam_ref, q_ref = refs[0], refs[1]
    k_refs = refs[2:2 + n_seg]
    v_refs = refs[2 + n_seg:2 + 2 * n_seg]
    g_ref, o_ref = refs[2 + 2 * n_seg], refs[3 + 2 * n_seg]
    tq = q_ref.shape[0] // n_sub
    lane = lax.broadcasted_iota(jnp.int32, (tq, LANES), 1)
    scores = []
    for sub in range(n_sub):
        q = q_ref[sub * tq:(sub + 1) * tq, :].astype(F32)
        qq = jnp.concatenate([jnp.where(lane < DA_HD, q, 0.0), jnp.where(lane >= DA_HD, q, 0.0)],
                             axis=0).astype(BF16)
        scores.append([_nt_dot(qq, k_ref[...]) for k_ref in k_refs])
    for sub in range(n_sub):
        rows = slice(sub * tq, (sub + 1) * tq)
        s = scores[sub]
        m = functools.reduce(jnp.maximum, [jnp.max(t, axis=-1, keepdims=True) for t in s])
        p = [jnp.exp2(t - m) for t in s]
        denom = functools.reduce(jnp.add, [jnp.sum(t, axis=-1, keepdims=True) for t in p])
        inv = 1.0 / denom
        c0 = inv[:tq]
        c1 = inv[tq:] * lam_ref[...]
        o = None
        for t, v_ref in zip(p, v_refs):
            w = (t[:tq] * c0 - t[tq:] * c1).astype(BF16)
            part = jnp.dot(w, v_ref[...], preferred_element_type=F32)
            o = part if o is None else o + part
        y = o * lax.rsqrt(jnp.mean(o * o, axis=-1, keepdims=True) + EPS) * g_ref[...]
        o_ref[rows, :] = (y * out_scale).astype(o_ref.dtype)


def _diff_attention(qr, kr, vb, lam, sub_g, *, q_row0, n_q, kv_segs, bsz, tq, out_scale, out_rows,
                    into=None):
    n_sub = 2 if n_q % (2 * tq) == 0 else 1
    tq = tq * n_sub
    nqb = n_q // tq
    n_seg = len(kv_segs)
    q_blk0 = q_row0 // tq

    def kv_spec(row0, length):
        return pl.BlockSpec((length, LANES), lambda b, h, i: (row0 // length + b, h))

    in_specs = [pl.BlockSpec((1, 1), lambda b, h, i: (0, 0)),
                pl.BlockSpec((tq, LANES), lambda b, h, i: (q_blk0 + b * nqb + i, h))]
    in_specs += [kv_spec(r0, ln) for r0, ln in kv_segs] * 2
    in_specs += [pl.BlockSpec((1, LANES), lambda b, h, i: (0, h))]
    args = [lam, qr] + [kr] * n_seg + [vb] * n_seg + [sub_g]
    kern = functools.partial(_attn_kernel, n_seg=n_seg, n_sub=n_sub, out_scale=out_scale)
    aliases = {}
    if into is not None:
        aliases = {len(args): 0}
        kern = functools.partial(_skip_refs, kern, len(args), 1)
        in_specs += [pl.BlockSpec(memory_space=pl.ANY)]
        args += [into]
    return pl.pallas_call(
        kern,
        out_shape=jax.ShapeDtypeStruct((out_rows, BRANCH_W), BF16),
        grid=(bsz, DA_HEADS, nqb),
        in_specs=in_specs,
        out_specs=pl.BlockSpec((tq, LANES), lambda b, h, i: (q_blk0 + b * nqb + i, h)),
        input_output_aliases=aliases,
        compiler_params=_cparams(("parallel", "parallel", "parallel")),
        name="diff_attention",
    )(*args)


def _ml_prep_kernel(x_ref, cw_ref, cb_ref, wq_ref, wk_ref, q_ref, k_ref, kt_ref):
    x = x_ref[...].astype(F32)
    n = x.shape[0]
    row = lax.broadcasted_iota(jnp.int32, x.shape, 0)
    x_prev = jnp.where(row == 0, 0.0, pltpu.roll(x, 1, 0))
    x_next = jnp.where(row == n - 1, 0.0, pltpu.roll(x, n - 1, 0))
    cw = cw_ref[...]
    y = x_prev * cw[0:1] + x * cw[1:2] + x_next * cw[2:3] + cb_ref[...]
    xc = (y * jax.nn.sigmoid(y)).astype(BF16)
    q_ref[...] = (jnp.dot(xc, wq_ref[...], preferred_element_type=F32) * (ML_HD ** -0.5)
                  ).astype(q_ref.dtype)
    k = jnp.dot(xc, wk_ref[...], preferred_element_type=F32)
    k_ref[...] = k.astype(k_ref.dtype)
    kt_ref[...] = k.T.astype(kt_ref.dtype)


def _ml_prep(z, conv_w, conv_b, wq, wk, *, row0, seq_len, bsz, into=None):
    rows = z.shape[0]
    blk0 = row0 // seq_len
    out = jax.ShapeDtypeStruct((rows, BRANCH_W), BF16)
    out_t = jax.ShapeDtypeStruct((BRANCH_W, rows), BF16)
    head_w = pl.BlockSpec((None, ML_HD, ML_HD), lambda b, h: (h, 0, 0))
    row_out = pl.BlockSpec((seq_len, LANES), lambda b, h: (blk0 + b, h))
    in_specs = [pl.BlockSpec((seq_len, LANES), lambda b, h: (blk0 + b, Z_XM // LANES + h)),
                pl.BlockSpec((3, LANES), lambda b, h: (0, h)),
                pl.BlockSpec((1, LANES), lambda b, h: (0, h)),
                head_w, head_w]
    args = [z, conv_w, conv_b, wq, wk]
    aliases = {}
    kern = _ml_prep_kernel
    if into is not None:
        n_in = len(args)
        in_specs += [pl.BlockSpec(memory_space=pl.ANY)] * len(into)
        args += list(into)
        aliases = {n_in + i: i for i in range(len(into))}
        kern = functools.partial(_skip_refs, _ml_prep_kernel, n_in, len(into))
    return pl.pallas_call(
        kern,
        out_shape=(out, out, out_t),
        grid=(bsz, ML_HEADS),
        in_specs=in_specs,
        out_specs=(row_out, row_out, pl.BlockSpec((LANES, seq_len), lambda b, h: (h, blk0 + b))),
        input_output_aliases=aliases,
        compiler_params=_cparams(("parallel", "parallel")),
        name="mlstm_prep",
    )(*args)


def _mlstm_kernel(qf_ref, kf_ref, ktf_ref, vf_ref, gcf_ref, grf_ref,
                  qb_ref, kb_ref, ktb_ref, vb_ref, gcb_ref, grb_ref,
                  hf_ref, hb_ref, c_ref, n_ref, m_ref):
    @pl.when(pl.program_id(1) == 0)
    def _():
        c_ref[...] = jnp.zeros_like(c_ref)
        n_ref[...] = jnp.zeros_like(n_ref)
        m_ref[...] = jnp.zeros_like(m_ref)

    t_idx = lax.broadcasted_iota(jnp.int32, (ML_CHUNK, ML_CHUNK), 0)
    s_idx = lax.broadcasted_iota(jnp.int32, (ML_CHUNK, ML_CHUNK), 1)
    nh = ML_HEADS
    dirs = ((qf_ref, kf_ref, ktf_ref, vf_ref, gcf_ref, grf_ref, hf_ref),
            (qb_ref, kb_ref, ktb_ref, vb_ref, gcb_ref, grb_ref, hb_ref))
    ones = jnp.ones((ML_CHUNK, LANES), BF16)
    units = []
    for d, (q_ref, k_ref, kt_ref, v_ref, gc_ref, gr_ref, h_ref) in enumerate(dirs):
        mask = s_idx <= t_idx if d == 0 else s_idx >= t_idx
        mask_f = mask.astype(F32)
        gc = gc_ref[...]
        gr = gr_ref[...]
        lf_c = _log_sigmoid(gc)
        lf_r = _log_sigmoid(gr)
        b_c = jnp.dot(mask_f, lf_c, preferred_element_type=F32, precision=lax.Precision.HIGHEST)
        b_r = lax.dot_general(lf_r, mask_f, (((1,), (1,)), ((), ())), preferred_element_type=F32,
                              precision=lax.Precision.HIGHEST)
        b_end_all = jnp.sum(lf_c, axis=0, keepdims=True)

        for h in range(nh):
            sl = slice(h * ML_HD, (h + 1) * ML_HD)
            q = q_ref[:, sl]
            k = k_ref[:, sl]
            kt = kt_ref[sl, :]
            v = v_ref[:, sl].astype(F32)
            li_c = gc[:, h:h + 1]
            li_r = gr[h:h + 1, :]
            bc = b_c[:, nh + h:nh + h + 1]
            br = b_r[nh + h:nh + h + 1, :]
            b_end = b_end_all[:, nh + h:nh + h + 1]
            m_prev = m_ref[d, h]
            c_prev = c_ref[d, h]
            n_prev = n_ref[d, h]
            g_c = b_end - bc + li_c
            g_r = b_end - br + li_r
            m_new = jnp.maximum(b_end + m_prev, jnp.max(g_r, axis=-1, keepdims=True))
            wg = jnp.broadcast_to(jnp.exp(g_c - m_new), (ML_CHUNK, LANES))
            decay = jnp.exp(b_end + m_prev - m_new)
            vw = (v * wg).astype(BF16)
            units.append(dict(
                d=d, h=h, sl=sl, h_ref=h_ref, v=v, mask=mask, m_prev=m_prev,
                bc=bc, br=br, li_r=li_r,
                qk=_nt_dot(q, k),
                qc=jnp.dot(q, c_prev.astype(BF16), preferred_element_type=F32),
                qn=jnp.dot(q, n_prev.astype(BF16), preferred_element_type=F32),
                c_new=decay * c_prev + jnp.dot(kt, vw, preferred_element_type=F32),
                n_new=decay * n_prev + jnp.dot(kt, wg.astype(BF16), preferred_element_type=F32),
                m_new=m_new))

    for u in units:
        dmat = jnp.where(u["mask"], u["bc"] - u["br"] + u["li_r"], -jnp.inf)
        inter = u["bc"] + u["m_prev"]
        m_t = jnp.maximum(inter, jnp.max(dmat, axis=-1, keepdims=True))
        s = (u["qk"] * jnp.exp(dmat - m_t)).astype(BF16)
        w_prev = jnp.exp(inter - m_t)
        num = jnp.dot(s, u["v"].astype(BF16), preferred_element_type=F32) + w_prev * u["qc"]
        den = jnp.dot(s, ones, preferred_element_type=F32) + w_prev * u["qn"]
        u["h_ref"][:, u["sl"]] = num / jnp.maximum(jnp.abs(den), jnp.exp(-m_t))
        c_ref[u["d"], u["h"]] = u["c_new"]
        n_ref[u["d"], u["h"]] = u["n_new"]
        m_ref[u["d"], u["h"]] = u["m_new"]


def _mlstm_scan(q_all, k_all, kt_all, z, gates_c, gates_r, *, bsz, seq, ctx_len):
    rows = q_all.shape[0]
    ncl = seq // ML_CHUNK
    ncc = ctx_len // ML_CHUNK
    lat_blocks = bsz * ncl
    nh = ML_HEADS

    def blk(d, b, s):
        jc = s if d == 0 else ncc - 1 - s
        jl = s - ncc if d == 0 else ncl - 1 - (s - ncc)
        return jnp.where(s < ncc, lat_blocks + b * ncc + jc, b * ncl + jl)

    def dir_specs(d):
        return [pl.BlockSpec((ML_CHUNK, BRANCH_W), lambda b, s: (blk(d, b, s), 0)),
                pl.BlockSpec((ML_CHUNK, BRANCH_W), lambda b, s: (blk(d, b, s), 0)),
                pl.BlockSpec((BRANCH_W, ML_CHUNK), lambda b, s: (0, blk(d, b, s))),
                pl.BlockSpec((ML_CHUNK, BRANCH_W), lambda b, s: (blk(d, b, s), Z_VM // BRANCH_W)),
                pl.BlockSpec((None, ML_CHUNK, 2 * nh), lambda b, s: (d, blk(d, b, s), 0)),
                pl.BlockSpec((None, 2 * nh, ML_CHUNK), lambda b, s: (d, 0, blk(d, b, s)))]

    out = jax.ShapeDtypeStruct((rows, BRANCH_W), F32)
    args = (q_all, k_all, kt_all, z, gates_c, gates_r)
    return pl.pallas_call(
        _mlstm_kernel,
        out_shape=(out, out),
        grid=(bsz, ncc + ncl),
        in_specs=dir_specs(0) + dir_specs(1),
        out_specs=(pl.BlockSpec((ML_CHUNK, BRANCH_W), lambda b, s: (blk(0, b, s), 0)),
                   pl.BlockSpec((ML_CHUNK, BRANCH_W), lambda b, s: (blk(1, b, s), 0))),
        scratch_shapes=[pltpu.VMEM((2, nh, ML_HD, ML_HD), F32),
                        pltpu.VMEM((2, nh, ML_HD, LANES), F32),
                        pltpu.VMEM((2, nh, 1, 1), F32)],
        compiler_params=_cparams(("parallel", "arbitrary")),
        name="mlstm_scan",
    )(*args, *args)


def _s5_kernel(uc_ref, ul_ref, m_ref, w_ref, v_ref, a1_ref, a2_ref, yc_ref, yl_ref,
               ec_ref, el_ref, xc_ref, xl_ref, *, bsz):
    w = w_ref[...]
    ec_ref[...] = jnp.dot(uc_ref[...], w, preferred_element_type=F32)
    el_ref[...] = jnp.dot(ul_ref[...], w, preferred_element_type=F32)
    a1 = a1_ref[...]
    a2 = a2_ref[...]
    half = LANES
    a1f, a2f, a1b, a2b = a1[:, 0:half], a2[:, 0:half], a1[:, half:], a2[:, half:]

    def scan(e_ref, x_ref, state):
        n_chunks = e_ref.shape[0] // bsz

        def body(j, st):
            xf, xfs, xb, xbs = st
            rf = pl.multiple_of(j * bsz, bsz)
            rb = pl.multiple_of((n_chunks - 1 - j) * bsz, bsz)
            x_ref[pl.ds(rf, bsz), 0:half] = xf
            x_ref[pl.ds(rb, bsz), half:2 * half] = xb
            ef, eb = e_ref[pl.ds(rf, bsz), 0:half], e_ref[pl.ds(rb, bsz), half:2 * half]
            efs = e_ref[pl.ds(rf, bsz), 2 * half:3 * half]
            ebs = e_ref[pl.ds(rb, bsz), 3 * half:4 * half]
            return (a1f * xf + a2f * xfs + ef, a1f * xfs - a2f * xf + efs,
                    a1b * xb + a2b * xbs + eb, a1b * xbs - a2b * xb + ebs)

        return lax.fori_loop(0, n_chunks, body, state)

    zero = jnp.zeros((bsz, half), F32)
    state = scan(ec_ref, xc_ref, (zero, zero, zero, zero))
    scan(el_ref, xl_ref, state)
    mm = m_ref[...]
    vv = v_ref[...]
    yc_ref[...] = (jnp.dot(uc_ref[...], mm, preferred_element_type=F32)
                   + jnp.dot(xc_ref[...].astype(BF16), vv, preferred_element_type=F32)
                   ).astype(yc_ref.dtype)
    yl_ref[...] = (jnp.dot(ul_ref[...], mm, preferred_element_type=F32)
                   + jnp.dot(xl_ref[...].astype(BF16), vv, preferred_element_type=F32)
                   ).astype(yl_ref.dtype)


def _s5_tables(lam_re, lam_im, log_dt, b_re, b_im, c_re, c_im, d_skip):
    t, g, n, p = S5_T, S5_G, S5_N, S5_P
    dt = jnp.exp(log_dt)[..., None]
    den = lam_re * lam_re + lam_im * lam_im
    kk = jnp.arange(t + 1, dtype=F32)[None, None, :, None]
    mag = jnp.exp(kk * (lam_re * dt)[:, :, None, :])
    ang = kk * (lam_im * dt)[:, :, None, :]
    p_re, p_im = mag * jnp.cos(ang), mag * jnp.sin(ang)
    ab_re, ab_im = p_re[:, :, 1], p_im[:, :, 1]
    z_re = ((ab_re - 1.0) * lam_re + ab_im * lam_im) / den
    z_im = (ab_im * lam_re - (ab_re - 1.0) * lam_im) / den
    bt_re, bt_im = jnp.swapaxes(b_re, 2, 3), jnp.swapaxes(b_im, 2, 3)
    bb_re = z_re[:, :, None, :] * bt_re - z_im[:, :, None, :] * bt_im
    bb_im = z_re[:, :, None, :] * bt_im + z_im[:, :, None, :] * bt_re
    pk_re, pk_im = p_re[:, :, :t, None, :], p_im[:, :, :t, None, :]
    abk_re = pk_re * bb_re[:, :, None] - pk_im * bb_im[:, :, None]
    abk_im = pk_re * bb_im[:, :, None] + pk_im * bb_re[:, :, None]
    taps = (jnp.einsum('rgpn,rgkqn->rgqkp', c_re, abk_re)
            - jnp.einsum('rgpn,rgkqn->rgqkp', c_im, abk_im))
    base_f = taps[0].reshape(g, p, t * p)
    base_b = taps[1][:, :, ::-1].reshape(g, p, t * p)
    width = t * p
    m_f = jnp.stack([jnp.pad(base_f, ((0, 0), (0, 0), (p * s, 0)))[..., :width]
                     for s in range(t)], axis=1)
    m_b = jnp.stack([jnp.pad(base_b, ((0, 0), (0, 0), (0, p * (t - 1 - s))))[..., p * (t - 1 - s):]
                     for s in range(t)], axis=1)
    skip = jnp.eye(width, dtype=F32)[None] * jnp.tile(d_skip.reshape(g, p), (1, t))[:, None, :]
    m = (m_f + m_b).reshape(g, width, width) + skip
    wf_re, wf_im = (a[0][:, ::-1].reshape(g, width, n) for a in (abk_re, abk_im))
    wb_re, wb_im = (a[1].reshape(g, width, n) for a in (abk_re, abk_im))
    w = jnp.concatenate([wf_re, wf_im, wb_re, wb_im, wf_im, wf_re, wb_im, wb_re], axis=-1)
    ct_re, ct_im = jnp.swapaxes(c_re, 2, 3), jnp.swapaxes(c_im, 2, 3)
    pn_re, pn_im = jnp.swapaxes(p_re, 2, 3)[..., None], jnp.swapaxes(p_im, 2, 3)[..., None]
    ca_re = ct_re[:, :, :, None, :] * pn_re - ct_im[:, :, :, None, :] * pn_im
    ca_im = ct_re[:, :, :, None, :] * pn_im + ct_im[:, :, :, None, :] * pn_re
    v = jnp.concatenate([ca_re[0][:, :, 1:].reshape(g, n, width),
                         -ca_im[0][:, :, 1:].reshape(g, n, width),
                         ca_re[1][:, :, :0:-1].reshape(g, n, width),
                         -ca_im[1][:, :, :0:-1].reshape(g, n, width)], axis=1)
    at_re, at_im = p_re[:, :, t], p_im[:, :, t]
    a1 = jnp.concatenate([at_re[0], at_re[0], at_re[1], at_re[1]], axis=-1)[:, None, :]
    a2 = jnp.concatenate([-at_im[0], at_im[0], -at_im[1], at_im[1]], axis=-1)[:, None, :]
    return m.astype(BF16), w.astype(BF16), v.astype(BF16), a1, a2


def _s5_rows(u, bsz, length):
    nc = length // S5_T
    u = u.reshape(bsz, nc, S5_T, S5_G, S5_P)
    return jnp.transpose(u, (3, 1, 0, 2, 4)).reshape(S5_G, nc * bsz, S5_T * S5_P)


def _s5_unrows(y, bsz, length):
    nc = length // S5_T
    y = y.reshape(S5_G, nc, bsz, S5_T, S5_P)
    return jnp.transpose(y, (2, 1, 3, 0, 4)).reshape(bsz * length, BRANCH_W)


def _s5_mix(u_ctx, u_lat, tables, *, bsz):
    m, w, v, a1, a2 = tables
    rc, rl = u_ctx.shape[1], u_lat.shape[1]
    wd = S5_T * S5_P

    def rows_spec(r):
        return pl.BlockSpec((None, r, wd), lambda g: (g, 0, 0))

    sq = pl.BlockSpec((None, wd, wd), lambda g: (g, 0, 0))
    vec = pl.BlockSpec((None, 1, wd), lambda g: (g, 0, 0))
    return pl.pallas_call(
        functools.partial(_s5_kernel, bsz=bsz),
        out_shape=(jax.ShapeDtypeStruct((S5_G, rc, wd), BF16),
                   jax.ShapeDtypeStruct((S5_G, rl, wd), BF16)),
        grid=(S5_G,),
        in_specs=[rows_spec(rc), rows_spec(rl), sq,
                  pl.BlockSpec((None, wd, 2 * wd), lambda g: (g, 0, 0)), sq, vec, vec],
        out_specs=(rows_spec(rc), rows_spec(rl)),
        scratch_shapes=[pltpu.VMEM((rc, 2 * wd), F32), pltpu.VMEM((rl, 2 * wd), F32),
                        pltpu.VMEM((rc, wd), F32), pltpu.VMEM((rl, wd), F32)],
        compiler_params=_cparams(("parallel",)),
        name="s5_mix",
    )(u_ctx, u_lat, m, w, v, a1, a2)


def _merge_kernel(x_ref, ya_ref, hf_ref, hb_ref, om_ref, ys_ref, gate_ref, g1_ref, mlg_ref,
                  gluw_ref, glub_ref, wbr_ref, wout_ref, o_ref):
    hsum = hf_ref[...] + hb_ref[...]
    og = jax.nn.sigmoid(om_ref[...].astype(F32))
    mlg = mlg_ref[...]
    yb_parts = []
    for h in range(ML_HEADS):
        sl = slice(h * ML_HD, (h + 1) * ML_HD)
        hh = hsum[:, sl]
        hn = hh * lax.rsqrt(jnp.mean(hh * hh, axis=-1, keepdims=True) + EPS) * mlg[:, sl]
        yb_parts.append(hn * og[:, sl])
    yb = jnp.concatenate(yb_parts, axis=-1)
    gl = _gelu(ys_ref[...].astype(F32))
    ys = gl * jax.nn.sigmoid(
        jnp.dot(gl.astype(BF16), gluw_ref[...], preferred_element_type=F32) + glub_ref[...])
    d = o_ref.shape[1]
    merged = None
    for r, y in enumerate((ya_ref[...], yb, ys)):
        proj = jnp.dot(y.astype(BF16), wbr_ref[r], preferred_element_type=F32)
        term = jax.nn.sigmoid(gate_ref[:, r * d:(r + 1) * d].astype(F32)) * proj
        merged = term if merged is None else merged + term
    y = jnp.dot(merged.astype(BF16), wout_ref[...], preferred_element_type=F32)
    o_ref[...] = x_ref[...] + g1_ref[...] * y


def _merge(x_all, ya, h_dirs, z, ys, g1, ml_norm_g, glu_w, glu_b, w_br, w_out, *, n_rows, bsz, seq, tm):
    d = x_all.shape[1]
    nlb = bsz * seq // tm
    bpb = seq // tm
    w = BRANCH_W

    def rows(width, col_blk=0):
        return pl.BlockSpec((tm, width), lambda i: (i, col_blk))

    def full(shape):
        return pl.BlockSpec(shape, lambda i: (0,) * len(shape))

    return pl.pallas_call(
        _merge_kernel,
        out_shape=jax.ShapeDtypeStruct((n_rows, d), F32),
        grid=(n_rows // tm,),
        in_specs=[rows(d), rows(w), rows(w), rows(w),
                  rows(w, Z_OM // w), rows(w),
                  rows(3 * d, Z_GATE // (3 * d)),
                  pl.BlockSpec((None, 1, d), lambda i: (_row_batch(i, nlb, bpb, bsz), 0, 0)),
                  full((1, w)), full((w, w)), full((1, w)), full((3, w, d)), full((d, d))],
        out_specs=rows(d),
        compiler_params=_cparams(("parallel",)),
        name="merge",
    )(x_all, ya, h_dirs[0], h_dirs[1], z, ys, z, g1, ml_norm_g, glu_w, glu_b, w_br, w_out)


NOT_TOP = 127.0


def _take_top(s, n_take, vals_ref, want_rank):
    rows = s.shape[0]
    ridx = lax.broadcasted_iota(jnp.int32, s.shape, 0).astype(F32)
    rank = jnp.full(s.shape, NOT_TOP, F32) if want_rank else None
    for i in range(n_take):
        mx = jnp.max(s, axis=0, keepdims=True)
        first = jnp.min(jnp.where(s == mx, ridx, float(rows)), axis=0, keepdims=True)
        hit = ridx == first
        s = jnp.where(hit, -jnp.inf, s)
        if want_rank:
            rank = jnp.where(hit, float(i), rank)
        vals_ref[i:i + 1, :] = mx
    return s, rank


SUBLANES = 8


def _sort_network_pairs(n):
    pairs = []
    p = 1
    while p < n:
        k = p
        while k >= 1:
            for j in range(k % p, n - k, 2 * k):
                for i in range(min(k, n - j - k)):
                    if (i + j) // (2 * p) == (i + j + k) // (2 * p):
                        pairs.append((i + j, i + j + k))
            k //= 2
        p *= 2
    return pairs


def _top_sorted(s):
    n = PEER_TOPK
    x = _slabs(s)
    assert len(x) <= n
    x = x + [jnp.full(x[0].shape, -jnp.inf, F32)] * (n - len(x))

    def exchange(i, j):
        x[i], x[j] = jnp.maximum(x[i], x[j]), jnp.minimum(x[i], x[j])

    for i, j in _sort_network_pairs(n):
        exchange(i, j)
    shift = SUBLANES // 2
    while shift >= 1:
        y = [pltpu.roll(v, shift, 0) for v in x]
        x = [jnp.maximum(x[i], y[n - 1 - i]) for i in range(n)]
        d = n // 2
        while d >= 1:
            for i in range(n):
                if not i & d:
                    exchange(i, i + d)
            d //= 2
        shift //= 2
    return x


def _slabs(s):
    return [s[SUBLANES * i:SUBLANES * (i + 1), :] for i in range(s.shape[0] // SUBLANES)]


def _tie_count(s, top):
    flag = jnp.zeros(top[0].shape, F32)
    for i in range(len(top) - 1):
        flag = flag + jnp.where(top[i] == top[i + 1], 1.0, 0.0)
    reach = functools.reduce(jnp.add, [jnp.where(v >= top[-1], 1.0, 0.0) for v in _slabs(s)])
    reach = jnp.sum(reach, axis=0, keepdims=True)
    return flag + jnp.where(reach != float(PEER_TOPK), 1.0, 0.0)


def _candidates(v1_ref, v2_ref):
    k = PEER_TOPK
    v2_head = v2_ref[0:8, :]
    jrow = lax.broadcasted_iota(jnp.int32, v2_head.shape, 0)
    pieces = [v1_ref[0:1, :] + v2_ref[...]]
    for i in range(1, 8):
        pieces.append(jnp.where(jrow < k // (i + 1), v1_ref[i:i + 1, :] + v2_head, -jnp.inf))
    pieces.append(v1_ref[8:16, :] + v2_ref[0:1, :])
    return jnp.concatenate(pieces, axis=0)


def _write_counts(picked, cnt_ref):
    cnt_ref[0:1, :] = jnp.sum(picked[0:16], axis=0, keepdims=True)
    for i in range(1, 8):
        cnt_ref[i:i + 1, :] = jnp.sum(picked[8 + 8 * i:16 + 8 * i], axis=0, keepdims=True)
    cnt_ref[8:16, :] = picked[72:80]


def _candidate_counts(v1_ref, v2_ref, top_ref, cnt_ref):
    cand = _candidates(v1_ref, v2_ref)
    cand_left, _ = _take_top(cand, PEER_TOPK, top_ref, False)
    _write_counts(jnp.where(cand_left != cand, 1.0, 0.0), cnt_ref)
    top = top_ref[...]
    return jnp.sum(jnp.exp(top - top[0:1, :]), axis=0, keepdims=True)


TOPK_HEADS = 2


def _peer_topk_kernel(q_ref, k1_ref, k2_ref, e1_ref, cnt1_ref, e2_ref, rank2_ref, *scratch):
    for hp in range(TOPK_HEADS):
        _peer_topk_head(q_ref[:, 2 * LANES * hp:2 * LANES * (hp + 1)], k1_ref, k2_ref,
                        e1_ref.at[hp], cnt1_ref.at[hp], e2_ref.at[hp], rank2_ref.at[hp], *scratch)


def _peer_topk_head(q, k1_ref, k2_ref, e1_ref, cnt1_ref, e2_ref, rank2_ref,
                    v1_ref, v2_ref, top_ref, cnt_ref):
    k = PEER_TOPK
    q = q.astype(BF16)
    s1 = _nt_dot(k1_ref[...], q[:, :LANES])
    s2 = _nt_dot(k2_ref[...], q[:, LANES:])
    top1 = _top_sorted(s1)
    top2 = _top_sorted(s2)
    for i in range(k):
        v1_ref[i:i + 1, :] = top1[i][0:1, :]
        v2_ref[i:i + 1, :] = top2[i][0:1, :]
    cand = _candidates(v1_ref, v2_ref)
    topc = _top_sorted(cand)
    any_tie = jnp.max(_tie_count(s1, top1) + _tie_count(s2, top2) + _tie_count(cand, topc)) > 0.0

    @pl.when(any_tie)
    def _():
        _, rank1 = _take_top(s1, k, v1_ref, True)
        _, rank2 = _take_top(s2, k, v2_ref, True)
        zsum = _candidate_counts(v1_ref, v2_ref, top_ref, cnt_ref)
        cnt1 = jnp.zeros(s1.shape, F32)
        for i in range(k):
            cnt1 = jnp.where(rank1 == float(i), cnt_ref[i:i + 1, :], cnt1)
        e1_ref[...] = jnp.where(rank1 < k, jnp.exp(s1 - v1_ref[0:1, :]), 0.0) / zsum
        cnt1_ref[...] = cnt1
        e2_ref[...] = jnp.where(rank2 < k, jnp.exp(s2 - v2_ref[0:1, :]), 0.0).astype(BF16)
        rank2_ref[...] = rank2.astype(BF16)

    @pl.when(jnp.logical_not(any_tie))
    def _():
        _write_counts(jnp.where(cand >= topc[k - 1][0:1, :], 1.0, 0.0), cnt_ref)
        zsum = functools.reduce(jnp.add, [jnp.exp(t[0:1, :] - topc[0][0:1, :]) for t in topc])
        cnt_rows = [jnp.broadcast_to(cnt_ref[i:i + 1, :], top1[0].shape) for i in range(k)]
        cnt1, rank2 = [], []
        for v in _slabs(s1):
            c = jnp.zeros(v.shape, F32)
            for i in range(k):
                c = jnp.where(v == top1[i], cnt_rows[i], c)
            cnt1.append(c)
        for v in _slabs(s2):
            r = jnp.full(v.shape, NOT_TOP, F32)
            for i in range(k):
                r = jnp.where(v == top2[i], float(i), r)
            rank2.append(r)
        e1_ref[...] = jnp.where(s1 >= v1_ref[k - 1:k, :], jnp.exp(s1 - v1_ref[0:1, :]), 0.0) / zsum
        cnt1_ref[...] = jnp.concatenate(cnt1, axis=0)
        e2_ref[...] = jnp.where(s2 >= v2_ref[k - 1:k, :], jnp.exp(s2 - v2_ref[0:1, :]), 0.0
                                ).astype(BF16)
        rank2_ref[...] = jnp.concatenate(rank2, axis=0).astype(BF16)


def _peer_topk(q, sub_k, *, n_rows, tm):
    nk = PEER_NKEYS
    row_tab = jax.ShapeDtypeStruct((n_rows // tm, PEER_HEADS, nk, tm), F32)
    tile_tab = jax.ShapeDtypeStruct((n_rows // tm, PEER_HEADS, nk, tm), BF16)
    tab_spec = pl.BlockSpec((None, TOPK_HEADS, nk, tm), lambda i, h: (i, h, 0, 0))
    return pl.pallas_call(
        _peer_topk_kernel,
        out_shape=(row_tab, row_tab, tile_tab, tile_tab),
        grid=(n_rows // tm, PEER_HEADS // TOPK_HEADS),
        in_specs=[pl.BlockSpec((tm, 2 * LANES * TOPK_HEADS), lambda i, h: (i, h)),
                  pl.BlockSpec((None, nk, LANES), lambda i, h: (0, 0, 0)),
                  pl.BlockSpec((None, nk, LANES), lambda i, h: (1, 0, 0))],
        out_specs=(tab_spec,) * 4,
        scratch_shapes=[pltpu.VMEM((PEER_TOPK, tm), F32)] * 4,
        compiler_params=_cparams(("parallel", "parallel")),
        name="peer_topk",
    )(q, sub_k, sub_k)


GATE_KEYS = 2
KEY_GROUP = 8


def _peer_dense_kernel(ht_ref, u_ref, vt_ref, e1_ref, cnt1_ref, e2_ref, rank2_ref,
                       x_ref, g2_ref, fg_ref, o_ref, acc_ref, act_ref, p_ref, e2s_ref, rank2s_ref, *,
                       a_per_blk, final_norm):
    j = pl.program_id(1)
    n_blk = pl.num_programs(1) - 1
    cur = j % 2
    tm = ht_ref.shape[1]
    nk = PEER_NKEYS

    @pl.when(j == 0)
    def _():
        acc_ref[...] = jnp.zeros_like(acc_ref)
        p_ref[1] = jnp.zeros(p_ref.shape[1:], p_ref.dtype)
        e2s_ref[...] = e2_ref[...]
        rank2s_ref[...] = rank2_ref[...]

    @pl.when(j < n_blk)
    def _():
        act_ref[...] = _gelu(jnp.dot(u_ref[...], ht_ref[...], preferred_element_type=F32)
                             ).astype(BF16)
        acc_ref[...] += jnp.dot(vt_ref[...], p_ref[1 - cur], preferred_element_type=F32)
        a0 = pl.multiple_of(j * a_per_blk, KEY_GROUP)

        def row_bf16(ref, h, i, ls):
            grp = (i // KEY_GROUP) * KEY_GROUP
            w = ref[h, pl.ds(a0 + grp, KEY_GROUP), ls][i % KEY_GROUP:i % KEY_GROUP + 1]
            return jnp.broadcast_to(w, (nk, LANES)).astype(BF16)

        for lt in range(tm // LANES):
            ls = slice(lt * LANES, (lt + 1) * LANES)
            for i0 in range(0, a_per_blk, GATE_KEYS):
                gates = [None] * GATE_KEYS
                for h in range(PEER_HEADS):
                    for ii in range(GATE_KEYS):
                        lead = row_bf16(cnt1_ref, h, i0 + ii, ls) - rank2s_ref[h, :, ls]
                        term = jnp.minimum(jnp.maximum(lead, 0.0),
                                           row_bf16(e1_ref, h, i0 + ii, ls)) * e2s_ref[h, :, ls]
                        gates[ii] = term if gates[ii] is None else gates[ii] + term
                for ii in range(GATE_KEYS):
                    r0 = (i0 + ii) * nk
                    p_ref[cur, r0:r0 + nk, ls] = gates[ii] * act_ref[r0:r0 + nk, ls]

    @pl.when(j == n_blk)
    def _():
        acc = acc_ref[...] + jnp.dot(vt_ref[...], p_ref[1 - cur], preferred_element_type=F32)
        y = x_ref[...] + g2_ref[...] * acc.T
        if final_norm:
            y = y * lax.rsqrt(jnp.mean(y * y, axis=-1, keepdims=True) + EPS) * fg_ref[...]
        o_ref[...] = y


def _peer_dense(h2, u_tab, vt_tab, tabs, x_all, g2, final_g, *, n_rows, bsz, seq, tm, te,
                final_norm):
    d = x_all.shape[1]
    n_exp = u_tab.shape[0]
    nlb = bsz * seq // tm
    bpb = seq // tm
    nk = PEER_NKEYS
    n_blk = n_exp // te
    assert (te // nk) % KEY_GROUP == 0
    tab_spec = pl.BlockSpec((None, PEER_HEADS, nk, tm), lambda i, j: (i, 0, 0, 0))
    return pl.pallas_call(
        functools.partial(_peer_dense_kernel, a_per_blk=te // nk, final_norm=final_norm),
        out_shape=jax.ShapeDtypeStruct((n_rows, d), F32),
        grid=(n_rows // tm, n_blk + 1),
        in_specs=[pl.BlockSpec((d, tm), lambda i, j: (0, i)),
                  pl.BlockSpec((te, d), lambda i, j: (jnp.minimum(j, n_blk - 1), 0)),
                  pl.BlockSpec((None, d, te), lambda i, j: (jnp.maximum(j - 1, 0), 0, 0)),
                  tab_spec, tab_spec, tab_spec, tab_spec,
                  pl.BlockSpec((tm, d), lambda i, j: (i, 0)),
                  pl.BlockSpec((None, 1, d), lambda i, j: (_row_batch(i, nlb, bpb, bsz), 0, 0)),
                  pl.BlockSpec((1, d), lambda i, j: (0, 0))],
        out_specs=pl.BlockSpec((tm, d), lambda i, j: (i, 0)),
        scratch_shapes=[pltpu.VMEM((d, tm), F32), pltpu.VMEM((te, tm), BF16),
                        pltpu.VMEM((2, te, tm), BF16),
                        pltpu.VMEM((PEER_HEADS, nk, tm), BF16), pltpu.VMEM((PEER_HEADS, nk, tm), BF16)],
        compiler_params=_cparams(("parallel", "arbitrary")),
        name="peer_dense",
    )(h2, u_tab, vt_tab, *tabs, x_all, g2, final_g)


def _transpose_cast_kernel(x_ref, o_ref):
    o_ref[...] = x_ref[...].T.astype(o_ref.dtype)


def _transpose_cast(w, *, tr):
    rows, cols = w.shape
    return pl.pallas_call(
        _transpose_cast_kernel,
        out_shape=jax.ShapeDtypeStruct((rows // tr, cols, tr), BF16),
        grid=(rows // tr,),
        in_specs=[pl.BlockSpec((tr, cols), lambda i: (i, 0))],
        out_specs=pl.BlockSpec((None, cols, tr), lambda i: (i, 0, 0)),
        compiler_params=_cparams(("parallel",)),
        name="transpose_cast",
    )(w)


def _rope_tables(seq, tm):
    rows = seq // GRID_W
    n_freq = DA_HD // 4
    inv = ROPE_BASE ** (-jnp.arange(n_freq, dtype=F32) / n_freq)
    r = jnp.repeat(jnp.arange(rows, dtype=F32), GRID_W)
    col = jnp.tile(jnp.arange(GRID_W, dtype=F32), rows)
    ang = jnp.concatenate([r[:, None] * inv, col[:, None] * inv], axis=-1)
    cos, sin = jnp.cos(ang), jnp.sin(ang)
    cos_t = jnp.tile(cos, (1, 4))
    sin_t = jnp.tile(jnp.concatenate([-sin, sin], axis=-1), (1, 2))
    ident = jnp.ones((tm, LANES), F32)
    return (jnp.concatenate([cos_t, ident], axis=0),
            jnp.concatenate([sin_t, jnp.zeros((tm, LANES), F32)], axis=0))


def kernel(x, c, ctx, c_ctx, w_mod, b_mod, norm1_g, norm2_g, w_in, da_lam_q1, da_lam_k1, da_lam_q2, da_lam_k2, da_sub_g, ml_conv_w, ml_conv_b, ml_wq, ml_wk, ml_gate_b, ml_norm_g, s5_lam_re, s5_lam_im, s5_log_dt, s5_b_re, s5_b_im, s5_c_re, s5_c_im, s5_d, s5_glu_w, s5_glu_b, w_br, w_out, peer_wq, peer_sub_k, peer_u, peer_v, final_g):
    bsz, seq, d = x.shape
    ctx_len = ctx.shape[1]
    depth = w_in.shape[0]
    t_lat = bsz * seq
    t_ctx = bsz * ctx_len
    t_all = t_lat + t_ctx
    tm = math.gcd(512, math.gcd(seq, t_ctx))
    tm_merge = min(tm, 256)
    tq = min(256, ctx_len)
    nh = ML_HEADS

    x_all = jnp.concatenate([x.reshape(t_lat, d), ctx.reshape(t_ctx, d)], axis=0)
    pad = (-(bsz + 1)) % 8
    c_all = jnp.concatenate([c, c_ctx[None], jnp.zeros((pad, d), F32)], axis=0)
    cos_tab, sin_tab = _rope_tables(seq, tm)

    for l in range(depth):
        need_ctx = l < depth - 1
        n_rows = t_all if need_ctx else t_lat
        lam_init = 0.8 - 0.6 * math.exp(-0.3 * l)

        mods = _modulation(c_all, w_mod[l].astype(BF16), b_mod[l][None])
        sh1, sc1, g1, sh2, sc2, g2 = [mods[:bsz + 1, i * d:(i + 1) * d][:, None, :]
                                      for i in range(N_MOD)]

        wi = w_in[l]
        i_q, i_k, i_v, i_xm, i_vm, i_om, i_g, i_u, i_gate = (
            0, 512, 1024, 1536, 2048, 2560, 3072, 3088, 3600)
        w_main = jnp.concatenate([wi[:, i_gate:], wi[:, :i_g], wi[:, i_u:i_gate]], axis=1).astype(BF16)
        w_gate = jnp.pad(wi[:, i_g:i_u], ((0, 0), (0, LANES - (i_u - i_g)))).astype(BF16)
        n1 = norm1_g[l][None]
        z = _modnorm_matmul(x_all, t_all, n1, sc1, sh1, w_main, bsz=bsz, seq=seq, tm=tm,
                            tn=w_main.shape[1] // 2, out_dtype=BF16)
        zg = _modnorm_matmul(x_all, t_all, n1, sc1, sh1, w_gate, bsz=bsz, seq=seq, tm=tm, tn=LANES)

        qr, kr, vb = _rope_qkv(z, cos_tab, sin_tab, bsz=bsz, seq=seq, tm=tm)
        lam = (jnp.exp(jnp.sum(da_lam_q1[l] * da_lam_k1[l]))
               - jnp.exp(jnp.sum(da_lam_q2[l] * da_lam_k2[l])) + lam_init).reshape(1, 1)
        sub_g = da_sub_g[l][None]
        ya = _diff_attention(qr, kr, vb, lam, sub_g, q_row0=0, n_q=seq,
                             kv_segs=[(0, seq), (t_lat, ctx_len)], bsz=bsz, tq=tq,
                             out_scale=1.0 - lam_init, out_rows=n_rows)
        if need_ctx:
            ya = _diff_attention(qr, kr, vb, lam, sub_g, q_row0=t_lat, n_q=ctx_len,
                                 kv_segs=[(t_lat, ctx_len)], bsz=bsz, tq=tq,
                                 out_scale=1.0 - lam_init, out_rows=n_rows, into=ya)

        wq_b, wk_b = ml_wq[l].astype(BF16), ml_wk[l].astype(BF16)
        cb = ml_conv_b[l][None]
        qkt = _ml_prep(z, ml_conv_w[l], cb, wq_b, wk_b, row0=0, seq_len=seq, bsz=bsz)
        q_ml, k_ml, kt_ml = _ml_prep(z, ml_conv_w[l], cb, wq_b, wk_b, row0=t_lat, seq_len=ctx_len,
                                     bsz=bsz, into=qkt)
        gates = (zg[:, :4 * nh] + ml_gate_b[l]).reshape(t_all, 2, 2 * nh)
        gates_c = jnp.transpose(gates, (1, 0, 2))
        gates_r = jnp.transpose(gates, (1, 2, 0))
        h_dirs = _mlstm_scan(q_ml, k_ml, kt_ml, z, gates_c, gates_r, bsz=bsz, seq=seq,
                             ctx_len=ctx_len)

        tables = _s5_tables(s5_lam_re[l], s5_lam_im[l], s5_log_dt[l], s5_b_re[l], s5_b_im[l],
                            s5_c_re[l], s5_c_im[l], s5_d[l])
        u = z[:, Z_U:Z_U + BRANCH_W]
        y_c, y_l = _s5_mix(_s5_rows(u[t_lat:], bsz, ctx_len), _s5_rows(u[:t_lat], bsz, seq),
                           tables, bsz=bsz)
        ys = _s5_unrows(y_l, bsz, seq)
        if need_ctx:
            ys = jnp.concatenate([ys, _s5_unrows(y_c, bsz, ctx_len)], axis=0)

        x_all = _merge(x_all, ya, h_dirs, z, ys, g1, ml_norm_g[l][None],
                       s5_glu_w[l].astype(BF16), s5_glu_b[l][None], w_br[l].astype(BF16),
                       w_out[l].astype(BF16), n_rows=n_rows, bsz=bsz, seq=seq, tm=tm_merge)

        n2 = norm2_g[l][None]
        wq_p = peer_wq[l].astype(BF16)
        pq, h2t = _modnorm_matmul(x_all, n_rows, n2, sc2, sh2, wq_p, bsz=bsz, seq=seq, tm=tm,
                                  tn=wq_p.shape[1], emit_h=True)
        tabs = _peer_topk(pq, peer_sub_k[l].astype(BF16), n_rows=n_rows, tm=tm)
        te = 2048
        x_all = _peer_dense(h2t, peer_u[l].astype(BF16), _transpose_cast(peer_v[l], tr=te),
                            tabs, x_all, g2, final_g[None], n_rows=n_rows, bsz=bsz, seq=seq, tm=tm,
                            te=te, final_norm=not need_ctx)

    return x_all.reshape(bsz, seq, d)
```

```python
import functools
import math

import jax
import jax.numpy as jnp
from jax import lax
from jax.experimental import pallas as pl
from jax.experimental.pallas import tpu as pltpu

F32 = jnp.float32
BF16 = jnp.bfloat16

EPS = 1e-6
LOG2E = 1.4426950408889634
N_MOD = 6
BRANCH_W = 512
GRID_W = 64
ROPE_BASE = 10000.0
DA_HEADS = 4
DA_HD = 64
DA_VD = 2 * DA_HD
ML_HEADS = 4
ML_HD = BRANCH_W // ML_HEADS
ML_CHUNK = 128
S5_P = 16
S5_G = BRANCH_W // S5_P
S5_N = 64
S5_T = 16
PEER_HEADS = 8
PEER_NKEYS = 128
PEER_TOPK = 16
LANES = 128
VMEM_LIMIT = 56 * 1024 * 1024

Z_GATE = 0
Z_Q, Z_K, Z_V, Z_XM, Z_VM, Z_OM, Z_U = (3072 + i * BRANCH_W for i in range(7))


def _cparams(sem):
    return pltpu.CompilerParams(dimension_semantics=sem, vmem_limit_bytes=VMEM_LIMIT)


def _nt_dot(a, b):
    return lax.dot_general(a, b, (((1,), (1,)), ((), ())), preferred_element_type=F32)


def _skip_refs(kernel_fn, start, count, *refs):
    return kernel_fn(*refs[:start], *refs[start + count:])


def _gelu(x):
    return 0.5 * x * (1.0 + lax.erf(x * (2.0 ** -0.5)))


def _log_sigmoid(x):
    return jnp.minimum(x, 0.0) - jnp.log1p(jnp.exp(-jnp.abs(x)))


def _mod_kernel(c_ref, w_ref, b_ref, o_ref):
    c = c_ref[...]
    a = c * jax.nn.sigmoid(c)
    o_ref[...] = jnp.dot(a.astype(BF16), w_ref[...], preferred_element_type=F32) + b_ref[...]


def _modulation(c_all, w, b):
    rows, d = c_all.shape
    n = w.shape[1]
    tn = 1536
    return pl.pallas_call(
        _mod_kernel,
        out_shape=jax.ShapeDtypeStruct((rows, n), F32),
        grid=(n // tn,),
        in_specs=[pl.BlockSpec((rows, d), lambda j: (0, 0)),
                  pl.BlockSpec((d, tn), lambda j: (0, j)),
                  pl.BlockSpec((1, tn), lambda j: (0, j))],
        out_specs=pl.BlockSpec((rows, tn), lambda j: (0, j)),
        compiler_params=_cparams(("parallel",)),
        name="modulation",
    )(c_all, w, b)


def _modnorm_matmul_kernel(x_ref, g_ref, sc_ref, sh_ref, w_ref, o_ref, *h_out):
    x = x_ref[...]
    y = x * lax.rsqrt(jnp.mean(x * x, axis=-1, keepdims=True) + EPS) * g_ref[...]
    hf = y * (1.0 + sc_ref[...]) + sh_ref[...]
    h = hf.astype(BF16)
    o_ref[...] = jnp.dot(h, w_ref[...], preferred_element_type=F32).astype(o_ref.dtype)
    if h_out:
        h_out[0][...] = hf.T.astype(BF16)


def _row_batch(i, n_lat_blocks, blocks_per_batch, bsz):
    return jnp.where(i < n_lat_blocks, i // blocks_per_batch, bsz)


def _modnorm_matmul(x_all, n_rows, g, sc, sh, w, *, bsz, seq, tm, tn, out_dtype=F32, emit_h=False):
    d = x_all.shape[1]
    n = w.shape[1]
    nlb = bsz * seq // tm
    bpb = seq // tm
    assert not emit_h or tn == n
    mod_spec = pl.BlockSpec((None, 1, d), lambda j, i: (_row_batch(i, nlb, bpb, bsz), 0, 0))
    out_shape = [jax.ShapeDtypeStruct((n_rows, n), out_dtype)]
    out_specs = [pl.BlockSpec((tm, tn), lambda j, i: (i, j))]
    if emit_h:
        out_shape.append(jax.ShapeDtypeStruct((d, n_rows), BF16))
        out_specs.append(pl.BlockSpec((d, tm), lambda j, i: (0, i)))
    out = pl.pallas_call(
        _modnorm_matmul_kernel,
        out_shape=tuple(out_shape),
        grid=(n // tn, n_rows // tm),
        in_specs=[pl.BlockSpec((tm, d), lambda j, i: (i, 0)),
                  pl.BlockSpec((1, d), lambda j, i: (0, 0)),
                  mod_spec, mod_spec,
                  pl.BlockSpec((d, tn), lambda j, i: (0, j))],
        out_specs=tuple(out_specs),
        compiler_params=_cparams(("parallel", "parallel")),
        name="modnorm_matmul",
    )(x_all, g, sc, sh, w)
    return out if emit_h else out[0]


def _rope_kernel(q_ref, k_ref, v_ref, cos_ref, sin_ref, qo_ref, ko_ref, vo_ref):
    cos = cos_ref[...]
    sin = sin_ref[...]
    lane = lax.broadcasted_iota(jnp.int32, cos.shape, 1)
    first_half = (lane % DA_HD) < (DA_HD // 2)

    def rope(x):
        partner = jnp.where(first_half, pltpu.roll(x, LANES - DA_HD // 2, 1),
                            pltpu.roll(x, DA_HD // 2, 1))
        return x * cos + partner * sin

    for h in range(DA_HEADS):
        sl = slice(h * LANES, (h + 1) * LANES)
        qo_ref[:, sl] = (rope(q_ref[:, sl].astype(F32)) * (DA_HD ** -0.5 * LOG2E)).astype(BF16)
        ko_ref[:, sl] = rope(k_ref[:, sl].astype(F32)).astype(BF16)
    vo_ref[...] = v_ref[...].astype(BF16)


def _rope_qkv(z, cos_tab, sin_tab, *, bsz, seq, tm):
    rows = z.shape[0]
    nlb = bsz * seq // tm
    bpb = seq // tm
    w = BRANCH_W
    tab_spec = pl.BlockSpec((tm, LANES), lambda i: (jnp.where(i < nlb, i % bpb, bpb), 0))
    out = jax.ShapeDtypeStruct((rows, w), BF16)
    return pl.pallas_call(
        _rope_kernel,
        out_shape=(out, out, out),
        grid=(rows // tm,),
        in_specs=[pl.BlockSpec((tm, w), lambda i: (i, Z_Q // w)),
                  pl.BlockSpec((tm, w), lambda i: (i, Z_K // w)),
                  pl.BlockSpec((tm, w), lambda i: (i, Z_V // w)),
                  tab_spec, tab_spec],
        out_specs=(pl.BlockSpec((tm, w), lambda i: (i, 0)),) * 3,
        compiler_params=_cparams(("parallel",)),
        name="rope_qkv",
    )(z, z, z, cos_tab, sin_tab)


def _attn_kernel(*refs, n_seg, n_sub, out_scale):
    lam_ref, q_ref = refs[0], refs[1]
    k_refs = refs[2:2 + n_seg]
    v_refs = refs[2 + n_seg:2 + 2 * n_seg]
    g_ref, o_ref = refs[2 + 2 * n_seg], refs[3 + 2 * n_seg]
    tq = q_ref.shape[0] // n_sub
    lane = lax.broadcasted_iota(jnp.int32, (tq, LANES), 1)
    scores = []
    for sub in range(n_sub):
        q = q_ref[sub * tq:(sub + 1) * tq, :].astype(F32)
        qq = jnp.concatenate([jnp.where(lane < DA_HD, q, 0.0), jnp.where(lane >= DA_HD, q, 0.0)],
                             axis=0).astype(BF16)
        scores.append([_nt_dot(qq, k_ref[...]) for k_ref in k_refs])
    for sub in range(n_sub):
        rows = slice(sub * tq, (sub + 1) * tq)
        s = scores[sub]
        m = functools.reduce(jnp.maximum, [jnp.max(t, axis=-1, keepdims=True) for t in s])
        p = [jnp.exp2(t - m) for t in s]
        denom = functools.reduce(jnp.add, [jnp.sum(t, axis=-1, keepdims=True) for t in p])
        inv = 1.0 / denom
        c0 = inv[:tq]
        c1 = inv[tq:] * lam_ref[...]
        o = None
        for t, v_ref in zip(p, v_refs):
            w = (t[:tq] * c0 - t[tq:] * c1).astype(BF16)
            part = jnp.dot(w, v_ref[...], preferred_element_type=F32)
            o = part if o is None else o + part
        y = o * lax.rsqrt(jnp.mean(o * o, axis=-1, keepdims=True) + EPS) * g_ref[...]
        o_ref[rows, :] = (y * out_scale).astype(o_ref.dtype)


def _diff_attention(qr, kr, vb, lam, sub_g, *, q_row0, n_q, kv_segs, bsz, tq, out_scale, out_rows,
                    into=None):
    n_sub = 2 if n_q % (2 * tq) == 0 else 1
    tq = tq * n_sub
    nqb = n_q // tq
    n_seg = len(kv_segs)
    q_blk0 = q_row0 // tq

    def kv_spec(row0, length):
        return pl.BlockSpec((length, LANES), lambda b, h, i: (row0 // length + b, h))

    in_specs = [pl.BlockSpec((1, 1), lambda b, h, i: (0, 0)),
                pl.BlockSpec((tq, LANES), lambda b, h, i: (q_blk0 + b * nqb + i, h))]
    in_specs += [kv_spec(r0, ln) for r0, ln in kv_segs] * 2
    in_specs += [pl.BlockSpec((1, LANES), lambda b, h, i: (0, h))]
    args = [lam, qr] + [kr] * n_seg + [vb] * n_seg + [sub_g]
    kern = functools.partial(_attn_kernel, n_seg=n_seg, n_sub=n_sub, out_scale=out_scale)
    aliases = {}
    if into is not None:
        aliases = {len(args): 0}
        kern = functools.partial(_skip_refs, kern, len(args), 1)
        in_specs += [pl.BlockSpec(memory_space=pl.ANY)]
        args += [into]
    return pl.pallas_call(
        kern,
        out_shape=jax.ShapeDtypeStruct((out_rows, BRANCH_W), BF16),
        grid=(bsz, DA_HEADS, nqb),
        in_specs=in_specs,
        out_specs=pl.BlockSpec((tq, LANES), lambda b, h, i: (q_blk0 + b * nqb + i, h)),
        input_output_aliases=aliases,
        compiler_params=_cparams(("parallel", "parallel", "parallel")),
        name="diff_attention",
    )(*args)


def _ml_prep_kernel(x_ref, cw_ref, cb_ref, wq_ref, wk_ref, q_ref, k_ref, kt_ref):
    x = x_ref[...].astype(F32)
    n = x.shape[0]
    row = lax.broadcasted_iota(jnp.int32, x.shape, 0)
    x_prev = jnp.where(row == 0, 0.0, pltpu.roll(x, 1, 0))
    x_next = jnp.where(row == n - 1, 0.0, pltpu.roll(x, n - 1, 0))
    cw = cw_ref[...]
    y = x_prev * cw[0:1] + x * cw[1:2] + x_next * cw[2:3] + cb_ref[...]
    xc = (y * jax.nn.sigmoid(y)).astype(BF16)
    q_ref[...] = (jnp.dot(xc, wq_ref[...], preferred_element_type=F32) * (ML_HD ** -0.5)
                  ).astype(q_ref.dtype)
    k = jnp.dot(xc, wk_ref[...], preferred_element_type=F32)
    k_ref[...] = k.astype(k_ref.dtype)
    kt_ref[...] = k.T.astype(kt_ref.dtype)


def _ml_prep(z, conv_w, conv_b, wq, wk, *, row0, seq_len, bsz, into=None):
    rows = z.shape[0]
    blk0 = row0 // seq_len
    out = jax.ShapeDtypeStruct((rows, BRANCH_W), BF16)
    out_t = jax.ShapeDtypeStruct((BRANCH_W, rows), BF16)
    head_w = pl.BlockSpec((None, ML_HD, ML_HD), lambda b, h: (h, 0, 0))
    row_out = pl.BlockSpec((seq_len, LANES), lambda b, h: (blk0 + b, h))
    in_specs = [pl.BlockSpec((seq_len, LANES), lambda b, h: (blk0 + b, Z_XM // LANES + h)),
                pl.BlockSpec((3, LANES), lambda b, h: (0, h)),
                pl.BlockSpec((1, LANES), lambda b, h: (0, h)),
                head_w, head_w]
    args = [z, conv_w, conv_b, wq, wk]
    aliases = {}
    kern = _ml_prep_kernel
    if into is not None:
        n_in = len(args)
        in_specs += [pl.BlockSpec(memory_space=pl.ANY)] * len(into)
        args += list(into)
        aliases = {n_in + i: i for i in range(len(into))}
        kern = functools.partial(_skip_refs, _ml_prep_kernel, n_in, len(into))
    return pl.pallas_call(
        kern,
        out_shape=(out, out, out_t),
        grid=(bsz, ML_HEADS),
        in_specs=in_specs,
        out_specs=(row_out, row_out, pl.BlockSpec((LANES, seq_len), lambda b, h: (h, blk0 + b))),
        input_output_aliases=aliases,
        compiler_params=_cparams(("parallel", "parallel")),
        name="mlstm_prep",
    )(*args)


def _mlstm_kernel(qf_ref, kf_ref, ktf_ref, vf_ref, gcf_ref, grf_ref,
                  qb_ref, kb_ref, ktb_ref, vb_ref, gcb_ref, grb_ref,
                  hf_ref, hb_ref, c_ref, n_ref, m_ref):
    @pl.when(pl.program_id(1) == 0)
    def _():
        c_ref[...] = jnp.zeros_like(c_ref)
        n_ref[...] = jnp.zeros_like(n_ref)
        m_ref[...] = jnp.zeros_like(m_ref)

    t_idx = lax.broadcasted_iota(jnp.int32, (ML_CHUNK, ML_CHUNK), 0)
    s_idx = lax.broadcasted_iota(jnp.int32, (ML_CHUNK, ML_CHUNK), 1)
    nh = ML_HEADS
    dirs = ((qf_ref, kf_ref, ktf_ref, vf_ref, gcf_ref, grf_ref, hf_ref),
            (qb_ref, kb_ref, ktb_ref, vb_ref, gcb_ref, grb_ref, hb_ref))
    ones = jnp.ones((ML_CHUNK, LANES), BF16)
    units = []
    for d, (q_ref, k_ref, kt_ref, v_ref, gc_ref, gr_ref, h_ref) in enumerate(dirs):
        mask = s_idx <= t_idx if d == 0 else s_idx >= t_idx
        mask_f = mask.astype(F32)
        gc = gc_ref[...]
        gr = gr_ref[...]
        lf_c = _log_sigmoid(gc)
        lf_r = _log_sigmoid(gr)
        b_c = jnp.dot(mask_f, lf_c, preferred_element_type=F32, precision=lax.Precision.HIGHEST)
        b_r = lax.dot_general(lf_r, mask_f, (((1,), (1,)), ((), ())), preferred_element_type=F32,
                              precision=lax.Precision.HIGHEST)
        b_end_all = jnp.sum(lf_c, axis=0, keepdims=True)

        for h in range(nh):
            sl = slice(h * ML_HD, (h + 1) * ML_HD)
            q = q_ref[:, sl]
            k = k_ref[:, sl]
            kt = kt_ref[sl, :]
            v = v_ref[:, sl].astype(F32)
            li_c = gc[:, h:h + 1]
            li_r = gr[h:h + 1, :]
            bc = b_c[:, nh + h:nh + h + 1]
            br = b_r[nh + h:nh + h + 1, :]
            b_end = b_end_all[:, nh + h:nh + h + 1]
            m_prev = m_ref[d, h]
            c_prev = c_ref[d, h]
            n_prev = n_ref[d, h]
            g_c = b_end - bc + li_c
            g_r = b_end - br + li_r
            m_new = jnp.maximum(b_end + m_prev, jnp.max(g_r, axis=-1, keepdims=True))
            wg = jnp.broadcast_to(jnp.exp(g_c - m_new), (ML_CHUNK, LANES))
            decay = jnp.exp(b_end + m_prev - m_new)
            vw = (v * wg).astype(BF16)
            units.append(dict(
                d=d, h=h, sl=sl, h_ref=h_ref, v=v, mask=mask, m_prev=m_prev,
                bc=bc, br=br, li_r=li_r,
                qk=_nt_dot(q, k),
                qc=jnp.dot(q, c_prev.astype(BF16), preferred_element_type=F32),
                qn=jnp.dot(q, n_prev.astype(BF16), preferred_element_type=F32),
                c_new=decay * c_prev + jnp.dot(kt, vw, preferred_element_type=F32),
                n_new=decay * n_prev + jnp.dot(kt, wg.astype(BF16), preferred_element_type=F32),
                m_new=m_new))

    for u in units:
        dmat = jnp.where(u["mask"], u["bc"] - u["br"] + u["li_r"], -jnp.inf)
        inter = u["bc"] + u["m_prev"]
        m_t = jnp.maximum(inter, jnp.max(dmat, axis=-1, keepdims=True))
        s = (u["qk"] * jnp.exp(dmat - m_t)).astype(BF16)
        w_prev = jnp.exp(inter - m_t)
        num = jnp.dot(s, u["v"].astype(BF16), preferred_element_type=F32) + w_prev * u["qc"]
        den = jnp.dot(s, ones, preferred_element_type=F32) + w_prev * u["qn"]
        u["h_ref"][:, u["sl"]] = num / jnp.maximum(jnp.abs(den), jnp.exp(-m_t))
        c_ref[u["d"], u["h"]] = u["c_new"]
        n_ref[u["d"], u["h"]] = u["n_new"]
        m_ref[u["d"], u["h"]] = u["m_new"]


def _mlstm_scan(q_all, k_all, kt_all, z, gates_c, gates_r, *, bsz, seq, ctx_len):
    rows = q_all.shape[0]
    ncl = seq // ML_CHUNK
    ncc = ctx_len // ML_CHUNK
    lat_blocks = bsz * ncl
    nh = ML_HEADS

    def blk(d, b, s):
        jc = s if d == 0 else ncc - 1 - s
        jl = s - ncc if d == 0 else ncl - 1 - (s - ncc)
        return jnp.where(s < ncc, lat_blocks + b * ncc + jc, b * ncl + jl)

    def dir_specs(d):
        return [pl.BlockSpec((ML_CHUNK, BRANCH_W), lambda b, s: (blk(d, b, s), 0)),
                pl.BlockSpec((ML_CHUNK, BRANCH_W), lambda b, s: (blk(d, b, s), 0)),
                pl.BlockSpec((BRANCH_W, ML_CHUNK), lambda b, s: (0, blk(d, b, s))),
                pl.BlockSpec((ML_CHUNK, BRANCH_W), lambda b, s: (blk(d, b, s), Z_VM // BRANCH_W)),
                pl.BlockSpec((None, ML_CHUNK, 2 * nh), lambda b, s: (d, blk(d, b, s), 0)),
                pl.BlockSpec((None, 2 * nh, ML_CHUNK), lambda b, s: (d, 0, blk(d, b, s)))]

    out = jax.ShapeDtypeStruct((rows, BRANCH_W), F32)
    args = (q_all, k_all, kt_all, z, gates_c, gates_r)
    return pl.pallas_call(
        _mlstm_kernel,
        out_shape=(out, out),
        grid=(bsz, ncc + ncl),
        in_specs=dir_specs(0) + dir_specs(1),
        out_specs=(pl.BlockSpec((ML_CHUNK, BRANCH_W), lambda b, s: (blk(0, b, s), 0)),
                   pl.BlockSpec((ML_CHUNK, BRANCH_W), lambda b, s: (blk(1, b, s), 0))),
        scratch_shapes=[pltpu.VMEM((2, nh, ML_HD, ML_HD), F32),
                        pltpu.VMEM((2, nh, ML_HD, LANES), F32),
                        pltpu.VMEM((2, nh, 1, 1), F32)],
        compiler_params=_cparams(("parallel", "arbitrary")),
        name="mlstm_scan",
    )(*args, *args)


def _s5_kernel(uc_ref, ult_ref, m_ref, w_ref, v_ref, a1_ref, a2_ref, yc_ref, ylt_ref,
               ec_ref, el_ref, xc_ref, xl_ref, ul_ref, *, bsz):
    n_cl = ul_ref.shape[0] // bsz
    per_batch = []
    for b in range(bsz):
        cols = [ult_ref[:, (b * S5_T + t) * n_cl:(b * S5_T + t + 1) * n_cl].astype(F32)
                for t in range(S5_T)]
        per_batch.append(jnp.concatenate(cols, axis=0).T)
    ul_ref[...] = pltpu.einshape("bcf->(cb)f", jnp.stack(per_batch)).astype(BF16)

    w = w_ref[...]
    ec_ref[...] = jnp.dot(uc_ref[...], w, preferred_element_type=F32)
    el_ref[...] = jnp.dot(ul_ref[...], w, preferred_element_type=F32)
    a1 = a1_ref[...]
    a2 = a2_ref[...]
    half = LANES
    a1f, a2f, a1b, a2b = a1[:, 0:half], a2[:, 0:half], a1[:, half:], a2[:, half:]

    def scan(e_ref, x_ref, state):
        n_chunks = e_ref.shape[0] // bsz

        def body(j, st):
            xf, xfs, xb, xbs = st
            rf = pl.multiple_of(j * bsz, bsz)
            rb = pl.multiple_of((n_chunks - 1 - j) * bsz, bsz)
            x_ref[pl.ds(rf, bsz), 0:half] = xf
            x_ref[pl.ds(rb, bsz), half:2 * half] = xb
            ef, eb = e_ref[pl.ds(rf, bsz), 0:half], e_ref[pl.ds(rb, bsz), half:2 * half]
            efs = e_ref[pl.ds(rf, bsz), 2 * half:3 * half]
            ebs = e_ref[pl.ds(rb, bsz), 3 * half:4 * half]
            return (a1f * xf + a2f * xfs + ef, a1f * xfs - a2f * xf + efs,
                    a1b * xb + a2b * xbs + eb, a1b * xbs - a2b * xb + ebs)

        return lax.fori_loop(0, n_chunks, body, state)

    zero = jnp.zeros((bsz, half), F32)
    state = scan(ec_ref, xc_ref, (zero, zero, zero, zero))
    scan(el_ref, xl_ref, state)
    mm = m_ref[...]
    vv = v_ref[...]
    yc_ref[...] = (jnp.dot(uc_ref[...], mm, preferred_element_type=F32)
                   + jnp.dot(xc_ref[...].astype(BF16), vv, preferred_element_type=F32)
                   ).astype(yc_ref.dtype)
    yl = (jnp.dot(ul_ref[...], mm, preferred_element_type=F32)
          + jnp.dot(xl_ref[...].astype(BF16), vv, preferred_element_type=F32))
    yl = pltpu.einshape("(cb)f->bcf", yl, b=bsz)
    for b in range(bsz):
        ybt = yl[b].T
        for t in range(S5_T):
            ylt_ref[:, (b * S5_T + t) * n_cl:(b * S5_T + t + 1) * n_cl] = (
                ybt[t * S5_P:(t + 1) * S5_P, :].astype(ylt_ref.dtype))


def _s5_pack_kernel(x_ref, o_ref, f_ref):
    n_cl = x_ref.shape[0] // S5_T
    for j in range(f_ref.shape[0]):
        cs = slice(j * LANES, (j + 1) * LANES)
        f_ref[j] = x_ref[:, cs].astype(F32)
        for t in range(S5_T):
            o_ref[cs, t * n_cl:(t + 1) * n_cl] = (
                f_ref[j, pl.ds(t, n_cl, stride=S5_T), :].T.astype(o_ref.dtype))


def _s5_unpack_kernel(x_ref, o_ref, f_ref):
    n_cl = o_ref.shape[0] // S5_T
    for j in range(f_ref.shape[0]):
        cs = slice(j * LANES, (j + 1) * LANES)
        for t in range(S5_T):
            f_ref[j, pl.ds(t, n_cl, stride=S5_T), :] = (
                x_ref[cs, t * n_cl:(t + 1) * n_cl].astype(F32).T)
        o_ref[:, cs] = f_ref[j].astype(o_ref.dtype)


def _s5_pack(z, *, bsz, seq):
    return pl.pallas_call(
        _s5_pack_kernel,
        out_shape=jax.ShapeDtypeStruct((BRANCH_W, bsz * seq), BF16),
        grid=(bsz,),
        in_specs=[pl.BlockSpec((seq, BRANCH_W), lambda b: (b, Z_U // BRANCH_W))],
        out_specs=pl.BlockSpec((BRANCH_W, seq), lambda b: (0, b)),
        scratch_shapes=[pltpu.VMEM((BRANCH_W // LANES, seq, LANES), F32)],
        compiler_params=_cparams(("parallel",)),
        name="s5_pack",
    )(z)


def _s5_unpack(yt, *, bsz, seq):
    return pl.pallas_call(
        _s5_unpack_kernel,
        out_shape=jax.ShapeDtypeStruct((bsz * seq, BRANCH_W), BF16),
        grid=(bsz,),
        in_specs=[pl.BlockSpec((BRANCH_W, seq), lambda b: (0, b))],
        out_specs=pl.BlockSpec((seq, BRANCH_W), lambda b: (b, 0)),
        scratch_shapes=[pltpu.VMEM((BRANCH_W // LANES, seq, LANES), F32)],
        compiler_params=_cparams(("parallel",)),
        name="s5_unpack",
    )(yt)


def _s5_tables(lam_re, lam_im, log_dt, b_re, b_im, c_re, c_im, d_skip):
    t, g, n, p = S5_T, S5_G, S5_N, S5_P
    dt = jnp.exp(log_dt)[..., None]
    den = lam_re * lam_re + lam_im * lam_im
    kk = jnp.arange(t + 1, dtype=F32)[None, None, :, None]
    mag = jnp.exp(kk * (lam_re * dt)[:, :, None, :])
    ang = kk * (lam_im * dt)[:, :, None, :]
    p_re, p_im = mag * jnp.cos(ang), mag * jnp.sin(ang)
    ab_re, ab_im = p_re[:, :, 1], p_im[:, :, 1]
    z_re = ((ab_re - 1.0) * lam_re + ab_im * lam_im) / den
    z_im = (ab_im * lam_re - (ab_re - 1.0) * lam_im) / den
    bt_re, bt_im = jnp.swapaxes(b_re, 2, 3), jnp.swapaxes(b_im, 2, 3)
    bb_re = z_re[:, :, None, :] * bt_re - z_im[:, :, None, :] * bt_im
    bb_im = z_re[:, :, None, :] * bt_im + z_im[:, :, None, :] * bt_re
    pk_re, pk_im = p_re[:, :, :t, None, :], p_im[:, :, :t, None, :]
    abk_re = pk_re * bb_re[:, :, None] - pk_im * bb_im[:, :, None]
    abk_im = pk_re * bb_im[:, :, None] + pk_im * bb_re[:, :, None]
    taps = (jnp.einsum('rgpn,rgkqn->rgqkp', c_re, abk_re)
            - jnp.einsum('rgpn,rgkqn->rgqkp', c_im, abk_im))
    base_f = taps[0].reshape(g, p, t * p)
    base_b = taps[1][:, :, ::-1].reshape(g, p, t * p)
    width = t * p
    m_f = jnp.stack([jnp.pad(base_f, ((0, 0), (0, 0), (p * s, 0)))[..., :width]
                     for s in range(t)], axis=1)
    m_b = jnp.stack([jnp.pad(base_b, ((0, 0), (0, 0), (0, p * (t - 1 - s))))[..., p * (t - 1 - s):]
                     for s in range(t)], axis=1)
    skip = jnp.eye(width, dtype=F32)[None] * jnp.tile(d_skip.reshape(g, p), (1, t))[:, None, :]
    m = (m_f + m_b).reshape(g, width, width) + skip
    wf_re, wf_im = (a[0][:, ::-1].reshape(g, width, n) for a in (abk_re, abk_im))
    wb_re, wb_im = (a[1].reshape(g, width, n) for a in (abk_re, abk_im))
    w = jnp.concatenate([wf_re, wf_im, wb_re, wb_im, wf_im, wf_re, wb_im, wb_re], axis=-1)
    ct_re, ct_im = jnp.swapaxes(c_re, 2, 3), jnp.swapaxes(c_im, 2, 3)
    pn_re, pn_im = jnp.swapaxes(p_re, 2, 3)[..., None], jnp.swapaxes(p_im, 2, 3)[..., None]
    ca_re = ct_re[:, :, :, None, :] * pn_re - ct_im[:, :, :, None, :] * pn_im
    ca_im = ct_re[:, :, :, None, :] * pn_im + ct_im[:, :, :, None, :] * pn_re
    v = jnp.concatenate([ca_re[0][:, :, 1:].reshape(g, n, width),
                         -ca_im[0][:, :, 1:].reshape(g, n, width),
                         ca_re[1][:, :, :0:-1].reshape(g, n, width),
                         -ca_im[1][:, :, :0:-1].reshape(g, n, width)], axis=1)
    at_re, at_im = p_re[:, :, t], p_im[:, :, t]
    a1 = jnp.concatenate([at_re[0], at_re[0], at_re[1], at_re[1]], axis=-1)[:, None, :]
    a2 = jnp.concatenate([-at_im[0], at_im[0], -at_im[1], at_im[1]], axis=-1)[:, None, :]
    return m.astype(BF16), w.astype(BF16), v.astype(BF16), a1, a2


def _s5_rows(u, bsz, length):
    nc = length // S5_T
    u = u.reshape(bsz, nc, S5_T, S5_G, S5_P)
    return jnp.transpose(u, (3, 1, 0, 2, 4)).reshape(S5_G, nc * bsz, S5_T * S5_P)


def _s5_unrows(y, bsz, length):
    nc = length // S5_T
    y = y.reshape(S5_G, nc, bsz, S5_T, S5_P)
    return jnp.transpose(y, (2, 1, 3, 0, 4)).reshape(bsz * length, BRANCH_W)


def _s5_mix(u_ctx, ut_lat, tables, *, bsz):
    m, w, v, a1, a2 = tables
    rc = u_ctx.shape[1]
    t_lat = ut_lat.shape[1]
    rl = t_lat // S5_T
    wd = S5_T * S5_P

    def rows_spec(r):
        return pl.BlockSpec((None, r, wd), lambda g: (g, 0, 0))

    lat_spec = pl.BlockSpec((S5_P, t_lat), lambda g: (g, 0))
    sq = pl.BlockSpec((None, wd, wd), lambda g: (g, 0, 0))
    vec = pl.BlockSpec((None, 1, wd), lambda g: (g, 0, 0))
    return pl.pallas_call(
        functools.partial(_s5_kernel, bsz=bsz),
        out_shape=(jax.ShapeDtypeStruct((S5_G, rc, wd), BF16),
                   jax.ShapeDtypeStruct((BRANCH_W, t_lat), BF16)),
        grid=(S5_G,),
        in_specs=[rows_spec(rc), lat_spec, sq,
                  pl.BlockSpec((None, wd, 2 * wd), lambda g: (g, 0, 0)), sq, vec, vec],
        out_specs=(rows_spec(rc), lat_spec),
        scratch_shapes=[pltpu.VMEM((rc, 2 * wd), F32), pltpu.VMEM((rl, 2 * wd), F32),
                        pltpu.VMEM((rc, wd), F32), pltpu.VMEM((rl, wd), F32),
                        pltpu.VMEM((rl, wd), BF16)],
        compiler_params=_cparams(("parallel",)),
        name="s5_mix",
    )(u_ctx, ut_lat, m, w, v, a1, a2)


def _merge_kernel(x_ref, ya_ref, hf_ref, hb_ref, om_ref, ys_ref, gate_ref, g1_ref, mlg_ref,
                  gluw_ref, glub_ref, wbr_ref, wout_ref, o_ref):
    hsum = hf_ref[...] + hb_ref[...]
    og = jax.nn.sigmoid(om_ref[...].astype(F32))
    mlg = mlg_ref[...]
    yb_parts = []
    for h in range(ML_HEADS):
        sl = slice(h * ML_HD, (h + 1) * ML_HD)
        hh = hsum[:, sl]
        hn = hh * lax.rsqrt(jnp.mean(hh * hh, axis=-1, keepdims=True) + EPS) * mlg[:, sl]
        yb_parts.append(hn * og[:, sl])
    yb = jnp.concatenate(yb_parts, axis=-1)
    gl = _gelu(ys_ref[...].astype(F32))
    ys = gl * jax.nn.sigmoid(
        jnp.dot(gl.astype(BF16), gluw_ref[...], preferred_element_type=F32) + glub_ref[...])
    d = o_ref.shape[1]
    merged = None
    for r, y in enumerate((ya_ref[...], yb, ys)):
        proj = jnp.dot(y.astype(BF16), wbr_ref[r], preferred_element_type=F32)
        term = jax.nn.sigmoid(gate_ref[:, r * d:(r + 1) * d].astype(F32)) * proj
        merged = term if merged is None else merged + term
    y = jnp.dot(merged.astype(BF16), wout_ref[...], preferred_element_type=F32)
    o_ref[...] = x_ref[...] + g1_ref[...] * y


def _merge(x_all, ya, h_dirs, z, ys, g1, ml_norm_g, glu_w, glu_b, w_br, w_out, *, n_rows, bsz, seq, tm):
    d = x_all.shape[1]
    nlb = bsz * seq // tm
    bpb = seq // tm
    w = BRANCH_W

    def rows(width, col_blk=0):
        return pl.BlockSpec((tm, width), lambda i: (i, col_blk))

    def full(shape):
        return pl.BlockSpec(shape, lambda i: (0,) * len(shape))

    return pl.pallas_call(
        _merge_kernel,
        out_shape=jax.ShapeDtypeStruct((n_rows, d), F32),
        grid=(n_rows // tm,),
        in_specs=[rows(d), rows(w), rows(w), rows(w),
                  rows(w, Z_OM // w), rows(w),
                  rows(3 * d, Z_GATE // (3 * d)),
                  pl.BlockSpec((None, 1, d), lambda i: (_row_batch(i, nlb, bpb, bsz), 0, 0)),
                  full((1, w)), full((w, w)), full((1, w)), full((3, w, d)), full((d, d))],
        out_specs=rows(d),
        compiler_params=_cparams(("parallel",)),
        name="merge",
    )(x_all, ya, h_dirs[0], h_dirs[1], z, ys, z, g1, ml_norm_g, glu_w, glu_b, w_br, w_out)


NOT_TOP = 127.0


def _take_top(s, n_take, vals_ref, want_rank):
    rows = s.shape[0]
    ridx = lax.broadcasted_iota(jnp.int32, s.shape, 0).astype(F32)
    rank = jnp.full(s.shape, NOT_TOP, F32) if want_rank else None
    for i in range(n_take):
        mx = jnp.max(s, axis=0, keepdims=True)
        first = jnp.min(jnp.where(s == mx, ridx, float(rows)), axis=0, keepdims=True)
        hit = ridx == first
        s = jnp.where(hit, -jnp.inf, s)
        if want_rank:
            rank = jnp.where(hit, float(i), rank)
        vals_ref[i:i + 1, :] = mx
    return s, rank


SUBLANES = 8


def _sort_network_pairs(n):
    pairs = []
    p = 1
    while p < n:
        k = p
        while k >= 1:
            for j in range(k % p, n - k, 2 * k):
                for i in range(min(k, n - j - k)):
                    if (i + j) // (2 * p) == (i + j + k) // (2 * p):
                        pairs.append((i + j, i + j + k))
            k //= 2
        p *= 2
    return pairs


def _top_sorted(s):
    n = PEER_TOPK
    x = _slabs(s)
    assert len(x) <= n
    x = x + [jnp.full(x[0].shape, -jnp.inf, F32)] * (n - len(x))

    def exchange(i, j):
        x[i], x[j] = jnp.maximum(x[i], x[j]), jnp.minimum(x[i], x[j])

    for i, j in _sort_network_pairs(n):
        exchange(i, j)
    shift = SUBLANES // 2
    while shift >= 1:
        y = [pltpu.roll(v, shift, 0) for v in x]
        x = [jnp.maximum(x[i], y[n - 1 - i]) for i in range(n)]
        d = n // 2
        while d >= 1:
            for i in range(n):
                if not i & d:
                    exchange(i, i + d)
            d //= 2
        shift //= 2
    return x


def _slabs(s):
    return [s[SUBLANES * i:SUBLANES * (i + 1), :] for i in range(s.shape[0] // SUBLANES)]


def _tie_count(s, top):
    flag = jnp.zeros(top[0].shape, F32)
    for i in range(len(top) - 1):
        flag = flag + jnp.where(top[i] == top[i + 1], 1.0, 0.0)
    reach = functools.reduce(jnp.add, [jnp.where(v >= top[-1], 1.0, 0.0) for v in _slabs(s)])
    reach = jnp.sum(reach, axis=0, keepdims=True)
    return flag + jnp.where(reach != float(PEER_TOPK), 1.0, 0.0)


def _candidates(v1_ref, v2_ref):
    k = PEER_TOPK
    v2_head = v2_ref[0:8, :]
    jrow = lax.broadcasted_iota(jnp.int32, v2_head.shape, 0)
    pieces = [v1_ref[0:1, :] + v2_ref[...]]
    for i in range(1, 8):
        pieces.append(jnp.where(jrow < k // (i + 1), v1_ref[i:i + 1, :] + v2_head, -jnp.inf))
    pieces.append(v1_ref[8:16, :] + v2_ref[0:1, :])
    return jnp.concatenate(pieces, axis=0)


def _write_counts(picked, cnt_ref):
    cnt_ref[0:1, :] = jnp.sum(picked[0:16], axis=0, keepdims=True)
    for i in range(1, 8):
        cnt_ref[i:i + 1, :] = jnp.sum(picked[8 + 8 * i:16 + 8 * i], axis=0, keepdims=True)
    cnt_ref[8:16, :] = picked[72:80]


def _candidate_counts(v1_ref, v2_ref, top_ref, cnt_ref):
    cand = _candidates(v1_ref, v2_ref)
    cand_left, _ = _take_top(cand, PEER_TOPK, top_ref, False)
    _write_counts(jnp.where(cand_left != cand, 1.0, 0.0), cnt_ref)
    top = top_ref[...]
    return jnp.sum(jnp.exp(top - top[0:1, :]), axis=0, keepdims=True)


def _peer_topk_kernel(q_ref, k1_ref, k2_ref, e1_ref, cnt1_ref, e2_ref, rank2_ref,
                      v1_ref, v2_ref, top_ref, cnt_ref):
    k = PEER_TOPK
    q = q_ref[...].astype(BF16)
    s1 = _nt_dot(k1_ref[...], q[:, :LANES])
    s2 = _nt_dot(k2_ref[...], q[:, LANES:])
    top1 = _top_sorted(s1)
    top2 = _top_sorted(s2)
    for i in range(k):
        v1_ref[i:i + 1, :] = top1[i][0:1, :]
        v2_ref[i:i + 1, :] = top2[i][0:1, :]
    cand = _candidates(v1_ref, v2_ref)
    topc = _top_sorted(cand)
    any_tie = jnp.max(_tie_count(s1, top1) + _tie_count(s2, top2) + _tie_count(cand, topc)) > 0.0

    @pl.when(any_tie)
    def _():
        _, rank1 = _take_top(s1, k, v1_ref, True)
        _, rank2 = _take_top(s2, k, v2_ref, True)
        zsum = _candidate_counts(v1_ref, v2_ref, top_ref, cnt_ref)
        cnt1 = jnp.zeros(s1.shape, F32)
        for i in range(k):
            cnt1 = jnp.where(rank1 == float(i), cnt_ref[i:i + 1, :], cnt1)
        e1_ref[...] = jnp.where(rank1 < k, jnp.exp(s1 - v1_ref[0:1, :]), 0.0) / zsum
        cnt1_ref[...] = cnt1
        e2_ref[...] = jnp.where(rank2 < k, jnp.exp(s2 - v2_ref[0:1, :]), 0.0).astype(BF16)
        rank2_ref[...] = rank2.astype(BF16)

    @pl.when(jnp.logical_not(any_tie))
    def _():
        _write_counts(jnp.where(cand >= topc[k - 1][0:1, :], 1.0, 0.0), cnt_ref)
        zsum = functools.reduce(jnp.add, [jnp.exp(t[0:1, :] - topc[0][0:1, :]) for t in topc])
        cnt_rows = [jnp.broadcast_to(cnt_ref[i:i + 1, :], top1[0].shape) for i in range(k)]
        cnt1, rank2 = [], []
        for v in _slabs(s1):
            c = jnp.zeros(v.shape, F32)
            for i in range(k):
                c = jnp.where(v == top1[i], cnt_rows[i], c)
            cnt1.append(c)
        for v in _slabs(s2):
            r = jnp.full(v.shape, NOT_TOP, F32)
            for i in range(k):
                r = jnp.where(v == top2[i], float(i), r)
            rank2.append(r)
        e1_ref[...] = jnp.where(s1 >= v1_ref[k - 1:k, :], jnp.exp(s1 - v1_ref[0:1, :]), 0.0) / zsum
        cnt1_ref[...] = jnp.concatenate(cnt1, axis=0)
        e2_ref[...] = jnp.where(s2 >= v2_ref[k - 1:k, :], jnp.exp(s2 - v2_ref[0:1, :]), 0.0
                                ).astype(BF16)
        rank2_ref[...] = jnp.concatenate(rank2, axis=0).astype(BF16)


def _peer_topk(q, sub_k, *, n_rows, tm):
    nk = PEER_NKEYS
    row_tab = jax.ShapeDtypeStruct((n_rows // tm, PEER_HEADS, nk, tm), F32)
    tile_tab = jax.ShapeDtypeStruct((n_rows // tm, PEER_HEADS, nk, tm), BF16)
    tab_spec = pl.BlockSpec((None, None, nk, tm), lambda i, h: (i, h, 0, 0))
    return pl.pallas_call(
        _peer_topk_kernel,
        out_shape=(row_tab, row_tab, tile_tab, tile_tab),
        grid=(n_rows // tm, PEER_HEADS),
        in_specs=[pl.BlockSpec((tm, 2 * LANES), lambda i, h: (i, h)),
                  pl.BlockSpec((None, nk, LANES), lambda i, h: (0, 0, 0)),
                  pl.BlockSpec((None, nk, LANES), lambda i, h: (1, 0, 0))],
        out_specs=(tab_spec,) * 4,
        scratch_shapes=[pltpu.VMEM((PEER_TOPK, tm), F32)] * 4,
        compiler_params=_cparams(("parallel", "parallel")),
        name="peer_topk",
    )(q, sub_k, sub_k)


GATE_KEYS = 2
KEY_GROUP = 8


def _peer_dense_kernel(ht_ref, u_ref, vt_ref, e1_ref, cnt1_ref, e2_ref, rank2_ref,
                       x_ref, g2_ref, fg_ref, o_ref, acc_ref, act_ref, p_ref, e2s_ref, rank2s_ref, *,
                       a_per_blk, final_norm):
    j = pl.program_id(1)
    n_blk = pl.num_programs(1) - 1
    cur = j % 2
    tm = ht_ref.shape[1]
    nk = PEER_NKEYS

    @pl.when(j == 0)
    def _():
        acc_ref[...] = jnp.zeros_like(acc_ref)
        p_ref[1] = jnp.zeros(p_ref.shape[1:], p_ref.dtype)
        e2s_ref[...] = e2_ref[...]
        rank2s_ref[...] = rank2_ref[...]

    @pl.when(j < n_blk)
    def _():
        act_ref[...] = _gelu(jnp.dot(u_ref[...], ht_ref[...], preferred_element_type=F32)
                             ).astype(BF16)
        acc_ref[...] += jnp.dot(vt_ref[...], p_ref[1 - cur], preferred_element_type=F32)
        a0 = pl.multiple_of(j * a_per_blk, KEY_GROUP)

        def row_bf16(ref, h, i, ls):
            grp = (i // KEY_GROUP) * KEY_GROUP
            w = ref[h, pl.ds(a0 + grp, KEY_GROUP), ls][i % KEY_GROUP:i % KEY_GROUP + 1]
            return jnp.broadcast_to(w, (nk, LANES)).astype(BF16)

        for lt in range(tm // LANES):
            ls = slice(lt * LANES, (lt + 1) * LANES)
            for i0 in range(0, a_per_blk, GATE_KEYS):
                gates = [None] * GATE_KEYS
                for h in range(PEER_HEADS):
                    for ii in range(GATE_KEYS):
                        lead = row_bf16(cnt1_ref, h, i0 + ii, ls) - rank2s_ref[h, :, ls]
                        term = jnp.minimum(jnp.maximum(lead, 0.0),
                                           row_bf16(e1_ref, h, i0 + ii, ls)) * e2s_ref[h, :, ls]
                        gates[ii] = term if gates[ii] is None else gates[ii] + term
                for ii in range(GATE_KEYS):
                    r0 = (i0 + ii) * nk
                    p_ref[cur, r0:r0 + nk, ls] = gates[ii] * act_ref[r0:r0 + nk, ls]

    @pl.when(j == n_blk)
    def _():
        acc = acc_ref[...] + jnp.dot(vt_ref[...], p_ref[1 - cur], preferred_element_type=F32)
        y = x_ref[...] + g2_ref[...] * acc.T
        if final_norm:
            y = y * lax.rsqrt(jnp.mean(y * y, axis=-1, keepdims=True) + EPS) * fg_ref[...]
        o_ref[...] = y


def _peer_dense(h2, u_tab, vt_tab, tabs, x_all, g2, final_g, *, n_rows, bsz, seq, tm, te,
                final_norm):
    d = x_all.shape[1]
    n_exp = u_tab.shape[0]
    nlb = bsz * seq // tm
    bpb = seq // tm
    nk = PEER_NKEYS
    n_blk = n_exp // te
    assert (te // nk) % KEY_GROUP == 0
    tab_spec = pl.BlockSpec((None, PEER_HEADS, nk, tm), lambda i, j: (i, 0, 0, 0))
    return pl.pallas_call(
        functools.partial(_peer_dense_kernel, a_per_blk=te // nk, final_norm=final_norm),
        out_shape=jax.ShapeDtypeStruct((n_rows, d), F32),
        grid=(n_rows // tm, n_blk + 1),
        in_specs=[pl.BlockSpec((d, tm), lambda i, j: (0, i)),
                  pl.BlockSpec((te, d), lambda i, j: (jnp.minimum(j, n_blk - 1), 0)),
                  pl.BlockSpec((None, d, te), lambda i, j: (jnp.maximum(j - 1, 0), 0, 0)),
                  tab_spec, tab_spec, tab_spec, tab_spec,
                  pl.BlockSpec((tm, d), lambda i, j: (i, 0)),
                  pl.BlockSpec((None, 1, d), lambda i, j: (_row_batch(i, nlb, bpb, bsz), 0, 0)),
                  pl.BlockSpec((1, d), lambda i, j: (0, 0))],
        out_specs=pl.BlockSpec((tm, d), lambda i, j: (i, 0)),
        scratch_shapes=[pltpu.VMEM((d, tm), F32), pltpu.VMEM((te, tm), BF16),
                        pltpu.VMEM((2, te, tm), BF16),
                        pltpu.VMEM((PEER_HEADS, nk, tm), BF16), pltpu.VMEM((PEER_HEADS, nk, tm), BF16)],
        compiler_params=_cparams(("parallel", "arbitrary")),
        name="peer_dense",
    )(h2, u_tab, vt_tab, *tabs, x_all, g2, final_g)


def _transpose_cast_kernel(x_ref, o_ref):
    o_ref[...] = x_ref[...].T.astype(o_ref.dtype)


def _transpose_cast(w, *, tr):
    rows, cols = w.shape
    return pl.pallas_call(
        _transpose_cast_kernel,
        out_shape=jax.ShapeDtypeStruct((rows // tr, cols, tr), BF16),
        grid=(rows // tr,),
        in_specs=[pl.BlockSpec((tr, cols), lambda i: (i, 0))],
        out_specs=pl.BlockSpec((None, cols, tr), lambda i: (i, 0, 0)),
        compiler_params=_cparams(("parallel",)),
        name="transpose_cast",
    )(w)


def _rope_tables(seq, tm):
    rows = seq // GRID_W
    n_freq = DA_HD // 4
    inv = ROPE_BASE ** (-jnp.arange(n_freq, dtype=F32) / n_freq)
    r = jnp.repeat(jnp.arange(rows, dtype=F32), GRID_W)
    col = jnp.tile(jnp.arange(GRID_W, dtype=F32), rows)
    ang = jnp.concatenate([r[:, None] * inv, col[:, None] * inv], axis=-1)
    cos, sin = jnp.cos(ang), jnp.sin(ang)
    cos_t = jnp.tile(cos, (1, 4))
    sin_t = jnp.tile(jnp.concatenate([-sin, sin], axis=-1), (1, 2))
    ident = jnp.ones((tm, LANES), F32)
    return (jnp.concatenate([cos_t, ident], axis=0),
            jnp.concatenate([sin_t, jnp.zeros((tm, LANES), F32)], axis=0))


def kernel(x, c, ctx, c_ctx, w_mod, b_mod, norm1_g, norm2_g, w_in, da_lam_q1, da_lam_k1, da_lam_q2, da_lam_k2, da_sub_g, ml_conv_w, ml_conv_b, ml_wq, ml_wk, ml_gate_b, ml_norm_g, s5_lam_re, s5_lam_im, s5_log_dt, s5_b_re, s5_b_im, s5_c_re, s5_c_im, s5_d, s5_glu_w, s5_glu_b, w_br, w_out, peer_wq, peer_sub_k, peer_u, peer_v, final_g):
    bsz, seq, d = x.shape
    ctx_len = ctx.shape[1]
    depth = w_in.shape[0]
    t_lat = bsz * seq
    t_ctx = bsz * ctx_len
    t_all = t_lat + t_ctx
    tm = math.gcd(512, math.gcd(seq, t_ctx))
    tm_merge = min(tm, 256)
    tq = min(256, ctx_len)
    nh = ML_HEADS

    x_all = jnp.concatenate([x.reshape(t_lat, d), ctx.reshape(t_ctx, d)], axis=0)
    pad = (-(bsz + 1)) % 8
    c_all = jnp.concatenate([c, c_ctx[None], jnp.zeros((pad, d), F32)], axis=0)
    cos_tab, sin_tab = _rope_tables(seq, tm)

    for l in range(depth):
        need_ctx = l < depth - 1
        n_rows = t_all if need_ctx else t_lat
        lam_init = 0.8 - 0.6 * math.exp(-0.3 * l)

        mods = _modulation(c_all, w_mod[l].astype(BF16), b_mod[l][None])
        sh1, sc1, g1, sh2, sc2, g2 = [mods[:bsz + 1, i * d:(i + 1) * d][:, None, :]
                                      for i in range(N_MOD)]

        wi = w_in[l]
        i_q, i_k, i_v, i_xm, i_vm, i_om, i_g, i_u, i_gate = (
            0, 512, 1024, 1536, 2048, 2560, 3072, 3088, 3600)
        w_main = jnp.concatenate([wi[:, i_gate:], wi[:, :i_g], wi[:, i_u:i_gate]], axis=1).astype(BF16)
        w_gate = jnp.pad(wi[:, i_g:i_u], ((0, 0), (0, LANES - (i_u - i_g)))).astype(BF16)
        n1 = norm1_g[l][None]
        z = _modnorm_matmul(x_all, t_all, n1, sc1, sh1, w_main, bsz=bsz, seq=seq, tm=tm,
                            tn=w_main.shape[1] // 2, out_dtype=BF16)
        zg = _modnorm_matmul(x_all, t_all, n1, sc1, sh1, w_gate, bsz=bsz, seq=seq, tm=tm, tn=LANES)

        qr, kr, vb = _rope_qkv(z, cos_tab, sin_tab, bsz=bsz, seq=seq, tm=tm)
        lam = (jnp.exp(jnp.sum(da_lam_q1[l] * da_lam_k1[l]))
               - jnp.exp(jnp.sum(da_lam_q2[l] * da_lam_k2[l])) + lam_init).reshape(1, 1)
        sub_g = da_sub_g[l][None]
        ya = _diff_attention(qr, kr, vb, lam, sub_g, q_row0=0, n_q=seq,
                             kv_segs=[(0, seq), (t_lat, ctx_len)], bsz=bsz, tq=tq,
                             out_scale=1.0 - lam_init, out_rows=n_rows)
        if need_ctx:
            ya = _diff_attention(qr, kr, vb, lam, sub_g, q_row0=t_lat, n_q=ctx_len,
                                 kv_segs=[(t_lat, ctx_len)], bsz=bsz, tq=tq,
                                 out_scale=1.0 - lam_init, out_rows=n_rows, into=ya)

        wq_b, wk_b = ml_wq[l].astype(BF16), ml_wk[l].astype(BF16)
        cb = ml_conv_b[l][None]
        qkt = _ml_prep(z, ml_conv_w[l], cb, wq_b, wk_b, row0=0, seq_len=seq, bsz=bsz)
        q_ml, k_ml, kt_ml = _ml_prep(z, ml_conv_w[l], cb, wq_b, wk_b, row0=t_lat, seq_len=ctx_len,
                                     bsz=bsz, into=qkt)
        gates = (zg[:, :4 * nh] + ml_gate_b[l]).reshape(t_all, 2, 2 * nh)
        gates_c = jnp.transpose(gates, (1, 0, 2))
        gates_r = jnp.transpose(gates, (1, 2, 0))
        h_dirs = _mlstm_scan(q_ml, k_ml, kt_ml, z, gates_c, gates_r, bsz=bsz, seq=seq,
                             ctx_len=ctx_len)

        tables = _s5_tables(s5_lam_re[l], s5_lam_im[l], s5_log_dt[l], s5_b_re[l], s5_b_im[l],
                            s5_c_re[l], s5_c_im[l], s5_d[l])
        u_ctx = _s5_rows(z[t_lat:, Z_U:Z_U + BRANCH_W], bsz, ctx_len)
        y_c, yt_l = _s5_mix(u_ctx, _s5_pack(z, bsz=bsz, seq=seq), tables, bsz=bsz)
        ys = _s5_unpack(yt_l, bsz=bsz, seq=seq)
        if need_ctx:
            ys = jnp.concatenate([ys, _s5_unrows(y_c, bsz, ctx_len)], axis=0)

        x_all = _merge(x_all, ya, h_dirs, z, ys, g1, ml_norm_g[l][None],
                       s5_glu_w[l].astype(BF16), s5_glu_b[l][None], w_br[l].astype(BF16),
                       w_out[l].astype(BF16), n_rows=n_rows, bsz=bsz, seq=seq, tm=tm_merge)

        n2 = norm2_g[l][None]
        wq_p = peer_wq[l].astype(BF16)
        pq, h2t = _modnorm_matmul(x_all, n_rows, n2, sc2, sh2, wq_p, bsz=bsz, seq=seq, tm=tm,
                                  tn=wq_p.shape[1], emit_h=True)
        tabs = _peer_topk(pq, peer_sub_k[l].astype(BF16), n_rows=n_rows, tm=tm)
        te = 2048
        x_all = _peer_dense(h2t, peer_u[l].astype(BF16), _transpose_cast(peer_v[l], tr=te),
                            tabs, x_all, g2, final_g[None], n_rows=n_rows, bsz=bsz, seq=seq, tm=tm,
                            te=te, final_norm=not need_ctx)

    return x_all.reshape(bsz, seq, d)
```

```python
import functools
import math

import jax
import jax.numpy as jnp
from jax import lax
from jax.experimental import pallas as pl
from jax.experimental.pallas import tpu as pltpu

F32 = jnp.float32
BF16 = jnp.bfloat16

EPS = 1e-6
LOG2E = 1.4426950408889634
N_MOD = 6
BRANCH_W = 512
GRID_W = 64
ROPE_BASE = 10000.0
DA_HEADS = 4
DA_HD = 64
DA_VD = 2 * DA_HD
ML_HEADS = 4
ML_HD = BRANCH_W // ML_HEADS
ML_CHUNK = 128
S5_P = 16
S5_G = BRANCH_W // S5_P
S5_N = 64
S5_T = 16
PEER_HEADS = 8
PEER_NKEYS = 128
PEER_TOPK = 16
LANES = 128
VMEM_LIMIT = 56 * 1024 * 1024

Z_GATE = 0
Z_Q, Z_K, Z_V, Z_XM, Z_VM, Z_OM, Z_U = (3072 + i * BRANCH_W for i in range(7))


def _cparams(sem):
    return pltpu.CompilerParams(dimension_semantics=sem, vmem_limit_bytes=VMEM_LIMIT)


def _nt_dot(a, b):
    return lax.dot_general(a, b, (((1,), (1,)), ((), ())), preferred_element_type=F32)


def _skip_refs(kernel_fn, start, count, *refs):
    return kernel_fn(*refs[:start], *refs[start + count:])


def _gelu(x):
    return 0.5 * x * (1.0 + lax.erf(x * (2.0 ** -0.5)))


def _log_sigmoid(x):
    return jnp.minimum(x, 0.0) - jnp.log1p(jnp.exp(-jnp.abs(x)))


def _mod_kernel(c_ref, w_ref, b_ref, o_ref):
    c = c_ref[...]
    a = c * jax.nn.sigmoid(c)
    o_ref[...] = jnp.dot(a.astype(BF16), w_ref[...], preferred_element_type=F32) + b_ref[...]


def _modulation(c_all, w, b):
    rows, d = c_all.shape
    n = w.shape[1]
    tn = 1536
    return pl.pallas_call(
        _mod_kernel,
        out_shape=jax.ShapeDtypeStruct((rows, n), F32),
        grid=(n // tn,),
        in_specs=[pl.BlockSpec((rows, d), lambda j: (0, 0)),
                  pl.BlockSpec((d, tn), lambda j: (0, j)),
                  pl.BlockSpec((1, tn), lambda j: (0, j))],
        out_specs=pl.BlockSpec((rows, tn), lambda j: (0, j)),
        compiler_params=_cparams(("parallel",)),
        name="modulation",
    )(c_all, w, b)


def _modnorm_matmul_kernel(x_ref, g_ref, sc_ref, sh_ref, w_ref, o_ref, *h_out):
    x = x_ref[...]
    y = x * lax.rsqrt(jnp.mean(x * x, axis=-1, keepdims=True) + EPS) * g_ref[...]
    hf = y * (1.0 + sc_ref[...]) + sh_ref[...]
    h = hf.astype(BF16)
    o_ref[...] = jnp.dot(h, w_ref[...], preferred_element_type=F32).astype(o_ref.dtype)
    if h_out:
        h_out[0][...] = hf.T.astype(BF16)


def _row_batch(i, n_lat_blocks, blocks_per_batch, bsz):
    return jnp.where(i < n_lat_blocks, i // blocks_per_batch, bsz)


def _modnorm_matmul(x_all, n_rows, g, sc, sh, w, *, bsz, seq, tm, tn, out_dtype=F32, emit_h=False):
    d = x_all.shape[1]
    n = w.shape[1]
    nlb = bsz * seq // tm
    bpb = seq // tm
    assert not emit_h or tn == n
    mod_spec = pl.BlockSpec((None, 1, d), lambda j, i: (_row_batch(i, nlb, bpb, bsz), 0, 0))
    out_shape = [jax.ShapeDtypeStruct((n_rows, n), out_dtype)]
    out_specs = [pl.BlockSpec((tm, tn), lambda j, i: (i, j))]
    if emit_h:
        out_shape.append(jax.ShapeDtypeStruct((d, n_rows), BF16))
        out_specs.append(pl.BlockSpec((d, tm), lambda j, i: (0, i)))
    out = pl.pallas_call(
        _modnorm_matmul_kernel,
        out_shape=tuple(out_shape),
        grid=(n // tn, n_rows // tm),
        in_specs=[pl.BlockSpec((tm, d), lambda j, i: (i, 0)),
                  pl.BlockSpec((1, d), lambda j, i: (0, 0)),
                  mod_spec, mod_spec,
                  pl.BlockSpec((d, tn), lambda j, i: (0, j))],
        out_specs=tuple(out_specs),
        compiler_params=_cparams(("parallel", "parallel")),
        name="modnorm_matmul",
    )(x_all, g, sc, sh, w)
    return out if emit_h else out[0]


def _rope_kernel(q_ref, k_ref, v_ref, cos_ref, sin_ref, qo_ref, ko_ref, vo_ref):
    cos = cos_ref[...]
    sin = sin_ref[...]
    lane = lax.broadcasted_iota(jnp.int32, cos.shape, 1)
    first_half = (lane % DA_HD) < (DA_HD // 2)

    def rope(x):
        partner = jnp.where(first_half, pltpu.roll(x, LANES - DA_HD // 2, 1),
                            pltpu.roll(x, DA_HD // 2, 1))
        return x * cos + partner * sin

    for h in range(DA_HEADS):
        sl = slice(h * LANES, (h + 1) * LANES)
        qo_ref[:, sl] = (rope(q_ref[:, sl].astype(F32)) * (DA_HD ** -0.5 * LOG2E)).astype(BF16)
        ko_ref[:, sl] = rope(k_ref[:, sl].astype(F32)).astype(BF16)
    vo_ref[...] = v_ref[...].astype(BF16)


def _rope_qkv(z, cos_tab, sin_tab, *, bsz, seq, tm):
    rows = z.shape[0]
    nlb = bsz * seq // tm
    bpb = seq // tm
    w = BRANCH_W
    tab_spec = pl.BlockSpec((tm, LANES), lambda i: (jnp.where(i < nlb, i % bpb, bpb), 0))
    out = jax.ShapeDtypeStruct((rows, w), BF16)
    return pl.pallas_call(
        _rope_kernel,
        out_shape=(out, out, out),
        grid=(rows // tm,),
        in_specs=[pl.BlockSpec((tm, w), lambda i: (i, Z_Q // w)),
                  pl.BlockSpec((tm, w), lambda i: (i, Z_K // w)),
                  pl.BlockSpec((tm, w), lambda i: (i, Z_V // w)),
                  tab_spec, tab_spec],
        out_specs=(pl.BlockSpec((tm, w), lambda i: (i, 0)),) * 3,
        compiler_params=_cparams(("parallel",)),
        name="rope_qkv",
    )(z, z, z, cos_tab, sin_tab)


def _attn_kernel(*refs, n_seg, n_sub, out_scale):
    lam_ref, q_ref = refs[0], refs[1]
    k_refs = refs[2:2 + n_seg]
    v_refs = refs[2 + n_seg:2 + 2 * n_seg]
    g_ref, o_ref = refs[2 + 2 * n_seg], refs[3 + 2 * n_seg]
    tq = q_ref.shape[0] // n_sub
    lane = lax.broadcasted_iota(jnp.int32, (tq, LANES), 1)
    scores = []
    for sub in range(n_sub):
        q = q_ref[sub * tq:(sub + 1) * tq, :].astype(F32)
        qq = jnp.concatenate([jnp.where(lane < DA_HD, q, 0.0), jnp.where(lane >= DA_HD, q, 0.0)],
                             axis=0).astype(BF16)
        scores.append([_nt_dot(qq, k_ref[...]) for k_ref in k_refs])
    for sub in range(n_sub):
        rows = slice(sub * tq, (sub + 1) * tq)
        s = scores[sub]
        m = functools.reduce(jnp.maximum, [jnp.max(t, axis=-1, keepdims=True) for t in s])
        p = [jnp.exp2(t - m) for t in s]
        denom = functools.reduce(jnp.add, [jnp.sum(t, axis=-1, keepdims=True) for t in p])
        inv = 1.0 / denom
        c0 = inv[:tq]
        c1 = inv[tq:] * lam_ref[...]
        o = None
        for t, v_ref in zip(p, v_refs):
            w = (t[:tq] * c0 - t[tq:] * c1).astype(BF16)
            part = jnp.dot(w, v_ref[...], preferred_element_type=F32)
            o = part if o is None else o + part
        y = o * lax.rsqrt(jnp.mean(o * o, axis=-1, keepdims=True) + EPS) * g_ref[...]
        o_ref[rows, :] = (y * out_scale).astype(o_ref.dtype)


def _diff_attention(qr, kr, vb, lam, sub_g, *, q_row0, n_q, kv_segs, bsz, tq, out_scale, out_rows,
                    into=None):
    n_sub = 2 if n_q % (2 * tq) == 0 else 1
    tq = tq * n_sub
    nqb = n_q // tq
    n_seg = len(kv_segs)
    q_blk0 = q_row0 // tq

    def kv_spec(row0, length):
        return pl.BlockSpec((length, LANES), lambda b, h, i: (row0 // length + b, h))

    in_specs = [pl.BlockSpec((1, 1), lambda b, h, i: (0, 0)),
                pl.BlockSpec((tq, LANES), lambda b, h, i: (q_blk0 + b * nqb + i, h))]
    in_specs += [kv_spec(r0, ln) for r0, ln in kv_segs] * 2
    in_specs += [pl.BlockSpec((1, LANES), lambda b, h, i: (0, h))]
    args = [lam, qr] + [kr] * n_seg + [vb] * n_seg + [sub_g]
    kern = functools.partial(_attn_kernel, n_seg=n_seg, n_sub=n_sub, out_scale=out_scale)
    aliases = {}
    if into is not None:
        aliases = {len(args): 0}
        kern = functools.partial(_skip_refs, kern, len(args), 1)
        in_specs += [pl.BlockSpec(memory_space=pl.ANY)]
        args += [into]
    return pl.pallas_call(
        kern,
        out_shape=jax.ShapeDtypeStruct((out_rows, BRANCH_W), BF16),
        grid=(bsz, DA_HEADS, nqb),
        in_specs=in_specs,
        out_specs=pl.BlockSpec((tq, LANES), lambda b, h, i: (q_blk0 + b * nqb + i, h)),
        input_output_aliases=aliases,
        compiler_params=_cparams(("parallel", "parallel", "parallel")),
        name="diff_attention",
    )(*args)


def _ml_prep_kernel(x_ref, cw_ref, cb_ref, wq_ref, wk_ref, q_ref, k_ref, kt_ref):
    x = x_ref[...].astype(F32)
    n = x.shape[0]
    row = lax.broadcasted_iota(jnp.int32, x.shape, 0)
    x_prev = jnp.where(row == 0, 0.0, pltpu.roll(x, 1, 0))
    x_next = jnp.where(row == n - 1, 0.0, pltpu.roll(x, n - 1, 0))
    cw = cw_ref[...]
    y = x_prev * cw[0:1] + x * cw[1:2] + x_next * cw[2:3] + cb_ref[...]
    xc = (y * jax.nn.sigmoid(y)).astype(BF16)
    q_ref[...] = (jnp.dot(xc, wq_ref[...], preferred_element_type=F32) * (ML_HD ** -0.5)
                  ).astype(q_ref.dtype)
    k = jnp.dot(xc, wk_ref[...], preferred_element_type=F32)
    k_ref[...] = k.astype(k_ref.dtype)
    kt_ref[...] = k.T.astype(kt_ref.dtype)


def _ml_prep(z, conv_w, conv_b, wq, wk, *, row0, seq_len, bsz, into=None):
    rows = z.shape[0]
    blk0 = row0 // seq_len
    out = jax.ShapeDtypeStruct((rows, BRANCH_W), BF16)
    out_t = jax.ShapeDtypeStruct((BRANCH_W, rows), BF16)
    head_w = pl.BlockSpec((None, ML_HD, ML_HD), lambda b, h: (h, 0, 0))
    row_out = pl.BlockSpec((seq_len, LANES), lambda b, h: (blk0 + b, h))
    in_specs = [pl.BlockSpec((seq_len, LANES), lambda b, h: (blk0 + b, Z_XM // LANES + h)),
                pl.BlockSpec((3, LANES), lambda b, h: (0, h)),
                pl.BlockSpec((1, LANES), lambda b, h: (0, h)),
                head_w, head_w]
    args = [z, conv_w, conv_b, wq, wk]
    aliases = {}
    kern = _ml_prep_kernel
    if into is not None:
        n_in = len(args)
        in_specs += [pl.BlockSpec(memory_space=pl.ANY)] * len(into)
        args += list(into)
        aliases = {n_in + i: i for i in range(len(into))}
        kern = functools.partial(_skip_refs, _ml_prep_kernel, n_in, len(into))
    return pl.pallas_call(
        kern,
        out_shape=(out, out, out_t),
        grid=(bsz, ML_HEADS),
        in_specs=in_specs,
        out_specs=(row_out, row_out, pl.BlockSpec((LANES, seq_len), lambda b, h: (h, blk0 + b))),
        input_output_aliases=aliases,
        compiler_params=_cparams(("parallel", "parallel")),
        name="mlstm_prep",
    )(*args)


def _mlstm_kernel(qf_ref, kf_ref, ktf_ref, vf_ref, gcf_ref, grf_ref,
                  qb_ref, kb_ref, ktb_ref, vb_ref, gcb_ref, grb_ref,
                  hf_ref, hb_ref, c_ref, n_ref, m_ref):
    @pl.when(pl.program_id(1) == 0)
    def _():
        c_ref[...] = jnp.zeros_like(c_ref)
        n_ref[...] = jnp.zeros_like(n_ref)
        m_ref[...] = jnp.zeros_like(m_ref)

    t_idx = lax.broadcasted_iota(jnp.int32, (ML_CHUNK, ML_CHUNK), 0)
    s_idx = lax.broadcasted_iota(jnp.int32, (ML_CHUNK, ML_CHUNK), 1)
    nh = ML_HEADS
    dirs = ((qf_ref, kf_ref, ktf_ref, vf_ref, gcf_ref, grf_ref, hf_ref),
            (qb_ref, kb_ref, ktb_ref, vb_ref, gcb_ref, grb_ref, hb_ref))
    ones = jnp.ones((ML_CHUNK, LANES), BF16)
    units = []
    for d, (q_ref, k_ref, kt_ref, v_ref, gc_ref, gr_ref, h_ref) in enumerate(dirs):
        mask = s_idx <= t_idx if d == 0 else s_idx >= t_idx
        mask_f = mask.astype(F32)
        gc = gc_ref[...]
        gr = gr_ref[...]
        lf_c = _log_sigmoid(gc)
        lf_r = _log_sigmoid(gr)
        b_c = jnp.dot(mask_f, lf_c, preferred_element_type=F32, precision=lax.Precision.HIGHEST)
        b_r = lax.dot_general(lf_r, mask_f, (((1,), (1,)), ((), ())), preferred_element_type=F32,
                              precision=lax.Precision.HIGHEST)
        b_end_all = jnp.sum(lf_c, axis=0, keepdims=True)

        for h in range(nh):
            sl = slice(h * ML_HD, (h + 1) * ML_HD)
            q = q_ref[:, sl]
            k = k_ref[:, sl]
            kt = kt_ref[sl, :]
            v = v_ref[:, sl].astype(F32)
            li_c = gc[:, h:h + 1]
            li_r = gr[h:h + 1, :]
            bc = b_c[:, nh + h:nh + h + 1]
            br = b_r[nh + h:nh + h + 1, :]
            b_end = b_end_all[:, nh + h:nh + h + 1]
            m_prev = m_ref[d, h]
            c_prev = c_ref[d, h]
            n_prev = n_ref[d, h]
            g_c = b_end - bc + li_c
            g_r = b_end - br + li_r
            m_new = jnp.maximum(b_end + m_prev, jnp.max(g_r, axis=-1, keepdims=True))
            wg = jnp.broadcast_to(jnp.exp(g_c - m_new), (ML_CHUNK, LANES))
            decay = jnp.exp(b_end + m_prev - m_new)
            vw = (v * wg).astype(BF16)
            units.append(dict(
                d=d, h=h, sl=sl, h_ref=h_ref, v=v, mask=mask, m_prev=m_prev,
                bc=bc, br=br, li_r=li_r,
                qk=_nt_dot(q, k),
                qc=jnp.dot(q, c_prev.astype(BF16), preferred_element_type=F32),
                qn=jnp.dot(q, n_prev.astype(BF16), preferred_element_type=F32),
                c_new=decay * c_prev + jnp.dot(kt, vw, preferred_element_type=F32),
                n_new=decay * n_prev + jnp.dot(kt, wg.astype(BF16), preferred_element_type=F32),
                m_new=m_new))

    for u in units:
        dmat = jnp.where(u["mask"], u["bc"] - u["br"] + u["li_r"], -jnp.inf)
        inter = u["bc"] + u["m_prev"]
        m_t = jnp.maximum(inter, jnp.max(dmat, axis=-1, keepdims=True))
        s = (u["qk"] * jnp.exp(dmat - m_t)).astype(BF16)
        w_prev = jnp.exp(inter - m_t)
        num = jnp.dot(s, u["v"].astype(BF16), preferred_element_type=F32) + w_prev * u["qc"]
        den = jnp.dot(s, ones, preferred_element_type=F32) + w_prev * u["qn"]
        u["h_ref"][:, u["sl"]] = num / jnp.maximum(jnp.abs(den), jnp.exp(-m_t))
        c_ref[u["d"], u["h"]] = u["c_new"]
        n_ref[u["d"], u["h"]] = u["n_new"]
        m_ref[u["d"], u["h"]] = u["m_new"]


def _mlstm_scan(q_all, k_all, kt_all, z, gates_c, gates_r, *, bsz, seq, ctx_len):
    rows = q_all.shape[0]
    ncl = seq // ML_CHUNK
    ncc = ctx_len // ML_CHUNK
    lat_blocks = bsz * ncl
    nh = ML_HEADS

    def blk(d, b, s):
        jc = s if d == 0 else ncc - 1 - s
        jl = s - ncc if d == 0 else ncl - 1 - (s - ncc)
        return jnp.where(s < ncc, lat_blocks + b * ncc + jc, b * ncl + jl)

    def dir_specs(d):
        return [pl.BlockSpec((ML_CHUNK, BRANCH_W), lambda b, s: (blk(d, b, s), 0)),
                pl.BlockSpec((ML_CHUNK, BRANCH_W), lambda b, s: (blk(d, b, s), 0)),
                pl.BlockSpec((BRANCH_W, ML_CHUNK), lambda b, s: (0, blk(d, b, s))),
                pl.BlockSpec((ML_CHUNK, BRANCH_W), lambda b, s: (blk(d, b, s), Z_VM // BRANCH_W)),
                pl.BlockSpec((None, ML_CHUNK, 2 * nh), lambda b, s: (d, blk(d, b, s), 0)),
                pl.BlockSpec((None, 2 * nh, ML_CHUNK), lambda b, s: (d, 0, blk(d, b, s)))]

    out = jax.ShapeDtypeStruct((rows, BRANCH_W), F32)
    args = (q_all, k_all, kt_all, z, gates_c, gates_r)
    return pl.pallas_call(
        _mlstm_kernel,
        out_shape=(out, out),
        grid=(bsz, ncc + ncl),
        in_specs=dir_specs(0) + dir_specs(1),
        out_specs=(pl.BlockSpec((ML_CHUNK, BRANCH_W), lambda b, s: (blk(0, b, s), 0)),
                   pl.BlockSpec((ML_CHUNK, BRANCH_W), lambda b, s: (blk(1, b, s), 0))),
        scratch_shapes=[pltpu.VMEM((2, nh, ML_HD, ML_HD), F32),
                        pltpu.VMEM((2, nh, ML_HD, LANES), F32),
                        pltpu.VMEM((2, nh, 1, 1), F32)],
        compiler_params=_cparams(("parallel", "arbitrary")),
        name="mlstm_scan",
    )(*args, *args)


def _s5_kernel(uc_ref, ult_ref, m_ref, w_ref, v_ref, a1_ref, a2_ref, yc_ref, ylt_ref,
               ec_ref, el_ref, xc_ref, xl_ref, ul_ref, *, bsz):
    n_cl = ul_ref.shape[0] // bsz
    per_batch = []
    for b in range(bsz):
        cols = [ult_ref[:, (b * S5_T + t) * n_cl:(b * S5_T + t + 1) * n_cl].astype(F32)
                for t in range(S5_T)]
        per_batch.append(jnp.concatenate(cols, axis=0).T)
    ul_ref[...] = pltpu.einshape("bcf->(cb)f", jnp.stack(per_batch)).astype(BF16)

    w = w_ref[...]
    ec_ref[...] = jnp.dot(uc_ref[...], w, preferred_element_type=F32)
    el_ref[...] = jnp.dot(ul_ref[...], w, preferred_element_type=F32)
    a1 = a1_ref[...]
    a2 = a2_ref[...]
    half = LANES
    a1f, a2f, a1b, a2b = a1[:, 0:half], a2[:, 0:half], a1[:, half:], a2[:, half:]

    def scan(e_ref, x_ref, state):
        n_chunks = e_ref.shape[0] // bsz

        def body(j, st):
            xf, xfs, xb, xbs = st
            rf = pl.multiple_of(j * bsz, bsz)
            rb = pl.multiple_of((n_chunks - 1 - j) * bsz, bsz)
            x_ref[pl.ds(rf, bsz), 0:half] = xf
            x_ref[pl.ds(rb, bsz), half:2 * half] = xb
            ef, eb = e_ref[pl.ds(rf, bsz), 0:half], e_ref[pl.ds(rb, bsz), half:2 * half]
            efs = e_ref[pl.ds(rf, bsz), 2 * half:3 * half]
            ebs = e_ref[pl.ds(rb, bsz), 3 * half:4 * half]
            return (a1f * xf + a2f * xfs + ef, a1f * xfs - a2f * xf + efs,
                    a1b * xb + a2b * xbs + eb, a1b * xbs - a2b * xb + ebs)

        return lax.fori_loop(0, n_chunks, body, state)

    zero = jnp.zeros((bsz, half), F32)
    state = scan(ec_ref, xc_ref, (zero, zero, zero, zero))
    scan(el_ref, xl_ref, state)
    mm = m_ref[...]
    vv = v_ref[...]
    yc_ref[...] = (jnp.dot(uc_ref[...], mm, preferred_element_type=F32)
                   + jnp.dot(xc_ref[...].astype(BF16), vv, preferred_element_type=F32)
                   ).astype(yc_ref.dtype)
    yl = (jnp.dot(ul_ref[...], mm, preferred_element_type=F32)
          + jnp.dot(xl_ref[...].astype(BF16), vv, preferred_element_type=F32))
    yl = pltpu.einshape("(cb)f->bcf", yl, b=bsz)
    for b in range(bsz):
        ybt = yl[b].T
        for t in range(S5_T):
            ylt_ref[:, (b * S5_T + t) * n_cl:(b * S5_T + t + 1) * n_cl] = (
                ybt[t * S5_P:(t + 1) * S5_P, :].astype(ylt_ref.dtype))


def _s5_pack_kernel(x_ref, o_ref, f_ref):
    n_cl = x_ref.shape[0] // S5_T
    for j in range(f_ref.shape[0]):
        cs = slice(j * LANES, (j + 1) * LANES)
        f_ref[j] = x_ref[:, cs].astype(F32)
        for t in range(S5_T):
            o_ref[cs, t * n_cl:(t + 1) * n_cl] = (
                f_ref[j, pl.ds(t, n_cl, stride=S5_T), :].T.astype(o_ref.dtype))


def _s5_unpack_kernel(x_ref, o_ref, f_ref):
    n_cl = o_ref.shape[0] // S5_T
    for j in range(f_ref.shape[0]):
        cs = slice(j * LANES, (j + 1) * LANES)
        for t in range(S5_T):
            f_ref[j, pl.ds(t, n_cl, stride=S5_T), :] = (
                x_ref[cs, t * n_cl:(t + 1) * n_cl].astype(F32).T)
        o_ref[:, cs] = f_ref[j].astype(o_ref.dtype)


def _s5_pack(z, *, bsz, seq):
    return pl.pallas_call(
        _s5_pack_kernel,
        out_shape=jax.ShapeDtypeStruct((BRANCH_W, bsz * seq), BF16),
        grid=(bsz,),
        in_specs=[pl.BlockSpec((seq, BRANCH_W), lambda b: (b, Z_U // BRANCH_W))],
        out_specs=pl.BlockSpec((BRANCH_W, seq), lambda b: (0, b)),
        scratch_shapes=[pltpu.VMEM((BRANCH_W // LANES, seq, LANES), F32)],
        compiler_params=_cparams(("parallel",)),
        name="s5_pack",
    )(z)


def _s5_unpack(yt, *, bsz, seq, out_rows):
    return pl.pallas_call(
        _s5_unpack_kernel,
        out_shape=jax.ShapeDtypeStruct((out_rows, BRANCH_W), BF16),
        grid=(bsz,),
        in_specs=[pl.BlockSpec((BRANCH_W, seq), lambda b: (0, b))],
        out_specs=pl.BlockSpec((seq, BRANCH_W), lambda b: (b, 0)),
        scratch_shapes=[pltpu.VMEM((BRANCH_W // LANES, seq, LANES), F32)],
        compiler_params=_cparams(("parallel",)),
        name="s5_unpack",
    )(yt)


def _s5_tables(lam_re, lam_im, log_dt, b_re, b_im, c_re, c_im, d_skip):
    t, g, n, p = S5_T, S5_G, S5_N, S5_P
    dt = jnp.exp(log_dt)[..., None]
    den = lam_re * lam_re + lam_im * lam_im
    kk = jnp.arange(t + 1, dtype=F32)[None, None, :, None]
    mag = jnp.exp(kk * (lam_re * dt)[:, :, None, :])
    ang = kk * (lam_im * dt)[:, :, None, :]
    p_re, p_im = mag * jnp.cos(ang), mag * jnp.sin(ang)
    ab_re, ab_im = p_re[:, :, 1], p_im[:, :, 1]
    z_re = ((ab_re - 1.0) * lam_re + ab_im * lam_im) / den
    z_im = (ab_im * lam_re - (ab_re - 1.0) * lam_im) / den
    bt_re, bt_im = jnp.swapaxes(b_re, 2, 3), jnp.swapaxes(b_im, 2, 3)
    bb_re = z_re[:, :, None, :] * bt_re - z_im[:, :, None, :] * bt_im
    bb_im = z_re[:, :, None, :] * bt_im + z_im[:, :, None, :] * bt_re
    pk_re, pk_im = p_re[:, :, :t, None, :], p_im[:, :, :t, None, :]
    abk_re = pk_re * bb_re[:, :, None] - pk_im * bb_im[:, :, None]
    abk_im = pk_re * bb_im[:, :, None] + pk_im * bb_re[:, :, None]
    taps = (jnp.einsum('rgpn,rgkqn->rgqkp', c_re, abk_re)
            - jnp.einsum('rgpn,rgkqn->rgqkp', c_im, abk_im))
    base_f = taps[0].reshape(g, p, t * p)
    base_b = taps[1][:, :, ::-1].reshape(g, p, t * p)
    width = t * p
    m_f = jnp.stack([jnp.pad(base_f, ((0, 0), (0, 0), (p * s, 0)))[..., :width]
                     for s in range(t)], axis=1)
    m_b = jnp.stack([jnp.pad(base_b, ((0, 0), (0, 0), (0, p * (t - 1 - s))))[..., p * (t - 1 - s):]
                     for s in range(t)], axis=1)
    skip = jnp.eye(width, dtype=F32)[None] * jnp.tile(d_skip.reshape(g, p), (1, t))[:, None, :]
    m = (m_f + m_b).reshape(g, width, width) + skip
    wf_re, wf_im = (a[0][:, ::-1].reshape(g, width, n) for a in (abk_re, abk_im))
    wb_re, wb_im = (a[1].reshape(g, width, n) for a in (abk_re, abk_im))
    w = jnp.concatenate([wf_re, wf_im, wb_re, wb_im, wf_im, wf_re, wb_im, wb_re], axis=-1)
    ct_re, ct_im = jnp.swapaxes(c_re, 2, 3), jnp.swapaxes(c_im, 2, 3)
    pn_re, pn_im = jnp.swapaxes(p_re, 2, 3)[..., None], jnp.swapaxes(p_im, 2, 3)[..., None]
    ca_re = ct_re[:, :, :, None, :] * pn_re - ct_im[:, :, :, None, :] * pn_im
    ca_im = ct_re[:, :, :, None, :] * pn_im + ct_im[:, :, :, None, :] * pn_re
    v = jnp.concatenate([ca_re[0][:, :, 1:].reshape(g, n, width),
                         -ca_im[0][:, :, 1:].reshape(g, n, width),
                         ca_re[1][:, :, :0:-1].reshape(g, n, width),
                         -ca_im[1][:, :, :0:-1].reshape(g, n, width)], axis=1)
    at_re, at_im = p_re[:, :, t], p_im[:, :, t]
    a1 = jnp.concatenate([at_re[0], at_re[0], at_re[1], at_re[1]], axis=-1)[:, None, :]
    a2 = jnp.concatenate([-at_im[0], at_im[0], -at_im[1], at_im[1]], axis=-1)[:, None, :]
    return m.astype(BF16), w.astype(BF16), v.astype(BF16), a1, a2


def _s5_rows(u, bsz, length):
    nc = length // S5_T
    u = u.reshape(bsz, nc, S5_T, S5_G, S5_P)
    return jnp.transpose(u, (3, 1, 0, 2, 4)).reshape(S5_G, nc * bsz, S5_T * S5_P)


def _s5_unrows(y, bsz, length):
    nc = length // S5_T
    y = y.reshape(S5_G, nc, bsz, S5_T, S5_P)
    return jnp.transpose(y, (2, 1, 3, 0, 4)).reshape(bsz * length, BRANCH_W)


def _s5_mix(u_ctx, ut_lat, tables, *, bsz):
    m, w, v, a1, a2 = tables
    rc = u_ctx.shape[1]
    t_lat = ut_lat.shape[1]
    rl = t_lat // S5_T
    wd = S5_T * S5_P

    def rows_spec(r):
        return pl.BlockSpec((None, r, wd), lambda g: (g, 0, 0))

    lat_spec = pl.BlockSpec((S5_P, t_lat), lambda g: (g, 0))
    sq = pl.BlockSpec((None, wd, wd), lambda g: (g, 0, 0))
    vec = pl.BlockSpec((None, 1, wd), lambda g: (g, 0, 0))
    return pl.pallas_call(
        functools.partial(_s5_kernel, bsz=bsz),
        out_shape=(jax.ShapeDtypeStruct((S5_G, rc, wd), BF16),
                   jax.ShapeDtypeStruct((BRANCH_W, t_lat), BF16)),
        grid=(S5_G,),
        in_specs=[rows_spec(rc), lat_spec, sq,
                  pl.BlockSpec((None, wd, 2 * wd), lambda g: (g, 0, 0)), sq, vec, vec],
        out_specs=(rows_spec(rc), lat_spec),
        scratch_shapes=[pltpu.VMEM((rc, 2 * wd), F32), pltpu.VMEM((rl, 2 * wd), F32),
                        pltpu.VMEM((rc, wd), F32), pltpu.VMEM((rl, wd), F32),
                        pltpu.VMEM((rl, wd), BF16)],
        compiler_params=_cparams(("parallel",)),
        name="s5_mix",
    )(u_ctx, ut_lat, m, w, v, a1, a2)


def _merge_kernel(x_ref, ya_ref, hf_ref, hb_ref, om_ref, ys_ref, gate_ref, g1_ref, mlg_ref,
                  gluw_ref, glub_ref, wbr_ref, wout_ref, o_ref):
    hsum = hf_ref[...] + hb_ref[...]
    og = jax.nn.sigmoid(om_ref[...].astype(F32))
    mlg = mlg_ref[...]
    yb_parts = []
    for h in range(ML_HEADS):
        sl = slice(h * ML_HD, (h + 1) * ML_HD)
        hh = hsum[:, sl]
        hn = hh * lax.rsqrt(jnp.mean(hh * hh, axis=-1, keepdims=True) + EPS) * mlg[:, sl]
        yb_parts.append(hn * og[:, sl])
    yb = jnp.concatenate(yb_parts, axis=-1)
    gl = _gelu(ys_ref[...].astype(F32))
    ys = gl * jax.nn.sigmoid(
        jnp.dot(gl.astype(BF16), gluw_ref[...], preferred_element_type=F32) + glub_ref[...])
    d = o_ref.shape[1]
    merged = None
    for r, y in enumerate((ya_ref[...], yb, ys)):
        proj = jnp.dot(y.astype(BF16), wbr_ref[r], preferred_element_type=F32)
        term = jax.nn.sigmoid(gate_ref[:, r * d:(r + 1) * d].astype(F32)) * proj
        merged = term if merged is None else merged + term
    y = jnp.dot(merged.astype(BF16), wout_ref[...], preferred_element_type=F32)
    o_ref[...] = x_ref[...] + g1_ref[...] * y


def _merge(x_all, ya, h_dirs, z, ys, g1, ml_norm_g, glu_w, glu_b, w_br, w_out, *, n_rows, bsz, seq, tm):
    d = x_all.shape[1]
    nlb = bsz * seq // tm
    bpb = seq // tm
    w = BRANCH_W

    def rows(width, col_blk=0):
        return pl.BlockSpec((tm, width), lambda i: (i, col_blk))

    def full(shape):
        return pl.BlockSpec(shape, lambda i: (0,) * len(shape))

    return pl.pallas_call(
        _merge_kernel,
        out_shape=jax.ShapeDtypeStruct((n_rows, d), F32),
        grid=(n_rows // tm,),
        in_specs=[rows(d), rows(w), rows(w), rows(w),
                  rows(w, Z_OM // w), rows(w),
                  rows(3 * d, Z_GATE // (3 * d)),
                  pl.BlockSpec((None, 1, d), lambda i: (_row_batch(i, nlb, bpb, bsz), 0, 0)),
                  full((1, w)), full((w, w)), full((1, w)), full((3, w, d)), full((d, d))],
        out_specs=rows(d),
        compiler_params=_cparams(("parallel",)),
        name="merge",
    )(x_all, ya, h_dirs[0], h_dirs[1], z, ys, z, g1, ml_norm_g, glu_w, glu_b, w_br, w_out)


NOT_TOP = 127.0


def _take_top(s, n_take, vals_ref, want_rank):
    rows = s.shape[0]
    ridx = lax.broadcasted_iota(jnp.int32, s.shape, 0).astype(F32)
    rank = jnp.full(s.shape, NOT_TOP, F32) if want_rank else None
    for i in range(n_take):
        mx = jnp.max(s, axis=0, keepdims=True)
        first = jnp.min(jnp.where(s == mx, ridx, float(rows)), axis=0, keepdims=True)
        hit = ridx == first
        s = jnp.where(hit, -jnp.inf, s)
        if want_rank:
            rank = jnp.where(hit, float(i), rank)
        vals_ref[i:i + 1, :] = mx
    return s, rank


SUBLANES = 8


def _sort_network_pairs(n):
    pairs = []
    p = 1
    while p < n:
        k = p
        while k >= 1:
            for j in range(k % p, n - k, 2 * k):
                for i in range(min(k, n - j - k)):
                    if (i + j) // (2 * p) == (i + j + k) // (2 * p):
                        pairs.append((i + j, i + j + k))
            k //= 2
        p *= 2
    return pairs


def _top_sorted(s):
    n = PEER_TOPK
    x = _slabs(s)
    assert len(x) <= n
    x = x + [jnp.full(x[0].shape, -jnp.inf, F32)] * (n - len(x))

    def exchange(i, j):
        x[i], x[j] = jnp.maximum(x[i], x[j]), jnp.minimum(x[i], x[j])

    for i, j in _sort_network_pairs(n):
        exchange(i, j)
    shift = SUBLANES // 2
    while shift >= 1:
        y = [pltpu.roll(v, shift, 0) for v in x]
        x = [jnp.maximum(x[i], y[n - 1 - i]) for i in range(n)]
        d = n // 2
        while d >= 1:
            for i in range(n):
                if not i & d:
                    exchange(i, i + d)
            d //= 2
        shift //= 2
    return x


def _slabs(s):
    return [s[SUBLANES * i:SUBLANES * (i + 1), :] for i in range(s.shape[0] // SUBLANES)]


def _tie_count(s, top):
    flag = jnp.zeros(top[0].shape, F32)
    for i in range(len(top) - 1):
        flag = flag + jnp.where(top[i] == top[i + 1], 1.0, 0.0)
    reach = functools.reduce(jnp.add, [jnp.where(v >= top[-1], 1.0, 0.0) for v in _slabs(s)])
    reach = jnp.sum(reach, axis=0, keepdims=True)
    return flag + jnp.where(reach != float(PEER_TOPK), 1.0, 0.0)


def _candidates(v1_ref, v2_ref):
    k = PEER_TOPK
    v2_head = v2_ref[0:8, :]
    jrow = lax.broadcasted_iota(jnp.int32, v2_head.shape, 0)
    pieces = [v1_ref[0:1, :] + v2_ref[...]]
    for i in range(1, 8):
        pieces.append(jnp.where(jrow < k // (i + 1), v1_ref[i:i + 1, :] + v2_head, -jnp.inf))
    pieces.append(v1_ref[8:16, :] + v2_ref[0:1, :])
    return jnp.concatenate(pieces, axis=0)


def _write_counts(picked, cnt_ref):
    cnt_ref[0:1, :] = jnp.sum(picked[0:16], axis=0, keepdims=True)
    for i in range(1, 8):
        cnt_ref[i:i + 1, :] = jnp.sum(picked[8 + 8 * i:16 + 8 * i], axis=0, keepdims=True)
    cnt_ref[8:16, :] = picked[72:80]


def _candidate_counts(v1_ref, v2_ref, top_ref, cnt_ref):
    cand = _candidates(v1_ref, v2_ref)
    cand_left, _ = _take_top(cand, PEER_TOPK, top_ref, False)
    _write_counts(jnp.where(cand_left != cand, 1.0, 0.0), cnt_ref)
    top = top_ref[...]
    return jnp.sum(jnp.exp(top - top[0:1, :]), axis=0, keepdims=True)


def _peer_topk_kernel(q_ref, k1_ref, k2_ref, e1_ref, cnt1_ref, e2_ref, rank2_ref,
                      v1_ref, v2_ref, top_ref, cnt_ref):
    k = PEER_TOPK
    q = q_ref[...].astype(BF16)
    s1 = _nt_dot(k1_ref[...], q[:, :LANES])
    s2 = _nt_dot(k2_ref[...], q[:, LANES:])
    top1 = _top_sorted(s1)
    top2 = _top_sorted(s2)
    for i in range(k):
        v1_ref[i:i + 1, :] = top1[i][0:1, :]
        v2_ref[i:i + 1, :] = top2[i][0:1, :]
    cand = _candidates(v1_ref, v2_ref)
    topc = _top_sorted(cand)
    any_tie = jnp.max(_tie_count(s1, top1) + _tie_count(s2, top2) + _tie_count(cand, topc)) > 0.0

    @pl.when(any_tie)
    def _():
        _, rank1 = _take_top(s1, k, v1_ref, True)
        _, rank2 = _take_top(s2, k, v2_ref, True)
        zsum = _candidate_counts(v1_ref, v2_ref, top_ref, cnt_ref)
        cnt1 = jnp.zeros(s1.shape, F32)
        for i in range(k):
            cnt1 = jnp.where(rank1 == float(i), cnt_ref[i:i + 1, :], cnt1)
        e1_ref[...] = jnp.where(rank1 < k, jnp.exp(s1 - v1_ref[0:1, :]), 0.0) / zsum
        cnt1_ref[...] = cnt1
        e2_ref[...] = jnp.where(rank2 < k, jnp.exp(s2 - v2_ref[0:1, :]), 0.0).astype(BF16)
        rank2_ref[...] = rank2.astype(BF16)

    @pl.when(jnp.logical_not(any_tie))
    def _():
        _write_counts(jnp.where(cand >= topc[k - 1][0:1, :], 1.0, 0.0), cnt_ref)
        zsum = functools.reduce(jnp.add, [jnp.exp(t[0:1, :] - topc[0][0:1, :]) for t in topc])
        cnt_rows = [jnp.broadcast_to(cnt_ref[i:i + 1, :], top1[0].shape) for i in range(k)]
        cnt1, rank2 = [], []
        for v in _slabs(s1):
            c = jnp.zeros(v.shape, F32)
            for i in range(k):
                c = jnp.where(v == top1[i], cnt_rows[i], c)
            cnt1.append(c)
        for v in _slabs(s2):
            r = jnp.full(v.shape, NOT_TOP, F32)
            for i in range(k):
                r = jnp.where(v == top2[i], float(i), r)
            rank2.append(r)
        e1_ref[...] = jnp.where(s1 >= v1_ref[k - 1:k, :], jnp.exp(s1 - v1_ref[0:1, :]), 0.0) / zsum
        cnt1_ref[...] = jnp.concatenate(cnt1, axis=0)
        e2_ref[...] = jnp.where(s2 >= v2_ref[k - 1:k, :], jnp.exp(s2 - v2_ref[0:1, :]), 0.0
                                ).astype(BF16)
        rank2_ref[...] = jnp.concatenate(rank2, axis=0).astype(BF16)


def _peer_topk(q, sub_k, *, n_rows, tm):
    nk = PEER_NKEYS
    row_tab = jax.ShapeDtypeStruct((n_rows // tm, PEER_HEADS, nk, tm), F32)
    tile_tab = jax.ShapeDtypeStruct((n_rows // tm, PEER_HEADS, nk, tm), BF16)
    tab_spec = pl.BlockSpec((None, None, nk, tm), lambda i, h: (i, h, 0, 0))
    return pl.pallas_call(
        _peer_topk_kernel,
        out_shape=(row_tab, row_tab, tile_tab, tile_tab),
        grid=(n_rows // tm, PEER_HEADS),
        in_specs=[pl.BlockSpec((tm, 2 * LANES), lambda i, h: (i, h)),
                  pl.BlockSpec((None, nk, LANES), lambda i, h: (0, 0, 0)),
                  pl.BlockSpec((None, nk, LANES), lambda i, h: (1, 0, 0))],
        out_specs=(tab_spec,) * 4,
        scratch_shapes=[pltpu.VMEM((PEER_TOPK, tm), F32)] * 4,
        compiler_params=_cparams(("parallel", "parallel")),
        name="peer_topk",
    )(q, sub_k, sub_k)


GATE_KEYS = 2
KEY_GROUP = 8


def _peer_dense_kernel(ht_ref, u_ref, vt_ref, e1_ref, cnt1_ref, e2_ref, rank2_ref,
                       x_ref, g2_ref, fg_ref, o_ref, acc_ref, act_ref, p_ref, e2s_ref, rank2s_ref, *,
                       a_per_blk, final_norm):
    j = pl.program_id(1)
    n_blk = pl.num_programs(1) - 1
    cur = j % 2
    tm = ht_ref.shape[1]
    nk = PEER_NKEYS

    @pl.when(j == 0)
    def _():
        acc_ref[...] = jnp.zeros_like(acc_ref)
        p_ref[1] = jnp.zeros(p_ref.shape[1:], p_ref.dtype)
        e2s_ref[...] = e2_ref[...]
        rank2s_ref[...] = rank2_ref[...]

    @pl.when(j < n_blk)
    def _():
        act_ref[...] = _gelu(jnp.dot(u_ref[...], ht_ref[...], preferred_element_type=F32)
                             ).astype(BF16)
        acc_ref[...] += jnp.dot(vt_ref[...], p_ref[1 - cur], preferred_element_type=F32)
        a0 = pl.multiple_of(j * a_per_blk, KEY_GROUP)

        def row_bf16(ref, h, i, ls):
            grp = (i // KEY_GROUP) * KEY_GROUP
            w = ref[h, pl.ds(a0 + grp, KEY_GROUP), ls][i % KEY_GROUP:i % KEY_GROUP + 1]
            return jnp.broadcast_to(w, (nk, LANES)).astype(BF16)

        for lt in range(tm // LANES):
            ls = slice(lt * LANES, (lt + 1) * LANES)
            for i0 in range(0, a_per_blk, GATE_KEYS):
                gates = [None] * GATE_KEYS
                for h in range(PEER_HEADS):
                    for ii in range(GATE_KEYS):
                        lead = row_bf16(cnt1_ref, h, i0 + ii, ls) - rank2s_ref[h, :, ls]
                        term = jnp.minimum(jnp.maximum(lead, 0.0),
                                           row_bf16(e1_ref, h, i0 + ii, ls)) * e2s_ref[h, :, ls]
                        gates[ii] = term if gates[ii] is None else gates[ii] + term
                for ii in range(GATE_KEYS):
                    r0 = (i0 + ii) * nk
                    p_ref[cur, r0:r0 + nk, ls] = gates[ii] * act_ref[r0:r0 + nk, ls]

    @pl.when(j == n_blk)
    def _():
        acc = acc_ref[...] + jnp.dot(vt_ref[...], p_ref[1 - cur], preferred_element_type=F32)
        y = x_ref[...] + g2_ref[...] * acc.T
        if final_norm:
            y = y * lax.rsqrt(jnp.mean(y * y, axis=-1, keepdims=True) + EPS) * fg_ref[...]
        o_ref[...] = y


def _peer_dense(h2, u_tab, vt_tab, tabs, x_all, g2, final_g, *, n_rows, bsz, seq, tm, te,
                final_norm):
    d = x_all.shape[1]
    n_exp = u_tab.shape[0]
    nlb = bsz * seq // tm
    bpb = seq // tm
    nk = PEER_NKEYS
    n_blk = n_exp // te
    assert (te // nk) % KEY_GROUP == 0
    tab_spec = pl.BlockSpec((None, PEER_HEADS, nk, tm), lambda i, j: (i, 0, 0, 0))
    return pl.pallas_call(
        functools.partial(_peer_dense_kernel, a_per_blk=te // nk, final_norm=final_norm),
        out_shape=jax.ShapeDtypeStruct((n_rows, d), F32),
        grid=(n_rows // tm, n_blk + 1),
        in_specs=[pl.BlockSpec((d, tm), lambda i, j: (0, i)),
                  pl.BlockSpec((te, d), lambda i, j: (jnp.minimum(j, n_blk - 1), 0)),
                  pl.BlockSpec((None, d, te), lambda i, j: (jnp.maximum(j - 1, 0), 0, 0)),
                  tab_spec, tab_spec, tab_spec, tab_spec,
                  pl.BlockSpec((tm, d), lambda i, j: (i, 0)),
                  pl.BlockSpec((None, 1, d), lambda i, j: (_row_batch(i, nlb, bpb, bsz), 0, 0)),
                  pl.BlockSpec((1, d), lambda i, j: (0, 0))],
        out_specs=pl.BlockSpec((tm, d), lambda i, j: (i, 0)),
        scratch_shapes=[pltpu.VMEM((d, tm), F32), pltpu.VMEM((te, tm), BF16),
                        pltpu.VMEM((2, te, tm), BF16),
                        pltpu.VMEM((PEER_HEADS, nk, tm), BF16), pltpu.VMEM((PEER_HEADS, nk, tm), BF16)],
        compiler_params=_cparams(("parallel", "arbitrary")),
        name="peer_dense",
    )(h2, u_tab, vt_tab, *tabs, x_all, g2, final_g)


def _transpose_cast_kernel(x_ref, o_ref):
    o_ref[...] = x_ref[...].T.astype(o_ref.dtype)


def _transpose_cast(w, layer, *, tr):
    _, rows, cols = w.shape
    return pl.pallas_call(
        _transpose_cast_kernel,
        out_shape=jax.ShapeDtypeStruct((rows // tr, cols, tr), BF16),
        grid=(rows // tr,),
        in_specs=[pl.BlockSpec((None, tr, cols), lambda i: (layer, i, 0))],
        out_specs=pl.BlockSpec((None, cols, tr), lambda i: (i, 0, 0)),
        compiler_params=_cparams(("parallel",)),
        name="transpose_cast",
    )(w)


def _rope_tables(seq, tm):
    rows = seq // GRID_W
    n_freq = DA_HD // 4
    inv = ROPE_BASE ** (-jnp.arange(n_freq, dtype=F32) / n_freq)
    r = jnp.repeat(jnp.arange(rows, dtype=F32), GRID_W)
    col = jnp.tile(jnp.arange(GRID_W, dtype=F32), rows)
    ang = jnp.concatenate([r[:, None] * inv, col[:, None] * inv], axis=-1)
    cos, sin = jnp.cos(ang), jnp.sin(ang)
    cos_t = jnp.tile(cos, (1, 4))
    sin_t = jnp.tile(jnp.concatenate([-sin, sin], axis=-1), (1, 2))
    ident = jnp.ones((tm, LANES), F32)
    return (jnp.concatenate([cos_t, ident], axis=0),
            jnp.concatenate([sin_t, jnp.zeros((tm, LANES), F32)], axis=0))


def kernel(x, c, ctx, c_ctx, w_mod, b_mod, norm1_g, norm2_g, w_in, da_lam_q1, da_lam_k1, da_lam_q2, da_lam_k2, da_sub_g, ml_conv_w, ml_conv_b, ml_wq, ml_wk, ml_gate_b, ml_norm_g, s5_lam_re, s5_lam_im, s5_log_dt, s5_b_re, s5_b_im, s5_c_re, s5_c_im, s5_d, s5_glu_w, s5_glu_b, w_br, w_out, peer_wq, peer_sub_k, peer_u, peer_v, final_g):
    bsz, seq, d = x.shape
    ctx_len = ctx.shape[1]
    depth = w_in.shape[0]
    t_lat = bsz * seq
    t_ctx = bsz * ctx_len
    t_all = t_lat + t_ctx
    tm = math.gcd(512, math.gcd(seq, t_ctx))
    tm_merge = min(tm, 256)
    tq = min(256, ctx_len)
    nh = ML_HEADS

    x_all = jnp.concatenate([x.reshape(t_lat, d), ctx.reshape(t_ctx, d)], axis=0)
    pad = (-(bsz + 1)) % 8
    c_all = jnp.concatenate([c, c_ctx[None], jnp.zeros((pad, d), F32)], axis=0)
    cos_tab, sin_tab = _rope_tables(seq, tm)

    for l in range(depth):
        need_ctx = l < depth - 1
        n_rows = t_all if need_ctx else t_lat
        lam_init = 0.8 - 0.6 * math.exp(-0.3 * l)

        mods = _modulation(c_all, w_mod[l].astype(BF16), b_mod[l][None])
        sh1, sc1, g1, sh2, sc2, g2 = [mods[:bsz + 1, i * d:(i + 1) * d][:, None, :]
                                      for i in range(N_MOD)]

        wi = w_in[l]
        i_q, i_k, i_v, i_xm, i_vm, i_om, i_g, i_u, i_gate = (
            0, 512, 1024, 1536, 2048, 2560, 3072, 3088, 3600)
        w_main = jnp.concatenate([wi[:, i_gate:], wi[:, :i_g], wi[:, i_u:i_gate]], axis=1).astype(BF16)
        w_gate = jnp.pad(wi[:, i_g:i_u], ((0, 0), (0, LANES - (i_u - i_g)))).astype(BF16)
        n1 = norm1_g[l][None]
        z = _modnorm_matmul(x_all, t_all, n1, sc1, sh1, w_main, bsz=bsz, seq=seq, tm=tm,
                            tn=w_main.shape[1] // 2, out_dtype=BF16)
        zg = _modnorm_matmul(x_all, t_all, n1, sc1, sh1, w_gate, bsz=bsz, seq=seq, tm=tm, tn=LANES)

        qr, kr, vb = _rope_qkv(z, cos_tab, sin_tab, bsz=bsz, seq=seq, tm=tm)
        lam = (jnp.exp(jnp.sum(da_lam_q1[l] * da_lam_k1[l]))
               - jnp.exp(jnp.sum(da_lam_q2[l] * da_lam_k2[l])) + lam_init).reshape(1, 1)
        sub_g = da_sub_g[l][None]
        ya = _diff_attention(qr, kr, vb, lam, sub_g, q_row0=0, n_q=seq,
                             kv_segs=[(0, seq), (t_lat, ctx_len)], bsz=bsz, tq=tq,
                             out_scale=1.0 - lam_init, out_rows=n_rows)
        if need_ctx:
            ya = _diff_attention(qr, kr, vb, lam, sub_g, q_row0=t_lat, n_q=ctx_len,
                                 kv_segs=[(t_lat, ctx_len)], bsz=bsz, tq=tq,
                                 out_scale=1.0 - lam_init, out_rows=n_rows, into=ya)

        wq_b, wk_b = ml_wq[l].astype(BF16), ml_wk[l].astype(BF16)
        cb = ml_conv_b[l][None]
        qkt = _ml_prep(z, ml_conv_w[l], cb, wq_b, wk_b, row0=0, seq_len=seq, bsz=bsz)
        q_ml, k_ml, kt_ml = _ml_prep(z, ml_conv_w[l], cb, wq_b, wk_b, row0=t_lat, seq_len=ctx_len,
                                     bsz=bsz, into=qkt)
        gates = (zg[:, :4 * nh] + ml_gate_b[l]).reshape(t_all, 2, 2 * nh)
        gates_c = jnp.transpose(gates, (1, 0, 2))
        gates_r = jnp.transpose(gates, (1, 2, 0))
        h_dirs = _mlstm_scan(q_ml, k_ml, kt_ml, z, gates_c, gates_r, bsz=bsz, seq=seq,
                             ctx_len=ctx_len)

        tables = _s5_tables(s5_lam_re[l], s5_lam_im[l], s5_log_dt[l], s5_b_re[l], s5_b_im[l],
                            s5_c_re[l], s5_c_im[l], s5_d[l])
        u_ctx = _s5_rows(z[t_lat:, Z_U:Z_U + BRANCH_W], bsz, ctx_len)
        y_c, yt_l = _s5_mix(u_ctx, _s5_pack(z, bsz=bsz, seq=seq), tables, bsz=bsz)
        ys = _s5_unpack(yt_l, bsz=bsz, seq=seq, out_rows=n_rows)
        if need_ctx:
            ys = lax.dynamic_update_slice(ys, _s5_unrows(y_c, bsz, ctx_len), (t_lat, 0))

        x_all = _merge(x_all, ya, h_dirs, z, ys, g1, ml_norm_g[l][None],
                       s5_glu_w[l].astype(BF16), s5_glu_b[l][None], w_br[l].astype(BF16),
                       w_out[l].astype(BF16), n_rows=n_rows, bsz=bsz, seq=seq, tm=tm_merge)

        n2 = norm2_g[l][None]
        wq_p = peer_wq[l].astype(BF16)
        pq, h2t = _modnorm_matmul(x_all, n_rows, n2, sc2, sh2, wq_p, bsz=bsz, seq=seq, tm=tm,
                                  tn=wq_p.shape[1], emit_h=True)
        tabs = _peer_topk(pq, peer_sub_k[l].astype(BF16), n_rows=n_rows, tm=tm)
        te = 2048
        x_all = _peer_dense(h2t, peer_u[l].astype(BF16), _transpose_cast(peer_v, l, tr=te),
                            tabs, x_all, g2, final_g[None], n_rows=n_rows, bsz=bsz, seq=seq, tm=tm,
                            te=te, final_norm=not need_ctx)

    return x_all.reshape(bsz, seq, d)
```

```python
import functools
import math

import jax
import jax.numpy as jnp
from jax import lax
from jax.experimental import pallas as pl
from jax.experimental.pallas import tpu as pltpu

F32 = jnp.float32
BF16 = jnp.bfloat16

EPS = 1e-6
LOG2E = 1.4426950408889634
N_MOD = 6
BRANCH_W = 512
GRID_W = 64
ROPE_BASE = 10000.0
DA_HEADS = 4
DA_HD = 64
DA_VD = 2 * DA_HD
ML_HEADS = 4
ML_HD = BRANCH_W // ML_HEADS
ML_CHUNK = 128
S5_P = 16
S5_G = BRANCH_W // S5_P
S5_N = 64
S5_T = 16
PEER_HEADS = 8
PEER_NKEYS = 128
PEER_TOPK = 16
LANES = 128
VMEM_LIMIT = 56 * 1024 * 1024

Z_GATE = 0
Z_Q, Z_K, Z_V, Z_XM, Z_VM, Z_OM, Z_U = (3072 + i * BRANCH_W for i in range(7))


def _cparams(sem):
    return pltpu.CompilerParams(dimension_semantics=sem, vmem_limit_bytes=VMEM_LIMIT)


def _nt_dot(a, b):
    return lax.dot_general(a, b, (((1,), (1,)), ((), ())), preferred_element_type=F32)


def _skip_refs(kernel_fn, start, count, *refs):
    return kernel_fn(*refs[:start], *refs[start + count:])


def _gelu(x):
    return 0.5 * x * (1.0 + lax.erf(x * (2.0 ** -0.5)))


def _log_sigmoid(x):
    return jnp.minimum(x, 0.0) - jnp.log1p(jnp.exp(-jnp.abs(x)))


def _mod_kernel(c_ref, w_ref, b_ref, o_ref):
    c = c_ref[...]
    a = c * jax.nn.sigmoid(c)
    o_ref[...] = jnp.dot(a.astype(BF16), w_ref[...], preferred_element_type=F32) + b_ref[...]


def _modulation(c_all, w, b):
    rows, d = c_all.shape
    n = w.shape[1]
    tn = 1536
    return pl.pallas_call(
        _mod_kernel,
        out_shape=jax.ShapeDtypeStruct((rows, n), F32),
        grid=(n // tn,),
        in_specs=[pl.BlockSpec((rows, d), lambda j: (0, 0)),
                  pl.BlockSpec((d, tn), lambda j: (0, j)),
                  pl.BlockSpec((1, tn), lambda j: (0, j))],
        out_specs=pl.BlockSpec((rows, tn), lambda j: (0, j)),
        compiler_params=_cparams(("parallel",)),
        name="modulation",
    )(c_all, w, b)


def _modnorm_matmul_kernel(x_ref, g_ref, sc_ref, sh_ref, w_ref, o_ref, *h_out):
    x = x_ref[...]
    y = x * lax.rsqrt(jnp.mean(x * x, axis=-1, keepdims=True) + EPS) * g_ref[...]
    hf = y * (1.0 + sc_ref[...]) + sh_ref[...]
    h = hf.astype(BF16)
    o_ref[...] = jnp.dot(h, w_ref[...], preferred_element_type=F32).astype(o_ref.dtype)
    if h_out:
        h_out[0][...] = hf.T.astype(BF16)


def _row_batch(i, n_lat_blocks, blocks_per_batch, bsz):
    return jnp.where(i < n_lat_blocks, i // blocks_per_batch, bsz)


def _modnorm_matmul(x_all, n_rows, g, sc, sh, w, *, bsz, seq, tm, tn, out_dtype=F32, emit_h=False):
    d = x_all.shape[1]
    n = w.shape[1]
    nlb = bsz * seq // tm
    bpb = seq // tm
    assert not emit_h or tn == n
    mod_spec = pl.BlockSpec((None, 1, d), lambda j, i: (_row_batch(i, nlb, bpb, bsz), 0, 0))
    out_shape = [jax.ShapeDtypeStruct((n_rows, n), out_dtype)]
    out_specs = [pl.BlockSpec((tm, tn), lambda j, i: (i, j))]
    if emit_h:
        out_shape.append(jax.ShapeDtypeStruct((d, n_rows), BF16))
        out_specs.append(pl.BlockSpec((d, tm), lambda j, i: (0, i)))
    out = pl.pallas_call(
        _modnorm_matmul_kernel,
        out_shape=tuple(out_shape),
        grid=(n // tn, n_rows // tm),
        in_specs=[pl.BlockSpec((tm, d), lambda j, i: (i, 0)),
                  pl.BlockSpec((1, d), lambda j, i: (0, 0)),
                  mod_spec, mod_spec,
                  pl.BlockSpec((d, tn), lambda j, i: (0, j))],
        out_specs=tuple(out_specs),
        compiler_params=_cparams(("parallel", "parallel")),
        name="modnorm_matmul",
    )(x_all, g, sc, sh, w)
    return out if emit_h else out[0]


def _rope_kernel(q_ref, k_ref, v_ref, cos_ref, sin_ref, qo_ref, ko_ref, vo_ref):
    cos = cos_ref[...]
    sin = sin_ref[...]
    lane = lax.broadcasted_iota(jnp.int32, cos.shape, 1)
    first_half = (lane % DA_HD) < (DA_HD // 2)

    def rope(x):
        partner = jnp.where(first_half, pltpu.roll(x, LANES - DA_HD // 2, 1),
                            pltpu.roll(x, DA_HD // 2, 1))
        return x * cos + partner * sin

    for h in range(DA_HEADS):
        sl = slice(h * LANES, (h + 1) * LANES)
        qo_ref[:, sl] = (rope(q_ref[:, sl].astype(F32)) * (DA_HD ** -0.5 * LOG2E)).astype(BF16)
        ko_ref[:, sl] = rope(k_ref[:, sl].astype(F32)).astype(BF16)
    vo_ref[...] = v_ref[...].astype(BF16)


def _rope_qkv(z, cos_tab, sin_tab, *, bsz, seq, tm):
    rows = z.shape[0]
    nlb = bsz * seq // tm
    bpb = seq // tm
    w = BRANCH_W
    tab_spec = pl.BlockSpec((tm, LANES), lambda i: (jnp.where(i < nlb, i % bpb, bpb), 0))
    out = jax.ShapeDtypeStruct((rows, w), BF16)
    return pl.pallas_call(
        _rope_kernel,
        out_shape=(out, out, out),
        grid=(rows // tm,),
        in_specs=[pl.BlockSpec((tm, w), lambda i: (i, Z_Q // w)),
                  pl.BlockSpec((tm, w), lambda i: (i, Z_K // w)),
                  pl.BlockSpec((tm, w), lambda i: (i, Z_V // w)),
                  tab_spec, tab_spec],
        out_specs=(pl.BlockSpec((tm, w), lambda i: (i, 0)),) * 3,
        compiler_params=_cparams(("parallel",)),
        name="rope_qkv",
    )(z, z, z, cos_tab, sin_tab)


def _attn_kernel(*refs, n_seg, n_sub, out_scale):
    lam_ref, q_ref = refs[0], refs[1]
    k_refs = refs[2:2 + n_seg]
    v_refs = refs[2 + n_seg:2 + 2 * n_seg]
    g_ref, o_ref = refs[2 + 2 * n_seg], refs[3 + 2 * n_seg]
    tq = q_ref.shape[0] // n_sub
    lane = lax.broadcasted_iota(jnp.int32, (tq, LANES), 1)
    scores = []
    for sub in range(n_sub):
        q = q_ref[sub * tq:(sub + 1) * tq, :].astype(F32)
        qq = jnp.concatenate([jnp.where(lane < DA_HD, q, 0.0), jnp.where(lane >= DA_HD, q, 0.0)],
                             axis=0).astype(BF16)
        scores.append([_nt_dot(qq, k_ref[...]) for k_ref in k_refs])
    for sub in range(n_sub):
        rows = slice(sub * tq, (sub + 1) * tq)
        s = scores[sub]
        m = functools.reduce(jnp.maximum, [jnp.max(t, axis=-1, keepdims=True) for t in s])
        p = [jnp.exp2(t - m) for t in s]
        denom = functools.reduce(jnp.add, [jnp.sum(t, axis=-1, keepdims=True) for t in p])
        inv = 1.0 / denom
        c0 = inv[:tq]
        c1 = inv[tq:] * lam_ref[...]
        o = None
        for t, v_ref in zip(p, v_refs):
            w = (t[:tq] * c0 - t[tq:] * c1).astype(BF16)
            part = jnp.dot(w, v_ref[...], preferred_element_type=F32)
            o = part if o is None else o + part
        y = o * lax.rsqrt(jnp.mean(o * o, axis=-1, keepdims=True) + EPS) * g_ref[...]
        o_ref[rows, :] = (y * out_scale).astype(o_ref.dtype)


def _diff_attention(qr, kr, vb, lam, sub_g, *, q_row0, n_q, kv_segs, bsz, tq, out_scale, out_rows,
                    into=None):
    n_sub = 2 if n_q % (2 * tq) == 0 else 1
    tq = tq * n_sub
    nqb = n_q // tq
    n_seg = len(kv_segs)
    q_blk0 = q_row0 // tq

    def kv_spec(row0, length):
        return pl.BlockSpec((length, LANES), lambda b, h, i: (row0 // length + b, h))

    in_specs = [pl.BlockSpec((1, 1), lambda b, h, i: (0, 0)),
                pl.BlockSpec((tq, LANES), lambda b, h, i: (q_blk0 + b * nqb + i, h))]
    in_specs += [kv_spec(r0, ln) for r0, ln in kv_segs] * 2
    in_specs += [pl.BlockSpec((1, LANES), lambda b, h, i: (0, h))]
    args = [lam, qr] + [kr] * n_seg + [vb] * n_seg + [sub_g]
    kern = functools.partial(_attn_kernel, n_seg=n_seg, n_sub=n_sub, out_scale=out_scale)
    aliases = {}
    if into is not None:
        aliases = {len(args): 0}
        kern = functools.partial(_skip_refs, kern, len(args), 1)
        in_specs += [pl.BlockSpec(memory_space=pl.ANY)]
        args += [into]
    return pl.pallas_call(
        kern,
        out_shape=jax.ShapeDtypeStruct((out_rows, BRANCH_W), BF16),
        grid=(bsz, DA_HEADS, nqb),
        in_specs=in_specs,
        out_specs=pl.BlockSpec((tq, LANES), lambda b, h, i: (q_blk0 + b * nqb + i, h)),
        input_output_aliases=aliases,
        compiler_params=_cparams(("parallel", "parallel", "parallel")),
        name="diff_attention",
    )(*args)


def _ml_prep_kernel(x_ref, cw_ref, cb_ref, wq_ref, wk_ref, q_ref, k_ref, kt_ref):
    x = x_ref[...].astype(F32)
    n = x.shape[0]
    row = lax.broadcasted_iota(jnp.int32, x.shape, 0)
    x_prev = jnp.where(row == 0, 0.0, pltpu.roll(x, 1, 0))
    x_next = jnp.where(row == n - 1, 0.0, pltpu.roll(x, n - 1, 0))
    cw = cw_ref[...]
    y = x_prev * cw[0:1] + x * cw[1:2] + x_next * cw[2:3] + cb_ref[...]
    xc = (y * jax.nn.sigmoid(y)).astype(BF16)
    q_ref[...] = (jnp.dot(xc, wq_ref[...], preferred_element_type=F32) * (ML_HD ** -0.5)
                  ).astype(q_ref.dtype)
    k = jnp.dot(xc, wk_ref[...], preferred_element_type=F32)
    k_ref[...] = k.astype(k_ref.dtype)
    kt_ref[...] = k.T.astype(kt_ref.dtype)


def _ml_prep(z, conv_w, conv_b, wq, wk, *, row0, seq_len, bsz, into=None):
    rows = z.shape[0]
    blk0 = row0 // seq_len
    out = jax.ShapeDtypeStruct((rows, BRANCH_W), BF16)
    out_t = jax.ShapeDtypeStruct((BRANCH_W, rows), BF16)
    head_w = pl.BlockSpec((None, ML_HD, ML_HD), lambda b, h: (h, 0, 0))
    row_out = pl.BlockSpec((seq_len, LANES), lambda b, h: (blk0 + b, h))
    in_specs = [pl.BlockSpec((seq_len, LANES), lambda b, h: (blk0 + b, Z_XM // LANES + h)),
                pl.BlockSpec((3, LANES), lambda b, h: (0, h)),
                pl.BlockSpec((1, LANES), lambda b, h: (0, h)),
                head_w, head_w]
    args = [z, conv_w, conv_b, wq, wk]
    aliases = {}
    kern = _ml_prep_kernel
    if into is not None:
        n_in = len(args)
        in_specs += [pl.BlockSpec(memory_space=pl.ANY)] * len(into)
        args += list(into)
        aliases = {n_in + i: i for i in range(len(into))}
        kern = functools.partial(_skip_refs, _ml_prep_kernel, n_in, len(into))
    return pl.pallas_call(
        kern,
        out_shape=(out, out, out_t),
        grid=(bsz, ML_HEADS),
        in_specs=in_specs,
        out_specs=(row_out, row_out, pl.BlockSpec((LANES, seq_len), lambda b, h: (h, blk0 + b))),
        input_output_aliases=aliases,
        compiler_params=_cparams(("parallel", "parallel")),
        name="mlstm_prep",
    )(*args)


def _mlstm_kernel(qf_ref, kf_ref, ktf_ref, vf_ref, gcf_ref, grf_ref,
                  qb_ref, kb_ref, ktb_ref, vb_ref, gcb_ref, grb_ref,
                  hf_ref, hb_ref, c_ref, n_ref, m_ref):
    @pl.when(pl.program_id(1) == 0)
    def _():
        c_ref[...] = jnp.zeros_like(c_ref)
        n_ref[...] = jnp.zeros_like(n_ref)
        m_ref[...] = jnp.zeros_like(m_ref)

    t_idx = lax.broadcasted_iota(jnp.int32, (ML_CHUNK, ML_CHUNK), 0)
    s_idx = lax.broadcasted_iota(jnp.int32, (ML_CHUNK, ML_CHUNK), 1)
    nh = ML_HEADS
    dirs = ((qf_ref, kf_ref, ktf_ref, vf_ref, gcf_ref, grf_ref, hf_ref),
            (qb_ref, kb_ref, ktb_ref, vb_ref, gcb_ref, grb_ref, hb_ref))
    ones = jnp.ones((ML_CHUNK, LANES), BF16)
    units = []
    for d, (q_ref, k_ref, kt_ref, v_ref, gc_ref, gr_ref, h_ref) in enumerate(dirs):
        mask = s_idx <= t_idx if d == 0 else s_idx >= t_idx
        mask_f = mask.astype(F32)
        gc = gc_ref[...]
        gr = gr_ref[...]
        lf_c = _log_sigmoid(gc)
        lf_r = _log_sigmoid(gr)
        b_c = jnp.dot(mask_f, lf_c, preferred_element_type=F32, precision=lax.Precision.HIGHEST)
        b_r = lax.dot_general(lf_r, mask_f, (((1,), (1,)), ((), ())), preferred_element_type=F32,
                              precision=lax.Precision.HIGHEST)
        b_end_all = jnp.sum(lf_c, axis=0, keepdims=True)

        for h in range(nh):
            sl = slice(h * ML_HD, (h + 1) * ML_HD)
            q = q_ref[:, sl]
            k = k_ref[:, sl]
            kt = kt_ref[sl, :]
            v = v_ref[:, sl].astype(F32)
            li_c = gc[:, h:h + 1]
            li_r = gr[h:h + 1, :]
            bc = b_c[:, nh + h:nh + h + 1]
            br = b_r[nh + h:nh + h + 1, :]
            b_end = b_end_all[:, nh + h:nh + h + 1]
            m_prev = m_ref[d, h]
            c_prev = c_ref[d, h]
            n_prev = n_ref[d, h]
            g_c = b_end - bc + li_c
            g_r = b_end - br + li_r
            m_new = jnp.maximum(b_end + m_prev, jnp.max(g_r, axis=-1, keepdims=True))
            wg = jnp.broadcast_to(jnp.exp(g_c - m_new), (ML_CHUNK, LANES))
            decay = jnp.exp(b_end + m_prev - m_new)
            vw = (v * wg).astype(BF16)
            units.append(dict(
                d=d, h=h, sl=sl, h_ref=h_ref, v=v, mask=mask, m_prev=m_prev,
                bc=bc, br=br, li_r=li_r,
                qk=_nt_dot(q, k),
                qc=jnp.dot(q, c_prev.astype(BF16), preferred_element_type=F32),
                qn=jnp.dot(q, n_prev.astype(BF16), preferred_element_type=F32),
                c_new=decay * c_prev + jnp.dot(kt, vw, preferred_element_type=F32),
                n_new=decay * n_prev + jnp.dot(kt, wg.astype(BF16), preferred_element_type=F32),
                m_new=m_new))

    for u in units:
        dmat = jnp.where(u["mask"], u["bc"] - u["br"] + u["li_r"], -jnp.inf)
        inter = u["bc"] + u["m_prev"]
        m_t = jnp.maximum(inter, jnp.max(dmat, axis=-1, keepdims=True))
        s = (u["qk"] * jnp.exp(dmat - m_t)).astype(BF16)
        w_prev = jnp.exp(inter - m_t)
        num = jnp.dot(s, u["v"].astype(BF16), preferred_element_type=F32) + w_prev * u["qc"]
        den = jnp.dot(s, ones, preferred_element_type=F32) + w_prev * u["qn"]
        u["h_ref"][:, u["sl"]] = num / jnp.maximum(jnp.abs(den), jnp.exp(-m_t))
        c_ref[u["d"], u["h"]] = u["c_new"]
        n_ref[u["d"], u["h"]] = u["n_new"]
        m_ref[u["d"], u["h"]] = u["m_new"]


def _mlstm_scan(q_all, k_all, kt_all, z, gates_c, gates_r, *, bsz, seq, ctx_len):
    rows = q_all.shape[0]
    ncl = seq // ML_CHUNK
    ncc = ctx_len // ML_CHUNK
    lat_blocks = bsz * ncl
    nh = ML_HEADS

    def blk(d, b, s):
        jc = s if d == 0 else ncc - 1 - s
        jl = s - ncc if d == 0 else ncl - 1 - (s - ncc)
        return jnp.where(s < ncc, lat_blocks + b * ncc + jc, b * ncl + jl)

    def dir_specs(d):
        return [pl.BlockSpec((ML_CHUNK, BRANCH_W), lambda b, s: (blk(d, b, s), 0)),
                pl.BlockSpec((ML_CHUNK, BRANCH_W), lambda b, s: (blk(d, b, s), 0)),
                pl.BlockSpec((BRANCH_W, ML_CHUNK), lambda b, s: (0, blk(d, b, s))),
                pl.BlockSpec((ML_CHUNK, BRANCH_W), lambda b, s: (blk(d, b, s), Z_VM // BRANCH_W)),
                pl.BlockSpec((None, ML_CHUNK, 2 * nh), lambda b, s: (d, blk(d, b, s), 0)),
                pl.BlockSpec((None, 2 * nh, ML_CHUNK), lambda b, s: (d, 0, blk(d, b, s)))]

    out = jax.ShapeDtypeStruct((rows, BRANCH_W), F32)
    args = (q_all, k_all, kt_all, z, gates_c, gates_r)
    return pl.pallas_call(
        _mlstm_kernel,
        out_shape=(out, out),
        grid=(bsz, ncc + ncl),
        in_specs=dir_specs(0) + dir_specs(1),
        out_specs=(pl.BlockSpec((ML_CHUNK, BRANCH_W), lambda b, s: (blk(0, b, s), 0)),
                   pl.BlockSpec((ML_CHUNK, BRANCH_W), lambda b, s: (blk(1, b, s), 0))),
        scratch_shapes=[pltpu.VMEM((2, nh, ML_HD, ML_HD), F32),
                        pltpu.VMEM((2, nh, ML_HD, LANES), F32),
                        pltpu.VMEM((2, nh, 1, 1), F32)],
        compiler_params=_cparams(("parallel", "arbitrary")),
        name="mlstm_scan",
    )(*args, *args)


def _s5_kernel(uc_ref, ult_ref, m_ref, w_ref, v_ref, a1_ref, a2_ref, yc_ref, ylt_ref,
               ec_ref, el_ref, xc_ref, xl_ref, ul_ref, *, bsz):
    n_cl = ul_ref.shape[0] // bsz
    per_batch = []
    for b in range(bsz):
        cols = [ult_ref[:, (b * S5_T + t) * n_cl:(b * S5_T + t + 1) * n_cl].astype(F32)
                for t in range(S5_T)]
        per_batch.append(jnp.concatenate(cols, axis=0).T)
    ul_ref[...] = pltpu.einshape("bcf->(cb)f", jnp.stack(per_batch)).astype(BF16)

    w = w_ref[...]
    ec_ref[...] = jnp.dot(uc_ref[...], w, preferred_element_type=F32)
    el_ref[...] = jnp.dot(ul_ref[...], w, preferred_element_type=F32)
    a1 = a1_ref[...]
    a2 = a2_ref[...]
    half = LANES
    a1f, a2f, a1b, a2b = a1[:, 0:half], a2[:, 0:half], a1[:, half:], a2[:, half:]

    def scan(e_ref, x_ref, state):
        n_chunks = e_ref.shape[0] // bsz

        def body(j, st):
            xf, xfs, xb, xbs = st
            rf = pl.multiple_of(j * bsz, bsz)
            rb = pl.multiple_of((n_chunks - 1 - j) * bsz, bsz)
            x_ref[pl.ds(rf, bsz), 0:half] = xf
            x_ref[pl.ds(rb, bsz), half:2 * half] = xb
            ef, eb = e_ref[pl.ds(rf, bsz), 0:half], e_ref[pl.ds(rb, bsz), half:2 * half]
            efs = e_ref[pl.ds(rf, bsz), 2 * half:3 * half]
            ebs = e_ref[pl.ds(rb, bsz), 3 * half:4 * half]
            return (a1f * xf + a2f * xfs + ef, a1f * xfs - a2f * xf + efs,
                    a1b * xb + a2b * xbs + eb, a1b * xbs - a2b * xb + ebs)

        return lax.fori_loop(0, n_chunks, body, state)

    zero = jnp.zeros((bsz, half), F32)
    state = scan(ec_ref, xc_ref, (zero, zero, zero, zero))
    scan(el_ref, xl_ref, state)
    mm = m_ref[...]
    vv = v_ref[...]
    yc_ref[...] = (jnp.dot(uc_ref[...], mm, preferred_element_type=F32)
                   + jnp.dot(xc_ref[...].astype(BF16), vv, preferred_element_type=F32)
                   ).astype(yc_ref.dtype)
    yl = (jnp.dot(ul_ref[...], mm, preferred_element_type=F32)
          + jnp.dot(xl_ref[...].astype(BF16), vv, preferred_element_type=F32))
    yl = pltpu.einshape("(cb)f->bcf", yl, b=bsz)
    for b in range(bsz):
        ybt = yl[b].T
        for t in range(S5_T):
            ylt_ref[:, (b * S5_T + t) * n_cl:(b * S5_T + t + 1) * n_cl] = (
                ybt[t * S5_P:(t + 1) * S5_P, :].astype(ylt_ref.dtype))


def _s5_pack_kernel(x_ref, o_ref, f_ref):
    n_cl = x_ref.shape[0] // S5_T
    for j in range(f_ref.shape[0]):
        cs = slice(j * LANES, (j + 1) * LANES)
        f_ref[j] = x_ref[:, cs].astype(F32)
        for t in range(S5_T):
            o_ref[cs, t * n_cl:(t + 1) * n_cl] = (
                f_ref[j, pl.ds(t, n_cl, stride=S5_T), :].T.astype(o_ref.dtype))


def _s5_unpack_kernel(x_ref, o_ref, f_ref):
    n_cl = o_ref.shape[0] // S5_T
    for j in range(f_ref.shape[0]):
        cs = slice(j * LANES, (j + 1) * LANES)
        for t in range(S5_T):
            f_ref[j, pl.ds(t, n_cl, stride=S5_T), :] = (
                x_ref[cs, t * n_cl:(t + 1) * n_cl].astype(F32).T)
        o_ref[:, cs] = f_ref[j].astype(o_ref.dtype)


def _s5_pack(z, *, bsz, seq):
    return pl.pallas_call(
        _s5_pack_kernel,
        out_shape=jax.ShapeDtypeStruct((BRANCH_W, bsz * seq), BF16),
        grid=(bsz,),
        in_specs=[pl.BlockSpec((seq, BRANCH_W), lambda b: (b, Z_U // BRANCH_W))],
        out_specs=pl.BlockSpec((BRANCH_W, seq), lambda b: (0, b)),
        scratch_shapes=[pltpu.VMEM((BRANCH_W // LANES, seq, LANES), F32)],
        compiler_params=_cparams(("parallel",)),
        name="s5_pack",
    )(z)


def _s5_unpack(yt, *, bsz, seq, out_rows):
    in_specs = [pl.BlockSpec((BRANCH_W, seq), lambda b: (0, b))]
    args = [yt]
    kern, aliases = _s5_unpack_kernel, {}
    if out_rows > bsz * seq:
        in_specs.append(pl.BlockSpec(memory_space=pl.ANY))
        args.append(jnp.zeros((out_rows, BRANCH_W), BF16))
        kern, aliases = functools.partial(_skip_refs, _s5_unpack_kernel, 1, 1), {1: 0}
    return pl.pallas_call(
        kern,
        out_shape=jax.ShapeDtypeStruct((out_rows, BRANCH_W), BF16),
        grid=(bsz,),
        in_specs=in_specs,
        out_specs=pl.BlockSpec((seq, BRANCH_W), lambda b: (b, 0)),
        scratch_shapes=[pltpu.VMEM((BRANCH_W // LANES, seq, LANES), F32)],
        input_output_aliases=aliases,
        compiler_params=_cparams(("parallel",)),
        name="s5_unpack",
    )(*args)


def _s5_tables(lam_re, lam_im, log_dt, b_re, b_im, c_re, c_im, d_skip):
    t, g, n, p = S5_T, S5_G, S5_N, S5_P
    dt = jnp.exp(log_dt)[..., None]
    den = lam_re * lam_re + lam_im * lam_im
    kk = jnp.arange(t + 1, dtype=F32)[None, None, :, None]
    mag = jnp.exp(kk * (lam_re * dt)[:, :, None, :])
    ang = kk * (lam_im * dt)[:, :, None, :]
    p_re, p_im = mag * jnp.cos(ang), mag * jnp.sin(ang)
    ab_re, ab_im = p_re[:, :, 1], p_im[:, :, 1]
    z_re = ((ab_re - 1.0) * lam_re + ab_im * lam_im) / den
    z_im = (ab_im * lam_re - (ab_re - 1.0) * lam_im) / den
    bt_re, bt_im = jnp.swapaxes(b_re, 2, 3), jnp.swapaxes(b_im, 2, 3)
    bb_re = z_re[:, :, None, :] * bt_re - z_im[:, :, None, :] * bt_im
    bb_im = z_re[:, :, None, :] * bt_im + z_im[:, :, None, :] * bt_re
    pk_re, pk_im = p_re[:, :, :t, None, :], p_im[:, :, :t, None, :]
    abk_re = pk_re * bb_re[:, :, None] - pk_im * bb_im[:, :, None]
    abk_im = pk_re * bb_im[:, :, None] + pk_im * bb_re[:, :, None]
    taps = (jnp.einsum('rgpn,rgkqn->rgqkp', c_re, abk_re)
            - jnp.einsum('rgpn,rgkqn->rgqkp', c_im, abk_im))
    base_f = taps[0].reshape(g, p, t * p)
    base_b = taps[1][:, :, ::-1].reshape(g, p, t * p)
    width = t * p
    m_f = jnp.stack([jnp.pad(base_f, ((0, 0), (0, 0), (p * s, 0)))[..., :width]
                     for s in range(t)], axis=1)
    m_b = jnp.stack([jnp.pad(base_b, ((0, 0), (0, 0), (0, p * (t - 1 - s))))[..., p * (t - 1 - s):]
                     for s in range(t)], axis=1)
    skip = jnp.eye(width, dtype=F32)[None] * jnp.tile(d_skip.reshape(g, p), (1, t))[:, None, :]
    m = (m_f + m_b).reshape(g, width, width) + skip
    wf_re, wf_im = (a[0][:, ::-1].reshape(g, width, n) for a in (abk_re, abk_im))
    wb_re, wb_im = (a[1].reshape(g, width, n) for a in (abk_re, abk_im))
    w = jnp.concatenate([wf_re, wf_im, wb_re, wb_im, wf_im, wf_re, wb_im, wb_re], axis=-1)
    ct_re, ct_im = jnp.swapaxes(c_re, 2, 3), jnp.swapaxes(c_im, 2, 3)
    pn_re, pn_im = jnp.swapaxes(p_re, 2, 3)[..., None], jnp.swapaxes(p_im, 2, 3)[..., None]
    ca_re = ct_re[:, :, :, None, :] * pn_re - ct_im[:, :, :, None, :] * pn_im
    ca_im = ct_re[:, :, :, None, :] * pn_im + ct_im[:, :, :, None, :] * pn_re
    v = jnp.concatenate([ca_re[0][:, :, 1:].reshape(g, n, width),
                         -ca_im[0][:, :, 1:].reshape(g, n, width),
                         ca_re[1][:, :, :0:-1].reshape(g, n, width),
                         -ca_im[1][:, :, :0:-1].reshape(g, n, width)], axis=1)
    at_re, at_im = p_re[:, :, t], p_im[:, :, t]
    a1 = jnp.concatenate([at_re[0], at_re[0], at_re[1], at_re[1]], axis=-1)[:, None, :]
    a2 = jnp.concatenate([-at_im[0], at_im[0], -at_im[1], at_im[1]], axis=-1)[:, None, :]
    return m.astype(BF16), w.astype(BF16), v.astype(BF16), a1, a2


def _s5_rows(u, bsz, length):
    nc = length // S5_T
    u = u.reshape(bsz, nc, S5_T, S5_G, S5_P)
    return jnp.transpose(u, (3, 1, 0, 2, 4)).reshape(S5_G, nc * bsz, S5_T * S5_P)


def _s5_unrows(y, bsz, length):
    nc = length // S5_T
    y = y.reshape(S5_G, nc, bsz, S5_T, S5_P)
    return jnp.transpose(y, (2, 1, 3, 0, 4)).reshape(bsz * length, BRANCH_W)


def _s5_mix(u_ctx, ut_lat, tables, *, bsz):
    m, w, v, a1, a2 = tables
    rc = u_ctx.shape[1]
    t_lat = ut_lat.shape[1]
    rl = t_lat // S5_T
    wd = S5_T * S5_P

    def rows_spec(r):
        return pl.BlockSpec((None, r, wd), lambda g: (g, 0, 0))

    lat_spec = pl.BlockSpec((S5_P, t_lat), lambda g: (g, 0))
    sq = pl.BlockSpec((None, wd, wd), lambda g: (g, 0, 0))
    vec = pl.BlockSpec((None, 1, wd), lambda g: (g, 0, 0))
    return pl.pallas_call(
        functools.partial(_s5_kernel, bsz=bsz),
        out_shape=(jax.ShapeDtypeStruct((S5_G, rc, wd), BF16),
                   jax.ShapeDtypeStruct((BRANCH_W, t_lat), BF16)),
        grid=(S5_G,),
        in_specs=[rows_spec(rc), lat_spec, sq,
                  pl.BlockSpec((None, wd, 2 * wd), lambda g: (g, 0, 0)), sq, vec, vec],
        out_specs=(rows_spec(rc), lat_spec),
        scratch_shapes=[pltpu.VMEM((rc, 2 * wd), F32), pltpu.VMEM((rl, 2 * wd), F32),
                        pltpu.VMEM((rc, wd), F32), pltpu.VMEM((rl, wd), F32),
                        pltpu.VMEM((rl, wd), BF16)],
        compiler_params=_cparams(("parallel",)),
        name="s5_mix",
    )(u_ctx, ut_lat, m, w, v, a1, a2)


def _merge_kernel(x_ref, ya_ref, hf_ref, hb_ref, om_ref, ys_ref, gate_ref, g1_ref, mlg_ref,
                  gluw_ref, glub_ref, wbr_ref, wout_ref, o_ref):
    hsum = hf_ref[...] + hb_ref[...]
    og = jax.nn.sigmoid(om_ref[...].astype(F32))
    mlg = mlg_ref[...]
    yb_parts = []
    for h in range(ML_HEADS):
        sl = slice(h * ML_HD, (h + 1) * ML_HD)
        hh = hsum[:, sl]
        hn = hh * lax.rsqrt(jnp.mean(hh * hh, axis=-1, keepdims=True) + EPS) * mlg[:, sl]
        yb_parts.append(hn * og[:, sl])
    yb = jnp.concatenate(yb_parts, axis=-1)
    gl = _gelu(ys_ref[...].astype(F32))
    ys = gl * jax.nn.sigmoid(
        jnp.dot(gl.astype(BF16), gluw_ref[...], preferred_element_type=F32) + glub_ref[...])
    d = o_ref.shape[1]
    merged = None
    for r, y in enumerate((ya_ref[...], yb, ys)):
        proj = jnp.dot(y.astype(BF16), wbr_ref[r], preferred_element_type=F32)
        term = jax.nn.sigmoid(gate_ref[:, r * d:(r + 1) * d].astype(F32)) * proj
        merged = term if merged is None else merged + term
    y = jnp.dot(merged.astype(BF16), wout_ref[...], preferred_element_type=F32)
    o_ref[...] = x_ref[...] + g1_ref[...] * y


def _merge(x_all, ya, h_dirs, z, ys, g1, ml_norm_g, glu_w, glu_b, w_br, w_out, *, n_rows, bsz, seq, tm):
    d = x_all.shape[1]
    nlb = bsz * seq // tm
    bpb = seq // tm
    w = BRANCH_W

    def rows(width, col_blk=0):
        return pl.BlockSpec((tm, width), lambda i: (i, col_blk))

    def full(shape):
        return pl.BlockSpec(shape, lambda i: (0,) * len(shape))

    return pl.pallas_call(
        _merge_kernel,
        out_shape=jax.ShapeDtypeStruct((n_rows, d), F32),
        grid=(n_rows // tm,),
        in_specs=[rows(d), rows(w), rows(w), rows(w),
                  rows(w, Z_OM // w), rows(w),
                  rows(3 * d, Z_GATE // (3 * d)),
                  pl.BlockSpec((None, 1, d), lambda i: (_row_batch(i, nlb, bpb, bsz), 0, 0)),
                  full((1, w)), full((w, w)), full((1, w)), full((3, w, d)), full((d, d))],
        out_specs=rows(d),
        compiler_params=_cparams(("parallel",)),
        name="merge",
    )(x_all, ya, h_dirs[0], h_dirs[1], z, ys, z, g1, ml_norm_g, glu_w, glu_b, w_br, w_out)


NOT_TOP = 127.0


def _take_top(s, n_take, vals_ref, want_rank):
    rows = s.shape[0]
    ridx = lax.broadcasted_iota(jnp.int32, s.shape, 0).astype(F32)
    rank = jnp.full(s.shape, NOT_TOP, F32) if want_rank else None
    for i in range(n_take):
        mx = jnp.max(s, axis=0, keepdims=True)
        first = jnp.min(jnp.where(s == mx, ridx, float(rows)), axis=0, keepdims=True)
        hit = ridx == first
        s = jnp.where(hit, -jnp.inf, s)
        if want_rank:
            rank = jnp.where(hit, float(i), rank)
        vals_ref[i:i + 1, :] = mx
    return s, rank


SUBLANES = 8


def _sort_network_pairs(n):
    pairs = []
    p = 1
    while p < n:
        k = p
        while k >= 1:
            for j in range(k % p, n - k, 2 * k):
                for i in range(min(k, n - j - k)):
                    if (i + j) // (2 * p) == (i + j + k) // (2 * p):
                        pairs.append((i + j, i + j + k))
            k //= 2
        p *= 2
    return pairs


def _top_sorted(s):
    n = PEER_TOPK
    x = _slabs(s)
    assert len(x) <= n
    x = x + [jnp.full(x[0].shape, -jnp.inf, F32)] * (n - len(x))

    def exchange(i, j):
        x[i], x[j] = jnp.maximum(x[i], x[j]), jnp.minimum(x[i], x[j])

    for i, j in _sort_network_pairs(n):
        exchange(i, j)
    shift = SUBLANES // 2
    while shift >= 1:
        y = [pltpu.roll(v, shift, 0) for v in x]
        x = [jnp.maximum(x[i], y[n - 1 - i]) for i in range(n)]
        d = n // 2
        while d >= 1:
            for i in range(n):
                if not i & d:
                    exchange(i, i + d)
            d //= 2
        shift //= 2
    return x


def _slabs(s):
    return [s[SUBLANES * i:SUBLANES * (i + 1), :] for i in range(s.shape[0] // SUBLANES)]


def _tie_count(s, top):
    flag = jnp.zeros(top[0].shape, F32)
    for i in range(len(top) - 1):
        flag = flag + jnp.where(top[i] == top[i + 1], 1.0, 0.0)
    reach = functools.reduce(jnp.add, [jnp.where(v >= top[-1], 1.0, 0.0) for v in _slabs(s)])
    reach = jnp.sum(reach, axis=0, keepdims=True)
    return flag + jnp.where(reach != float(PEER_TOPK), 1.0, 0.0)


def _candidates(v1_ref, v2_ref):
    k = PEER_TOPK
    v2_head = v2_ref[0:8, :]
    jrow = lax.broadcasted_iota(jnp.int32, v2_head.shape, 0)
    pieces = [v1_ref[0:1, :] + v2_ref[...]]
    for i in range(1, 8):
        pieces.append(jnp.where(jrow < k // (i + 1), v1_ref[i:i + 1, :] + v2_head, -jnp.inf))
    pieces.append(v1_ref[8:16, :] + v2_ref[0:1, :])
    return jnp.concatenate(pieces, axis=0)


def _write_counts(picked, cnt_ref):
    cnt_ref[0:1, :] = jnp.sum(picked[0:16], axis=0, keepdims=True)
    for i in range(1, 8):
        cnt_ref[i:i + 1, :] = jnp.sum(picked[8 + 8 * i:16 + 8 * i], axis=0, keepdims=True)
    cnt_ref[8:16, :] = picked[72:80]


def _candidate_counts(v1_ref, v2_ref, top_ref, cnt_ref):
    cand = _candidates(v1_ref, v2_ref)
    cand_left, _ = _take_top(cand, PEER_TOPK, top_ref, False)
    _write_counts(jnp.where(cand_left != cand, 1.0, 0.0), cnt_ref)
    top = top_ref[...]
    return jnp.sum(jnp.exp(top - top[0:1, :]), axis=0, keepdims=True)


def _peer_topk_kernel(q_ref, k1_ref, k2_ref, e1_ref, cnt1_ref, e2_ref, rank2_ref,
                      v1_ref, v2_ref, top_ref, cnt_ref):
    k = PEER_TOPK
    q = q_ref[...].astype(BF16)
    s1 = _nt_dot(k1_ref[...], q[:, :LANES])
    s2 = _nt_dot(k2_ref[...], q[:, LANES:])
    top1 = _top_sorted(s1)
    top2 = _top_sorted(s2)
    for i in range(k):
        v1_ref[i:i + 1, :] = top1[i][0:1, :]
        v2_ref[i:i + 1, :] = top2[i][0:1, :]
    cand = _candidates(v1_ref, v2_ref)
    topc = _top_sorted(cand)
    any_tie = jnp.max(_tie_count(s1, top1) + _tie_count(s2, top2) + _tie_count(cand, topc)) > 0.0

    @pl.when(any_tie)
    def _():
        _, rank1 = _take_top(s1, k, v1_ref, True)
        _, rank2 = _take_top(s2, k, v2_ref, True)
        zsum = _candidate_counts(v1_ref, v2_ref, top_ref, cnt_ref)
        cnt1 = jnp.zeros(s1.shape, F32)
        for i in range(k):
            cnt1 = jnp.where(rank1 == float(i), cnt_ref[i:i + 1, :], cnt1)
        e1_ref[...] = jnp.where(rank1 < k, jnp.exp(s1 - v1_ref[0:1, :]), 0.0) / zsum
        cnt1_ref[...] = cnt1
        e2_ref[...] = jnp.where(rank2 < k, jnp.exp(s2 - v2_ref[0:1, :]), 0.0).astype(BF16)
        rank2_ref[...] = rank2.astype(BF16)

    @pl.when(jnp.logical_not(any_tie))
    def _():
        _write_counts(jnp.where(cand >= topc[k - 1][0:1, :], 1.0, 0.0), cnt_ref)
        zsum = functools.reduce(jnp.add, [jnp.exp(t[0:1, :] - topc[0][0:1, :]) for t in topc])
        cnt_rows = [jnp.broadcast_to(cnt_ref[i:i + 1, :], top1[0].shape) for i in range(k)]
        cnt1, rank2 = [], []
        for v in _slabs(s1):
            c = jnp.zeros(v.shape, F32)
            for i in range(k):
                c = jnp.where(v == top1[i], cnt_rows[i], c)
            cnt1.append(c)
        for v in _slabs(s2):
            r = jnp.full(v.shape, NOT_TOP, F32)
            for i in range(k):
                r = jnp.where(v == top2[i], float(i), r)
            rank2.append(r)
        e1_ref[...] = jnp.where(s1 >= v1_ref[k - 1:k, :], jnp.exp(s1 - v1_ref[0:1, :]), 0.0) / zsum
        cnt1_ref[...] = jnp.concatenate(cnt1, axis=0)
        e2_ref[...] = jnp.where(s2 >= v2_ref[k - 1:k, :], jnp.exp(s2 - v2_ref[0:1, :]), 0.0
                                ).astype(BF16)
        rank2_ref[...] = jnp.concatenate(rank2, axis=0).astype(BF16)


def _peer_topk(q, sub_k, *, n_rows, tm):
    nk = PEER_NKEYS
    row_tab = jax.ShapeDtypeStruct((n_rows // tm, PEER_HEADS, nk, tm), F32)
    tile_tab = jax.ShapeDtypeStruct((n_rows // tm, PEER_HEADS, nk, tm), BF16)
    tab_spec = pl.BlockSpec((None, None, nk, tm), lambda i, h: (i, h, 0, 0))
    return pl.pallas_call(
        _peer_topk_kernel,
        out_shape=(row_tab, row_tab, tile_tab, tile_tab),
        grid=(n_rows // tm, PEER_HEADS),
        in_specs=[pl.BlockSpec((tm, 2 * LANES), lambda i, h: (i, h)),
                  pl.BlockSpec((None, nk, LANES), lambda i, h: (0, 0, 0)),
                  pl.BlockSpec((None, nk, LANES), lambda i, h: (1, 0, 0))],
        out_specs=(tab_spec,) * 4,
        scratch_shapes=[pltpu.VMEM((PEER_TOPK, tm), F32)] * 4,
        compiler_params=_cparams(("parallel", "parallel")),
        name="peer_topk",
    )(q, sub_k, sub_k)


GATE_KEYS = 2
KEY_GROUP = 8


def _peer_dense_kernel(ht_ref, u_ref, vt_ref, e1_ref, cnt1_ref, e2_ref, rank2_ref,
                       x_ref, g2_ref, fg_ref, o_ref, acc_ref, act_ref, p_ref, e2s_ref, rank2s_ref, *,
                       a_per_blk, final_norm):
    j = pl.program_id(1)
    n_blk = pl.num_programs(1) - 1
    cur = j % 2
    tm = ht_ref.shape[1]
    nk = PEER_NKEYS

    @pl.when(j == 0)
    def _():
        acc_ref[...] = jnp.zeros_like(acc_ref)
        p_ref[1] = jnp.zeros(p_ref.shape[1:], p_ref.dtype)
        e2s_ref[...] = e2_ref[...]
        rank2s_ref[...] = rank2_ref[...]

    @pl.when(j < n_blk)
    def _():
        act_ref[...] = _gelu(jnp.dot(u_ref[...], ht_ref[...], preferred_element_type=F32)
                             ).astype(BF16)
        acc_ref[...] += jnp.dot(vt_ref[...], p_ref[1 - cur], preferred_element_type=F32)
        a0 = pl.multiple_of(j * a_per_blk, KEY_GROUP)

        def row_bf16(ref, h, i, ls):
            grp = (i // KEY_GROUP) * KEY_GROUP
            w = ref[h, pl.ds(a0 + grp, KEY_GROUP), ls][i % KEY_GROUP:i % KEY_GROUP + 1]
            return jnp.broadcast_to(w, (nk, LANES)).astype(BF16)

        for lt in range(tm // LANES):
            ls = slice(lt * LANES, (lt + 1) * LANES)
            for i0 in range(0, a_per_blk, GATE_KEYS):
                gates = [None] * GATE_KEYS
                for h in range(PEER_HEADS):
                    for ii in range(GATE_KEYS):
                        lead = row_bf16(cnt1_ref, h, i0 + ii, ls) - rank2s_ref[h, :, ls]
                        term = jnp.minimum(jnp.maximum(lead, 0.0),
                                           row_bf16(e1_ref, h, i0 + ii, ls)) * e2s_ref[h, :, ls]
                        gates[ii] = term if gates[ii] is None else gates[ii] + term
                for ii in range(GATE_KEYS):
                    r0 = (i0 + ii) * nk
                    p_ref[cur, r0:r0 + nk, ls] = gates[ii] * act_ref[r0:r0 + nk, ls]

    @pl.when(j == n_blk)
    def _():
        acc = acc_ref[...] + jnp.dot(vt_ref[...], p_ref[1 - cur], preferred_element_type=F32)
        y = x_ref[...] + g2_ref[...] * acc.T
        if final_norm:
            y = y * lax.rsqrt(jnp.mean(y * y, axis=-1, keepdims=True) + EPS) * fg_ref[...]
        o_ref[...] = y


def _peer_dense(h2, u_tab, vt_tab, tabs, x_all, g2, final_g, *, n_rows, bsz, seq, tm, te,
                final_norm):
    d = x_all.shape[1]
    n_exp = u_tab.shape[0]
    nlb = bsz * seq // tm
    bpb = seq // tm
    nk = PEER_NKEYS
    n_blk = n_exp // te
    assert (te // nk) % KEY_GROUP == 0
    tab_spec = pl.BlockSpec((None, PEER_HEADS, nk, tm), lambda i, j: (i, 0, 0, 0))
    return pl.pallas_call(
        functools.partial(_peer_dense_kernel, a_per_blk=te // nk, final_norm=final_norm),
        out_shape=jax.ShapeDtypeStruct((n_rows, d), F32),
        grid=(n_rows // tm, n_blk + 1),
        in_specs=[pl.BlockSpec((d, tm), lambda i, j: (0, i)),
                  pl.BlockSpec((te, d), lambda i, j: (jnp.minimum(j, n_blk - 1), 0)),
                  pl.BlockSpec((None, d, te), lambda i, j: (jnp.maximum(j - 1, 0), 0, 0)),
                  tab_spec, tab_spec, tab_spec, tab_spec,
                  pl.BlockSpec((tm, d), lambda i, j: (i, 0)),
                  pl.BlockSpec((None, 1, d), lambda i, j: (_row_batch(i, nlb, bpb, bsz), 0, 0)),
                  pl.BlockSpec((1, d), lambda i, j: (0, 0))],
        out_specs=pl.BlockSpec((tm, d), lambda i, j: (i, 0)),
        scratch_shapes=[pltpu.VMEM((d, tm), F32), pltpu.VMEM((te, tm), BF16),
                        pltpu.VMEM((2, te, tm), BF16),
                        pltpu.VMEM((PEER_HEADS, nk, tm), BF16), pltpu.VMEM((PEER_HEADS, nk, tm), BF16)],
        compiler_params=_cparams(("parallel", "arbitrary")),
        name="peer_dense",
    )(h2, u_tab, vt_tab, *tabs, x_all, g2, final_g)


def _transpose_cast_kernel(x_ref, o_ref):
    o_ref[...] = x_ref[...].T.astype(o_ref.dtype)


def _transpose_cast(w, layer, *, tr):
    _, rows, cols = w.shape
    return pl.pallas_call(
        _transpose_cast_kernel,
        out_shape=jax.ShapeDtypeStruct((rows // tr, cols, tr), BF16),
        grid=(rows // tr,),
        in_specs=[pl.BlockSpec((None, tr, cols), lambda i: (layer, i, 0))],
        out_specs=pl.BlockSpec((None, cols, tr), lambda i: (i, 0, 0)),
        compiler_params=_cparams(("parallel",)),
        name="transpose_cast",
    )(w)


def _rope_tables(seq, tm):
    rows = seq // GRID_W
    n_freq = DA_HD // 4
    inv = ROPE_BASE ** (-jnp.arange(n_freq, dtype=F32) / n_freq)
    r = jnp.repeat(jnp.arange(rows, dtype=F32), GRID_W)
    col = jnp.tile(jnp.arange(GRID_W, dtype=F32), rows)
    ang = jnp.concatenate([r[:, None] * inv, col[:, None] * inv], axis=-1)
    cos, sin = jnp.cos(ang), jnp.sin(ang)
    cos_t = jnp.tile(cos, (1, 4))
    sin_t = jnp.tile(jnp.concatenate([-sin, sin], axis=-1), (1, 2))
    ident = jnp.ones((tm, LANES), F32)
    return (jnp.concatenate([cos_t, ident], axis=0),
            jnp.concatenate([sin_t, jnp.zeros((tm, LANES), F32)], axis=0))


def kernel(x, c, ctx, c_ctx, w_mod, b_mod, norm1_g, norm2_g, w_in, da_lam_q1, da_lam_k1, da_lam_q2, da_lam_k2, da_sub_g, ml_conv_w, ml_conv_b, ml_wq, ml_wk, ml_gate_b, ml_norm_g, s5_lam_re, s5_lam_im, s5_log_dt, s5_b_re, s5_b_im, s5_c_re, s5_c_im, s5_d, s5_glu_w, s5_glu_b, w_br, w_out, peer_wq, peer_sub_k, peer_u, peer_v, final_g):
    bsz, seq, d = x.shape
    ctx_len = ctx.shape[1]
    depth = w_in.shape[0]
    t_lat = bsz * seq
    t_ctx = bsz * ctx_len
    t_all = t_lat + t_ctx
    tm = math.gcd(512, math.gcd(seq, t_ctx))
    tm_merge = min(tm, 256)
    tq = min(256, ctx_len)
    nh = ML_HEADS

    x_all = jnp.concatenate([x.reshape(t_lat, d), ctx.reshape(t_ctx, d)], axis=0)
    pad = (-(bsz + 1)) % 8
    c_all = jnp.concatenate([c, c_ctx[None], jnp.zeros((pad, d), F32)], axis=0)
    cos_tab, sin_tab = _rope_tables(seq, tm)

    for l in range(depth):
        need_ctx = l < depth - 1
        n_rows = t_all if need_ctx else t_lat
        lam_init = 0.8 - 0.6 * math.exp(-0.3 * l)

        mods = _modulation(c_all, w_mod[l].astype(BF16), b_mod[l][None])
        sh1, sc1, g1, sh2, sc2, g2 = [mods[:bsz + 1, i * d:(i + 1) * d][:, None, :]
                                      for i in range(N_MOD)]

        wi = w_in[l]
        i_q, i_k, i_v, i_xm, i_vm, i_om, i_g, i_u, i_gate = (
            0, 512, 1024, 1536, 2048, 2560, 3072, 3088, 3600)
        w_main = jnp.concatenate([wi[:, i_gate:], wi[:, :i_g], wi[:, i_u:i_gate]], axis=1).astype(BF16)
        w_gate = jnp.pad(wi[:, i_g:i_u], ((0, 0), (0, LANES - (i_u - i_g)))).astype(BF16)
        n1 = norm1_g[l][None]
        z = _modnorm_matmul(x_all, t_all, n1, sc1, sh1, w_main, bsz=bsz, seq=seq, tm=tm,
                            tn=w_main.shape[1] // 2, out_dtype=BF16)
        zg = _modnorm_matmul(x_all, t_all, n1, sc1, sh1, w_gate, bsz=bsz, seq=seq, tm=tm, tn=LANES)

        qr, kr, vb = _rope_qkv(z, cos_tab, sin_tab, bsz=bsz, seq=seq, tm=tm)
        lam = (jnp.exp(jnp.sum(da_lam_q1[l] * da_lam_k1[l]))
               - jnp.exp(jnp.sum(da_lam_q2[l] * da_lam_k2[l])) + lam_init).reshape(1, 1)
        sub_g = da_sub_g[l][None]
        ya = _diff_attention(qr, kr, vb, lam, sub_g, q_row0=0, n_q=seq,
                             kv_segs=[(0, seq), (t_lat, ctx_len)], bsz=bsz, tq=tq,
                             out_scale=1.0 - lam_init, out_rows=n_rows,
                             into=jnp.zeros((n_rows, BRANCH_W), BF16) if need_ctx else None)
        if need_ctx:
            ya = _diff_attention(qr, kr, vb, lam, sub_g, q_row0=t_lat, n_q=ctx_len,
                                 kv_segs=[(t_lat, ctx_len)], bsz=bsz, tq=tq,
                                 out_scale=1.0 - lam_init, out_rows=n_rows, into=ya)

        wq_b, wk_b = ml_wq[l].astype(BF16), ml_wk[l].astype(BF16)
        cb = ml_conv_b[l][None]
        blank = jnp.zeros((t_all, BRANCH_W), BF16)
        qkt = _ml_prep(z, ml_conv_w[l], cb, wq_b, wk_b, row0=0, seq_len=seq, bsz=bsz,
                       into=(blank, blank, blank.T))
        q_ml, k_ml, kt_ml = _ml_prep(z, ml_conv_w[l], cb, wq_b, wk_b, row0=t_lat, seq_len=ctx_len,
                                     bsz=bsz, into=qkt)
        gates = (zg[:, :4 * nh] + ml_gate_b[l]).reshape(t_all, 2, 2 * nh)
        gates_c = jnp.transpose(gates, (1, 0, 2))
        gates_r = jnp.transpose(gates, (1, 2, 0))
        h_dirs = _mlstm_scan(q_ml, k_ml, kt_ml, z, gates_c, gates_r, bsz=bsz, seq=seq,
                             ctx_len=ctx_len)

        tables = _s5_tables(s5_lam_re[l], s5_lam_im[l], s5_log_dt[l], s5_b_re[l], s5_b_im[l],
                            s5_c_re[l], s5_c_im[l], s5_d[l])
        u_ctx = _s5_rows(z[t_lat:, Z_U:Z_U + BRANCH_W], bsz, ctx_len)
        y_c, yt_l = _s5_mix(u_ctx, _s5_pack(z, bsz=bsz, seq=seq), tables, bsz=bsz)
        ys = _s5_unpack(yt_l, bsz=bsz, seq=seq, out_rows=n_rows)
        if need_ctx:
            ys = lax.dynamic_update_slice(ys, _s5_unrows(y_c, bsz, ctx_len), (t_lat, 0))

        x_all = _merge(x_all, ya, h_dirs, z, ys, g1, ml_norm_g[l][None],
                       s5_glu_w[l].astype(BF16), s5_glu_b[l][None], w_br[l].astype(BF16),
                       w_out[l].astype(BF16), n_rows=n_rows, bsz=bsz, seq=seq, tm=tm_merge)

        n2 = norm2_g[l][None]
        wq_p = peer_wq[l].astype(BF16)
        pq, h2t = _modnorm_matmul(x_all, n_rows, n2, sc2, sh2, wq_p, bsz=bsz, seq=seq, tm=tm,
                                  tn=wq_p.shape[1], emit_h=True)
        tabs = _peer_topk(pq, peer_sub_k[l].astype(BF16), n_rows=n_rows, tm=tm)
        te = 2048
        x_all = _peer_dense(h2t, peer_u[l].astype(BF16), _transpose_cast(peer_v, l, tr=te),
                            tabs, x_all, g2, final_g[None], n_rows=n_rows, bsz=bsz, seq=seq, tm=tm,
                            te=te, final_norm=not need_ctx)

    return x_all.reshape(bsz, seq, d)
```

```python
import functools
import math

import jax
import jax.numpy as jnp
from jax import lax
from jax.experimental import pallas as pl
from jax.experimental.pallas import tpu as pltpu

F32 = jnp.float32
BF16 = jnp.bfloat16

EPS = 1e-6
LOG2E = 1.4426950408889634
N_MOD = 6
BRANCH_W = 512
GRID_W = 64
ROPE_BASE = 10000.0
DA_HEADS = 4
DA_HD = 64
DA_VD = 2 * DA_HD
ML_HEADS = 4
ML_HD = BRANCH_W // ML_HEADS
ML_CHUNK = 128
S5_P = 16
S5_G = BRANCH_W // S5_P
S5_N = 64
S5_T = 16
PEER_HEADS = 8
PEER_NKEYS = 128
PEER_TOPK = 16
LANES = 128
VMEM_LIMIT = 56 * 1024 * 1024

Z_GATE = 0
Z_Q, Z_K, Z_V, Z_XM, Z_VM, Z_OM, Z_U = (3072 + i * BRANCH_W for i in range(7))


def _cparams(sem):
    return pltpu.CompilerParams(dimension_semantics=sem, vmem_limit_bytes=VMEM_LIMIT)


def _nt_dot(a, b):
    return lax.dot_general(a, b, (((1,), (1,)), ((), ())), preferred_element_type=F32)


def _skip_refs(kernel_fn, start, count, *refs):
    return kernel_fn(*refs[:start], *refs[start + count:])


def _gelu(x):
    return 0.5 * x * (1.0 + lax.erf(x * (2.0 ** -0.5)))


def _log_sigmoid(x):
    return jnp.minimum(x, 0.0) - jnp.log1p(jnp.exp(-jnp.abs(x)))


def _mod_kernel(c_ref, w_ref, b_ref, o_ref):
    c = c_ref[...]
    a = c * jax.nn.sigmoid(c)
    o_ref[...] = jnp.dot(a.astype(BF16), w_ref[...], preferred_element_type=F32) + b_ref[...]


def _modulation(c_all, w, b):
    rows, d = c_all.shape
    n = w.shape[1]
    tn = 1536
    return pl.pallas_call(
        _mod_kernel,
        out_shape=jax.ShapeDtypeStruct((rows, n), F32),
        grid=(n // tn,),
        in_specs=[pl.BlockSpec((rows, d), lambda j: (0, 0)),
                  pl.BlockSpec((d, tn), lambda j: (0, j)),
                  pl.BlockSpec((1, tn), lambda j: (0, j))],
        out_specs=pl.BlockSpec((rows, tn), lambda j: (0, j)),
        compiler_params=_cparams(("parallel",)),
        name="modulation",
    )(c_all, w, b)


def _modnorm_matmul_kernel(x_ref, g_ref, sc_ref, sh_ref, w_ref, o_ref, *h_out):
    x = x_ref[...]
    y = x * lax.rsqrt(jnp.mean(x * x, axis=-1, keepdims=True) + EPS) * g_ref[...]
    hf = y * (1.0 + sc_ref[...]) + sh_ref[...]
    h = hf.astype(BF16)
    o_ref[...] = jnp.dot(h, w_ref[...], preferred_element_type=F32).astype(o_ref.dtype)
    if h_out:
        h_out[0][...] = hf.T.astype(BF16)


def _row_batch(i, n_lat_blocks, blocks_per_batch, bsz):
    return jnp.where(i < n_lat_blocks, i // blocks_per_batch, bsz)


def _modnorm_matmul(x_all, n_rows, g, sc, sh, w, *, bsz, seq, tm, tn, out_dtype=F32, emit_h=False):
    d = x_all.shape[1]
    n = w.shape[1]
    nlb = bsz * seq // tm
    bpb = seq // tm
    assert not emit_h or tn == n
    mod_spec = pl.BlockSpec((None, 1, d), lambda j, i: (_row_batch(i, nlb, bpb, bsz), 0, 0))
    out_shape = [jax.ShapeDtypeStruct((n_rows, n), out_dtype)]
    out_specs = [pl.BlockSpec((tm, tn), lambda j, i: (i, j))]
    if emit_h:
        out_shape.append(jax.ShapeDtypeStruct((d, n_rows), BF16))
        out_specs.append(pl.BlockSpec((d, tm), lambda j, i: (0, i)))
    out = pl.pallas_call(
        _modnorm_matmul_kernel,
        out_shape=tuple(out_shape),
        grid=(n // tn, n_rows // tm),
        in_specs=[pl.BlockSpec((tm, d), lambda j, i: (i, 0)),
                  pl.BlockSpec((1, d), lambda j, i: (0, 0)),
                  mod_spec, mod_spec,
                  pl.BlockSpec((d, tn), lambda j, i: (0, j))],
        out_specs=tuple(out_specs),
        compiler_params=_cparams(("parallel", "parallel")),
        name="modnorm_matmul",
    )(x_all, g, sc, sh, w)
    return out if emit_h else out[0]


def _rope_kernel(q_ref, k_ref, v_ref, cos_ref, sin_ref, qo_ref, ko_ref, vo_ref):
    cos = cos_ref[...]
    sin = sin_ref[...]
    lane = lax.broadcasted_iota(jnp.int32, cos.shape, 1)
    first_half = (lane % DA_HD) < (DA_HD // 2)

    def rope(x):
        partner = jnp.where(first_half, pltpu.roll(x, LANES - DA_HD // 2, 1),
                            pltpu.roll(x, DA_HD // 2, 1))
        return x * cos + partner * sin

    for h in range(DA_HEADS):
        sl = slice(h * LANES, (h + 1) * LANES)
        qo_ref[:, sl] = (rope(q_ref[:, sl].astype(F32)) * (DA_HD ** -0.5 * LOG2E)).astype(BF16)
        ko_ref[:, sl] = rope(k_ref[:, sl].astype(F32)).astype(BF16)
    vo_ref[...] = v_ref[...].astype(BF16)


def _rope_qkv(z, cos_tab, sin_tab, *, bsz, seq, tm):
    rows = z.shape[0]
    nlb = bsz * seq // tm
    bpb = seq // tm
    w = BRANCH_W
    tab_spec = pl.BlockSpec((tm, LANES), lambda i: (jnp.where(i < nlb, i % bpb, bpb), 0))
    out = jax.ShapeDtypeStruct((rows, w), BF16)
    return pl.pallas_call(
        _rope_kernel,
        out_shape=(out, out, out),
        grid=(rows // tm,),
        in_specs=[pl.BlockSpec((tm, w), lambda i: (i, Z_Q // w)),
                  pl.BlockSpec((tm, w), lambda i: (i, Z_K // w)),
                  pl.BlockSpec((tm, w), lambda i: (i, Z_V // w)),
                  tab_spec, tab_spec],
        out_specs=(pl.BlockSpec((tm, w), lambda i: (i, 0)),) * 3,
        compiler_params=_cparams(("parallel",)),
        name="rope_qkv",
    )(z, z, z, cos_tab, sin_tab)


def _attn_kernel(*refs, n_seg, n_sub, out_scale):
    lam_ref, q_ref = refs[0], refs[1]
    k_refs = refs[2:2 + n_seg]
    v_refs = refs[2 + n_seg:2 + 2 * n_seg]
    g_ref, o_ref = refs[2 + 2 * n_seg], refs[3 + 2 * n_seg]
    tq = q_ref.shape[0] // n_sub
    lane = lax.broadcasted_iota(jnp.int32, (tq, LANES), 1)
    scores = []
    for sub in range(n_sub):
        q = q_ref[sub * tq:(sub + 1) * tq, :].astype(F32)
        qq = jnp.concatenate([jnp.where(lane < DA_HD, q, 0.0), jnp.where(lane >= DA_HD, q, 0.0)],
                             axis=0).astype(BF16)
        scores.append([_nt_dot(qq, k_ref[...]) for k_ref in k_refs])
    for sub in range(n_sub):
        rows = slice(sub * tq, (sub + 1) * tq)
        s = scores[sub]
        m = functools.reduce(jnp.maximum, [jnp.max(t, axis=-1, keepdims=True) for t in s])
        p = [jnp.exp2(t - m) for t in s]
        denom = functools.reduce(jnp.add, [jnp.sum(t, axis=-1, keepdims=True) for t in p])
        inv = 1.0 / denom
        c0 = inv[:tq]
        c1 = inv[tq:] * lam_ref[...]
        o = None
        for t, v_ref in zip(p, v_refs):
            w = (t[:tq] * c0 - t[tq:] * c1).astype(BF16)
            part = jnp.dot(w, v_ref[...], preferred_element_type=F32)
            o = part if o is None else o + part
        y = o * lax.rsqrt(jnp.mean(o * o, axis=-1, keepdims=True) + EPS) * g_ref[...]
        o_ref[rows, :] = (y * out_scale).astype(o_ref.dtype)


def _diff_attention(qr, kr, vb, lam, sub_g, *, q_row0, n_q, kv_segs, bsz, tq, out_scale, out_rows,
                    into=None):
    n_sub = 4 if n_q % (4 * tq) == 0 else 1
    tq = tq * n_sub
    nqb = n_q // tq
    n_seg = len(kv_segs)
    q_blk0 = q_row0 // tq

    def kv_spec(row0, length):
        return pl.BlockSpec((length, LANES), lambda b, h, i: (row0 // length + b, h))

    in_specs = [pl.BlockSpec((1, 1), lambda b, h, i: (0, 0)),
                pl.BlockSpec((tq, LANES), lambda b, h, i: (q_blk0 + b * nqb + i, h))]
    in_specs += [kv_spec(r0, ln) for r0, ln in kv_segs] * 2
    in_specs += [pl.BlockSpec((1, LANES), lambda b, h, i: (0, h))]
    args = [lam, qr] + [kr] * n_seg + [vb] * n_seg + [sub_g]
    kern = functools.partial(_attn_kernel, n_seg=n_seg, n_sub=n_sub, out_scale=out_scale)
    aliases = {}
    if into is not None:
        aliases = {len(args): 0}
        kern = functools.partial(_skip_refs, kern, len(args), 1)
        in_specs += [pl.BlockSpec(memory_space=pl.ANY)]
        args += [into]
    return pl.pallas_call(
        kern,
        out_shape=jax.ShapeDtypeStruct((out_rows, BRANCH_W), BF16),
        grid=(bsz, DA_HEADS, nqb),
        in_specs=in_specs,
        out_specs=pl.BlockSpec((tq, LANES), lambda b, h, i: (q_blk0 + b * nqb + i, h)),
        input_output_aliases=aliases,
        compiler_params=_cparams(("parallel", "parallel", "parallel")),
        name="diff_attention",
    )(*args)


def _ml_prep_kernel(x_ref, cw_ref, cb_ref, wq_ref, wk_ref, q_ref, k_ref, kt_ref):
    x = x_ref[...].astype(F32)
    n = x.shape[0]
    row = lax.broadcasted_iota(jnp.int32, x.shape, 0)
    x_prev = jnp.where(row == 0, 0.0, pltpu.roll(x, 1, 0))
    x_next = jnp.where(row == n - 1, 0.0, pltpu.roll(x, n - 1, 0))
    cw = cw_ref[...]
    y = x_prev * cw[0:1] + x * cw[1:2] + x_next * cw[2:3] + cb_ref[...]
    xc = (y * jax.nn.sigmoid(y)).astype(BF16)
    q_ref[...] = (jnp.dot(xc, wq_ref[...], preferred_element_type=F32) * (ML_HD ** -0.5)
                  ).astype(q_ref.dtype)
    k = jnp.dot(xc, wk_ref[...], preferred_element_type=F32)
    k_ref[...] = k.astype(k_ref.dtype)
    kt_ref[...] = k.T.astype(kt_ref.dtype)


def _ml_prep(z, conv_w, conv_b, wq, wk, *, row0, seq_len, bsz, into=None):
    rows = z.shape[0]
    blk0 = row0 // seq_len
    out = jax.ShapeDtypeStruct((rows, BRANCH_W), BF16)
    out_t = jax.ShapeDtypeStruct((BRANCH_W, rows), BF16)
    head_w = pl.BlockSpec((None, ML_HD, ML_HD), lambda b, h: (h, 0, 0))
    row_out = pl.BlockSpec((seq_len, LANES), lambda b, h: (blk0 + b, h))
    in_specs = [pl.BlockSpec((seq_len, LANES), lambda b, h: (blk0 + b, Z_XM // LANES + h)),
                pl.BlockSpec((3, LANES), lambda b, h: (0, h)),
                pl.BlockSpec((1, LANES), lambda b, h: (0, h)),
                head_w, head_w]
    args = [z, conv_w, conv_b, wq, wk]
    aliases = {}
    kern = _ml_prep_kernel
    if into is not None:
        n_in = len(args)
        in_specs += [pl.BlockSpec(memory_space=pl.ANY)] * len(into)
        args += list(into)
        aliases = {n_in + i: i for i in range(len(into))}
        kern = functools.partial(_skip_refs, _ml_prep_kernel, n_in, len(into))
    return pl.pallas_call(
        kern,
        out_shape=(out, out, out_t),
        grid=(bsz, ML_HEADS),
        in_specs=in_specs,
        out_specs=(row_out, row_out, pl.BlockSpec((LANES, seq_len), lambda b, h: (h, blk0 + b))),
        input_output_aliases=aliases,
        compiler_params=_cparams(("parallel", "parallel")),
        name="mlstm_prep",
    )(*args)


def _mlstm_kernel(qf_ref, kf_ref, ktf_ref, vf_ref, gcf_ref, grf_ref,
                  qb_ref, kb_ref, ktb_ref, vb_ref, gcb_ref, grb_ref,
                  hf_ref, hb_ref, c_ref, n_ref, m_ref):
    @pl.when(pl.program_id(1) == 0)
    def _():
        c_ref[...] = jnp.zeros_like(c_ref)
        n_ref[...] = jnp.zeros_like(n_ref)
        m_ref[...] = jnp.zeros_like(m_ref)

    t_idx = lax.broadcasted_iota(jnp.int32, (ML_CHUNK, ML_CHUNK), 0)
    s_idx = lax.broadcasted_iota(jnp.int32, (ML_CHUNK, ML_CHUNK), 1)
    nh = ML_HEADS
    dirs = ((qf_ref, kf_ref, ktf_ref, vf_ref, gcf_ref, grf_ref, hf_ref),
            (qb_ref, kb_ref, ktb_ref, vb_ref, gcb_ref, grb_ref, hb_ref))
    ones = jnp.ones((ML_CHUNK, LANES), BF16)
    units = []
    for d, (q_ref, k_ref, kt_ref, v_ref, gc_ref, gr_ref, h_ref) in enumerate(dirs):
        mask = s_idx <= t_idx if d == 0 else s_idx >= t_idx
        mask_f = mask.astype(F32)
        gc = gc_ref[...]
        gr = gr_ref[...]
        lf_c = _log_sigmoid(gc)
        lf_r = _log_sigmoid(gr)
        b_c = jnp.dot(mask_f, lf_c, preferred_element_type=F32, precision=lax.Precision.HIGHEST)
        b_r = lax.dot_general(lf_r, mask_f, (((1,), (1,)), ((), ())), preferred_element_type=F32,
                              precision=lax.Precision.HIGHEST)
        b_end_all = jnp.sum(lf_c, axis=0, keepdims=True)

        for h in range(nh):
            sl = slice(h * ML_HD, (h + 1) * ML_HD)
            q = q_ref[:, sl]
            k = k_ref[:, sl]
            kt = kt_ref[sl, :]
            v = v_ref[:, sl].astype(F32)
            li_c = gc[:, h:h + 1]
            li_r = gr[h:h + 1, :]
            bc = b_c[:, nh + h:nh + h + 1]
            br = b_r[nh + h:nh + h + 1, :]
            b_end = b_end_all[:, nh + h:nh + h + 1]
            m_prev = m_ref[d, h]
            c_prev = c_ref[d, h]
            n_prev = n_ref[d, h]
            g_c = b_end - bc + li_c
            g_r = b_end - br + li_r
            m_new = jnp.maximum(b_end + m_prev, jnp.max(g_r, axis=-1, keepdims=True))
            wg = jnp.broadcast_to(jnp.exp(g_c - m_new), (ML_CHUNK, LANES))
            decay = jnp.exp(b_end + m_prev - m_new)
            vw = (v * wg).astype(BF16)
            units.append(dict(
                d=d, h=h, sl=sl, h_ref=h_ref, v=v, mask=mask, m_prev=m_prev,
                bc=bc, br=br, li_r=li_r,
                qk=_nt_dot(q, k),
                qc=jnp.dot(q, c_prev.astype(BF16), preferred_element_type=F32),
                qn=jnp.dot(q, n_prev.astype(BF16), preferred_element_type=F32),
                c_new=decay * c_prev + jnp.dot(kt, vw, preferred_element_type=F32),
                n_new=decay * n_prev + jnp.dot(kt, wg.astype(BF16), preferred_element_type=F32),
                m_new=m_new))

    for u in units:
        dmat = jnp.where(u["mask"], u["bc"] - u["br"] + u["li_r"], -jnp.inf)
        inter = u["bc"] + u["m_prev"]
        m_t = jnp.maximum(inter, jnp.max(dmat, axis=-1, keepdims=True))
        s = (u["qk"] * jnp.exp(dmat - m_t)).astype(BF16)
        w_prev = jnp.exp(inter - m_t)
        num = jnp.dot(s, u["v"].astype(BF16), preferred_element_type=F32) + w_prev * u["qc"]
        den = jnp.dot(s, ones, preferred_element_type=F32) + w_prev * u["qn"]
        u["h_ref"][:, u["sl"]] = num / jnp.maximum(jnp.abs(den), jnp.exp(-m_t))
        c_ref[u["d"], u["h"]] = u["c_new"]
        n_ref[u["d"], u["h"]] = u["n_new"]
        m_ref[u["d"], u["h"]] = u["m_new"]


def _mlstm_scan(q_all, k_all, kt_all, z, gates_c, gates_r, *, bsz, seq, ctx_len):
    rows = q_all.shape[0]
    ncl = seq // ML_CHUNK
    ncc = ctx_len // ML_CHUNK
    lat_blocks = bsz * ncl
    nh = ML_HEADS

    def blk(d, b, s):
        jc = s if d == 0 else ncc - 1 - s
        jl = s - ncc if d == 0 else ncl - 1 - (s - ncc)
        return jnp.where(s < ncc, lat_blocks + b * ncc + jc, b * ncl + jl)

    def dir_specs(d):
        return [pl.BlockSpec((ML_CHUNK, BRANCH_W), lambda b, s: (blk(d, b, s), 0)),
                pl.BlockSpec((ML_CHUNK, BRANCH_W), lambda b, s: (blk(d, b, s), 0)),
                pl.BlockSpec((BRANCH_W, ML_CHUNK), lambda b, s: (0, blk(d, b, s))),
                pl.BlockSpec((ML_CHUNK, BRANCH_W), lambda b, s: (blk(d, b, s), Z_VM // BRANCH_W)),
                pl.BlockSpec((None, ML_CHUNK, 2 * nh), lambda b, s: (d, blk(d, b, s), 0)),
                pl.BlockSpec((None, 2 * nh, ML_CHUNK), lambda b, s: (d, 0, blk(d, b, s)))]

    out = jax.ShapeDtypeStruct((rows, BRANCH_W), F32)
    args = (q_all, k_all, kt_all, z, gates_c, gates_r)
    return pl.pallas_call(
        _mlstm_kernel,
        out_shape=(out, out),
        grid=(bsz, ncc + ncl),
        in_specs=dir_specs(0) + dir_specs(1),
        out_specs=(pl.BlockSpec((ML_CHUNK, BRANCH_W), lambda b, s: (blk(0, b, s), 0)),
                   pl.BlockSpec((ML_CHUNK, BRANCH_W), lambda b, s: (blk(1, b, s), 0))),
        scratch_shapes=[pltpu.VMEM((2, nh, ML_HD, ML_HD), F32),
                        pltpu.VMEM((2, nh, ML_HD, LANES), F32),
                        pltpu.VMEM((2, nh, 1, 1), F32)],
        compiler_params=_cparams(("parallel", "arbitrary")),
        name="mlstm_scan",
    )(*args, *args)


def _s5_kernel(uc_ref, ult_ref, m_ref, w_ref, v_ref, a1_ref, a2_ref, yc_ref, ylt_ref,
               ec_ref, el_ref, xc_ref, xl_ref, ul_ref, *, bsz):
    n_cl = ul_ref.shape[0] // bsz
    per_batch = []
    for b in range(bsz):
        cols = [ult_ref[:, (b * S5_T + t) * n_cl:(b * S5_T + t + 1) * n_cl].astype(F32)
                for t in range(S5_T)]
        per_batch.append(jnp.concatenate(cols, axis=0).T)
    ul_ref[...] = pltpu.einshape("bcf->(cb)f", jnp.stack(per_batch)).astype(BF16)

    w = w_ref[...]
    ec_ref[...] = jnp.dot(uc_ref[...], w, preferred_element_type=F32)
    el_ref[...] = jnp.dot(ul_ref[...], w, preferred_element_type=F32)
    a1 = a1_ref[...]
    a2 = a2_ref[...]
    half = LANES
    a1f, a2f, a1b, a2b = a1[:, 0:half], a2[:, 0:half], a1[:, half:], a2[:, half:]

    def scan(e_ref, x_ref, state):
        n_chunks = e_ref.shape[0] // bsz

        def body(j, st):
            xf, xfs, xb, xbs = st
            rf = pl.multiple_of(j * bsz, bsz)
            rb = pl.multiple_of((n_chunks - 1 - j) * bsz, bsz)
            x_ref[pl.ds(rf, bsz), 0:half] = xf
            x_ref[pl.ds(rb, bsz), half:2 * half] = xb
            ef, eb = e_ref[pl.ds(rf, bsz), 0:half], e_ref[pl.ds(rb, bsz), half:2 * half]
            efs = e_ref[pl.ds(rf, bsz), 2 * half:3 * half]
            ebs = e_ref[pl.ds(rb, bsz), 3 * half:4 * half]
            return (a1f * xf + a2f * xfs + ef, a1f * xfs - a2f * xf + efs,
                    a1b * xb + a2b * xbs + eb, a1b * xbs - a2b * xb + ebs)

        return lax.fori_loop(0, n_chunks, body, state)

    zero = jnp.zeros((bsz, half), F32)
    state = scan(ec_ref, xc_ref, (zero, zero, zero, zero))
    scan(el_ref, xl_ref, state)
    mm = m_ref[...]
    vv = v_ref[...]
    yc_ref[...] = (jnp.dot(uc_ref[...], mm, preferred_element_type=F32)
                   + jnp.dot(xc_ref[...].astype(BF16), vv, preferred_element_type=F32)
                   ).astype(yc_ref.dtype)
    yl = (jnp.dot(ul_ref[...], mm, preferred_element_type=F32)
          + jnp.dot(xl_ref[...].astype(BF16), vv, preferred_element_type=F32))
    yl = pltpu.einshape("(cb)f->bcf", yl, b=bsz)
    for b in range(bsz):
        ybt = yl[b].T
        for t in range(S5_T):
            ylt_ref[:, (b * S5_T + t) * n_cl:(b * S5_T + t + 1) * n_cl] = (
                ybt[t * S5_P:(t + 1) * S5_P, :].astype(ylt_ref.dtype))


def _s5_pack_kernel(x_ref, o_ref, f_ref):
    n_cl = x_ref.shape[0] // S5_T
    for j in range(f_ref.shape[0]):
        cs = slice(j * LANES, (j + 1) * LANES)
        f_ref[j] = x_ref[:, cs].astype(F32)
        for t in range(S5_T):
            o_ref[cs, t * n_cl:(t + 1) * n_cl] = (
                f_ref[j, pl.ds(t, n_cl, stride=S5_T), :].T.astype(o_ref.dtype))


def _s5_unpack_kernel(x_ref, o_ref, f_ref):
    n_cl = o_ref.shape[0] // S5_T
    for j in range(f_ref.shape[0]):
        cs = slice(j * LANES, (j + 1) * LANES)
        for t in range(S5_T):
            f_ref[j, pl.ds(t, n_cl, stride=S5_T), :] = (
                x_ref[cs, t * n_cl:(t + 1) * n_cl].astype(F32).T)
        o_ref[:, cs] = f_ref[j].astype(o_ref.dtype)


def _s5_pack(z, *, bsz, seq):
    return pl.pallas_call(
        _s5_pack_kernel,
        out_shape=jax.ShapeDtypeStruct((BRANCH_W, bsz * seq), BF16),
        grid=(bsz,),
        in_specs=[pl.BlockSpec((seq, BRANCH_W), lambda b: (b, Z_U // BRANCH_W))],
        out_specs=pl.BlockSpec((BRANCH_W, seq), lambda b: (0, b)),
        scratch_shapes=[pltpu.VMEM((BRANCH_W // LANES, seq, LANES), F32)],
        compiler_params=_cparams(("parallel",)),
        name="s5_pack",
    )(z)


def _s5_unpack(yt, *, bsz, seq, out_rows):
    in_specs = [pl.BlockSpec((BRANCH_W, seq), lambda b: (0, b))]
    args = [yt]
    kern, aliases = _s5_unpack_kernel, {}
    if out_rows > bsz * seq:
        in_specs.append(pl.BlockSpec(memory_space=pl.ANY))
        args.append(jnp.zeros((out_rows, BRANCH_W), BF16))
        kern, aliases = functools.partial(_skip_refs, _s5_unpack_kernel, 1, 1), {1: 0}
    return pl.pallas_call(
        kern,
        out_shape=jax.ShapeDtypeStruct((out_rows, BRANCH_W), BF16),
        grid=(bsz,),
        in_specs=in_specs,
        out_specs=pl.BlockSpec((seq, BRANCH_W), lambda b: (b, 0)),
        scratch_shapes=[pltpu.VMEM((BRANCH_W // LANES, seq, LANES), F32)],
        input_output_aliases=aliases,
        compiler_params=_cparams(("parallel",)),
        name="s5_unpack",
    )(*args)


def _s5_tables(lam_re, lam_im, log_dt, b_re, b_im, c_re, c_im, d_skip):
    t, g, n, p = S5_T, S5_G, S5_N, S5_P
    dt = jnp.exp(log_dt)[..., None]
    den = lam_re * lam_re + lam_im * lam_im
    kk = jnp.arange(t + 1, dtype=F32)[None, None, :, None]
    mag = jnp.exp(kk * (lam_re * dt)[:, :, None, :])
    ang = kk * (lam_im * dt)[:, :, None, :]
    p_re, p_im = mag * jnp.cos(ang), mag * jnp.sin(ang)
    ab_re, ab_im = p_re[:, :, 1], p_im[:, :, 1]
    z_re = ((ab_re - 1.0) * lam_re + ab_im * lam_im) / den
    z_im = (ab_im * lam_re - (ab_re - 1.0) * lam_im) / den
    bt_re, bt_im = jnp.swapaxes(b_re, 2, 3), jnp.swapaxes(b_im, 2, 3)
    bb_re = z_re[:, :, None, :] * bt_re - z_im[:, :, None, :] * bt_im
    bb_im = z_re[:, :, None, :] * bt_im + z_im[:, :, None, :] * bt_re
    pk_re, pk_im = p_re[:, :, :t, None, :], p_im[:, :, :t, None, :]
    abk_re = pk_re * bb_re[:, :, None] - pk_im * bb_im[:, :, None]
    abk_im = pk_re * bb_im[:, :, None] + pk_im * bb_re[:, :, None]
    taps = (jnp.einsum('rgpn,rgkqn->rgqkp', c_re, abk_re)
            - jnp.einsum('rgpn,rgkqn->rgqkp', c_im, abk_im))
    base_f = taps[0].reshape(g, p, t * p)
    base_b = taps[1][:, :, ::-1].reshape(g, p, t * p)
    width = t * p
    m_f = jnp.stack([jnp.pad(base_f, ((0, 0), (0, 0), (p * s, 0)))[..., :width]
                     for s in range(t)], axis=1)
    m_b = jnp.stack([jnp.pad(base_b, ((0, 0), (0, 0), (0, p * (t - 1 - s))))[..., p * (t - 1 - s):]
                     for s in range(t)], axis=1)
    skip = jnp.eye(width, dtype=F32)[None] * jnp.tile(d_skip.reshape(g, p), (1, t))[:, None, :]
    m = (m_f + m_b).reshape(g, width, width) + skip
    wf_re, wf_im = (a[0][:, ::-1].reshape(g, width, n) for a in (abk_re, abk_im))
    wb_re, wb_im = (a[1].reshape(g, width, n) for a in (abk_re, abk_im))
    w = jnp.concatenate([wf_re, wf_im, wb_re, wb_im, wf_im, wf_re, wb_im, wb_re], axis=-1)
    ct_re, ct_im = jnp.swapaxes(c_re, 2, 3), jnp.swapaxes(c_im, 2, 3)
    pn_re, pn_im = jnp.swapaxes(p_re, 2, 3)[..., None], jnp.swapaxes(p_im, 2, 3)[..., None]
    ca_re = ct_re[:, :, :, None, :] * pn_re - ct_im[:, :, :, None, :] * pn_im
    ca_im = ct_re[:, :, :, None, :] * pn_im + ct_im[:, :, :, None, :] * pn_re
    v = jnp.concatenate([ca_re[0][:, :, 1:].reshape(g, n, width),
                         -ca_im[0][:, :, 1:].reshape(g, n, width),
                         ca_re[1][:, :, :0:-1].reshape(g, n, width),
                         -ca_im[1][:, :, :0:-1].reshape(g, n, width)], axis=1)
    at_re, at_im = p_re[:, :, t], p_im[:, :, t]
    a1 = jnp.concatenate([at_re[0], at_re[0], at_re[1], at_re[1]], axis=-1)[:, None, :]
    a2 = jnp.concatenate([-at_im[0], at_im[0], -at_im[1], at_im[1]], axis=-1)[:, None, :]
    return m.astype(BF16), w.astype(BF16), v.astype(BF16), a1, a2


def _s5_rows(u, bsz, length):
    nc = length // S5_T
    u = u.reshape(bsz, nc, S5_T, S5_G, S5_P)
    return jnp.transpose(u, (3, 1, 0, 2, 4)).reshape(S5_G, nc * bsz, S5_T * S5_P)


def _s5_unrows(y, bsz, length):
    nc = length // S5_T
    y = y.reshape(S5_G, nc, bsz, S5_T, S5_P)
    return jnp.transpose(y, (2, 1, 3, 0, 4)).reshape(bsz * length, BRANCH_W)


def _s5_mix(u_ctx, ut_lat, tables, *, bsz):
    m, w, v, a1, a2 = tables
    rc = u_ctx.shape[1]
    t_lat = ut_lat.shape[1]
    rl = t_lat // S5_T
    wd = S5_T * S5_P

    def rows_spec(r):
        return pl.BlockSpec((None, r, wd), lambda g: (g, 0, 0))

    lat_spec = pl.BlockSpec((S5_P, t_lat), lambda g: (g, 0))
    sq = pl.BlockSpec((None, wd, wd), lambda g: (g, 0, 0))
    vec = pl.BlockSpec((None, 1, wd), lambda g: (g, 0, 0))
    return pl.pallas_call(
        functools.partial(_s5_kernel, bsz=bsz),
        out_shape=(jax.ShapeDtypeStruct((S5_G, rc, wd), BF16),
                   jax.ShapeDtypeStruct((BRANCH_W, t_lat), BF16)),
        grid=(S5_G,),
        in_specs=[rows_spec(rc), lat_spec, sq,
                  pl.BlockSpec((None, wd, 2 * wd), lambda g: (g, 0, 0)), sq, vec, vec],
        out_specs=(rows_spec(rc), lat_spec),
        scratch_shapes=[pltpu.VMEM((rc, 2 * wd), F32), pltpu.VMEM((rl, 2 * wd), F32),
                        pltpu.VMEM((rc, wd), F32), pltpu.VMEM((rl, wd), F32),
                        pltpu.VMEM((rl, wd), BF16)],
        compiler_params=_cparams(("parallel",)),
        name="s5_mix",
    )(u_ctx, ut_lat, m, w, v, a1, a2)


def _merge_kernel(x_ref, ya_ref, hf_ref, hb_ref, om_ref, ys_ref, gate_ref, g1_ref, mlg_ref,
                  gluw_ref, glub_ref, wbr_ref, wout_ref, o_ref):
    hsum = hf_ref[...] + hb_ref[...]
    og = jax.nn.sigmoid(om_ref[...].astype(F32))
    mlg = mlg_ref[...]
    yb_parts = []
    for h in range(ML_HEADS):
        sl = slice(h * ML_HD, (h + 1) * ML_HD)
        hh = hsum[:, sl]
        hn = hh * lax.rsqrt(jnp.mean(hh * hh, axis=-1, keepdims=True) + EPS) * mlg[:, sl]
        yb_parts.append(hn * og[:, sl])
    yb = jnp.concatenate(yb_parts, axis=-1)
    gl = _gelu(ys_ref[...].astype(F32))
    ys = gl * jax.nn.sigmoid(
        jnp.dot(gl.astype(BF16), gluw_ref[...], preferred_element_type=F32) + glub_ref[...])
    d = o_ref.shape[1]
    merged = None
    for r, y in enumerate((ya_ref[...], yb, ys)):
        proj = jnp.dot(y.astype(BF16), wbr_ref[r], preferred_element_type=F32)
        term = jax.nn.sigmoid(gate_ref[:, r * d:(r + 1) * d].astype(F32)) * proj
        merged = term if merged is None else merged + term
    y = jnp.dot(merged.astype(BF16), wout_ref[...], preferred_element_type=F32)
    o_ref[...] = x_ref[...] + g1_ref[...] * y


def _merge(x_all, ya, h_dirs, z, ys, g1, ml_norm_g, glu_w, glu_b, w_br, w_out, *, n_rows, bsz, seq, tm):
    d = x_all.shape[1]
    nlb = bsz * seq // tm
    bpb = seq // tm
    w = BRANCH_W

    def rows(width, col_blk=0):
        return pl.BlockSpec((tm, width), lambda i: (i, col_blk))

    def full(shape):
        return pl.BlockSpec(shape, lambda i: (0,) * len(shape))

    return pl.pallas_call(
        _merge_kernel,
        out_shape=jax.ShapeDtypeStruct((n_rows, d), F32),
        grid=(n_rows // tm,),
        in_specs=[rows(d), rows(w), rows(w), rows(w),
                  rows(w, Z_OM // w), rows(w),
                  rows(3 * d, Z_GATE // (3 * d)),
                  pl.BlockSpec((None, 1, d), lambda i: (_row_batch(i, nlb, bpb, bsz), 0, 0)),
                  full((1, w)), full((w, w)), full((1, w)), full((3, w, d)), full((d, d))],
        out_specs=rows(d),
        compiler_params=_cparams(("parallel",)),
        name="merge",
    )(x_all, ya, h_dirs[0], h_dirs[1], z, ys, z, g1, ml_norm_g, glu_w, glu_b, w_br, w_out)


NOT_TOP = 127.0


def _take_top(s, n_take, vals_ref, want_rank):
    rows = s.shape[0]
    ridx = lax.broadcasted_iota(jnp.int32, s.shape, 0).astype(F32)
    rank = jnp.full(s.shape, NOT_TOP, F32) if want_rank else None
    for i in range(n_take):
        mx = jnp.max(s, axis=0, keepdims=True)
        first = jnp.min(jnp.where(s == mx, ridx, float(rows)), axis=0, keepdims=True)
        hit = ridx == first
        s = jnp.where(hit, -jnp.inf, s)
        if want_rank:
            rank = jnp.where(hit, float(i), rank)
        vals_ref[i:i + 1, :] = mx
    return s, rank


SUBLANES = 8


def _sort_network_pairs(n):
    pairs = []
    p = 1
    while p < n:
        k = p
        while k >= 1:
            for j in range(k % p, n - k, 2 * k):
                for i in range(min(k, n - j - k)):
                    if (i + j) // (2 * p) == (i + j + k) // (2 * p):
                        pairs.append((i + j, i + j + k))
            k //= 2
        p *= 2
    return pairs


def _top_sorted(s):
    n = PEER_TOPK
    x = _slabs(s)
    assert len(x) <= n
    x = x + [jnp.full(x[0].shape, -jnp.inf, F32)] * (n - len(x))

    def exchange(i, j):
        x[i], x[j] = jnp.maximum(x[i], x[j]), jnp.minimum(x[i], x[j])

    for i, j in _sort_network_pairs(n):
        exchange(i, j)
    shift = SUBLANES // 2
    while shift >= 1:
        y = [pltpu.roll(v, shift, 0) for v in x]
        x = [jnp.maximum(x[i], y[n - 1 - i]) for i in range(n)]
        d = n // 2
        while d >= 1:
            for i in range(n):
                if not i & d:
                    exchange(i, i + d)
            d //= 2
        shift //= 2
    return x


def _slabs(s):
    return [s[SUBLANES * i:SUBLANES * (i + 1), :] for i in range(s.shape[0] // SUBLANES)]


def _tie_count(s, top):
    flag = jnp.zeros(top[0].shape, F32)
    for i in range(len(top) - 1):
        flag = flag + jnp.where(top[i] == top[i + 1], 1.0, 0.0)
    reach = functools.reduce(jnp.add, [jnp.where(v >= top[-1], 1.0, 0.0) for v in _slabs(s)])
    reach = jnp.sum(reach, axis=0, keepdims=True)
    return flag + jnp.where(reach != float(PEER_TOPK), 1.0, 0.0)


def _candidates(v1_ref, v2_ref):
    k = PEER_TOPK
    v2_head = v2_ref[0:8, :]
    jrow = lax.broadcasted_iota(jnp.int32, v2_head.shape, 0)
    pieces = [v1_ref[0:1, :] + v2_ref[...]]
    for i in range(1, 8):
        pieces.append(jnp.where(jrow < k // (i + 1), v1_ref[i:i + 1, :] + v2_head, -jnp.inf))
    pieces.append(v1_ref[8:16, :] + v2_ref[0:1, :])
    return jnp.concatenate(pieces, axis=0)


def _write_counts(picked, cnt_ref):
    cnt_ref[0:1, :] = jnp.sum(picked[0:16], axis=0, keepdims=True)
    for i in range(1, 8):
        cnt_ref[i:i + 1, :] = jnp.sum(picked[8 + 8 * i:16 + 8 * i], axis=0, keepdims=True)
    cnt_ref[8:16, :] = picked[72:80]


def _candidate_counts(v1_ref, v2_ref, top_ref, cnt_ref):
    cand = _candidates(v1_ref, v2_ref)
    cand_left, _ = _take_top(cand, PEER_TOPK, top_ref, False)
    _write_counts(jnp.where(cand_left != cand, 1.0, 0.0), cnt_ref)
    top = top_ref[...]
    return jnp.sum(jnp.exp(top - top[0:1, :]), axis=0, keepdims=True)


def _peer_topk_kernel(q_ref, k1_ref, k2_ref, e1_ref, cnt1_ref, e2_ref, rank2_ref,
                      v1_ref, v2_ref, top_ref, cnt_ref):
    k = PEER_TOPK
    q = q_ref[...].astype(BF16)
    s1 = _nt_dot(k1_ref[...], q[:, :LANES])
    s2 = _nt_dot(k2_ref[...], q[:, LANES:])
    top1 = _top_sorted(s1)
    top2 = _top_sorted(s2)
    for i in range(k):
        v1_ref[i:i + 1, :] = top1[i][0:1, :]
        v2_ref[i:i + 1, :] = top2[i][0:1, :]
    cand = _candidates(v1_ref, v2_ref)
    topc = _top_sorted(cand)
    any_tie = jnp.max(_tie_count(s1, top1) + _tie_count(s2, top2) + _tie_count(cand, topc)) > 0.0

    @pl.when(any_tie)
    def _():
        _, rank1 = _take_top(s1, k, v1_ref, True)
        _, rank2 = _take_top(s2, k, v2_ref, True)
        zsum = _candidate_counts(v1_ref, v2_ref, top_ref, cnt_ref)
        cnt1 = jnp.zeros(s1.shape, F32)
        for i in range(k):
            cnt1 = jnp.where(rank1 == float(i), cnt_ref[i:i + 1, :], cnt1)
        e1_ref[...] = jnp.where(rank1 < k, jnp.exp(s1 - v1_ref[0:1, :]), 0.0) / zsum
        cnt1_ref[...] = cnt1
        e2_ref[...] = jnp.where(rank2 < k, jnp.exp(s2 - v2_ref[0:1, :]), 0.0).astype(BF16)
        rank2_ref[...] = rank2.astype(BF16)

    @pl.when(jnp.logical_not(any_tie))
    def _():
        _write_counts(jnp.where(cand >= topc[k - 1][0:1, :], 1.0, 0.0), cnt_ref)
        zsum = functools.reduce(jnp.add, [jnp.exp(t[0:1, :] - topc[0][0:1, :]) for t in topc])
        cnt_rows = [jnp.broadcast_to(cnt_ref[i:i + 1, :], top1[0].shape) for i in range(k)]
        cnt1, rank2 = [], []
        for v in _slabs(s1):
            c = jnp.zeros(v.shape, F32)
            for i in range(k):
                c = jnp.where(v == top1[i], cnt_rows[i], c)
            cnt1.append(c)
        for v in _slabs(s2):
            r = jnp.full(v.shape, NOT_TOP, F32)
            for i in range(k):
                r = jnp.where(v == top2[i], float(i), r)
            rank2.append(r)
        e1_ref[...] = jnp.where(s1 >= v1_ref[k - 1:k, :], jnp.exp(s1 - v1_ref[0:1, :]), 0.0) / zsum
        cnt1_ref[...] = jnp.concatenate(cnt1, axis=0)
        e2_ref[...] = jnp.where(s2 >= v2_ref[k - 1:k, :], jnp.exp(s2 - v2_ref[0:1, :]), 0.0
                                ).astype(BF16)
        rank2_ref[...] = jnp.concatenate(rank2, axis=0).astype(BF16)


def _peer_topk(q, sub_k, *, n_rows, tm):
    nk = PEER_NKEYS
    row_tab = jax.ShapeDtypeStruct((n_rows // tm, PEER_HEADS, nk, tm), F32)
    tile_tab = jax.ShapeDtypeStruct((n_rows // tm, PEER_HEADS, nk, tm), BF16)
    tab_spec = pl.BlockSpec((None, None, nk, tm), lambda i, h: (i, h, 0, 0))
    return pl.pallas_call(
        _peer_topk_kernel,
        out_shape=(row_tab, row_tab, tile_tab, tile_tab),
        grid=(n_rows // tm, PEER_HEADS),
        in_specs=[pl.BlockSpec((tm, 2 * LANES), lambda i, h: (i, h)),
                  pl.BlockSpec((None, nk, LANES), lambda i, h: (0, 0, 0)),
                  pl.BlockSpec((None, nk, LANES), lambda i, h: (1, 0, 0))],
        out_specs=(tab_spec,) * 4,
        scratch_shapes=[pltpu.VMEM((PEER_TOPK, tm), F32)] * 4,
        compiler_params=_cparams(("parallel", "parallel")),
        name="peer_topk",
    )(q, sub_k, sub_k)


GATE_KEYS = 2
KEY_GROUP = 8


def _peer_dense_kernel(ht_ref, u_ref, vt_ref, e1_ref, cnt1_ref, e2_ref, rank2_ref,
                       x_ref, g2_ref, fg_ref, o_ref, acc_ref, act_ref, p_ref, e2s_ref, rank2s_ref, *,
                       a_per_blk, final_norm):
    j = pl.program_id(1)
    n_blk = pl.num_programs(1) - 1
    cur = j % 2
    tm = ht_ref.shape[1]
    nk = PEER_NKEYS

    @pl.when(j == 0)
    def _():
        acc_ref[...] = jnp.zeros_like(acc_ref)
        p_ref[1] = jnp.zeros(p_ref.shape[1:], p_ref.dtype)
        e2s_ref[...] = e2_ref[...]
        rank2s_ref[...] = rank2_ref[...]

    @pl.when(j < n_blk)
    def _():
        act_ref[...] = _gelu(jnp.dot(u_ref[...], ht_ref[...], preferred_element_type=F32)
                             ).astype(BF16)
        acc_ref[...] += jnp.dot(vt_ref[...], p_ref[1 - cur], preferred_element_type=F32)
        a0 = pl.multiple_of(j * a_per_blk, KEY_GROUP)

        def row_bf16(ref, h, i, ls):
            grp = (i // KEY_GROUP) * KEY_GROUP
            w = ref[h, pl.ds(a0 + grp, KEY_GROUP), ls][i % KEY_GROUP:i % KEY_GROUP + 1]
            return jnp.broadcast_to(w, (nk, LANES)).astype(BF16)

        for lt in range(tm // LANES):
            ls = slice(lt * LANES, (lt + 1) * LANES)
            for i0 in range(0, a_per_blk, GATE_KEYS):
                gates = [None] * GATE_KEYS
                for h in range(PEER_HEADS):
                    for ii in range(GATE_KEYS):
                        lead = row_bf16(cnt1_ref, h, i0 + ii, ls) - rank2s_ref[h, :, ls]
                        term = jnp.minimum(jnp.maximum(lead, 0.0),
                                           row_bf16(e1_ref, h, i0 + ii, ls)) * e2s_ref[h, :, ls]
                        gates[ii] = term if gates[ii] is None else gates[ii] + term
                for ii in range(GATE_KEYS):
                    r0 = (i0 + ii) * nk
                    p_ref[cur, r0:r0 + nk, ls] = gates[ii] * act_ref[r0:r0 + nk, ls]

    @pl.when(j == n_blk)
    def _():
        acc = acc_ref[...] + jnp.dot(vt_ref[...], p_ref[1 - cur], preferred_element_type=F32)
        y = x_ref[...] + g2_ref[...] * acc.T
        if final_norm:
            y = y * lax.rsqrt(jnp.mean(y * y, axis=-1, keepdims=True) + EPS) * fg_ref[...]
        o_ref[...] = y


def _peer_dense(h2, u_tab, vt_tab, tabs, x_all, g2, final_g, *, n_rows, bsz, seq, tm, te,
                final_norm):
    d = x_all.shape[1]
    n_exp = u_tab.shape[0]
    nlb = bsz * seq // tm
    bpb = seq // tm
    nk = PEER_NKEYS
    n_blk = n_exp // te
    assert (te // nk) % KEY_GROUP == 0
    tab_spec = pl.BlockSpec((None, PEER_HEADS, nk, tm), lambda i, j: (i, 0, 0, 0))
    return pl.pallas_call(
        functools.partial(_peer_dense_kernel, a_per_blk=te // nk, final_norm=final_norm),
        out_shape=jax.ShapeDtypeStruct((n_rows, d), F32),
        grid=(n_rows // tm, n_blk + 1),
        in_specs=[pl.BlockSpec((d, tm), lambda i, j: (0, i)),
                  pl.BlockSpec((te, d), lambda i, j: (jnp.minimum(j, n_blk - 1), 0)),
                  pl.BlockSpec((None, d, te), lambda i, j: (jnp.maximum(j - 1, 0), 0, 0)),
                  tab_spec, tab_spec, tab_spec, tab_spec,
                  pl.BlockSpec((tm, d), lambda i, j: (i, 0)),
                  pl.BlockSpec((None, 1, d), lambda i, j: (_row_batch(i, nlb, bpb, bsz), 0, 0)),
                  pl.BlockSpec((1, d), lambda i, j: (0, 0))],
        out_specs=pl.BlockSpec((tm, d), lambda i, j: (i, 0)),
        scratch_shapes=[pltpu.VMEM((d, tm), F32), pltpu.VMEM((te, tm), BF16),
                        pltpu.VMEM((2, te, tm), BF16),
                        pltpu.VMEM((PEER_HEADS, nk, tm), BF16), pltpu.VMEM((PEER_HEADS, nk, tm), BF16)],
        compiler_params=_cparams(("parallel", "arbitrary")),
        name="peer_dense",
    )(h2, u_tab, vt_tab, *tabs, x_all, g2, final_g)


def _transpose_cast_kernel(x_ref, o_ref):
    o_ref[...] = x_ref[...].T.astype(o_ref.dtype)


def _transpose_cast(w, layer, *, tr):
    _, rows, cols = w.shape
    return pl.pallas_call(
        _transpose_cast_kernel,
        out_shape=jax.ShapeDtypeStruct((rows // tr, cols, tr), BF16),
        grid=(rows // tr,),
        in_specs=[pl.BlockSpec((None, tr, cols), lambda i: (layer, i, 0))],
        out_specs=pl.BlockSpec((None, cols, tr), lambda i: (i, 0, 0)),
        compiler_params=_cparams(("parallel",)),
        name="transpose_cast",
    )(w)


def _rope_tables(seq, tm):
    rows = seq // GRID_W
    n_freq = DA_HD // 4
    inv = ROPE_BASE ** (-jnp.arange(n_freq, dtype=F32) / n_freq)
    r = jnp.repeat(jnp.arange(rows, dtype=F32), GRID_W)
    col = jnp.tile(jnp.arange(GRID_W, dtype=F32), rows)
    ang = jnp.concatenate([r[:, None] * inv, col[:, None] * inv], axis=-1)
    cos, sin = jnp.cos(ang), jnp.sin(ang)
    cos_t = jnp.tile(cos, (1, 4))
    sin_t = jnp.tile(jnp.concatenate([-sin, sin], axis=-1), (1, 2))
    ident = jnp.ones((tm, LANES), F32)
    return (jnp.concatenate([cos_t, ident], axis=0),
            jnp.concatenate([sin_t, jnp.zeros((tm, LANES), F32)], axis=0))


def kernel(x, c, ctx, c_ctx, w_mod, b_mod, norm1_g, norm2_g, w_in, da_lam_q1, da_lam_k1, da_lam_q2, da_lam_k2, da_sub_g, ml_conv_w, ml_conv_b, ml_wq, ml_wk, ml_gate_b, ml_norm_g, s5_lam_re, s5_lam_im, s5_log_dt, s5_b_re, s5_b_im, s5_c_re, s5_c_im, s5_d, s5_glu_w, s5_glu_b, w_br, w_out, peer_wq, peer_sub_k, peer_u, peer_v, final_g):
    bsz, seq, d = x.shape
    ctx_len = ctx.shape[1]
    depth = w_in.shape[0]
    t_lat = bsz * seq
    t_ctx = bsz * ctx_len
    t_all = t_lat + t_ctx
    tm = math.gcd(512, math.gcd(seq, t_ctx))
    tm_merge = tm
    tq = min(256, ctx_len)
    nh = ML_HEADS

    x_all = jnp.concatenate([x.reshape(t_lat, d), ctx.reshape(t_ctx, d)], axis=0)
    pad = (-(bsz + 1)) % 8
    c_all = jnp.concatenate([c, c_ctx[None], jnp.zeros((pad, d), F32)], axis=0)
    cos_tab, sin_tab = _rope_tables(seq, tm)

    for l in range(depth):
        need_ctx = l < depth - 1
        n_rows = t_all if need_ctx else t_lat
        lam_init = 0.8 - 0.6 * math.exp(-0.3 * l)

        mods = _modulation(c_all, w_mod[l].astype(BF16), b_mod[l][None])
        sh1, sc1, g1, sh2, sc2, g2 = [mods[:bsz + 1, i * d:(i + 1) * d][:, None, :]
                                      for i in range(N_MOD)]

        wi = w_in[l]
        i_q, i_k, i_v, i_xm, i_vm, i_om, i_g, i_u, i_gate = (
            0, 512, 1024, 1536, 2048, 2560, 3072, 3088, 3600)
        w_main = jnp.concatenate([wi[:, i_gate:], wi[:, :i_g], wi[:, i_u:i_gate]], axis=1).astype(BF16)
        w_gate = jnp.pad(wi[:, i_g:i_u], ((0, 0), (0, LANES - (i_u - i_g)))).astype(BF16)
        n1 = norm1_g[l][None]
        z = _modnorm_matmul(x_all, t_all, n1, sc1, sh1, w_main, bsz=bsz, seq=seq, tm=tm,
                            tn=w_main.shape[1] // 2, out_dtype=BF16)
        zg = _modnorm_matmul(x_all, t_all, n1, sc1, sh1, w_gate, bsz=bsz, seq=seq, tm=tm, tn=LANES)

        qr, kr, vb = _rope_qkv(z, cos_tab, sin_tab, bsz=bsz, seq=seq, tm=tm)
        lam = (jnp.exp(jnp.sum(da_lam_q1[l] * da_lam_k1[l]))
               - jnp.exp(jnp.sum(da_lam_q2[l] * da_lam_k2[l])) + lam_init).reshape(1, 1)
        sub_g = da_sub_g[l][None]
        ya = _diff_attention(qr, kr, vb, lam, sub_g, q_row0=0, n_q=seq,
                             kv_segs=[(0, seq), (t_lat, ctx_len)], bsz=bsz, tq=tq,
                             out_scale=1.0 - lam_init, out_rows=n_rows,
                             into=jnp.zeros((n_rows, BRANCH_W), BF16) if need_ctx else None)
        if need_ctx:
            ya = _diff_attention(qr, kr, vb, lam, sub_g, q_row0=t_lat, n_q=ctx_len,
                                 kv_segs=[(t_lat, ctx_len)], bsz=bsz, tq=tq,
                                 out_scale=1.0 - lam_init, out_rows=n_rows, into=ya)

        wq_b, wk_b = ml_wq[l].astype(BF16), ml_wk[l].astype(BF16)
        cb = ml_conv_b[l][None]
        blank = jnp.zeros((t_all, BRANCH_W), BF16)
        qkt = _ml_prep(z, ml_conv_w[l], cb, wq_b, wk_b, row0=0, seq_len=seq, bsz=bsz,
                       into=(blank, blank, blank.T))
        q_ml, k_ml, kt_ml = _ml_prep(z, ml_conv_w[l], cb, wq_b, wk_b, row0=t_lat, seq_len=ctx_len,
                                     bsz=bsz, into=qkt)
        gates = (zg[:, :4 * nh] + ml_gate_b[l]).reshape(t_all, 2, 2 * nh)
        gates_c = jnp.transpose(gates, (1, 0, 2))
        gates_r = jnp.transpose(gates, (1, 2, 0))
        h_dirs = _mlstm_scan(q_ml, k_ml, kt_ml, z, gates_c, gates_r, bsz=bsz, seq=seq,
                             ctx_len=ctx_len)

        tables = _s5_tables(s5_lam_re[l], s5_lam_im[l], s5_log_dt[l], s5_b_re[l], s5_b_im[l],
                            s5_c_re[l], s5_c_im[l], s5_d[l])
        u_ctx = _s5_rows(z[t_lat:, Z_U:Z_U + BRANCH_W], bsz, ctx_len)
        y_c, yt_l = _s5_mix(u_ctx, _s5_pack(z, bsz=bsz, seq=seq), tables, bsz=bsz)
        ys = _s5_unpack(yt_l, bsz=bsz, seq=seq, out_rows=n_rows)
        if need_ctx:
            ys = lax.dynamic_update_slice(ys, _s5_unrows(y_c, bsz, ctx_len), (t_lat, 0))

        x_all = _merge(x_all, ya, h_dirs, z, ys, g1, ml_norm_g[l][None],
                       s5_glu_w[l].astype(BF16), s5_glu_b[l][None], w_br[l].astype(BF16),
                       w_out[l].astype(BF16), n_rows=n_rows, bsz=bsz, seq=seq, tm=tm_merge)

        n2 = norm2_g[l][None]
        wq_p = peer_wq[l].astype(BF16)
        pq, h2t = _modnorm_matmul(x_all, n_rows, n2, sc2, sh2, wq_p, bsz=bsz, seq=seq, tm=tm,
                                  tn=wq_p.shape[1], emit_h=True)
        tabs = _peer_topk(pq, peer_sub_k[l].astype(BF16), n_rows=n_rows, tm=tm)
        te = 2048
        x_all = _peer_dense(h2t, peer_u[l].astype(BF16), _transpose_cast(peer_v, l, tr=te),
                            tabs, x_all, g2, final_g[None], n_rows=n_rows, bsz=bsz, seq=seq, tm=tm,
                            te=te, final_norm=not need_ctx)

    return x_all.reshape(bsz, seq, d)
```
